```python
import jax, jax.numpy as jnp
from jax import lax
import numpy as np

D_MODEL = 1024
BATCH = 8
SEQ = 2048
DEPTH = 2

HEAD_DIM = 64
ROT_DIM = HEAD_DIM // 4
ROPE_THETA = 500000.0
DIL_GROUPS = ((128, 1), (512, 4), (2048, 16))
N_GROUPS = len(DIL_GROUPS)
HEADS_PER_GROUP = 8
GROUP_WIDTH = HEADS_PER_GROUP * HEAD_DIM
BLOCK = 128
N_MEM = 256
MEM_HEADS = 4
MEM_WIDTH = MEM_HEADS * HEAD_DIM
CONV_WIDTH = D_MODEL
CONV_K = 3
N_MIXERS = 2
N_ATTN_LAYERS = (DEPTH + 1) // 2
N_CONV_LAYERS = DEPTH // 2
BRANCH_A = GROUP_WIDTH + MEM_WIDTH
BRANCH_B = CONV_WIDTH + MEM_WIDTH
IN_A = 3 * N_GROUPS * GROUP_WIDTH + MEM_WIDTH + BRANCH_A
IN_B = 3 * CONV_WIDTH + MEM_WIDTH + BRANCH_B
EPS = 1e-6

kernel_name = "hybrid_dilated_attn_shortconv_memory"


def rms_norm(x, g):
    xf = x.astype(jnp.float32)
    y = xf * lax.rsqrt(jnp.mean(xf * xf, axis=-1, keepdims=True) + EPS)
    return (y * g.astype(jnp.float32)).astype(x.dtype)


def partial_rope(t, positions):
    half = ROT_DIM // 2
    inv_freq = ROPE_THETA ** (-jnp.arange(half, dtype=jnp.float32) * (2.0 / ROT_DIM))
    ang = positions.astype(jnp.float32)[:, :, None] * inv_freq
    cos = jnp.cos(ang)[:, :, None, :]
    sin = jnp.sin(ang)[:, :, None, :]
    tr = t[..., :ROT_DIM].astype(jnp.float32)
    t1, t2 = tr[..., :half], tr[..., half:]
    rot = jnp.concatenate([t1 * cos - t2 * sin, t2 * cos + t1 * sin], axis=-1)
    return jnp.concatenate([rot.astype(t.dtype), t[..., ROT_DIM:]], axis=-1)


def dilated_window_attention(q, k, v, window, dilation):
    b, s, h, dh = q.shape
    n_stream = s // dilation
    span = window // dilation
    nb = -(-n_stream // BLOCK)
    lp = nb * BLOCK

    def to_streams(t):
        t = t.reshape(b, n_stream, dilation, h, dh).transpose(0, 2, 1, 3, 4)
        return jnp.pad(t, ((0, 0), (0, 0), (0, lp - n_stream), (0, 0), (0, 0)))

    def banded(t):
        t = jnp.pad(t, ((0, 0), (0, 0), (BLOCK, 0), (0, 0), (0, 0)))
        t = t.reshape(b, dilation, nb + 1, BLOCK, h, dh)
        return jnp.concatenate([t[:, :, :-1], t[:, :, 1:]], axis=3)

    qb = to_streams(q).reshape(b, dilation, nb, BLOCK, h, dh)
    kb = banded(to_streams(k))
    vb = banded(to_streams(v))

    scores = jnp.einsum('brnqhd,brnkhd->brnhqk', qb, kb).astype(jnp.float32) * (dh ** -0.5)
    qi = jnp.arange(BLOCK)[:, None]
    kj = jnp.arange(2 * BLOCK)[None, :]
    blk = jnp.arange(nb)[:, None, None]
    dist = qi + BLOCK - kj
    kpos = blk * BLOCK + kj - BLOCK
    valid = (dist >= 0) & (dist <= span) & (kpos >= 0)
    scores = jnp.where(valid[None, None, :, None], scores, -jnp.inf)
    lse = jax.nn.logsumexp(scores, axis=-1)
    p = jnp.exp(scores - lse[..., None])
    out = jnp.einsum('brnhqk,brnkhd->brnqhd', p.astype(vb.dtype), vb).astype(jnp.float32)

    out = out.reshape(b, dilation, lp, h, dh)[:, :, :n_stream]
    out = out.transpose(0, 2, 1, 3, 4).reshape(b, s, h, dh)
    lse = lse.transpose(0, 1, 2, 4, 3).reshape(b, dilation, lp, h)[:, :, :n_stream]
    lse = lse.transpose(0, 2, 1, 3).reshape(b, s, h)
    return out, lse


def memory_cross_attention(qm, mem_n, w_mem_kv):
    b, s, _ = qm.shape
    kv = mem_n @ w_mem_kv
    km, vm = jnp.split(kv, 2, axis=-1)
    km = km.reshape(b, N_MEM, MEM_HEADS, HEAD_DIM)
    vm = vm.reshape(b, N_MEM, MEM_HEADS, HEAD_DIM)
    qh = qm.reshape(b, s, MEM_HEADS, HEAD_DIM)
    scores = jnp.einsum('bshd,bmhd->bhsm', qh, km).astype(jnp.float32) * (HEAD_DIM ** -0.5)
    p = jax.nn.softmax(scores, axis=-1)
    out = jnp.einsum('bhsm,bmhd->bshd', p.astype(vm.dtype), vm)
    return out.reshape(b, s, MEM_WIDTH)


def dilated_attention_layer(hn, positions, mem_n, w_in, w_mem_kv, w_out):
    b, s, _ = hn.shape
    gw = N_GROUPS * GROUP_WIDTH
    proj = hn @ w_in
    q, k, v, qm, z = jnp.split(proj, [gw, 2 * gw, 3 * gw, 3 * gw + MEM_WIDTH], axis=-1)
    n_heads = N_GROUPS * HEADS_PER_GROUP
    q = partial_rope(q.reshape(b, s, n_heads, HEAD_DIM), positions)
    k = partial_rope(k.reshape(b, s, n_heads, HEAD_DIM), positions)
    v = v.reshape(b, s, n_heads, HEAD_DIM)
    outs, lses = [], []
    for g, (window, dilation) in enumerate(DIL_GROUPS):
        sl = slice(g * HEADS_PER_GROUP, (g + 1) * HEADS_PER_GROUP)
        o, l = dilated_window_attention(q[:, :, sl], k[:, :, sl], v[:, :, sl], window, dilation)
        outs.append(o)
        lses.append(l)
    wts = jax.nn.softmax(jnp.stack(lses, axis=0), axis=0)
    mix = jnp.sum(wts[..., None] * jnp.stack(outs, axis=0), axis=0)
    mix = mix.reshape(b, s, GROUP_WIDTH).astype(hn.dtype)
    mem_out = memory_cross_attention(qm, mem_n, w_mem_kv)
    y = jnp.concatenate([mix, mem_out], axis=-1) * jax.nn.silu(z)
    return y @ w_out


def short_conv_layer(hn, mem_n, w_in, conv_w, w_mem_kv, w_out):
    c = CONV_WIDTH
    proj = hn @ w_in
    bg, cg, u, qm, z = jnp.split(proj, [c, 2 * c, 3 * c, 3 * c + MEM_WIDTH], axis=-1)
    conv = lax.conv_general_dilated(
        cg * u, conv_w[:, None, :].astype(u.dtype),
        window_strides=(1,), padding=((CONV_K - 1, 0),),
        dimension_numbers=('NWC', 'WIO', 'NWC'), feature_group_count=c)
    mix = bg * conv
    mem_out = memory_cross_attention(qm, mem_n, w_mem_kv)
    y = jnp.concatenate([mix, mem_out], axis=-1) * jax.nn.silu(z)
    return y @ w_out


def setup_inputs(seed: int = 0) -> dict:
    key = jax.random.key(seed)
    ks = jax.random.split(key, 14)
    f32 = jnp.float32

    def nrm(k, shape, fan_in):
        return jax.random.normal(k, shape, f32) * (fan_in ** -0.5)

    x = jax.random.normal(ks[0], (BATCH, SEQ, D_MODEL), f32)
    mem = jax.random.normal(ks[1], (BATCH, N_MEM, D_MODEL), f32)
    offset = jax.random.randint(ks[2], (BATCH, 1), 0, 1024, dtype=jnp.int32)
    positions = offset + jnp.arange(SEQ, dtype=jnp.int32)[None, :]
    norm_g = 1.0 + 0.05 * jax.random.normal(ks[3], (DEPTH, D_MODEL), f32)
    mem_norm_g = 1.0 + 0.05 * jax.random.normal(ks[4], (DEPTH, D_MODEL), f32)
    w_mem_kv = nrm(ks[5], (DEPTH, D_MODEL, 2 * MEM_WIDTH), D_MODEL)
    attn_w_in = nrm(ks[6], (N_ATTN_LAYERS, D_MODEL, IN_A), D_MODEL)
    attn_w_out = nrm(ks[7], (N_ATTN_LAYERS, BRANCH_A, D_MODEL), BRANCH_A)
    conv_w_in = nrm(ks[8], (N_CONV_LAYERS, D_MODEL, IN_B), D_MODEL)
    conv_w = nrm(ks[9], (N_CONV_LAYERS, CONV_K, CONV_WIDTH), CONV_K)
    conv_w_out = nrm(ks[10], (N_CONV_LAYERS, BRANCH_B, D_MODEL), BRANCH_B)
    final_g = 1.0 + 0.05 * jax.random.normal(ks[11], (D_MODEL,), f32)
    return {"x": x, "mem": mem, "positions": positions, "norm_g": norm_g,
            "mem_norm_g": mem_norm_g, "w_mem_kv": w_mem_kv,
            "attn_w_in": attn_w_in, "attn_w_out": attn_w_out,
            "conv_w_in": conv_w_in, "conv_w": conv_w, "conv_w_out": conv_w_out,
            "final_g": final_g}


def reference(x, mem, positions, norm_g, mem_norm_g, w_mem_kv, attn_w_in, attn_w_out,
              conv_w_in, conv_w, conv_w_out, final_g):
    h = x
    for i in range(DEPTH):
        j = i // N_MIXERS
        hn = rms_norm(h, norm_g[i])
        mem_n = rms_norm(mem, mem_norm_g[i])
        if i % N_MIXERS == 0:
            delta = dilated_attention_layer(hn, positions, mem_n, attn_w_in[j],
                                            w_mem_kv[i], attn_w_out[j])
        else:
            delta = short_conv_layer(hn, mem_n, conv_w_in[j], conv_w[j],
                                     w_mem_kv[i], conv_w_out[j])
        h = h + delta
    return rms_norm(h, final_g)
```

```python
import functools

import numpy as np
import jax
import jax.numpy as jnp
from jax import lax
from jax.experimental import pallas as pl
from jax.experimental.pallas import tpu as pltpu

F32 = jnp.float32
BF16 = jnp.bfloat16

D_MODEL = 1024
HEAD_DIM = 64
ROT_DIM = 16
ROPE_THETA = 500000.0
DILATIONS = (1, 4, 16)
BLOCK = 128
GROUP_WIDTH = 512
N_MEM = 256
MEM_WIDTH = 256
CONV_WIDTH = 1024
EPS = 1e-6
SCORE_SCALE = HEAD_DIM ** -0.5

V7X_LANES = 128
ROW_TILE = 512
COL_CHUNK = 256
VMEM_LIMIT_BYTES = 56 * 1024 * 1024
NEG_BIG = -1e30

QKV_G = 3 * GROUP_WIDTH
IN_A_COLS = 3 * QKV_G + MEM_WIDTH + (GROUP_WIDTH + MEM_WIDTH)
A_QM0 = QKV_G
A_Z0 = A_QM0 + MEM_WIDTH
A_G1 = A_Z0 + GROUP_WIDTH + MEM_WIDTH
A_G2 = A_G1 + QKV_G
B_BG, B_CG, B_U = 0, CONV_WIDTH, 2 * CONV_WIDTH
B_QM = 3 * CONV_WIDTH
B_Z = B_QM + MEM_WIDTH
BRANCH_B = CONV_WIDTH + MEM_WIDTH


def _pair_lane_source():
    src = np.zeros(V7X_LANES, np.int32)
    for j in range(V7X_LANES):
        if j < 8:
            e, dim = 0, j
        elif j < 16:
            e, dim = 1, j - 8
        elif j < 64:
            e, dim = 0, j
        elif j < 72:
            e, dim = 0, 8 + (j - 64)
        elif j < 80:
            e, dim = 1, 8 + (j - 72)
        else:
            e, dim = 1, j - 64
        src[j] = 64 * e + dim
    return src


def _attn_in_columns():
    src = _pair_lane_source()
    gw3 = 3 * GROUP_WIDTH
    cols, scale = [], []

    def group(g):
        for part, off in (("q", 0), ("k", gw3), ("v", 2 * gw3)):
            for p in range(4):
                base = off + GROUP_WIDTH * g + V7X_LANES * p
                lanes = src if part != "v" else np.arange(V7X_LANES)
                cols.extend((base + lanes).tolist())
                scale.extend([SCORE_SCALE if part == "q" else 1.0] * V7X_LANES)

    group(0)
    cols.extend(range(3 * gw3, 3 * gw3 + MEM_WIDTH))
    scale.extend([SCORE_SCALE] * MEM_WIDTH)
    cols.extend(range(3 * gw3 + MEM_WIDTH, IN_A_COLS))
    scale.extend([1.0] * (GROUP_WIDTH + MEM_WIDTH))
    group(1)
    group(2)
    return np.asarray(cols, np.int32), np.asarray(scale, np.float32)


_A_COLS, _A_SCALE = _attn_in_columns()


def _rope_lane_tables():
    idx = np.zeros(V7X_LANES, np.int32)
    sgn = np.zeros(V7X_LANES, np.float32)
    for j in range(V7X_LANES):
        if j < 16:
            idx[j], sgn[j] = j % 8, -1.0
        elif 64 <= j < 80:
            idx[j], sgn[j] = j % 8, 1.0
    return idx, sgn


_ROPE_IDX, _ROPE_SGN = _rope_lane_tables()


def _params(*sem):
    return pltpu.CompilerParams(dimension_semantics=sem, vmem_limit_bytes=VMEM_LIMIT_BYTES)


def _resident(shape, index_map):
    return pl.BlockSpec(shape, index_map, pipeline_mode=pl.Buffered(1))


def _silu(z):
    return z * jax.nn.sigmoid(z)


def _dot(a, b):
    return jnp.dot(a, b, preferred_element_type=F32)


def _dot_nt(a, b):
    return lax.dot_general(a, b, (((1,), (1,)), ((), ())), preferred_element_type=F32)


def _rope_kernel(pos_ref, freq_ref, cos_ref, sin_ref):
    ang = pos_ref[0].astype(F32) * freq_ref[...]
    cos_ref[0] = jnp.cos(ang)
    sin_ref[0] = jnp.sin(ang)


def _rope_tables(pos_lanes, lane_freq):
    b, s, _ = pos_lanes.shape
    spec = pl.BlockSpec((1, ROW_TILE, V7X_LANES), lambda i, j: (i, j, 0))
    return pl.pallas_call(
        _rope_kernel,
        grid=(b, s // ROW_TILE),
        in_specs=[spec, pl.BlockSpec((1, V7X_LANES), lambda i, j: (0, 0))],
        out_specs=[spec, spec],
        out_shape=[jax.ShapeDtypeStruct((b, s, V7X_LANES), F32)] * 2,
        compiler_params=_params("parallel", "parallel"),
        name="rope_tables",
    )(pos_lanes, lane_freq)


def _memkv_kernel(mem_ref, g_ref, w_ref, kv_ref):
    m = mem_ref[0]
    ms = jnp.mean(m * m, axis=-1, keepdims=True)
    mn = (m * lax.rsqrt(ms + EPS) * g_ref[0]).astype(BF16)
    kv_ref[0, 0] = _dot(mn, w_ref[0]).astype(BF16)


def _mem_kv(mem, mem_norm_g, w_kv_bf):
    b = mem.shape[0]
    depth = w_kv_bf.shape[0]
    return pl.pallas_call(
        _memkv_kernel,
        grid=(depth, b),
        in_specs=[
            pl.BlockSpec((1, N_MEM, D_MODEL), lambda l, i: (i, 0, 0)),
            pl.BlockSpec((1, 1, D_MODEL), lambda l, i: (l, 0, 0)),
            pl.BlockSpec((1, D_MODEL, 2 * MEM_WIDTH), lambda l, i: (l, 0, 0)),
        ],
        out_specs=pl.BlockSpec((1, 1, N_MEM, 2 * MEM_WIDTH), lambda l, i: (l, i, 0, 0)),
        out_shape=jax.ShapeDtypeStruct((depth, b, N_MEM, 2 * MEM_WIDTH), BF16),
        compiler_params=_params("parallel", "parallel"),
        name="mem_kv",
    )(mem, mem_norm_g.reshape(depth, 1, D_MODEL), w_kv_bf)


def _rope(a, cs, sn):
    return a * cs + pltpu.roll(a, 64, 1) * sn


def _proj0_kernel(x_ref, g_ref, cos_ref, sin_ref, w_ref,
                  o0_ref, oqm_ref, oz_ref, o1_ref, o2_ref, hn_scr, perm_scr):
    tm = x_ref.shape[1]
    x = x_ref[0]
    ms = jnp.mean(x * x, axis=-1, keepdims=True)
    hn = x * lax.rsqrt(ms + EPS) * g_ref[...]
    hb = hn.astype(BF16)
    n_slab = D_MODEL // V7X_LANES
    for c in range(n_slab):
        hn_scr[c] = hn[:, V7X_LANES * c:V7X_LANES * (c + 1)]

    def qkv_chunk(lhs, c0, j, cs, sn):
        acc = _dot(lhs, w_ref[:, c0 + COL_CHUNK * j:c0 + COL_CHUNK * (j + 1)])
        if COL_CHUNK * j < 2 * GROUP_WIDTH:
            acc = jnp.concatenate(
                [_rope(acc[:, :V7X_LANES], cs, sn), _rope(acc[:, V7X_LANES:], cs, sn)], axis=1)
        return acc.astype(BF16)

    cos = cos_ref[0]
    sin = sin_ref[0]
    for j in range(QKV_G // COL_CHUNK):
        o0_ref[0, :, COL_CHUNK * j:COL_CHUNK * (j + 1)] = qkv_chunk(hb, 0, j, cos, sin)
    oqm_ref[0] = _dot(hb, w_ref[:, A_QM0:A_QM0 + MEM_WIDTH]).astype(BF16)
    for j in range((GROUP_WIDTH + MEM_WIDTH) // COL_CHUNK):
        oz_ref[0, :, COL_CHUNK * j:COL_CHUNK * (j + 1)] = _dot(
            hb, w_ref[:, A_Z0 + COL_CHUNK * j:A_Z0 + COL_CHUNK * (j + 1)])

    for d, o_ref, c0 in ((DILATIONS[1], o1_ref, A_G1), (DILATIONS[2], o2_ref, A_G2)):
        n = tm // d
        for r in range(d):
            for c in range(n_slab):
                perm_scr[r * n:(r + 1) * n, V7X_LANES * c:V7X_LANES * (c + 1)] = (
                    hn_scr[c, pl.ds(r, n, stride=d), :].astype(BF16))
        cosp = jnp.concatenate([cos_ref[0, pl.ds(r, n, stride=d), :] for r in range(d)], axis=0)
        sinp = jnp.concatenate([sin_ref[0, pl.ds(r, n, stride=d), :] for r in range(d)], axis=0)
        lhs = perm_scr[...]
        for j in range(QKV_G // COL_CHUNK):
            res = qkv_chunk(lhs, c0, j, cosp, sinp)
            for r in range(d):
                o_ref[0, r, :, COL_CHUNK * j:COL_CHUNK * (j + 1)] = res[r * n:(r + 1) * n]


def _proj0(x, norm_g, cos_t, sin_t, w0):
    b, s, _ = x.shape
    tm = ROW_TILE
    row = lambda width: pl.BlockSpec((1, tm, width), lambda i, j: (i, j, 0))
    stream = lambda d: pl.BlockSpec((1, d, tm // d, QKV_G), lambda i, j: (i, 0, j, 0))
    d1, d2 = DILATIONS[1], DILATIONS[2]
    return pl.pallas_call(
        _proj0_kernel,
        grid=(b, s // tm),
        in_specs=[
            row(D_MODEL),
            pl.BlockSpec((1, D_MODEL), lambda i, j: (0, 0)),
            row(V7X_LANES),
            row(V7X_LANES),
            _resident((D_MODEL, IN_A_COLS), lambda i, j: (0, 0)),
        ],
        out_specs=[row(QKV_G), row(MEM_WIDTH), row(GROUP_WIDTH + MEM_WIDTH), stream(d1), stream(d2)],
        out_shape=[
            jax.ShapeDtypeStruct((b, s, QKV_G), BF16),
            jax.ShapeDtypeStruct((b, s, MEM_WIDTH), BF16),
            jax.ShapeDtypeStruct((b, s, GROUP_WIDTH + MEM_WIDTH), F32),
            jax.ShapeDtypeStruct((b, d1, s // d1, QKV_G), BF16),
            jax.ShapeDtypeStruct((b, d2, s // d2, QKV_G), BF16),
        ],
        scratch_shapes=[
            pltpu.VMEM((D_MODEL // V7X_LANES, tm, V7X_LANES), F32),
            pltpu.VMEM((tm, D_MODEL), BF16),
        ],
        compiler_params=_params("parallel", "parallel"),
        name="proj0",
    )(x, norm_g, cos_t, sin_t, w0)


def _attn_kernel(q0, k0, v0, q1, k1, v1, q2, k2, v2, o_ref, num_scr, m_scr, l_scr):
    s_len = o_ref.shape[1]
    lane = lax.broadcasted_iota(jnp.int32, (1, V7X_LANES), 1)
    head0 = (lane < 8) | ((lane >= 16) & (lane < 72))
    keep0 = jnp.where(head0, 1.0, 0.0).astype(BF16)
    keep1 = jnp.where(head0, 0.0, 1.0).astype(BF16)
    first_half = lax.broadcasted_iota(jnp.int32, (BLOCK, V7X_LANES), 1) < HEAD_DIM
    qi = lax.broadcasted_iota(jnp.int32, (2 * BLOCK, 2 * BLOCK), 0) & (BLOCK - 1)
    kj = lax.broadcasted_iota(jnp.int32, (2 * BLOCK, 2 * BLOCK), 1)
    valid_two = ((kj < BLOCK) & (kj >= qi)) | ((kj >= BLOCK) & (kj - BLOCK <= qi))
    valid_one = (lax.broadcasted_iota(jnp.int32, (2 * BLOCK, BLOCK), 1)
                 <= (lax.broadcasted_iota(jnp.int32, (2 * BLOCK, BLOCK), 0) & (BLOCK - 1)))

    def block(refs, g, q_row, k_row, n_keys, valid, dst):
        q_ref, k_ref, v_ref = refs
        q = q_ref[0, pl.ds(q_row, BLOCK), :]
        qs = jnp.concatenate([q * keep0, q * keep1], axis=0)
        k = k_ref[0, pl.ds(k_row, n_keys), :]
        s = jnp.where(valid, _dot_nt(qs, k), NEG_BIG)
        m = jnp.max(s, axis=-1, keepdims=True)
        p = jnp.exp(s - m)
        l = jnp.sum(p, axis=-1, keepdims=True)
        v = v_ref[0, pl.ds(k_row, n_keys), :]
        pv = _dot(p.astype(BF16), v)
        num_scr[g, dst, :] = jnp.where(first_half, pv[:BLOCK], pv[BLOCK:])
        m_scr[g, dst, :] = jnp.where(first_half, m[:BLOCK], m[BLOCK:])
        l_scr[g, dst, :] = jnp.where(first_half, l[:BLOCK], l[BLOCK:])

    def aligned(v):
        return pl.multiple_of(v, BLOCK)

    g0 = (q0, k0, v0)
    block(g0, 0, 0, 0, BLOCK, valid_one, pl.ds(0, BLOCK))

    def g0_body(n, carry):
        block(g0, 0, aligned(n * BLOCK), aligned((n - 1) * BLOCK), 2 * BLOCK, valid_two,
              pl.ds(aligned(n * BLOCK), BLOCK))
        return carry

    lax.fori_loop(1, s_len // BLOCK, g0_body, 0)

    d1 = DILATIONS[1]
    g1 = (q1, k1, v1)
    stream_len = s_len // d1
    for r in range(d1):
        base = r * stream_len
        block(g1, 1, base, base, BLOCK, valid_one, pl.ds(r, BLOCK, stride=d1))

        def g1_body(n, carry, base=base, r=r):
            block(g1, 1, aligned(base + n * BLOCK), aligned(base + (n - 1) * BLOCK), 2 * BLOCK,
                  valid_two, pl.ds(n * BLOCK * d1 + r, BLOCK, stride=d1))
            return carry

        lax.fori_loop(1, stream_len // BLOCK, g1_body, 0)

    d2 = DILATIONS[2]
    g2 = (q2, k2, v2)

    def g2_body(r, carry):
        row = aligned(r * BLOCK)
        block(g2, 2, row, row, BLOCK, valid_one, pl.ds(r, BLOCK, stride=d2))
        return carry

    lax.fori_loop(0, d2, g2_body, 0)

    rows_per = 2 * BLOCK

    def merge(c, carry):
        rows = pl.ds(pl.multiple_of(c * rows_per, rows_per), rows_per)
        ms = [m_scr[g, rows, :] for g in range(3)]
        top = jnp.maximum(ms[0], jnp.maximum(ms[1], ms[2]))
        ws = [jnp.exp(mg - top) for mg in ms]
        num = ws[0] * num_scr[0, rows, :] + ws[1] * num_scr[1, rows, :] + ws[2] * num_scr[2, rows, :]
        den = ws[0] * l_scr[0, rows, :] + ws[1] * l_scr[1, rows, :] + ws[2] * l_scr[2, rows, :]
        o_ref[0, rows, :] = num / den
        return carry

    lax.fori_loop(0, s_len // rows_per, merge, 0)


def _attn(qkv0, qkv1, qkv2):
    b, s, _ = qkv0.shape
    n_pair = GROUP_WIDTH // V7X_LANES

    def part(k):
        return pl.BlockSpec((1, s, V7X_LANES), lambda i, p, k=k: (i, 0, k * n_pair + p))

    specs = [part(0), part(1), part(2)] * 3
    stat = pltpu.VMEM((3, s, V7X_LANES), F32)
    return pl.pallas_call(
        _attn_kernel,
        grid=(b, n_pair),
        in_specs=specs,
        out_specs=pl.BlockSpec((1, s, V7X_LANES), lambda i, p: (i, 0, p)),
        out_shape=jax.ShapeDtypeStruct((b, s, GROUP_WIDTH), F32),
        scratch_shapes=[stat, stat, stat],
        compiler_params=_params("parallel", "parallel"),
        name="dilated_attn",
    )(qkv0, qkv0, qkv0, qkv1, qkv1, qkv1, qkv2, qkv2, qkv2)


def _mem_attn(qm, kv_ref):
    rows = qm.shape[0]
    lane = lax.broadcasted_iota(jnp.int32, (1, V7X_LANES), 1)
    keep0 = jnp.where(lane < HEAD_DIM, 1.0, 0.0).astype(BF16)
    keep1 = jnp.where(lane < HEAD_DIM, 0.0, 1.0).astype(BF16)
    first_half = lax.broadcasted_iota(jnp.int32, (rows, V7X_LANES), 1) < HEAD_DIM
    outs = []
    for p in range(MEM_WIDTH // V7X_LANES):
        q = qm[:, V7X_LANES * p:V7X_LANES * (p + 1)]
        qs = jnp.concatenate([q * keep0, q * keep1], axis=0)
        k = kv_ref[0, 0, :, V7X_LANES * p:V7X_LANES * (p + 1)]
        s = _dot_nt(qs, k)
        m = jnp.max(s, axis=-1, keepdims=True)
        e = jnp.exp(s - m)
        l = jnp.sum(e, axis=-1, keepdims=True)
        v = kv_ref[0, 0, :, MEM_WIDTH + V7X_LANES * p:MEM_WIDTH + V7X_LANES * (p + 1)]
        pv = _dot(e.astype(BF16), v) / l
        outs.append(jnp.where(first_half, pv[:rows], pv[rows:]))
    return jnp.concatenate(outs, axis=1)


def _tail_kernel(x_ref, mix_ref, qm0_ref, z0_ref, kv0_ref, wo0_ref, g1_ref, w1_ref, cw_ref,
                 kv1_ref, wo1_ref, gf_ref, out_ref, a_scr, y_scr, h1_scr, hb_scr):
    tm = x_ref.shape[1]
    halo = 8
    n_out = D_MODEL // COL_CHUNK

    mem0 = _mem_attn(qm0_ref[0], kv0_ref)
    z0 = z0_ref[0]
    y0 = jnp.concatenate(
        [mix_ref[0] * _silu(z0[:, :GROUP_WIDTH]), mem0 * _silu(z0[:, GROUP_WIDTH:])], axis=1).astype(BF16)
    for j in range(n_out):
        cs = slice(COL_CHUNK * j, COL_CHUNK * (j + 1))
        h1_scr[:, cs] = x_ref[0, :, cs] + _dot(y0, wo0_ref[:, cs])

    h1 = h1_scr[...]
    ms = jnp.mean(h1 * h1, axis=-1, keepdims=True)
    hb_scr[...] = (h1 * lax.rsqrt(ms + EPS) * g1_ref[...]).astype(BF16)
    hb = hb_scr[...]

    @pl.when(pl.program_id(1) == 0)
    def _():
        a_scr[0:halo, :] = jnp.zeros((halo, CONV_WIDTH), F32)

    def w1(c0, j):
        return w1_ref[:, c0 + COL_CHUNK * j:c0 + COL_CHUNK * (j + 1)]

    for j in range(CONV_WIDTH // COL_CHUNK):
        cs = slice(COL_CHUNK * j, COL_CHUNK * (j + 1))
        a_now = _dot(hb, w1(B_CG, j)) * _dot(hb, w1(B_U, j))
        a_scr[halo:halo + tm, cs] = a_now
        conv = (cw_ref[0:1, cs] * a_scr[halo - 2:halo - 2 + tm, cs]
                + cw_ref[1:2, cs] * a_scr[halo - 1:halo - 1 + tm, cs]
                + cw_ref[2:3, cs] * a_now)
        mix1 = _dot(hb, w1(B_BG, j)) * conv
        y_scr[:, cs] = (mix1 * _silu(_dot(hb, w1(B_Z, j)))).astype(BF16)
    a_scr[0:halo, :] = a_scr[tm:tm + halo, :]

    qm1 = _dot(hb, w1_ref[:, B_QM:B_QM + MEM_WIDTH]).astype(BF16)
    mem1 = _mem_attn(qm1, kv1_ref)
    z_mem = _dot(hb, w1_ref[:, B_Z + CONV_WIDTH:B_Z + BRANCH_B])
    y_scr[:, CONV_WIDTH:BRANCH_B] = (mem1 * _silu(z_mem)).astype(BF16)

    y1 = y_scr[...]
    ssq = jnp.zeros((tm, 1), F32)
    for j in range(n_out):
        cs = slice(COL_CHUNK * j, COL_CHUNK * (j + 1))
        h2 = h1_scr[:, cs] + _dot(y1, wo1_ref[:, cs])
        ssq = ssq + jnp.sum(h2 * h2, axis=-1, keepdims=True)
        out_ref[0, :, cs] = h2
    scale = lax.rsqrt(ssq * (1.0 / D_MODEL) + EPS)
    out_ref[0] = out_ref[0] * scale * gf_ref[...]


def _tail(x, mix, qm0, z0, kv, wo0, norm_g1, w1, conv_w, wo1, final_g):
    b, s, _ = x.shape
    tm = ROW_TILE
    row = lambda width: pl.BlockSpec((1, tm, width), lambda i, j: (i, j, 0))
    kv_spec = lambda layer: pl.BlockSpec((1, 1, N_MEM, 2 * MEM_WIDTH), lambda i, j: (layer, i, 0, 0))
    const = lambda shape: pl.BlockSpec(shape, lambda i, j: (0, 0))
    return pl.pallas_call(
        _tail_kernel,
        grid=(b, s // tm),
        in_specs=[
            row(D_MODEL), row(GROUP_WIDTH), row(MEM_WIDTH), row(GROUP_WIDTH + MEM_WIDTH),
            kv_spec(0),
            _resident(wo0.shape, lambda i, j: (0, 0)),
            const((1, D_MODEL)),
            _resident(w1.shape, lambda i, j: (0, 0)),
            const((3, CONV_WIDTH)),
            kv_spec(1),
            _resident(wo1.shape, lambda i, j: (0, 0)),
            const((1, D_MODEL)),
        ],
        out_specs=row(D_MODEL),
        out_shape=jax.ShapeDtypeStruct((b, s, D_MODEL), F32),
        scratch_shapes=[
            pltpu.VMEM((tm + 8, CONV_WIDTH), F32),
            pltpu.VMEM((tm, BRANCH_B), BF16),
            pltpu.VMEM((tm, D_MODEL), F32),
            pltpu.VMEM((tm, D_MODEL), BF16),
        ],
        compiler_params=_params("arbitrary", "arbitrary"),
        name="tail",
    )(x, mix, qm0, z0, kv, wo0, norm_g1, w1, conv_w, kv, wo1, final_g)


def kernel(x, mem, positions, norm_g, mem_norm_g, w_mem_kv, attn_w_in, attn_w_out, conv_w_in, conv_w,
           conv_w_out, final_g):
    b, s, _ = x.shape
    w0 = (attn_w_in[0][:, _A_COLS] * _A_SCALE).astype(BF16)
    b_scale = np.ones((conv_w_in.shape[-1],), np.float32)
    b_scale[B_QM:B_QM + MEM_WIDTH] = SCORE_SCALE
    w1 = (conv_w_in[0] * b_scale).astype(BF16)
    wo0 = attn_w_out[0].astype(BF16)
    wo1 = conv_w_out[0].astype(BF16)
    w_kv = w_mem_kv.astype(BF16)

    half = ROT_DIM // 2
    inv_freq = ROPE_THETA ** (-jnp.arange(half, dtype=F32) * (2.0 / ROT_DIM))
    lane_freq = (inv_freq[_ROPE_IDX] * _ROPE_SGN).reshape(1, V7X_LANES)
    pos_lanes = jnp.broadcast_to(positions[:, :, None], (b, s, V7X_LANES))

    cos_t, sin_t = _rope_tables(pos_lanes, lane_freq)
    kv = _mem_kv(mem, mem_norm_g, w_kv)
    qkv0, qm0, z0, qkv1, qkv2 = _proj0(x, norm_g[0:1], cos_t, sin_t, w0)
    qkv1 = qkv1.reshape(b, s, QKV_G)
    qkv2 = qkv2.reshape(b, s, QKV_G)
    mix = _attn(qkv0, qkv1, qkv2)
    return _tail(x, mix, qm0, z0, kv, wo0, norm_g[1:2], w1, conv_w[0], wo1, final_g.reshape(1, D_MODEL))
```

```python
import functools

import numpy as np
import jax
import jax.numpy as jnp
from jax import lax
from jax.experimental import pallas as pl
from jax.experimental.pallas import tpu as pltpu

F32 = jnp.float32
BF16 = jnp.bfloat16

D_MODEL = 1024
HEAD_DIM = 64
ROT_DIM = 16
ROPE_THETA = 500000.0
DILATIONS = (1, 4, 16)
BLOCK = 128
GROUP_WIDTH = 512
N_MEM = 256
MEM_WIDTH = 256
CONV_WIDTH = 1024
EPS = 1e-6
SCORE_SCALE = HEAD_DIM ** -0.5

V7X_LANES = 128
ROW_TILE = 512
COL_CHUNK = 256
VMEM_LIMIT_BYTES = 56 * 1024 * 1024
NEG_BIG = -1e30
G0_UNROLL = 5
G2_UNROLL = 8

QKV_G = 3 * GROUP_WIDTH
IN_A_COLS = 3 * QKV_G + MEM_WIDTH + (GROUP_WIDTH + MEM_WIDTH)
A_QM0 = QKV_G
A_Z0 = A_QM0 + MEM_WIDTH
A_G1 = A_Z0 + GROUP_WIDTH + MEM_WIDTH
A_G2 = A_G1 + QKV_G
B_BG, B_CG, B_U = 0, CONV_WIDTH, 2 * CONV_WIDTH
B_QM = 3 * CONV_WIDTH
B_Z = B_QM + MEM_WIDTH
BRANCH_B = CONV_WIDTH + MEM_WIDTH


def _pair_lane_source():
    src = np.zeros(V7X_LANES, np.int32)
    for j in range(V7X_LANES):
        if j < 8:
            e, dim = 0, j
        elif j < 16:
            e, dim = 1, j - 8
        elif j < 64:
            e, dim = 0, j
        elif j < 72:
            e, dim = 0, 8 + (j - 64)
        elif j < 80:
            e, dim = 1, 8 + (j - 72)
        else:
            e, dim = 1, j - 64
        src[j] = 64 * e + dim
    return src


def _attn_in_columns():
    src = _pair_lane_source()
    gw3 = 3 * GROUP_WIDTH
    cols, scale = [], []

    def group(g):
        for part, off in (("q", 0), ("k", gw3), ("v", 2 * gw3)):
            for p in range(4):
                base = off + GROUP_WIDTH * g + V7X_LANES * p
                lanes = src if part != "v" else np.arange(V7X_LANES)
                cols.extend((base + lanes).tolist())
                scale.extend([SCORE_SCALE if part == "q" else 1.0] * V7X_LANES)

    group(0)
    cols.extend(range(3 * gw3, 3 * gw3 + MEM_WIDTH))
    scale.extend([SCORE_SCALE] * MEM_WIDTH)
    cols.extend(range(3 * gw3 + MEM_WIDTH, IN_A_COLS))
    scale.extend([1.0] * (GROUP_WIDTH + MEM_WIDTH))
    group(1)
    group(2)
    return np.asarray(cols, np.int32), np.asarray(scale, np.float32)


_A_COLS, _A_SCALE = _attn_in_columns()


def _rope_lane_tables():
    idx = np.zeros(V7X_LANES, np.int32)
    sgn = np.zeros(V7X_LANES, np.float32)
    for j in range(V7X_LANES):
        if j < 16:
            idx[j], sgn[j] = j % 8, -1.0
        elif 64 <= j < 80:
            idx[j], sgn[j] = j % 8, 1.0
    return idx, sgn


_ROPE_IDX, _ROPE_SGN = _rope_lane_tables()


def _params(*sem):
    return pltpu.CompilerParams(dimension_semantics=sem, vmem_limit_bytes=VMEM_LIMIT_BYTES)


def _resident(shape, index_map):
    return pl.BlockSpec(shape, index_map, pipeline_mode=pl.Buffered(1))


def _silu(z):
    return z * jax.nn.sigmoid(z)


def _dot(a, b):
    return jnp.dot(a, b, preferred_element_type=F32)


def _dot_nt(a, b):
    return lax.dot_general(a, b, (((1,), (1,)), ((), ())), preferred_element_type=F32)


def _rope_kernel(pos_ref, freq_ref, cos_ref, sin_ref):
    ang = pos_ref[0].astype(F32) * freq_ref[...]
    cos_ref[0] = jnp.cos(ang)
    sin_ref[0] = jnp.sin(ang)


def _rope_tables(pos_lanes, lane_freq):
    b, s, _ = pos_lanes.shape
    spec = pl.BlockSpec((1, ROW_TILE, V7X_LANES), lambda i, j: (i, j, 0))
    return pl.pallas_call(
        _rope_kernel,
        grid=(b, s // ROW_TILE),
        in_specs=[spec, pl.BlockSpec((1, V7X_LANES), lambda i, j: (0, 0))],
        out_specs=[spec, spec],
        out_shape=[jax.ShapeDtypeStruct((b, s, V7X_LANES), F32)] * 2,
        compiler_params=_params("parallel", "parallel"),
        name="rope_tables",
    )(pos_lanes, lane_freq)


def _memkv_kernel(mem_ref, g_ref, w_ref, kv_ref):
    m = mem_ref[0]
    ms = jnp.mean(m * m, axis=-1, keepdims=True)
    mn = (m * lax.rsqrt(ms + EPS) * g_ref[0]).astype(BF16)
    kv_ref[0, 0] = _dot(mn, w_ref[0]).astype(BF16)


def _mem_kv(mem, mem_norm_g, w_kv_bf):
    b = mem.shape[0]
    depth = w_kv_bf.shape[0]
    return pl.pallas_call(
        _memkv_kernel,
        grid=(depth, b),
        in_specs=[
            pl.BlockSpec((1, N_MEM, D_MODEL), lambda l, i: (i, 0, 0)),
            pl.BlockSpec((1, 1, D_MODEL), lambda l, i: (l, 0, 0)),
            pl.BlockSpec((1, D_MODEL, 2 * MEM_WIDTH), lambda l, i: (l, 0, 0)),
        ],
        out_specs=pl.BlockSpec((1, 1, N_MEM, 2 * MEM_WIDTH), lambda l, i: (l, i, 0, 0)),
        out_shape=jax.ShapeDtypeStruct((depth, b, N_MEM, 2 * MEM_WIDTH), BF16),
        compiler_params=_params("parallel", "parallel"),
        name="mem_kv",
    )(mem, mem_norm_g.reshape(depth, 1, D_MODEL), w_kv_bf)


def _rope(a, cs, sn):
    return a * cs + pltpu.roll(a, 64, 1) * sn


def _proj0_kernel(x_ref, g_ref, cos_ref, sin_ref, w_ref,
                  o0_ref, oqm_ref, oz_ref, o1_ref, o2_ref, hn_scr, perm_scr):
    tm = x_ref.shape[1]
    x = x_ref[0]
    ms = jnp.mean(x * x, axis=-1, keepdims=True)
    hn = x * lax.rsqrt(ms + EPS) * g_ref[...]
    hb = hn.astype(BF16)
    n_slab = D_MODEL // V7X_LANES
    for c in range(n_slab):
        hn_scr[c] = hn[:, V7X_LANES * c:V7X_LANES * (c + 1)]

    def qkv_chunk(lhs, c0, j, cs, sn):
        acc = _dot(lhs, w_ref[:, c0 + COL_CHUNK * j:c0 + COL_CHUNK * (j + 1)])
        if COL_CHUNK * j < 2 * GROUP_WIDTH:
            acc = jnp.concatenate(
                [_rope(acc[:, :V7X_LANES], cs, sn), _rope(acc[:, V7X_LANES:], cs, sn)], axis=1)
        return acc.astype(BF16)

    cos = cos_ref[0]
    sin = sin_ref[0]
    for j in range(QKV_G // COL_CHUNK):
        o0_ref[0, :, COL_CHUNK * j:COL_CHUNK * (j + 1)] = qkv_chunk(hb, 0, j, cos, sin)
    oqm_ref[0] = _dot(hb, w_ref[:, A_QM0:A_QM0 + MEM_WIDTH]).astype(BF16)
    for j in range((GROUP_WIDTH + MEM_WIDTH) // COL_CHUNK):
        oz_ref[0, :, COL_CHUNK * j:COL_CHUNK * (j + 1)] = _dot(
            hb, w_ref[:, A_Z0 + COL_CHUNK * j:A_Z0 + COL_CHUNK * (j + 1)])

    for d, o_ref, c0 in ((DILATIONS[1], o1_ref, A_G1), (DILATIONS[2], o2_ref, A_G2)):
        n = tm // d
        for r in range(d):
            for c in range(n_slab):
                perm_scr[r * n:(r + 1) * n, V7X_LANES * c:V7X_LANES * (c + 1)] = (
                    hn_scr[c, pl.ds(r, n, stride=d), :].astype(BF16))
        cosp = jnp.concatenate([cos_ref[0, pl.ds(r, n, stride=d), :] for r in range(d)], axis=0)
        sinp = jnp.concatenate([sin_ref[0, pl.ds(r, n, stride=d), :] for r in range(d)], axis=0)
        lhs = perm_scr[...]
        for j in range(QKV_G // COL_CHUNK):
            res = qkv_chunk(lhs, c0, j, cosp, sinp)
            for r in range(d):
                o_ref[0, r, :, COL_CHUNK * j:COL_CHUNK * (j + 1)] = res[r * n:(r + 1) * n]


def _proj0(x, norm_g, cos_t, sin_t, w0):
    b, s, _ = x.shape
    tm = ROW_TILE
    row = lambda width: pl.BlockSpec((1, tm, width), lambda i, j: (i, j, 0))
    stream = lambda d: pl.BlockSpec((1, d, tm // d, QKV_G), lambda i, j: (i, 0, j, 0))
    d1, d2 = DILATIONS[1], DILATIONS[2]
    return pl.pallas_call(
        _proj0_kernel,
        grid=(b, s // tm),
        in_specs=[
            row(D_MODEL),
            pl.BlockSpec((1, D_MODEL), lambda i, j: (0, 0)),
            row(V7X_LANES),
            row(V7X_LANES),
            _resident((D_MODEL, IN_A_COLS), lambda i, j: (0, 0)),
        ],
        out_specs=[row(QKV_G), row(MEM_WIDTH), row(GROUP_WIDTH + MEM_WIDTH), stream(d1), stream(d2)],
        out_shape=[
            jax.ShapeDtypeStruct((b, s, QKV_G), BF16),
            jax.ShapeDtypeStruct((b, s, MEM_WIDTH), BF16),
            jax.ShapeDtypeStruct((b, s, GROUP_WIDTH + MEM_WIDTH), F32),
            jax.ShapeDtypeStruct((b, d1, s // d1, QKV_G), BF16),
            jax.ShapeDtypeStruct((b, d2, s // d2, QKV_G), BF16),
        ],
        scratch_shapes=[
            pltpu.VMEM((D_MODEL // V7X_LANES, tm, V7X_LANES), F32),
            pltpu.VMEM((tm, D_MODEL), BF16),
        ],
        compiler_params=_params("parallel", "parallel"),
        name="proj0",
    )(x, norm_g, cos_t, sin_t, w0)


def _attn_kernel(q0, k0, v0, q1, k1, v1, q2, k2, v2, o_ref,
                 num_scr, m_scr, l_scr, ve_scr, bias2_scr, bias1_scr):
    s_len = o_ref.shape[1]
    lane = lax.broadcasted_iota(jnp.int32, (1, V7X_LANES), 1)
    head0 = (lane < 8) | ((lane >= 16) & (lane < 72))
    keep0 = jnp.where(head0, 1.0, 0.0).astype(BF16)
    keep1 = jnp.where(head0, 0.0, 1.0).astype(BF16)
    first_half = lax.broadcasted_iota(jnp.int32, (BLOCK, V7X_LANES), 1) < HEAD_DIM

    qi = lax.broadcasted_iota(jnp.int32, (2 * BLOCK, 2 * BLOCK), 0) & (BLOCK - 1)
    kj = lax.broadcasted_iota(jnp.int32, (2 * BLOCK, 2 * BLOCK), 1)
    valid_two = ((kj < BLOCK) & (kj >= qi)) | ((kj >= BLOCK) & (kj - BLOCK <= qi))
    bias2_scr[...] = jnp.where(valid_two, 0.0, NEG_BIG)
    valid_one = (lax.broadcasted_iota(jnp.int32, (2 * BLOCK, BLOCK), 1)
                 <= (lax.broadcasted_iota(jnp.int32, (2 * BLOCK, BLOCK), 0) & (BLOCK - 1)))
    bias1_scr[...] = jnp.where(valid_one, 0.0, NEG_BIG)

    half0 = jnp.where(lane < HEAD_DIM, 1.0, 0.0).astype(BF16)
    half1 = jnp.where(lane < HEAD_DIM, 0.0, 1.0).astype(BF16)
    for g, v_ref in enumerate((v0, v1, v2)):
        v = v_ref[0]
        for h, half in enumerate((half0, half1)):
            ve_scr[g, h, :, :V7X_LANES] = v * half
            ve_scr[g, h, :, V7X_LANES:] = jnp.broadcast_to(half, (s_len, V7X_LANES))

    def block(q_ref, k_ref, g, q_row, k_row, n_keys, bias_ref, dst):
        q = q_ref[0, pl.ds(q_row, BLOCK), :]
        qs = jnp.concatenate([q * keep0, q * keep1], axis=0)
        k = k_ref[0, pl.ds(k_row, n_keys), :]
        s = _dot_nt(qs, k) + bias_ref[...]
        m = jnp.max(s, axis=-1, keepdims=True)
        p = jnp.exp(s - m).astype(BF16)
        p_cat = jnp.concatenate([p[:BLOCK], p[BLOCK:]], axis=1)
        keys = pl.ds(k_row, n_keys)
        ve = jnp.concatenate([ve_scr[g, 0, keys, :], ve_scr[g, 1, keys, :]], axis=0)
        pv = _dot(p_cat, ve)
        num_scr[g, dst, :] = pv[:, :V7X_LANES]
        l_scr[g, dst, :] = pv[:, V7X_LANES:]
        m_scr[g, dst, :] = jnp.where(first_half, m[:BLOCK], m[BLOCK:])

    def aligned(v):
        return pl.multiple_of(v, BLOCK)

    block(q0, k0, 0, 0, 0, BLOCK, bias1_scr, pl.ds(0, BLOCK))

    def g0_body(n, carry):
        block(q0, k0, 0, aligned(n * BLOCK), aligned((n - 1) * BLOCK), 2 * BLOCK, bias2_scr,
              pl.ds(aligned(n * BLOCK), BLOCK))
        return carry

    lax.fori_loop(1, s_len // BLOCK, g0_body, 0, unroll=G0_UNROLL)

    d1 = DILATIONS[1]
    stream_len = s_len // d1
    for r in range(d1):
        base = r * stream_len
        block(q1, k1, 1, base, base, BLOCK, bias1_scr, pl.ds(r, BLOCK, stride=d1))
        for n in range(1, stream_len // BLOCK):
            block(q1, k1, 1, base + n * BLOCK, base + (n - 1) * BLOCK, 2 * BLOCK, bias2_scr,
                  pl.ds(n * BLOCK * d1 + r, BLOCK, stride=d1))

    d2 = DILATIONS[2]

    def g2_body(r, carry):
        row = aligned(r * BLOCK)
        block(q2, k2, 2, row, row, BLOCK, bias1_scr, pl.ds(r, BLOCK, stride=d2))
        return carry

    lax.fori_loop(0, d2, g2_body, 0, unroll=G2_UNROLL)

    rows_per = 2 * BLOCK

    def merge(c, carry):
        rows = pl.ds(pl.multiple_of(c * rows_per, rows_per), rows_per)
        ms = [m_scr[g, rows, :] for g in range(3)]
        top = jnp.maximum(ms[0], jnp.maximum(ms[1], ms[2]))
        ws = [jnp.exp(mg - top) for mg in ms]
        num = ws[0] * num_scr[0, rows, :] + ws[1] * num_scr[1, rows, :] + ws[2] * num_scr[2, rows, :]
        den = ws[0] * l_scr[0, rows, :] + ws[1] * l_scr[1, rows, :] + ws[2] * l_scr[2, rows, :]
        o_ref[0, rows, :] = num / den
        return carry

    lax.fori_loop(0, s_len // rows_per, merge, 0)


def _attn(qkv0, qkv1, qkv2):
    b, s, _ = qkv0.shape
    n_pair = GROUP_WIDTH // V7X_LANES

    def part(k):
        return pl.BlockSpec((1, s, V7X_LANES), lambda i, p, k=k: (i, 0, k * n_pair + p))

    specs = [part(0), part(1), part(2)] * 3
    stat = pltpu.VMEM((3, s, V7X_LANES), F32)
    scratch = [
        stat, stat, stat,
        pltpu.VMEM((3, 2, s, 2 * V7X_LANES), BF16),
        pltpu.VMEM((2 * BLOCK, 2 * BLOCK), F32),
        pltpu.VMEM((2 * BLOCK, BLOCK), F32),
    ]
    return pl.pallas_call(
        _attn_kernel,
        grid=(b, n_pair),
        in_specs=specs,
        out_specs=pl.BlockSpec((1, s, V7X_LANES), lambda i, p: (i, 0, p)),
        out_shape=jax.ShapeDtypeStruct((b, s, GROUP_WIDTH), F32),
        scratch_shapes=scratch,
        compiler_params=_params("parallel", "parallel"),
        name="dilated_attn",
    )(qkv0, qkv0, qkv0, qkv1, qkv1, qkv1, qkv2, qkv2, qkv2)


def _mem_attn(qm, kv_ref):
    rows = qm.shape[0]
    lane = lax.broadcasted_iota(jnp.int32, (1, V7X_LANES), 1)
    keep0 = jnp.where(lane < HEAD_DIM, 1.0, 0.0).astype(BF16)
    keep1 = jnp.where(lane < HEAD_DIM, 0.0, 1.0).astype(BF16)
    first_half = lax.broadcasted_iota(jnp.int32, (rows, V7X_LANES), 1) < HEAD_DIM
    outs = []
    for p in range(MEM_WIDTH // V7X_LANES):
        q = qm[:, V7X_LANES * p:V7X_LANES * (p + 1)]
        qs = jnp.concatenate([q * keep0, q * keep1], axis=0)
        k = kv_ref[0, 0, :, V7X_LANES * p:V7X_LANES * (p + 1)]
        s = _dot_nt(qs, k)
        m = jnp.max(s, axis=-1, keepdims=True)
        e = jnp.exp(s - m)
        l = jnp.sum(e, axis=-1, keepdims=True)
        v = kv_ref[0, 0, :, MEM_WIDTH + V7X_LANES * p:MEM_WIDTH + V7X_LANES * (p + 1)]
        pv = _dot(e.astype(BF16), v) / l
        outs.append(jnp.where(first_half, pv[:rows], pv[rows:]))
    return jnp.concatenate(outs, axis=1)


def _tail_kernel(x_ref, mix_ref, qm0_ref, z0_ref, kv0_ref, wo0_ref, g1_ref, w1_ref, cw_ref,
                 kv1_ref, wo1_ref, gf_ref, out_ref, a_scr, y_scr, h1_scr, hb_scr):
    tm = x_ref.shape[1]
    halo = 8
    n_out = D_MODEL // COL_CHUNK

    mem0 = _mem_attn(qm0_ref[0], kv0_ref)
    z0 = z0_ref[0]
    y0 = jnp.concatenate(
        [mix_ref[0] * _silu(z0[:, :GROUP_WIDTH]), mem0 * _silu(z0[:, GROUP_WIDTH:])], axis=1).astype(BF16)
    for j in range(n_out):
        cs = slice(COL_CHUNK * j, COL_CHUNK * (j + 1))
        h1_scr[:, cs] = x_ref[0, :, cs] + _dot(y0, wo0_ref[:, cs])

    h1 = h1_scr[...]
    ms = jnp.mean(h1 * h1, axis=-1, keepdims=True)
    hb_scr[...] = (h1 * lax.rsqrt(ms + EPS) * g1_ref[...]).astype(BF16)
    hb = hb_scr[...]

    @pl.when(pl.program_id(1) == 0)
    def _():
        a_scr[0:halo, :] = jnp.zeros((halo, CONV_WIDTH), F32)

    def w1(c0, j):
        return w1_ref[:, c0 + COL_CHUNK * j:c0 + COL_CHUNK * (j + 1)]

    for j in range(CONV_WIDTH // COL_CHUNK):
        cs = slice(COL_CHUNK * j, COL_CHUNK * (j + 1))
        a_now = _dot(hb, w1(B_CG, j)) * _dot(hb, w1(B_U, j))
        a_scr[halo:halo + tm, cs] = a_now
        conv = (cw_ref[0:1, cs] * a_scr[halo - 2:halo - 2 + tm, cs]
                + cw_ref[1:2, cs] * a_scr[halo - 1:halo - 1 + tm, cs]
                + cw_ref[2:3, cs] * a_now)
        mix1 = _dot(hb, w1(B_BG, j)) * conv
        y_scr[:, cs] = (mix1 * _silu(_dot(hb, w1(B_Z, j)))).astype(BF16)
    a_scr[0:halo, :] = a_scr[tm:tm + halo, :]

    qm1 = _dot(hb, w1_ref[:, B_QM:B_QM + MEM_WIDTH]).astype(BF16)
    mem1 = _mem_attn(qm1, kv1_ref)
    z_mem = _dot(hb, w1_ref[:, B_Z + CONV_WIDTH:B_Z + BRANCH_B])
    y_scr[:, CONV_WIDTH:BRANCH_B] = (mem1 * _silu(z_mem)).astype(BF16)

    y1 = y_scr[...]
    ssq = jnp.zeros((tm, 1), F32)
    for j in range(n_out):
        cs = slice(COL_CHUNK * j, COL_CHUNK * (j + 1))
        h2 = h1_scr[:, cs] + _dot(y1, wo1_ref[:, cs])
        ssq = ssq + jnp.sum(h2 * h2, axis=-1, keepdims=True)
        out_ref[0, :, cs] = h2
    scale = lax.rsqrt(ssq * (1.0 / D_MODEL) + EPS)
    out_ref[0] = out_ref[0] * scale * gf_ref[...]


def _tail(x, mix, qm0, z0, kv, wo0, norm_g1, w1, conv_w, wo1, final_g):
    b, s, _ = x.shape
    tm = ROW_TILE
    row = lambda width: pl.BlockSpec((1, tm, width), lambda i, j: (i, j, 0))
    kv_spec = lambda layer: pl.BlockSpec((1, 1, N_MEM, 2 * MEM_WIDTH), lambda i, j: (layer, i, 0, 0))
    const = lambda shape: pl.BlockSpec(shape, lambda i, j: (0, 0))
    return pl.pallas_call(
        _tail_kernel,
        grid=(b, s // tm),
        in_specs=[
            row(D_MODEL), row(GROUP_WIDTH), row(MEM_WIDTH), row(GROUP_WIDTH + MEM_WIDTH),
            kv_spec(0),
            _resident(wo0.shape, lambda i, j: (0, 0)),
            const((1, D_MODEL)),
            _resident(w1.shape, lambda i, j: (0, 0)),
            const((3, CONV_WIDTH)),
            kv_spec(1),
            _resident(wo1.shape, lambda i, j: (0, 0)),
            const((1, D_MODEL)),
        ],
        out_specs=row(D_MODEL),
        out_shape=jax.ShapeDtypeStruct((b, s, D_MODEL), F32),
        scratch_shapes=[
            pltpu.VMEM((tm + 8, CONV_WIDTH), F32),
            pltpu.VMEM((tm, BRANCH_B), BF16),
            pltpu.VMEM((tm, D_MODEL), F32),
            pltpu.VMEM((tm, D_MODEL), BF16),
        ],
        compiler_params=_params("arbitrary", "arbitrary"),
        name="tail",
    )(x, mix, qm0, z0, kv, wo0, norm_g1, w1, conv_w, kv, wo1, final_g)


def kernel(x, mem, positions, norm_g, mem_norm_g, w_mem_kv, attn_w_in, attn_w_out, conv_w_in, conv_w,
           conv_w_out, final_g):
    b, s, _ = x.shape
    w0 = (attn_w_in[0][:, _A_COLS] * _A_SCALE).astype(BF16)
    b_scale = np.ones((conv_w_in.shape[-1],), np.float32)
    b_scale[B_QM:B_QM + MEM_WIDTH] = SCORE_SCALE
    w1 = (conv_w_in[0] * b_scale).astype(BF16)
    wo0 = attn_w_out[0].astype(BF16)
    wo1 = conv_w_out[0].astype(BF16)
    w_kv = w_mem_kv.astype(BF16)

    half = ROT_DIM // 2
    inv_freq = ROPE_THETA ** (-jnp.arange(half, dtype=F32) * (2.0 / ROT_DIM))
    lane_freq = (inv_freq[_ROPE_IDX] * _ROPE_SGN).reshape(1, V7X_LANES)
    pos_lanes = jnp.broadcast_to(positions[:, :, None], (b, s, V7X_LANES))

    cos_t, sin_t = _rope_tables(pos_lanes, lane_freq)
    kv = _mem_kv(mem, mem_norm_g, w_kv)
    qkv0, qm0, z0, qkv1, qkv2 = _proj0(x, norm_g[0:1], cos_t, sin_t, w0)
    qkv1 = qkv1.reshape(b, s, QKV_G)
    qkv2 = qkv2.reshape(b, s, QKV_G)
    mix = _attn(qkv0, qkv1, qkv2)
    return _tail(x, mix, qm0, z0, kv, wo0, norm_g[1:2], w1, conv_w[0], wo1, final_g.reshape(1, D_MODEL))
```

```python
import numpy as np
import jax
import jax.numpy as jnp
from jax import lax
from jax.experimental import pallas as pl
from jax.experimental.pallas import tpu as pltpu

F32 = jnp.float32
BF16 = jnp.bfloat16

D_MODEL = 1024
HEAD_DIM = 64
ROT_DIM = 16
ROT_HALF = ROT_DIM // 2
ROPE_THETA = 500000.0
DILATIONS = (1, 4, 16)
BLOCK = 128
GROUP_WIDTH = 512
N_GROUPS = 3
N_MEM = 256
MEM_WIDTH = 256
CONV_WIDTH = 1024
EPS = 1e-6
SCORE_SCALE = HEAD_DIM ** -0.5

V7X_LANES = 128
V7X_SUBLANES = 8
ROW_TILE = 512
COL_CHUNK = 256
VMEM_LIMIT_BYTES = 56 * 1024 * 1024
NEG_BIG = -1e30
G0_UNROLL = 5
G2_UNROLL = 8

QKV_G = 3 * GROUP_WIDTH
A_PART = N_GROUPS * GROUP_WIDTH
A_QM = 3 * A_PART
A_Z = A_QM + MEM_WIDTH
BRANCH_A = GROUP_WIDTH + MEM_WIDTH
IN_A_COLS = A_Z + BRANCH_A
B_BG, B_CG, B_U = 0, CONV_WIDTH, 2 * CONV_WIDTH
B_QM = 3 * CONV_WIDTH
B_Z = B_QM + MEM_WIDTH
BRANCH_B = CONV_WIDTH + MEM_WIDTH
IN_B_COLS = B_Z + BRANCH_B


def _column_scale(n_cols, scaled):
    scale = np.ones((n_cols,), np.float32)
    for lo, hi in scaled:
        scale[lo:hi] = SCORE_SCALE
    return scale


_A_SCALE = _column_scale(IN_A_COLS, [(0, A_PART), (A_QM, A_QM + MEM_WIDTH)])
_B_SCALE = _column_scale(IN_B_COLS, [(B_QM, B_QM + MEM_WIDTH)])


def _params(*sem):
    return pltpu.CompilerParams(dimension_semantics=sem, vmem_limit_bytes=VMEM_LIMIT_BYTES)


def _resident(shape, index_map):
    return pl.BlockSpec(shape, index_map, pipeline_mode=pl.Buffered(1))


def _silu(z):
    return z * jax.nn.sigmoid(z)


def _dot(a, b):
    return jnp.dot(a, b, preferred_element_type=F32)


def _dot_nt(a, b):
    return lax.dot_general(a, b, (((1,), (1,)), ((), ())), preferred_element_type=F32)


def _half_masks():
    lane = lax.broadcasted_iota(jnp.int32, (1, V7X_LANES), 1)
    return (jnp.where(lane < HEAD_DIM, 1.0, 0.0).astype(BF16),
            jnp.where(lane < HEAD_DIM, 0.0, 1.0).astype(BF16))


def _memkv_kernel(mem_ref, g_ref, w_ref, kv_ref):
    m = mem_ref[0]
    ms = jnp.mean(m * m, axis=-1, keepdims=True)
    mn = (m * lax.rsqrt(ms + EPS) * g_ref[0]).astype(BF16)
    kv_ref[0, 0] = _dot(mn, w_ref[0]).astype(BF16)


def _mem_kv(mem, mem_norm_g, w_kv_bf):
    b = mem.shape[0]
    depth = w_kv_bf.shape[0]
    return pl.pallas_call(
        _memkv_kernel,
        grid=(depth, b),
        in_specs=[
            pl.BlockSpec((1, N_MEM, D_MODEL), lambda l, i: (i, 0, 0)),
            pl.BlockSpec((1, 1, D_MODEL), lambda l, i: (l, 0, 0)),
            pl.BlockSpec((1, D_MODEL, 2 * MEM_WIDTH), lambda l, i: (l, 0, 0)),
        ],
        out_specs=pl.BlockSpec((1, 1, N_MEM, 2 * MEM_WIDTH), lambda l, i: (l, i, 0, 0)),
        out_shape=jax.ShapeDtypeStruct((depth, b, N_MEM, 2 * MEM_WIDTH), BF16),
        compiler_params=_params("parallel", "parallel"),
        name="mem_kv",
    )(mem, mem_norm_g.reshape(depth, 1, D_MODEL), w_kv_bf)


def _rope_tables(pos_ref, freq_ref, tbl_scr):
    rows = tbl_scr.shape[1]
    freq = freq_ref[...]
    one = jnp.ones((V7X_SUBLANES, V7X_LANES), F32)
    zero = jnp.zeros((V7X_SUBLANES, V7X_LANES), F32)
    groups = V7X_LANES // V7X_SUBLANES
    per_head = HEAD_DIM // V7X_SUBLANES

    def lane_rows(first, second, other):
        pieces = []
        for i in range(groups):
            pieces.append(first if i % per_head == 0 else second if i % per_head == 1 else other)
        return jnp.concatenate(pieces, axis=0).T

    for c in range(rows // V7X_LANES):
        sl = slice(V7X_LANES * c, V7X_LANES * (c + 1))
        ang = pos_ref[0, :, sl].astype(F32) * freq
        cs = jnp.cos(ang)
        sn = jnp.sin(ang)
        tbl_scr[0, sl, :] = lane_rows(cs, cs, one)
        tbl_scr[1, sl, :] = lane_rows(-sn, zero, zero)
        tbl_scr[2, sl, :] = lane_rows(zero, sn, zero)


def _rope(a, cs, sn_next, sn_prev):
    return (a * cs + pltpu.roll(a, V7X_LANES - ROT_HALF, 1) * sn_next
            + pltpu.roll(a, ROT_HALF, 1) * sn_prev)


def _proj0_kernel(x_ref, g_ref, pos_ref, freq_ref, w_ref,
                  o0_ref, oqm_ref, oz_ref, o1_ref, o2_ref, hn_scr, perm_scr, tbl_scr):
    tm = x_ref.shape[1]
    x = x_ref[0]
    ms = jnp.mean(x * x, axis=-1, keepdims=True)
    hn = x * lax.rsqrt(ms + EPS) * g_ref[...]
    hb = hn.astype(BF16)
    n_slab = D_MODEL // V7X_LANES
    for c in range(n_slab):
        hn_scr[c] = hn[:, V7X_LANES * c:V7X_LANES * (c + 1)]
    _rope_tables(pos_ref, freq_ref, tbl_scr)

    def qkv_chunk(lhs, g, j, tables):
        part, half = divmod(j, GROUP_WIDTH // COL_CHUNK)
        c0 = A_PART * part + GROUP_WIDTH * g + COL_CHUNK * half
        acc = _dot(lhs, w_ref[:, c0:c0 + COL_CHUNK])
        if part < 2:
            acc = jnp.concatenate(
                [_rope(acc[:, :V7X_LANES], *tables), _rope(acc[:, V7X_LANES:], *tables)], axis=1)
        return acc.astype(BF16)

    tables = tuple(tbl_scr[t] for t in range(3))
    for j in range(QKV_G // COL_CHUNK):
        o0_ref[0, :, COL_CHUNK * j:COL_CHUNK * (j + 1)] = qkv_chunk(hb, 0, j, tables)
    oqm_ref[0] = _dot(hb, w_ref[:, A_QM:A_QM + MEM_WIDTH]).astype(BF16)
    for j in range(BRANCH_A // COL_CHUNK):
        oz_ref[0, :, COL_CHUNK * j:COL_CHUNK * (j + 1)] = _dot(
            hb, w_ref[:, A_Z + COL_CHUNK * j:A_Z + COL_CHUNK * (j + 1)])

    for g, o_ref in ((1, o1_ref), (2, o2_ref)):
        d = DILATIONS[g]
        n = tm // d
        for r in range(d):
            for c in range(n_slab):
                perm_scr[r * n:(r + 1) * n, V7X_LANES * c:V7X_LANES * (c + 1)] = (
                    hn_scr[c, pl.ds(r, n, stride=d), :].astype(BF16))
        tables = tuple(
            jnp.concatenate([tbl_scr[t, pl.ds(r, n, stride=d), :] for r in range(d)], axis=0)
            for t in range(3))
        lhs = perm_scr[...]
        for j in range(QKV_G // COL_CHUNK):
            res = qkv_chunk(lhs, g, j, tables)
            for r in range(d):
                o_ref[0, r, :, COL_CHUNK * j:COL_CHUNK * (j + 1)] = res[r * n:(r + 1) * n]


def _proj0(x, norm_g, pos_rows, freq_rows, w0):
    b, s, _ = x.shape
    tm = ROW_TILE
    row = lambda width: pl.BlockSpec((1, tm, width), lambda i, j: (i, j, 0))
    stream = lambda d: pl.BlockSpec((1, d, tm // d, QKV_G), lambda i, j: (i, 0, j, 0))
    d1, d2 = DILATIONS[1], DILATIONS[2]
    return pl.pallas_call(
        _proj0_kernel,
        grid=(b, s // tm),
        in_specs=[
            row(D_MODEL),
            pl.BlockSpec((1, D_MODEL), lambda i, j: (0, 0)),
            pl.BlockSpec((1, 1, tm), lambda i, j: (i, 0, j)),
            pl.BlockSpec((V7X_SUBLANES, V7X_LANES), lambda i, j: (0, 0)),
            _resident((D_MODEL, IN_A_COLS), lambda i, j: (0, 0)),
        ],
        out_specs=[row(QKV_G), row(MEM_WIDTH), row(BRANCH_A), stream(d1), stream(d2)],
        out_shape=[
            jax.ShapeDtypeStruct((b, s, QKV_G), BF16),
            jax.ShapeDtypeStruct((b, s, MEM_WIDTH), BF16),
            jax.ShapeDtypeStruct((b, s, BRANCH_A), F32),
            jax.ShapeDtypeStruct((b, d1, s // d1, QKV_G), BF16),
            jax.ShapeDtypeStruct((b, d2, s // d2, QKV_G), BF16),
        ],
        scratch_shapes=[
            pltpu.VMEM((D_MODEL // V7X_LANES, tm, V7X_LANES), F32),
            pltpu.VMEM((tm, D_MODEL), BF16),
            pltpu.VMEM((3, tm, V7X_LANES), F32),
        ],
        compiler_params=_params("parallel", "parallel"),
        name="proj0",
    )(x, norm_g, pos_rows, freq_rows, w0)


def _attn_kernel(q0, k0, v0, q1, k1, v1, q2, k2, v2, o_ref,
                 num_scr, m_scr, l_scr, ve_scr, bias2_scr, bias1_scr):
    s_len = o_ref.shape[1]
    half0, half1 = _half_masks()
    first_half = lax.broadcasted_iota(jnp.int32, (BLOCK, V7X_LANES), 1) < HEAD_DIM

    @pl.when((pl.program_id(0) == 0) & (pl.program_id(1) == 0))
    def _():
        qi = lax.broadcasted_iota(jnp.int32, (2 * BLOCK, 2 * BLOCK), 0) & (BLOCK - 1)
        kj = lax.broadcasted_iota(jnp.int32, (2 * BLOCK, 2 * BLOCK), 1)
        valid_two = ((kj < BLOCK) & (kj >= qi)) | ((kj >= BLOCK) & (kj - BLOCK <= qi))
        bias2_scr[...] = jnp.where(valid_two, 0.0, NEG_BIG)
        valid_one = (lax.broadcasted_iota(jnp.int32, (2 * BLOCK, BLOCK), 1)
                     <= (lax.broadcasted_iota(jnp.int32, (2 * BLOCK, BLOCK), 0) & (BLOCK - 1)))
        bias1_scr[...] = jnp.where(valid_one, 0.0, NEG_BIG)
        for g in range(N_GROUPS):
            for h, half in enumerate((half0, half1)):
                ve_scr[g, h, :, V7X_LANES:] = jnp.broadcast_to(half, (s_len, V7X_LANES))

    for g, v_ref in enumerate((v0, v1, v2)):
        v = v_ref[0]
        for h, half in enumerate((half0, half1)):
            ve_scr[g, h, :, :V7X_LANES] = v * half

    def block(q_ref, k_ref, g, q_row, k_row, n_keys, bias_ref, dst):
        q = q_ref[0, pl.ds(q_row, BLOCK), :]
        qs = jnp.concatenate([q * half0, q * half1], axis=0)
        k = k_ref[0, pl.ds(k_row, n_keys), :]
        s = _dot_nt(qs, k) + bias_ref[...]
        m = jnp.max(s, axis=-1, keepdims=True)
        p = jnp.exp(s - m).astype(BF16)
        p_cat = jnp.concatenate([p[:BLOCK], p[BLOCK:]], axis=1)
        keys = pl.ds(k_row, n_keys)
        ve = jnp.concatenate([ve_scr[g, 0, keys, :], ve_scr[g, 1, keys, :]], axis=0)
        pv = _dot(p_cat, ve)
        num_scr[g, dst, :] = pv[:, :V7X_LANES]
        l_scr[g, dst, :] = pv[:, V7X_LANES:]
        m_scr[g, dst, :] = jnp.where(first_half, m[:BLOCK], m[BLOCK:])

    def aligned(v):
        return pl.multiple_of(v, BLOCK)

    block(q0, k0, 0, 0, 0, BLOCK, bias1_scr, pl.ds(0, BLOCK))

    def g0_body(n, carry):
        block(q0, k0, 0, aligned(n * BLOCK), aligned((n - 1) * BLOCK), 2 * BLOCK, bias2_scr,
              pl.ds(aligned(n * BLOCK), BLOCK))
        return carry

    lax.fori_loop(1, s_len // BLOCK, g0_body, 0, unroll=G0_UNROLL)

    d1 = DILATIONS[1]
    stream_len = s_len // d1
    for r in range(d1):
        base = r * stream_len
        block(q1, k1, 1, base, base, BLOCK, bias1_scr, pl.ds(r, BLOCK, stride=d1))
        for n in range(1, stream_len // BLOCK):
            block(q1, k1, 1, base + n * BLOCK, base + (n - 1) * BLOCK, 2 * BLOCK, bias2_scr,
                  pl.ds(n * BLOCK * d1 + r, BLOCK, stride=d1))

    d2 = DILATIONS[2]

    def g2_body(r, carry):
        row = aligned(r * BLOCK)
        block(q2, k2, 2, row, row, BLOCK, bias1_scr, pl.ds(r, BLOCK, stride=d2))
        return carry

    lax.fori_loop(0, d2, g2_body, 0, unroll=G2_UNROLL)

    rows_per = 2 * BLOCK

    def merge(c, carry):
        rows = pl.ds(pl.multiple_of(c * rows_per, rows_per), rows_per)
        ms = [m_scr[g, rows, :] for g in range(N_GROUPS)]
        top = jnp.maximum(ms[0], jnp.maximum(ms[1], ms[2]))
        ws = [jnp.exp(mg - top) for mg in ms]
        num = ws[0] * num_scr[0, rows, :] + ws[1] * num_scr[1, rows, :] + ws[2] * num_scr[2, rows, :]
        den = ws[0] * l_scr[0, rows, :] + ws[1] * l_scr[1, rows, :] + ws[2] * l_scr[2, rows, :]
        o_ref[0, rows, :] = num / den
        return carry

    lax.fori_loop(0, s_len // rows_per, merge, 0)


def _attn(qkv0, qkv1, qkv2):
    b, s, _ = qkv0.shape
    n_pair = GROUP_WIDTH // V7X_LANES

    def part(k):
        return pl.BlockSpec((1, s, V7X_LANES), lambda i, p, k=k: (i, 0, k * n_pair + p))

    specs = [part(0), part(1), part(2)] * N_GROUPS
    stat = pltpu.VMEM((N_GROUPS, s, V7X_LANES), F32)
    scratch = [
        stat, stat, stat,
        pltpu.VMEM((N_GROUPS, 2, s, 2 * V7X_LANES), BF16),
        pltpu.VMEM((2 * BLOCK, 2 * BLOCK), F32),
        pltpu.VMEM((2 * BLOCK, BLOCK), F32),
    ]
    return pl.pallas_call(
        _attn_kernel,
        grid=(b, n_pair),
        in_specs=specs,
        out_specs=pl.BlockSpec((1, s, V7X_LANES), lambda i, p: (i, 0, p)),
        out_shape=jax.ShapeDtypeStruct((b, s, GROUP_WIDTH), F32),
        scratch_shapes=scratch,
        compiler_params=_params("arbitrary", "arbitrary"),
        name="dilated_attn",
    )(qkv0, qkv0, qkv0, qkv1, qkv1, qkv1, qkv2, qkv2, qkv2)


def _mem_attn(qm, kv_ref):
    rows = qm.shape[0]
    half0, half1 = _half_masks()
    first_half = lax.broadcasted_iota(jnp.int32, (rows, V7X_LANES), 1) < HEAD_DIM
    outs = []
    for p in range(MEM_WIDTH // V7X_LANES):
        q = qm[:, V7X_LANES * p:V7X_LANES * (p + 1)]
        qs = jnp.concatenate([q * half0, q * half1], axis=0)
        k = kv_ref[0, 0, :, V7X_LANES * p:V7X_LANES * (p + 1)]
        s = _dot_nt(qs, k)
        m = jnp.max(s, axis=-1, keepdims=True)
        e = jnp.exp(s - m)
        l = jnp.sum(e, axis=-1, keepdims=True)
        v = kv_ref[0, 0, :, MEM_WIDTH + V7X_LANES * p:MEM_WIDTH + V7X_LANES * (p + 1)]
        pv = _dot(e.astype(BF16), v) / l
        outs.append(jnp.where(first_half, pv[:rows], pv[rows:]))
    return jnp.concatenate(outs, axis=1)


def _tail_kernel(x_ref, mix_ref, qm0_ref, z0_ref, kv0_ref, wo0_ref, g1_ref, w1_ref, cw_ref,
                 kv1_ref, wo1_ref, gf_ref, out_ref, a_scr, y_scr, h1_scr, hb_scr):
    tm = x_ref.shape[1]
    halo = V7X_SUBLANES
    n_out = D_MODEL // COL_CHUNK

    mem0 = _mem_attn(qm0_ref[0], kv0_ref)
    z0 = z0_ref[0]
    y0 = jnp.concatenate(
        [mix_ref[0] * _silu(z0[:, :GROUP_WIDTH]), mem0 * _silu(z0[:, GROUP_WIDTH:])], axis=1).astype(BF16)
    for j in range(n_out):
        cs = slice(COL_CHUNK * j, COL_CHUNK * (j + 1))
        h1_scr[:, cs] = x_ref[0, :, cs] + _dot(y0, wo0_ref[:, cs])

    h1 = h1_scr[...]
    ms = jnp.mean(h1 * h1, axis=-1, keepdims=True)
    hb_scr[...] = (h1 * lax.rsqrt(ms + EPS) * g1_ref[...]).astype(BF16)
    hb = hb_scr[...]

    @pl.when(pl.program_id(1) == 0)
    def _():
        a_scr[0:halo, :] = jnp.zeros((halo, CONV_WIDTH), F32)

    def w1(c0, j):
        return w1_ref[:, c0 + COL_CHUNK * j:c0 + COL_CHUNK * (j + 1)]

    for j in range(CONV_WIDTH // COL_CHUNK):
        cs = slice(COL_CHUNK * j, COL_CHUNK * (j + 1))
        a_now = _dot(hb, w1(B_CG, j)) * _dot(hb, w1(B_U, j))
        a_scr[halo:halo + tm, cs] = a_now
        conv = (cw_ref[0:1, cs] * a_scr[halo - 2:halo - 2 + tm, cs]
                + cw_ref[1:2, cs] * a_scr[halo - 1:halo - 1 + tm, cs]
                + cw_ref[2:3, cs] * a_now)
        mix1 = _dot(hb, w1(B_BG, j)) * conv
        y_scr[:, cs] = (mix1 * _silu(_dot(hb, w1(B_Z, j)))).astype(BF16)
    a_scr[0:halo, :] = a_scr[tm:tm + halo, :]

    qm1 = _dot(hb, w1_ref[:, B_QM:B_QM + MEM_WIDTH]).astype(BF16)
    mem1 = _mem_attn(qm1, kv1_ref)
    z_mem = _dot(hb, w1_ref[:, B_Z + CONV_WIDTH:B_Z + BRANCH_B])
    y_scr[:, CONV_WIDTH:BRANCH_B] = (mem1 * _silu(z_mem)).astype(BF16)

    y1 = y_scr[...]
    ssq = jnp.zeros((tm, 1), F32)
    for j in range(n_out):
        cs = slice(COL_CHUNK * j, COL_CHUNK * (j + 1))
        h2 = h1_scr[:, cs] + _dot(y1, wo1_ref[:, cs])
        ssq = ssq + jnp.sum(h2 * h2, axis=-1, keepdims=True)
        out_ref[0, :, cs] = h2
    scale = lax.rsqrt(ssq * (1.0 / D_MODEL) + EPS)
    out_ref[0] = out_ref[0] * scale * gf_ref[...]


def _tail(x, mix, qm0, z0, kv, wo0, norm_g1, w1, conv_w, wo1, final_g):
    b, s, _ = x.shape
    tm = ROW_TILE
    row = lambda width: pl.BlockSpec((1, tm, width), lambda i, j: (i, j, 0))
    kv_spec = lambda layer: pl.BlockSpec((1, 1, N_MEM, 2 * MEM_WIDTH), lambda i, j: (layer, i, 0, 0))
    const = lambda shape: pl.BlockSpec(shape, lambda i, j: (0, 0))
    return pl.pallas_call(
        _tail_kernel,
        grid=(b, s // tm),
        in_specs=[
            row(D_MODEL), row(GROUP_WIDTH), row(MEM_WIDTH), row(BRANCH_A),
            kv_spec(0),
            _resident(wo0.shape, lambda i, j: (0, 0)),
            const((1, D_MODEL)),
            _resident(w1.shape, lambda i, j: (0, 0)),
            const((3, CONV_WIDTH)),
            kv_spec(1),
            _resident(wo1.shape, lambda i, j: (0, 0)),
            const((1, D_MODEL)),
        ],
        out_specs=row(D_MODEL),
        out_shape=jax.ShapeDtypeStruct((b, s, D_MODEL), F32),
        scratch_shapes=[
            pltpu.VMEM((tm + V7X_SUBLANES, CONV_WIDTH), F32),
            pltpu.VMEM((tm, BRANCH_B), BF16),
            pltpu.VMEM((tm, D_MODEL), F32),
            pltpu.VMEM((tm, D_MODEL), BF16),
        ],
        compiler_params=_params("arbitrary", "arbitrary"),
        name="tail",
    )(x, mix, qm0, z0, kv, wo0, norm_g1, w1, conv_w, kv, wo1, final_g)


def kernel(x, mem, positions, norm_g, mem_norm_g, w_mem_kv, attn_w_in, attn_w_out, conv_w_in, conv_w,
           conv_w_out, final_g):
    b, s, _ = x.shape
    w0 = (attn_w_in[0] * _A_SCALE).astype(BF16)
    w1 = (conv_w_in[0] * _B_SCALE).astype(BF16)
    wo0 = attn_w_out[0].astype(BF16)
    wo1 = conv_w_out[0].astype(BF16)
    w_kv = w_mem_kv.astype(BF16)

    inv_freq = ROPE_THETA ** (-jnp.arange(ROT_HALF, dtype=F32) * (2.0 / ROT_DIM))
    freq_rows = jnp.broadcast_to(inv_freq[:, None], (ROT_HALF, V7X_LANES))
    pos_rows = positions.reshape(b, 1, s)

    kv = _mem_kv(mem, mem_norm_g, w_kv)
    qkv0, qm0, z0, qkv1, qkv2 = _proj0(x, norm_g[0:1], pos_rows, freq_rows, w0)
    qkv1 = qkv1.reshape(b, s, QKV_G)
    qkv2 = qkv2.reshape(b, s, QKV_G)
    mix = _attn(qkv0, qkv1, qkv2)
    return _tail(x, mix, qm0, z0, kv, wo0, norm_g[1:2], w1, conv_w[0], wo1, final_g.reshape(1, D_MODEL))
```

```python
import numpy as np
import jax
import jax.numpy as jnp
from jax import lax
from jax.experimental import pallas as pl
from jax.experimental.pallas import tpu as pltpu

F32 = jnp.float32
BF16 = jnp.bfloat16

D_MODEL = 1024
HEAD_DIM = 64
ROT_DIM = 16
ROT_HALF = ROT_DIM // 2
ROPE_THETA = 500000.0
DILATIONS = (1, 4, 16)
BLOCK = 128
GROUP_WIDTH = 512
N_GROUPS = 3
N_MEM = 256
MEM_WIDTH = 256
CONV_WIDTH = 1024
EPS = 1e-6
SCORE_SCALE = HEAD_DIM ** -0.5

V7X_LANES = 128
V7X_SUBLANES = 8
ROW_TILE = 512
COL_CHUNK = 256
VMEM_LIMIT_BYTES = 56 * 1024 * 1024
NEG_BIG = -1e30
G0_UNROLL = 15
G2_UNROLL = 16

QKV_G = 3 * GROUP_WIDTH
A_PART = N_GROUPS * GROUP_WIDTH
A_QM = 3 * A_PART
A_Z = A_QM + MEM_WIDTH
BRANCH_A = GROUP_WIDTH + MEM_WIDTH
IN_A_COLS = A_Z + BRANCH_A
B_BG, B_CG, B_U = 0, CONV_WIDTH, 2 * CONV_WIDTH
B_QM = 3 * CONV_WIDTH
B_Z = B_QM + MEM_WIDTH
BRANCH_B = CONV_WIDTH + MEM_WIDTH
IN_B_COLS = B_Z + BRANCH_B


def _column_scale(n_cols, scaled):
    scale = np.ones((n_cols,), np.float32)
    for lo, hi in scaled:
        scale[lo:hi] = SCORE_SCALE
    return scale


_A_SCALE = _column_scale(IN_A_COLS, [(0, A_PART), (A_QM, A_QM + MEM_WIDTH)])
_B_SCALE = _column_scale(IN_B_COLS, [(B_QM, B_QM + MEM_WIDTH)])


def _params(*sem):
    return pltpu.CompilerParams(dimension_semantics=sem, vmem_limit_bytes=VMEM_LIMIT_BYTES)


def _resident(shape, index_map):
    return pl.BlockSpec(shape, index_map, pipeline_mode=pl.Buffered(1))


def _silu(z):
    return z * jax.nn.sigmoid(z)


def _dot(a, b):
    return jnp.dot(a, b, preferred_element_type=F32)


def _dot_nt(a, b):
    return lax.dot_general(a, b, (((1,), (1,)), ((), ())), preferred_element_type=F32)


def _half_masks():
    lane = lax.broadcasted_iota(jnp.int32, (1, V7X_LANES), 1)
    return (jnp.where(lane < HEAD_DIM, 1.0, 0.0).astype(BF16),
            jnp.where(lane < HEAD_DIM, 0.0, 1.0).astype(BF16))


def _memkv_kernel(mem_ref, g_ref, w_ref, kv_ref):
    m = mem_ref[0]
    ms = jnp.mean(m * m, axis=-1, keepdims=True)
    mn = (m * lax.rsqrt(ms + EPS) * g_ref[0]).astype(BF16)
    kv_ref[0, 0] = _dot(mn, w_ref[0]).astype(BF16)


def _mem_kv(mem, mem_norm_g, w_kv_bf):
    b = mem.shape[0]
    depth = w_kv_bf.shape[0]
    return pl.pallas_call(
        _memkv_kernel,
        grid=(depth, b),
        in_specs=[
            pl.BlockSpec((1, N_MEM, D_MODEL), lambda l, i: (i, 0, 0)),
            pl.BlockSpec((1, 1, D_MODEL), lambda l, i: (l, 0, 0)),
            pl.BlockSpec((1, D_MODEL, 2 * MEM_WIDTH), lambda l, i: (l, 0, 0)),
        ],
        out_specs=pl.BlockSpec((1, 1, N_MEM, 2 * MEM_WIDTH), lambda l, i: (l, i, 0, 0)),
        out_shape=jax.ShapeDtypeStruct((depth, b, N_MEM, 2 * MEM_WIDTH), BF16),
        compiler_params=_params("parallel", "parallel"),
        name="mem_kv",
    )(mem, mem_norm_g.reshape(depth, 1, D_MODEL), w_kv_bf)


def _rope_tables(pos_ref, freq_ref, tbl_scr):
    rows = tbl_scr.shape[1]
    freq = freq_ref[...]
    one = jnp.ones((V7X_SUBLANES, V7X_LANES), F32)
    zero = jnp.zeros((V7X_SUBLANES, V7X_LANES), F32)
    groups = V7X_LANES // V7X_SUBLANES
    per_head = HEAD_DIM // V7X_SUBLANES

    def lane_rows(first, second, other):
        pieces = []
        for i in range(groups):
            pieces.append(first if i % per_head == 0 else second if i % per_head == 1 else other)
        return jnp.concatenate(pieces, axis=0).T

    for c in range(rows // V7X_LANES):
        sl = slice(V7X_LANES * c, V7X_LANES * (c + 1))
        ang = pos_ref[0, :, sl].astype(F32) * freq
        cs = jnp.cos(ang)
        sn = jnp.sin(ang)
        tbl_scr[0, sl, :] = lane_rows(cs, cs, one)
        tbl_scr[1, sl, :] = lane_rows(-sn, zero, zero)
        tbl_scr[2, sl, :] = lane_rows(zero, sn, zero)


def _rope(a, cs, sn_next, sn_prev):
    return (a * cs + pltpu.roll(a, V7X_LANES - ROT_HALF, 1) * sn_next
            + pltpu.roll(a, ROT_HALF, 1) * sn_prev)


def _proj0_kernel(x_ref, g_ref, pos_ref, freq_ref, w_ref,
                  o0_ref, oqm_ref, oz_ref, o1_ref, o2_ref, hn_scr, perm_scr, tbl_scr):
    tm = x_ref.shape[1]
    x = x_ref[0]
    ms = jnp.mean(x * x, axis=-1, keepdims=True)
    hn = x * lax.rsqrt(ms + EPS) * g_ref[...]
    hb = hn.astype(BF16)
    n_slab = D_MODEL // V7X_LANES
    for c in range(n_slab):
        hn_scr[c] = hn[:, V7X_LANES * c:V7X_LANES * (c + 1)]
    _rope_tables(pos_ref, freq_ref, tbl_scr)

    def qkv_chunk(lhs, g, j, tables):
        part, half = divmod(j, GROUP_WIDTH // COL_CHUNK)
        c0 = A_PART * part + GROUP_WIDTH * g + COL_CHUNK * half
        acc = _dot(lhs, w_ref[:, c0:c0 + COL_CHUNK])
        if part < 2:
            acc = jnp.concatenate(
                [_rope(acc[:, :V7X_LANES], *tables), _rope(acc[:, V7X_LANES:], *tables)], axis=1)
        return acc.astype(BF16)

    tables = tuple(tbl_scr[t] for t in range(3))
    for j in range(QKV_G // COL_CHUNK):
        o0_ref[0, :, COL_CHUNK * j:COL_CHUNK * (j + 1)] = qkv_chunk(hb, 0, j, tables)
    oqm_ref[0] = _dot(hb, w_ref[:, A_QM:A_QM + MEM_WIDTH]).astype(BF16)
    for j in range(BRANCH_A // COL_CHUNK):
        oz_ref[0, :, COL_CHUNK * j:COL_CHUNK * (j + 1)] = _dot(
            hb, w_ref[:, A_Z + COL_CHUNK * j:A_Z + COL_CHUNK * (j + 1)])

    for g, o_ref in ((1, o1_ref), (2, o2_ref)):
        d = DILATIONS[g]
        n = tm // d
        for r in range(d):
            for c in range(n_slab):
                perm_scr[r * n:(r + 1) * n, V7X_LANES * c:V7X_LANES * (c + 1)] = (
                    hn_scr[c, pl.ds(r, n, stride=d), :].astype(BF16))
        tables = tuple(
            jnp.concatenate([tbl_scr[t, pl.ds(r, n, stride=d), :] for r in range(d)], axis=0)
            for t in range(3))
        lhs = perm_scr[...]
        for j in range(QKV_G // COL_CHUNK):
            res = qkv_chunk(lhs, g, j, tables)
            for r in range(d):
                o_ref[0, r, :, COL_CHUNK * j:COL_CHUNK * (j + 1)] = res[r * n:(r + 1) * n]


def _proj0(x, norm_g, pos_rows, freq_rows, w0):
    b, s, _ = x.shape
    tm = ROW_TILE
    row = lambda width: pl.BlockSpec((1, tm, width), lambda i, j: (i, j, 0))
    stream = lambda d: pl.BlockSpec((1, d, tm // d, QKV_G), lambda i, j: (i, 0, j, 0))
    d1, d2 = DILATIONS[1], DILATIONS[2]
    return pl.pallas_call(
        _proj0_kernel,
        grid=(b, s // tm),
        in_specs=[
            row(D_MODEL),
            pl.BlockSpec((1, D_MODEL), lambda i, j: (0, 0)),
            pl.BlockSpec((1, 1, tm), lambda i, j: (i, 0, j)),
            pl.BlockSpec((V7X_SUBLANES, V7X_LANES), lambda i, j: (0, 0)),
            _resident((D_MODEL, IN_A_COLS), lambda i, j: (0, 0)),
        ],
        out_specs=[row(QKV_G), row(MEM_WIDTH), row(BRANCH_A), stream(d1), stream(d2)],
        out_shape=[
            jax.ShapeDtypeStruct((b, s, QKV_G), BF16),
            jax.ShapeDtypeStruct((b, s, MEM_WIDTH), BF16),
            jax.ShapeDtypeStruct((b, s, BRANCH_A), F32),
            jax.ShapeDtypeStruct((b, d1, s // d1, QKV_G), BF16),
            jax.ShapeDtypeStruct((b, d2, s // d2, QKV_G), BF16),
        ],
        scratch_shapes=[
            pltpu.VMEM((D_MODEL // V7X_LANES, tm, V7X_LANES), F32),
            pltpu.VMEM((tm, D_MODEL), BF16),
            pltpu.VMEM((3, tm, V7X_LANES), F32),
        ],
        compiler_params=_params("parallel", "parallel"),
        name="proj0",
    )(x, norm_g, pos_rows, freq_rows, w0)


def _attn_kernel(q0, k0, v0, q1, k1, v1, q2, k2, v2, o_ref,
                 num_scr, m_scr, l_scr, ve_scr, bias2_scr, bias1_scr):
    s_len = o_ref.shape[1]
    half0, half1 = _half_masks()
    first_half = lax.broadcasted_iota(jnp.int32, (BLOCK, V7X_LANES), 1) < HEAD_DIM

    @pl.when((pl.program_id(0) == 0) & (pl.program_id(1) == 0))
    def _():
        qi = lax.broadcasted_iota(jnp.int32, (2 * BLOCK, 2 * BLOCK), 0) & (BLOCK - 1)
        kj = lax.broadcasted_iota(jnp.int32, (2 * BLOCK, 2 * BLOCK), 1)
        valid_two = ((kj < BLOCK) & (kj >= qi)) | ((kj >= BLOCK) & (kj - BLOCK <= qi))
        bias2_scr[...] = jnp.where(valid_two, 0.0, NEG_BIG)
        valid_one = (lax.broadcasted_iota(jnp.int32, (2 * BLOCK, BLOCK), 1)
                     <= (lax.broadcasted_iota(jnp.int32, (2 * BLOCK, BLOCK), 0) & (BLOCK - 1)))
        bias1_scr[...] = jnp.where(valid_one, 0.0, NEG_BIG)
        for g in range(N_GROUPS):
            for h, half in enumerate((half0, half1)):
                ve_scr[g, h, :, V7X_LANES:] = jnp.broadcast_to(half, (s_len, V7X_LANES))

    for g, v_ref in enumerate((v0, v1, v2)):
        v = v_ref[0]
        for h, half in enumerate((half0, half1)):
            ve_scr[g, h, :, :V7X_LANES] = v * half

    def block(q_ref, k_ref, g, q_row, k_row, n_keys, bias_ref, dst):
        q = q_ref[0, pl.ds(q_row, BLOCK), :]
        qs = jnp.concatenate([q * half0, q * half1], axis=0)
        k = k_ref[0, pl.ds(k_row, n_keys), :]
        s = _dot_nt(qs, k) + bias_ref[...]
        m = jnp.max(s, axis=-1, keepdims=True)
        p = jnp.exp(s - m).astype(BF16)
        p_cat = jnp.concatenate([p[:BLOCK], p[BLOCK:]], axis=1)
        keys = pl.ds(k_row, n_keys)
        ve = jnp.concatenate([ve_scr[g, 0, keys, :], ve_scr[g, 1, keys, :]], axis=0)
        pv = _dot(p_cat, ve)
        num_scr[g, dst, :] = pv[:, :V7X_LANES]
        l_scr[g, dst, :] = pv[:, V7X_LANES:]
        m_scr[g, dst, :] = jnp.where(first_half, m[:BLOCK], m[BLOCK:])

    def aligned(v):
        return pl.multiple_of(v, BLOCK)

    block(q0, k0, 0, 0, 0, BLOCK, bias1_scr, pl.ds(0, BLOCK))

    def g0_body(n, carry):
        block(q0, k0, 0, aligned(n * BLOCK), aligned((n - 1) * BLOCK), 2 * BLOCK, bias2_scr,
              pl.ds(aligned(n * BLOCK), BLOCK))
        return carry

    lax.fori_loop(1, s_len // BLOCK, g0_body, 0, unroll=G0_UNROLL)

    d1 = DILATIONS[1]
    stream_len = s_len // d1
    for r in range(d1):
        base = r * stream_len
        block(q1, k1, 1, base, base, BLOCK, bias1_scr, pl.ds(r, BLOCK, stride=d1))
        for n in range(1, stream_len // BLOCK):
            block(q1, k1, 1, base + n * BLOCK, base + (n - 1) * BLOCK, 2 * BLOCK, bias2_scr,
                  pl.ds(n * BLOCK * d1 + r, BLOCK, stride=d1))

    d2 = DILATIONS[2]

    def g2_body(r, carry):
        row = aligned(r * BLOCK)
        block(q2, k2, 2, row, row, BLOCK, bias1_scr, pl.ds(r, BLOCK, stride=d2))
        return carry

    lax.fori_loop(0, d2, g2_body, 0, unroll=G2_UNROLL)

    rows_per = 2 * BLOCK

    def merge(c, carry):
        rows = pl.ds(pl.multiple_of(c * rows_per, rows_per), rows_per)
        ms = [m_scr[g, rows, :] for g in range(N_GROUPS)]
        top = jnp.maximum(ms[0], jnp.maximum(ms[1], ms[2]))
        ws = [jnp.exp(mg - top) for mg in ms]
        num = ws[0] * num_scr[0, rows, :] + ws[1] * num_scr[1, rows, :] + ws[2] * num_scr[2, rows, :]
        den = ws[0] * l_scr[0, rows, :] + ws[1] * l_scr[1, rows, :] + ws[2] * l_scr[2, rows, :]
        o_ref[0, rows, :] = num / den
        return carry

    lax.fori_loop(0, s_len // rows_per, merge, 0)


def _attn(qkv0, qkv1, qkv2):
    b, s, _ = qkv0.shape
    n_pair = GROUP_WIDTH // V7X_LANES

    def part(k):
        return pl.BlockSpec((1, s, V7X_LANES), lambda i, p, k=k: (i, 0, k * n_pair + p))

    specs = [part(0), part(1), part(2)] * N_GROUPS
    stat = pltpu.VMEM((N_GROUPS, s, V7X_LANES), F32)
    scratch = [
        stat, stat, stat,
        pltpu.VMEM((N_GROUPS, 2, s, 2 * V7X_LANES), BF16),
        pltpu.VMEM((2 * BLOCK, 2 * BLOCK), F32),
        pltpu.VMEM((2 * BLOCK, BLOCK), F32),
    ]
    return pl.pallas_call(
        _attn_kernel,
        grid=(b, n_pair),
        in_specs=specs,
        out_specs=pl.BlockSpec((1, s, V7X_LANES), lambda i, p: (i, 0, p)),
        out_shape=jax.ShapeDtypeStruct((b, s, GROUP_WIDTH), F32),
        scratch_shapes=scratch,
        compiler_params=_params("arbitrary", "arbitrary"),
        name="dilated_attn",
    )(qkv0, qkv0, qkv0, qkv1, qkv1, qkv1, qkv2, qkv2, qkv2)


def _mem_attn(qm, kv_ref):
    rows = qm.shape[0]
    half0, half1 = _half_masks()
    first_half = lax.broadcasted_iota(jnp.int32, (rows, V7X_LANES), 1) < HEAD_DIM
    outs = []
    for p in range(MEM_WIDTH // V7X_LANES):
        q = qm[:, V7X_LANES * p:V7X_LANES * (p + 1)]
        qs = jnp.concatenate([q * half0, q * half1], axis=0)
        k = kv_ref[0, 0, :, V7X_LANES * p:V7X_LANES * (p + 1)]
        s = _dot_nt(qs, k)
        m = jnp.max(s, axis=-1, keepdims=True)
        e = jnp.exp(s - m)
        l = jnp.sum(e, axis=-1, keepdims=True)
        v = kv_ref[0, 0, :, MEM_WIDTH + V7X_LANES * p:MEM_WIDTH + V7X_LANES * (p + 1)]
        pv = _dot(e.astype(BF16), v) / l
        outs.append(jnp.where(first_half, pv[:rows], pv[rows:]))
    return jnp.concatenate(outs, axis=1)


def _tail_kernel(x_ref, mix_ref, qm0_ref, z0_ref, kv0_ref, wo0_ref, g1_ref, w1_ref, cw_ref,
                 kv1_ref, wo1_ref, gf_ref, out_ref, a_scr, y_scr, h1_scr, hb_scr):
    tm = x_ref.shape[1]
    halo = V7X_SUBLANES
    n_out = D_MODEL // COL_CHUNK

    mem0 = _mem_attn(qm0_ref[0], kv0_ref)
    z0 = z0_ref[0]
    y0 = jnp.concatenate(
        [mix_ref[0] * _silu(z0[:, :GROUP_WIDTH]), mem0 * _silu(z0[:, GROUP_WIDTH:])], axis=1).astype(BF16)
    for j in range(n_out):
        cs = slice(COL_CHUNK * j, COL_CHUNK * (j + 1))
        h1_scr[:, cs] = x_ref[0, :, cs] + _dot(y0, wo0_ref[:, cs])

    h1 = h1_scr[...]
    ms = jnp.mean(h1 * h1, axis=-1, keepdims=True)
    hb_scr[...] = (h1 * lax.rsqrt(ms + EPS) * g1_ref[...]).astype(BF16)
    hb = hb_scr[...]

    @pl.when(pl.program_id(1) == 0)
    def _():
        a_scr[0:halo, :] = jnp.zeros((halo, CONV_WIDTH), F32)

    def w1(c0, j):
        return w1_ref[:, c0 + COL_CHUNK * j:c0 + COL_CHUNK * (j + 1)]

    for j in range(CONV_WIDTH // COL_CHUNK):
        cs = slice(COL_CHUNK * j, COL_CHUNK * (j + 1))
        a_now = _dot(hb, w1(B_CG, j)) * _dot(hb, w1(B_U, j))
        a_scr[halo:halo + tm, cs] = a_now
        conv = (cw_ref[0:1, cs] * a_scr[halo - 2:halo - 2 + tm, cs]
                + cw_ref[1:2, cs] * a_scr[halo - 1:halo - 1 + tm, cs]
                + cw_ref[2:3, cs] * a_now)
        mix1 = _dot(hb, w1(B_BG, j)) * conv
        y_scr[:, cs] = (mix1 * _silu(_dot(hb, w1(B_Z, j)))).astype(BF16)
    a_scr[0:halo, :] = a_scr[tm:tm + halo, :]

    qm1 = _dot(hb, w1_ref[:, B_QM:B_QM + MEM_WIDTH]).astype(BF16)
    mem1 = _mem_attn(qm1, kv1_ref)
    z_mem = _dot(hb, w1_ref[:, B_Z + CONV_WIDTH:B_Z + BRANCH_B])
    y_scr[:, CONV_WIDTH:BRANCH_B] = (mem1 * _silu(z_mem)).astype(BF16)

    y1 = y_scr[...]
    ssq = jnp.zeros((tm, 1), F32)
    for j in range(n_out):
        cs = slice(COL_CHUNK * j, COL_CHUNK * (j + 1))
        h2 = h1_scr[:, cs] + _dot(y1, wo1_ref[:, cs])
        ssq = ssq + jnp.sum(h2 * h2, axis=-1, keepdims=True)
        out_ref[0, :, cs] = h2
    scale = lax.rsqrt(ssq * (1.0 / D_MODEL) + EPS)
    out_ref[0] = out_ref[0] * scale * gf_ref[...]


def _tail(x, mix, qm0, z0, kv, wo0, norm_g1, w1, conv_w, wo1, final_g):
    b, s, _ = x.shape
    tm = ROW_TILE
    row = lambda width: pl.BlockSpec((1, tm, width), lambda i, j: (i, j, 0))
    kv_spec = lambda layer: pl.BlockSpec((1, 1, N_MEM, 2 * MEM_WIDTH), lambda i, j: (layer, i, 0, 0))
    const = lambda shape: pl.BlockSpec(shape, lambda i, j: (0, 0))
    return pl.pallas_call(
        _tail_kernel,
        grid=(b, s // tm),
        in_specs=[
            row(D_MODEL), row(GROUP_WIDTH), row(MEM_WIDTH), row(BRANCH_A),
            kv_spec(0),
            _resident(wo0.shape, lambda i, j: (0, 0)),
            const((1, D_MODEL)),
            _resident(w1.shape, lambda i, j: (0, 0)),
            const((3, CONV_WIDTH)),
            kv_spec(1),
            _resident(wo1.shape, lambda i, j: (0, 0)),
            const((1, D_MODEL)),
        ],
        out_specs=row(D_MODEL),
        out_shape=jax.ShapeDtypeStruct((b, s, D_MODEL), F32),
        scratch_shapes=[
            pltpu.VMEM((tm + V7X_SUBLANES, CONV_WIDTH), F32),
            pltpu.VMEM((tm, BRANCH_B), BF16),
            pltpu.VMEM((tm, D_MODEL), F32),
            pltpu.VMEM((tm, D_MODEL), BF16),
        ],
        compiler_params=_params("arbitrary", "arbitrary"),
        name="tail",
    )(x, mix, qm0, z0, kv, wo0, norm_g1, w1, conv_w, kv, wo1, final_g)


def kernel(x, mem, positions, norm_g, mem_norm_g, w_mem_kv, attn_w_in, attn_w_out, conv_w_in, conv_w,
           conv_w_out, final_g):
    b, s, _ = x.shape
    w0 = (attn_w_in[0] * _A_SCALE).astype(BF16)
    w1 = (conv_w_in[0] * _B_SCALE).astype(BF16)
    wo0 = attn_w_out[0].astype(BF16)
    wo1 = conv_w_out[0].astype(BF16)
    w_kv = w_mem_kv.astype(BF16)

    inv_freq = ROPE_THETA ** (-jnp.arange(ROT_HALF, dtype=F32) * (2.0 / ROT_DIM))
    freq_rows = jnp.broadcast_to(inv_freq[:, None], (ROT_HALF, V7X_LANES))
    pos_rows = positions.reshape(b, 1, s)

    kv = _mem_kv(mem, mem_norm_g, w_kv)
    qkv0, qm0, z0, qkv1, qkv2 = _proj0(x, norm_g[0:1], pos_rows, freq_rows, w0)
    qkv1 = qkv1.reshape(b, s, QKV_G)
    qkv2 = qkv2.reshape(b, s, QKV_G)
    mix = _attn(qkv0, qkv1, qkv2)
    return _tail(x, mix, qm0, z0, kv, wo0, norm_g[1:2], w1, conv_w[0], wo1, final_g.reshape(1, D_MODEL))
```

```python
import functools

import numpy as np
import jax
import jax.numpy as jnp
from jax import lax
from jax.experimental import pallas as pl
from jax.experimental.pallas import tpu as pltpu

F32 = jnp.float32
BF16 = jnp.bfloat16

D_MODEL = 1024
HEAD_DIM = 64
ROT_DIM = 16
ROT_HALF = ROT_DIM // 2
ROPE_THETA = 500000.0
DILATIONS = (1, 4, 16)
BLOCK = 128
GROUP_WIDTH = 512
N_GROUPS = 3
N_MEM = 256
MEM_WIDTH = 256
CONV_WIDTH = 1024
EPS = 1e-6
SCORE_SCALE = HEAD_DIM ** -0.5

V7X_LANES = 128
V7X_SUBLANES = 8
ROW_TILE = 512
COL_CHUNK = 256
VMEM_LIMIT_BYTES = 56 * 1024 * 1024
NEG_BIG = -1e30
G0_UNROLL = 15
G2_UNROLL = 16
MEM_ROWS = 256

QKV_G = 3 * GROUP_WIDTH
A_PART = N_GROUPS * GROUP_WIDTH
A_QM = 3 * A_PART
A_Z = A_QM + MEM_WIDTH
BRANCH_A = GROUP_WIDTH + MEM_WIDTH
IN_A_COLS = A_Z + BRANCH_A
B_BG, B_CG, B_U = 0, CONV_WIDTH, 2 * CONV_WIDTH
B_QM = 3 * CONV_WIDTH
B_Z = B_QM + MEM_WIDTH
BRANCH_B = CONV_WIDTH + MEM_WIDTH
IN_B_COLS = B_Z + BRANCH_B


def _column_scale(n_cols, scaled):
    scale = np.ones((n_cols,), np.float32)
    for lo, hi in scaled:
        scale[lo:hi] = SCORE_SCALE
    return scale


_A_SCALE = _column_scale(IN_A_COLS, [(0, A_PART), (A_QM, A_QM + MEM_WIDTH)])
_B_SCALE = _column_scale(IN_B_COLS, [(B_QM, B_QM + MEM_WIDTH)])


def _params(*sem):
    return pltpu.CompilerParams(dimension_semantics=sem, vmem_limit_bytes=VMEM_LIMIT_BYTES)


def _resident(shape, index_map):
    return pl.BlockSpec(shape, index_map, pipeline_mode=pl.Buffered(1))


def _silu(z):
    return z * jax.nn.sigmoid(z)


def _dot(a, b):
    return jnp.dot(a, b, preferred_element_type=F32)


def _dot_nt(a, b):
    return lax.dot_general(a, b, (((1,), (1,)), ((), ())), preferred_element_type=F32)


def _half_masks():
    lane = lax.broadcasted_iota(jnp.int32, (1, V7X_LANES), 1)
    return (jnp.where(lane < HEAD_DIM, 1.0, 0.0).astype(BF16),
            jnp.where(lane < HEAD_DIM, 0.0, 1.0).astype(BF16))


def _memkv_kernel(mem_ref, g_ref, w_ref, kv_ref):
    m = mem_ref[0]
    ms = jnp.mean(m * m, axis=-1, keepdims=True)
    mn = (m * lax.rsqrt(ms + EPS) * g_ref[0]).astype(BF16)
    kv_ref[0, 0] = _dot(mn, w_ref[0]).astype(BF16)


def _mem_kv(mem, mem_norm_g, w_kv_bf):
    b = mem.shape[0]
    depth = w_kv_bf.shape[0]
    return pl.pallas_call(
        _memkv_kernel,
        grid=(depth, b),
        in_specs=[
            pl.BlockSpec((1, N_MEM, D_MODEL), lambda l, i: (i, 0, 0)),
            pl.BlockSpec((1, 1, D_MODEL), lambda l, i: (l, 0, 0)),
            pl.BlockSpec((1, D_MODEL, 2 * MEM_WIDTH), lambda l, i: (l, 0, 0)),
        ],
        out_specs=pl.BlockSpec((1, 1, N_MEM, 2 * MEM_WIDTH), lambda l, i: (l, i, 0, 0)),
        out_shape=jax.ShapeDtypeStruct((depth, b, N_MEM, 2 * MEM_WIDTH), BF16),
        compiler_params=_params("parallel", "parallel"),
        name="mem_kv",
    )(mem, mem_norm_g.reshape(depth, 1, D_MODEL), w_kv_bf)


def _rope_tables(pos_ref, freq_ref, tbl_scr):
    rows = tbl_scr.shape[1]
    freq = freq_ref[...]
    one = jnp.ones((V7X_SUBLANES, V7X_LANES), F32)
    zero = jnp.zeros((V7X_SUBLANES, V7X_LANES), F32)
    groups = V7X_LANES // V7X_SUBLANES
    per_head = HEAD_DIM // V7X_SUBLANES

    def lane_rows(first, second, other):
        pieces = []
        for i in range(groups):
            pieces.append(first if i % per_head == 0 else second if i % per_head == 1 else other)
        return jnp.concatenate(pieces, axis=0).T

    for c in range(rows // V7X_LANES):
        sl = slice(V7X_LANES * c, V7X_LANES * (c + 1))
        ang = pos_ref[0, :, sl].astype(F32) * freq
        cs = jnp.cos(ang)
        sn = jnp.sin(ang)
        tbl_scr[0, sl, :] = lane_rows(cs, cs, one)
        tbl_scr[1, sl, :] = lane_rows(-sn, zero, zero)
        tbl_scr[2, sl, :] = lane_rows(zero, sn, zero)


def _rope(a, cs, sn_next, sn_prev):
    return (a * cs + pltpu.roll(a, V7X_LANES - ROT_HALF, 1) * sn_next
            + pltpu.roll(a, ROT_HALF, 1) * sn_prev)


def _proj0_kernel(x_ref, g_ref, pos_ref, freq_ref, w_ref,
                  o0_ref, oqm_ref, oz_ref, o1_ref, o2_ref, hn_scr, perm_scr, tbl_scr):
    tm = x_ref.shape[1]
    x = x_ref[0]
    ms = jnp.mean(x * x, axis=-1, keepdims=True)
    hn = x * lax.rsqrt(ms + EPS) * g_ref[...]
    hb = hn.astype(BF16)
    n_slab = D_MODEL // V7X_LANES
    for c in range(n_slab):
        hn_scr[c] = hn[:, V7X_LANES * c:V7X_LANES * (c + 1)]
    _rope_tables(pos_ref, freq_ref, tbl_scr)

    def qkv_chunk(lhs, g, j, tables):
        part, half = divmod(j, GROUP_WIDTH // COL_CHUNK)
        c0 = A_PART * part + GROUP_WIDTH * g + COL_CHUNK * half
        acc = _dot(lhs, w_ref[:, c0:c0 + COL_CHUNK])
        if part < 2:
            acc = jnp.concatenate(
                [_rope(acc[:, :V7X_LANES], *tables), _rope(acc[:, V7X_LANES:], *tables)], axis=1)
        return acc.astype(BF16)

    tables = tuple(tbl_scr[t] for t in range(3))
    for j in range(QKV_G // COL_CHUNK):
        o0_ref[0, :, COL_CHUNK * j:COL_CHUNK * (j + 1)] = qkv_chunk(hb, 0, j, tables)
    oqm_ref[0] = _dot(hb, w_ref[:, A_QM:A_QM + MEM_WIDTH]).astype(BF16)
    for j in range(BRANCH_A // COL_CHUNK):
        oz_ref[0, :, COL_CHUNK * j:COL_CHUNK * (j + 1)] = _dot(
            hb, w_ref[:, A_Z + COL_CHUNK * j:A_Z + COL_CHUNK * (j + 1)])

    for g, o_ref in ((1, o1_ref), (2, o2_ref)):
        d = DILATIONS[g]
        n = tm // d
        for r in range(d):
            for c in range(n_slab):
                perm_scr[r * n:(r + 1) * n, V7X_LANES * c:V7X_LANES * (c + 1)] = (
                    hn_scr[c, pl.ds(r, n, stride=d), :].astype(BF16))
        tables = tuple(
            jnp.concatenate([tbl_scr[t, pl.ds(r, n, stride=d), :] for r in range(d)], axis=0)
            for t in range(3))
        lhs = perm_scr[...]
        for j in range(QKV_G // COL_CHUNK):
            res = qkv_chunk(lhs, g, j, tables)
            for r in range(d):
                o_ref[0, r, :, COL_CHUNK * j:COL_CHUNK * (j + 1)] = res[r * n:(r + 1) * n]


def _proj0(x, norm_g, pos_rows, freq_rows, w0):
    b, s, _ = x.shape
    tm = ROW_TILE
    row = lambda width: pl.BlockSpec((1, tm, width), lambda i, j: (i, j, 0))
    stream = lambda d: pl.BlockSpec((1, d, tm // d, QKV_G), lambda i, j: (i, 0, j, 0))
    d1, d2 = DILATIONS[1], DILATIONS[2]
    return pl.pallas_call(
        _proj0_kernel,
        grid=(b, s // tm),
        in_specs=[
            row(D_MODEL),
            pl.BlockSpec((1, D_MODEL), lambda i, j: (0, 0)),
            pl.BlockSpec((1, 1, tm), lambda i, j: (i, 0, j)),
            pl.BlockSpec((V7X_SUBLANES, V7X_LANES), lambda i, j: (0, 0)),
            _resident((D_MODEL, IN_A_COLS), lambda i, j: (0, 0)),
        ],
        out_specs=[row(QKV_G), row(MEM_WIDTH), row(BRANCH_A), stream(d1), stream(d2)],
        out_shape=[
            jax.ShapeDtypeStruct((b, s, QKV_G), BF16),
            jax.ShapeDtypeStruct((b, s, MEM_WIDTH), BF16),
            jax.ShapeDtypeStruct((b, s, BRANCH_A), F32),
            jax.ShapeDtypeStruct((b, d1, s // d1, QKV_G), BF16),
            jax.ShapeDtypeStruct((b, d2, s // d2, QKV_G), BF16),
        ],
        scratch_shapes=[
            pltpu.VMEM((D_MODEL // V7X_LANES, tm, V7X_LANES), F32),
            pltpu.VMEM((tm, D_MODEL), BF16),
            pltpu.VMEM((3, tm, V7X_LANES), F32),
        ],
        compiler_params=_params("parallel", "parallel"),
        name="proj0",
    )(x, norm_g, pos_rows, freq_rows, w0)


def _attn_kernel(q0, k0, v0, q1, k1, v1, q2, k2, v2, o_ref,
                 num_scr, m_scr, l_scr, ve_scr, bias2_scr, bias1_scr):
    s_len = o_ref.shape[1]
    half0, half1 = _half_masks()
    first_half = lax.broadcasted_iota(jnp.int32, (BLOCK, V7X_LANES), 1) < HEAD_DIM

    @pl.when((pl.program_id(0) == 0) & (pl.program_id(1) == 0))
    def _():
        qi = lax.broadcasted_iota(jnp.int32, (2 * BLOCK, 2 * BLOCK), 0) & (BLOCK - 1)
        kj = lax.broadcasted_iota(jnp.int32, (2 * BLOCK, 2 * BLOCK), 1)
        valid_two = ((kj < BLOCK) & (kj >= qi)) | ((kj >= BLOCK) & (kj - BLOCK <= qi))
        bias2_scr[...] = jnp.where(valid_two, 0.0, NEG_BIG)
        valid_one = (lax.broadcasted_iota(jnp.int32, (2 * BLOCK, BLOCK), 1)
                     <= (lax.broadcasted_iota(jnp.int32, (2 * BLOCK, BLOCK), 0) & (BLOCK - 1)))
        bias1_scr[...] = jnp.where(valid_one, 0.0, NEG_BIG)
        for g in range(N_GROUPS):
            for h, half in enumerate((half0, half1)):
                ve_scr[g, h, :, V7X_LANES:] = jnp.broadcast_to(half, (s_len, V7X_LANES))

    for g, v_ref in enumerate((v0, v1, v2)):
        v = v_ref[0]
        for h, half in enumerate((half0, half1)):
            ve_scr[g, h, :, :V7X_LANES] = v * half

    def block(q_ref, k_ref, g, q_row, k_row, n_keys, bias_ref, dst):
        q = q_ref[0, pl.ds(q_row, BLOCK), :]
        qs = jnp.concatenate([q * half0, q * half1], axis=0)
        k = k_ref[0, pl.ds(k_row, n_keys), :]
        s = _dot_nt(qs, k) + bias_ref[...]
        m = jnp.max(s, axis=-1, keepdims=True)
        p = jnp.exp(s - m).astype(BF16)
        p_cat = jnp.concatenate([p[:BLOCK], p[BLOCK:]], axis=1)
        keys = pl.ds(k_row, n_keys)
        ve = jnp.concatenate([ve_scr[g, 0, keys, :], ve_scr[g, 1, keys, :]], axis=0)
        pv = _dot(p_cat, ve)
        num_scr[g, dst, :] = pv[:, :V7X_LANES]
        l_scr[g, dst, :] = pv[:, V7X_LANES:]
        m_scr[g, dst, :] = jnp.where(first_half, m[:BLOCK], m[BLOCK:])

    def aligned(v):
        return pl.multiple_of(v, BLOCK)

    block(q0, k0, 0, 0, 0, BLOCK, bias1_scr, pl.ds(0, BLOCK))

    def g0_body(n, carry):
        block(q0, k0, 0, aligned(n * BLOCK), aligned((n - 1) * BLOCK), 2 * BLOCK, bias2_scr,
              pl.ds(aligned(n * BLOCK), BLOCK))
        return carry

    lax.fori_loop(1, s_len // BLOCK, g0_body, 0, unroll=G0_UNROLL)

    d1 = DILATIONS[1]
    stream_len = s_len // d1
    for r in range(d1):
        base = r * stream_len
        block(q1, k1, 1, base, base, BLOCK, bias1_scr, pl.ds(r, BLOCK, stride=d1))
        for n in range(1, stream_len // BLOCK):
            block(q1, k1, 1, base + n * BLOCK, base + (n - 1) * BLOCK, 2 * BLOCK, bias2_scr,
                  pl.ds(n * BLOCK * d1 + r, BLOCK, stride=d1))

    d2 = DILATIONS[2]

    def g2_body(r, carry):
        row = aligned(r * BLOCK)
        block(q2, k2, 2, row, row, BLOCK, bias1_scr, pl.ds(r, BLOCK, stride=d2))
        return carry

    lax.fori_loop(0, d2, g2_body, 0, unroll=G2_UNROLL)

    rows_per = 2 * BLOCK

    def merge(c, carry):
        rows = pl.ds(pl.multiple_of(c * rows_per, rows_per), rows_per)
        ms = [m_scr[g, rows, :] for g in range(N_GROUPS)]
        top = jnp.maximum(ms[0], jnp.maximum(ms[1], ms[2]))
        ws = [jnp.exp(mg - top) for mg in ms]
        num = ws[0] * num_scr[0, rows, :] + ws[1] * num_scr[1, rows, :] + ws[2] * num_scr[2, rows, :]
        den = ws[0] * l_scr[0, rows, :] + ws[1] * l_scr[1, rows, :] + ws[2] * l_scr[2, rows, :]
        o_ref[0, rows, :] = num / den
        return carry

    lax.fori_loop(0, s_len // rows_per, merge, 0)


def _attn(qkv0, qkv1, qkv2):
    b, s, _ = qkv0.shape
    n_pair = GROUP_WIDTH // V7X_LANES

    def part(k):
        return pl.BlockSpec((1, s, V7X_LANES), lambda i, p, k=k: (i, 0, k * n_pair + p))

    specs = [part(0), part(1), part(2)] * N_GROUPS
    stat = pltpu.VMEM((N_GROUPS, s, V7X_LANES), F32)
    scratch = [
        stat, stat, stat,
        pltpu.VMEM((N_GROUPS, 2, s, 2 * V7X_LANES), BF16),
        pltpu.VMEM((2 * BLOCK, 2 * BLOCK), F32),
        pltpu.VMEM((2 * BLOCK, BLOCK), F32),
    ]
    return pl.pallas_call(
        _attn_kernel,
        grid=(b, n_pair),
        in_specs=specs,
        out_specs=pl.BlockSpec((1, s, V7X_LANES), lambda i, p: (i, 0, p)),
        out_shape=jax.ShapeDtypeStruct((b, s, GROUP_WIDTH), F32),
        scratch_shapes=scratch,
        compiler_params=_params("arbitrary", "arbitrary"),
        name="dilated_attn",
    )(qkv0, qkv0, qkv0, qkv1, qkv1, qkv1, qkv2, qkv2, qkv2)


def _mem_scores(qm, kv_ref):
    rows = qm.shape[0]
    half0, half1 = _half_masks()
    scores = []
    for p in range(MEM_WIDTH // V7X_LANES):
        k = kv_ref[0, 0, :, V7X_LANES * p:V7X_LANES * (p + 1)]
        for c in range(rows // MEM_ROWS):
            q = qm[MEM_ROWS * c:MEM_ROWS * (c + 1), V7X_LANES * p:V7X_LANES * (p + 1)]
            scores.append(_dot_nt(jnp.concatenate([q * half0, q * half1], axis=0), k))
    return scores


def _mem_softmax_pv(scores, kv_ref):
    half0, half1 = _half_masks()
    n_pairs = MEM_WIDTH // V7X_LANES
    per_pair = len(scores) // n_pairs
    outs = []
    for p in range(n_pairs):
        v = kv_ref[0, 0, :, MEM_WIDTH + V7X_LANES * p:MEM_WIDTH + V7X_LANES * (p + 1)]
        ve = jnp.concatenate(
            [jnp.concatenate([v * half, jnp.broadcast_to(half, (N_MEM, V7X_LANES))], axis=1)
             for half in (half0, half1)], axis=0)
        chunks = []
        for s in scores[per_pair * p:per_pair * (p + 1)]:
            e = jnp.exp(s - jnp.max(s, axis=-1, keepdims=True)).astype(BF16)
            pv = _dot(jnp.concatenate([e[:MEM_ROWS], e[MEM_ROWS:]], axis=1), ve)
            chunks.append(pv[:, :V7X_LANES] / pv[:, V7X_LANES:])
        outs.append(jnp.concatenate(chunks, axis=0))
    return jnp.concatenate(outs, axis=1)


def _tail_kernel(x_ref, mix_ref, qm0_ref, z0_ref, kv0_ref, wo0_ref, g1_ref, w1_ref, cw_ref,
                 kv1_ref, wo1_ref, gf_ref, out_ref,
                 a_scr, y_scr, h1_scr, hb_scr, h1_old, *, tiles_per_seq):
    tm = x_ref.shape[1]
    halo = V7X_SUBLANES
    n_out = D_MODEL // COL_CHUNK
    step = pl.program_id(0)

    @pl.when(step == 0)
    def _():
        h1_scr[...] = jnp.zeros(h1_scr.shape, F32)
        hb_scr[...] = jnp.zeros(hb_scr.shape, BF16)
        a_scr[0:halo, :] = jnp.zeros((halo, CONV_WIDTH), F32)

    h1_old[...] = h1_scr[...]
    seq_start = (jnp.maximum(step - 1, 0) % tiles_per_seq) == 0
    a_scr[0:halo, :] = jnp.where(seq_start, 0.0, a_scr[0:halo, :])

    def proj1(c0, width=COL_CHUNK):
        return _dot(hb_scr[...], w1_ref[:, c0:c0 + width])

    def conv_chunk(j):
        cs = slice(COL_CHUNK * j, COL_CHUNK * (j + 1))
        a_now = proj1(B_CG + COL_CHUNK * j) * proj1(B_U + COL_CHUNK * j)
        a_scr[halo:halo + tm, cs] = a_now
        conv = (cw_ref[0:1, cs] * a_scr[halo - 2:halo - 2 + tm, cs]
                + cw_ref[1:2, cs] * a_scr[halo - 1:halo - 1 + tm, cs]
                + cw_ref[2:3, cs] * a_now)
        mix1 = proj1(B_BG + COL_CHUNK * j) * conv
        y_scr[:, cs] = (mix1 * _silu(proj1(B_Z + COL_CHUNK * j))).astype(BF16)

    scores0 = _mem_scores(qm0_ref[0], kv0_ref)
    qm1 = proj1(B_QM, MEM_WIDTH).astype(BF16)
    z_mem = proj1(B_Z + CONV_WIDTH, MEM_WIDTH)
    conv_chunk(0)
    scores1 = _mem_scores(qm1, kv1_ref)
    mem0 = _mem_softmax_pv(scores0, kv0_ref)
    z0 = z0_ref[0]
    y0 = jnp.concatenate(
        [mix_ref[0] * _silu(z0[:, :GROUP_WIDTH]), mem0 * _silu(z0[:, GROUP_WIDTH:])], axis=1).astype(BF16)
    conv_chunk(1)
    mem1 = _mem_softmax_pv(scores1, kv1_ref)
    y_scr[:, CONV_WIDTH:BRANCH_B] = (mem1 * _silu(z_mem)).astype(BF16)
    for j in range(n_out):
        cs = slice(COL_CHUNK * j, COL_CHUNK * (j + 1))
        h1_scr[:, cs] = x_ref[0, :, cs] + _dot(y0, wo0_ref[:, cs])
    conv_chunk(2)
    conv_chunk(3)
    a_scr[0:halo, :] = a_scr[tm:tm + halo, :]
    h1 = h1_scr[...]
    ms = jnp.mean(h1 * h1, axis=-1, keepdims=True)
    hb_scr[...] = (h1 * lax.rsqrt(ms + EPS) * g1_ref[...]).astype(BF16)

    y1 = y_scr[...]
    ssq = jnp.zeros((tm, 1), F32)
    for j in range(n_out):
        cs = slice(COL_CHUNK * j, COL_CHUNK * (j + 1))
        h2 = h1_old[:, cs] + _dot(y1, wo1_ref[:, cs])
        ssq = ssq + jnp.sum(h2 * h2, axis=-1, keepdims=True)
        out_ref[0, :, cs] = h2
    scale = lax.rsqrt(ssq * (1.0 / D_MODEL) + EPS)
    out_ref[0] = out_ref[0] * scale * gf_ref[...]


def _tail(x, mix, qm0, z0, kv, wo0, norm_g1, w1, conv_w, wo1, final_g):
    b, s, _ = x.shape
    tm = ROW_TILE
    per_seq = s // tm
    n_tiles = b * per_seq

    def tile_a(j):
        return jnp.minimum(j, n_tiles - 1)

    def tile_b(j):
        return jnp.maximum(j - 1, 0)

    def row(width, tile):
        return pl.BlockSpec((1, tm, width), lambda j: (tile(j) // per_seq, tile(j) % per_seq, 0))

    def kv_spec(layer, tile):
        return pl.BlockSpec((1, 1, N_MEM, 2 * MEM_WIDTH), lambda j: (layer, tile(j) // per_seq, 0, 0))

    const = lambda shape: pl.BlockSpec(shape, lambda j: (0, 0))
    tile_f32 = pltpu.VMEM((tm, D_MODEL), F32)
    tile_bf16 = pltpu.VMEM((tm, D_MODEL), BF16)
    return pl.pallas_call(
        functools.partial(_tail_kernel, tiles_per_seq=per_seq),
        grid=(n_tiles + 1,),
        in_specs=[
            row(D_MODEL, tile_a), row(GROUP_WIDTH, tile_a), row(MEM_WIDTH, tile_a), row(BRANCH_A, tile_a),
            kv_spec(0, tile_a),
            _resident(wo0.shape, lambda j: (0, 0)),
            const((1, D_MODEL)),
            _resident(w1.shape, lambda j: (0, 0)),
            const((3, CONV_WIDTH)),
            kv_spec(1, tile_b),
            _resident(wo1.shape, lambda j: (0, 0)),
            const((1, D_MODEL)),
        ],
        out_specs=row(D_MODEL, tile_b),
        out_shape=jax.ShapeDtypeStruct((b, s, D_MODEL), F32),
        scratch_shapes=[
            pltpu.VMEM((tm + V7X_SUBLANES, CONV_WIDTH), F32),
            pltpu.VMEM((tm, BRANCH_B), BF16),
            tile_f32, tile_bf16, tile_f32,
        ],
        compiler_params=_params("arbitrary"),
        name="tail",
    )(x, mix, qm0, z0, kv, wo0, norm_g1, w1, conv_w, kv, wo1, final_g)


def kernel(x, mem, positions, norm_g, mem_norm_g, w_mem_kv, attn_w_in, attn_w_out, conv_w_in, conv_w,
           conv_w_out, final_g):
    b, s, _ = x.shape
    w0 = (attn_w_in[0] * _A_SCALE).astype(BF16)
    w1 = (conv_w_in[0] * _B_SCALE).astype(BF16)
    wo0 = attn_w_out[0].astype(BF16)
    wo1 = conv_w_out[0].astype(BF16)
    w_kv = w_mem_kv.astype(BF16)

    inv_freq = ROPE_THETA ** (-jnp.arange(ROT_HALF, dtype=F32) * (2.0 / ROT_DIM))
    freq_rows = jnp.broadcast_to(inv_freq[:, None], (ROT_HALF, V7X_LANES))
    pos_rows = positions.reshape(b, 1, s)

    kv = _mem_kv(mem, mem_norm_g, w_kv)
    qkv0, qm0, z0, qkv1, qkv2 = _proj0(x, norm_g[0:1], pos_rows, freq_rows, w0)
    qkv1 = qkv1.reshape(b, s, QKV_G)
    qkv2 = qkv2.reshape(b, s, QKV_G)
    mix = _attn(qkv0, qkv1, qkv2)
    return _tail(x, mix, qm0, z0, kv, wo0, norm_g[1:2], w1, conv_w[0], wo1, final_g.reshape(1, D_MODEL))
```

```python
import functools

import numpy as np
import jax
import jax.numpy as jnp
from jax import lax
from jax.experimental import pallas as pl
from jax.experimental.pallas import tpu as pltpu

F32 = jnp.float32
BF16 = jnp.bfloat16

D_MODEL = 1024
HEAD_DIM = 64
ROT_DIM = 16
ROT_HALF = ROT_DIM // 2
ROPE_THETA = 500000.0
DILATIONS = (1, 4, 16)
BLOCK = 128
GROUP_WIDTH = 512
N_GROUPS = 3
N_MEM = 256
MEM_WIDTH = 256
CONV_WIDTH = 1024
EPS = 1e-6
SCORE_SCALE = HEAD_DIM ** -0.5 * float(np.log2(np.e))

V7X_LANES = 128
V7X_SUBLANES = 8
ROW_TILE = 512
COL_CHUNK = 256
VMEM_LIMIT_BYTES = 56 * 1024 * 1024
NEG_BIG = -1e30
MEM_ROWS = 256

QKV_G = 3 * GROUP_WIDTH
A_PART = N_GROUPS * GROUP_WIDTH
A_QM = 3 * A_PART
A_Z = A_QM + MEM_WIDTH
BRANCH_A = GROUP_WIDTH + MEM_WIDTH
IN_A_COLS = A_Z + BRANCH_A
B_BG, B_CG, B_U = 0, CONV_WIDTH, 2 * CONV_WIDTH
B_QM = 3 * CONV_WIDTH
B_Z = B_QM + MEM_WIDTH
BRANCH_B = CONV_WIDTH + MEM_WIDTH
IN_B_COLS = B_Z + BRANCH_B


def _column_scale(n_cols, scaled):
    scale = np.ones((n_cols,), np.float32)
    for lo, hi in scaled:
        scale[lo:hi] = SCORE_SCALE
    return scale


_A_SCALE = _column_scale(IN_A_COLS, [(0, A_PART), (A_QM, A_QM + MEM_WIDTH)])
_B_SCALE = _column_scale(IN_B_COLS, [(B_QM, B_QM + MEM_WIDTH)])


def _params(*sem):
    return pltpu.CompilerParams(dimension_semantics=sem, vmem_limit_bytes=VMEM_LIMIT_BYTES)


def _resident(shape, index_map):
    return pl.BlockSpec(shape, index_map, pipeline_mode=pl.Buffered(1))


def _silu(z):
    return z * jax.nn.sigmoid(z)


def _dot(a, b):
    return jnp.dot(a, b, preferred_element_type=F32)


def _dot_nt(a, b):
    return lax.dot_general(a, b, (((1,), (1,)), ((), ())), preferred_element_type=F32)


def _half_masks():
    lane = lax.broadcasted_iota(jnp.int32, (1, V7X_LANES), 1)
    return (jnp.where(lane < HEAD_DIM, 1.0, 0.0).astype(BF16),
            jnp.where(lane < HEAD_DIM, 0.0, 1.0).astype(BF16))


def _memkv_kernel(mem_ref, g_ref, w_ref, kv_ref):
    m = mem_ref[0]
    ms = jnp.mean(m * m, axis=-1, keepdims=True)
    mn = (m * lax.rsqrt(ms + EPS) * g_ref[0]).astype(BF16)
    kv_ref[0, 0] = _dot(mn, w_ref[0]).astype(BF16)


def _mem_kv(mem, mem_norm_g, w_kv_bf):
    b = mem.shape[0]
    depth = w_kv_bf.shape[0]
    return pl.pallas_call(
        _memkv_kernel,
        grid=(depth, b),
        in_specs=[
            pl.BlockSpec((1, N_MEM, D_MODEL), lambda l, i: (i, 0, 0)),
            pl.BlockSpec((1, 1, D_MODEL), lambda l, i: (l, 0, 0)),
            pl.BlockSpec((1, D_MODEL, 2 * MEM_WIDTH), lambda l, i: (l, 0, 0)),
        ],
        out_specs=pl.BlockSpec((1, 1, N_MEM, 2 * MEM_WIDTH), lambda l, i: (l, i, 0, 0)),
        out_shape=jax.ShapeDtypeStruct((depth, b, N_MEM, 2 * MEM_WIDTH), BF16),
        compiler_params=_params("parallel", "parallel"),
        name="mem_kv",
    )(mem, mem_norm_g.reshape(depth, 1, D_MODEL), w_kv_bf)


def _rope_tables(pos_ref, freq_ref, tbl_scr):
    rows = tbl_scr.shape[1]
    freq = freq_ref[...]
    one = jnp.ones((V7X_SUBLANES, V7X_LANES), F32)
    zero = jnp.zeros((V7X_SUBLANES, V7X_LANES), F32)
    groups = V7X_LANES // V7X_SUBLANES
    per_head = HEAD_DIM // V7X_SUBLANES

    def lane_rows(first, second, other):
        pieces = []
        for i in range(groups):
            pieces.append(first if i % per_head == 0 else second if i % per_head == 1 else other)
        return jnp.concatenate(pieces, axis=0).T

    for c in range(rows // V7X_LANES):
        sl = slice(V7X_LANES * c, V7X_LANES * (c + 1))
        ang = pos_ref[0, :, sl].astype(F32) * freq
        cs = jnp.cos(ang)
        sn = jnp.sin(ang)
        tbl_scr[0, sl, :] = lane_rows(cs, cs, one)
        tbl_scr[1, sl, :] = lane_rows(-sn, zero, zero)
        tbl_scr[2, sl, :] = lane_rows(zero, sn, zero)


def _rope(a, cs, sn_next, sn_prev):
    return (a * cs + pltpu.roll(a, V7X_LANES - ROT_HALF, 1) * sn_next
            + pltpu.roll(a, ROT_HALF, 1) * sn_prev)


def _proj0_kernel(x_ref, g_ref, pos_ref, freq_ref, w_ref,
                  o0_ref, oqm_ref, oz_ref, o1_ref, o2_ref, hn_scr, perm_scr, tbl_scr):
    tm = x_ref.shape[1]
    x = x_ref[0]
    ms = jnp.mean(x * x, axis=-1, keepdims=True)
    hn = x * lax.rsqrt(ms + EPS) * g_ref[...]
    hb = hn.astype(BF16)
    n_slab = D_MODEL // V7X_LANES
    for c in range(n_slab):
        hn_scr[c] = hn[:, V7X_LANES * c:V7X_LANES * (c + 1)]
    _rope_tables(pos_ref, freq_ref, tbl_scr)

    def qkv_chunk(lhs, g, j, tables):
        part, half = divmod(j, GROUP_WIDTH // COL_CHUNK)
        c0 = A_PART * part + GROUP_WIDTH * g + COL_CHUNK * half
        acc = _dot(lhs, w_ref[:, c0:c0 + COL_CHUNK])
        if part < 2:
            acc = jnp.concatenate(
                [_rope(acc[:, :V7X_LANES], *tables), _rope(acc[:, V7X_LANES:], *tables)], axis=1)
        return acc.astype(BF16)

    tables = tuple(tbl_scr[t] for t in range(3))
    for j in range(QKV_G // COL_CHUNK):
        o0_ref[0, :, COL_CHUNK * j:COL_CHUNK * (j + 1)] = qkv_chunk(hb, 0, j, tables)
    oqm_ref[0] = _dot(hb, w_ref[:, A_QM:A_QM + MEM_WIDTH]).astype(BF16)
    for j in range(BRANCH_A // COL_CHUNK):
        oz_ref[0, :, COL_CHUNK * j:COL_CHUNK * (j + 1)] = _dot(
            hb, w_ref[:, A_Z + COL_CHUNK * j:A_Z + COL_CHUNK * (j + 1)])

    for g, o_ref in ((1, o1_ref), (2, o2_ref)):
        d = DILATIONS[g]
        n = tm // d
        for r in range(d):
            for c in range(n_slab):
                perm_scr[r * n:(r + 1) * n, V7X_LANES * c:V7X_LANES * (c + 1)] = (
                    hn_scr[c, pl.ds(r, n, stride=d), :].astype(BF16))
        tables = tuple(
            jnp.concatenate([tbl_scr[t, pl.ds(r, n, stride=d), :] for r in range(d)], axis=0)
            for t in range(3))
        lhs = perm_scr[...]
        for j in range(QKV_G // COL_CHUNK):
            res = qkv_chunk(lhs, g, j, tables)
            for r in range(d):
                o_ref[0, r, :, COL_CHUNK * j:COL_CHUNK * (j + 1)] = res[r * n:(r + 1) * n]


def _proj0(x, norm_g, pos_rows, freq_rows, w0):
    b, s, _ = x.shape
    tm = ROW_TILE
    row = lambda width: pl.BlockSpec((1, tm, width), lambda i, j: (i, j, 0))
    stream = lambda d: pl.BlockSpec((1, d, tm // d, QKV_G), lambda i, j: (i, 0, j, 0))
    d1, d2 = DILATIONS[1], DILATIONS[2]
    return pl.pallas_call(
        _proj0_kernel,
        grid=(b, s // tm),
        in_specs=[
            row(D_MODEL),
            pl.BlockSpec((1, D_MODEL), lambda i, j: (0, 0)),
            pl.BlockSpec((1, 1, tm), lambda i, j: (i, 0, j)),
            pl.BlockSpec((V7X_SUBLANES, V7X_LANES), lambda i, j: (0, 0)),
            _resident((D_MODEL, IN_A_COLS), lambda i, j: (0, 0)),
        ],
        out_specs=[row(QKV_G), row(MEM_WIDTH), row(BRANCH_A), stream(d1), stream(d2)],
        out_shape=[
            jax.ShapeDtypeStruct((b, s, QKV_G), BF16),
            jax.ShapeDtypeStruct((b, s, MEM_WIDTH), BF16),
            jax.ShapeDtypeStruct((b, s, BRANCH_A), F32),
            jax.ShapeDtypeStruct((b, d1, s // d1, QKV_G), BF16),
            jax.ShapeDtypeStruct((b, d2, s // d2, QKV_G), BF16),
        ],
        scratch_shapes=[
            pltpu.VMEM((D_MODEL // V7X_LANES, tm, V7X_LANES), F32),
            pltpu.VMEM((tm, D_MODEL), BF16),
            pltpu.VMEM((3, tm, V7X_LANES), F32),
        ],
        compiler_params=_params("parallel", "parallel"),
        name="proj0",
    )(x, norm_g, pos_rows, freq_rows, w0)


def _attn_kernel(q0, k0, v0, q1, k1, v1, q2, k2, v2, o_ref,
                 num_scr, m_scr, l_scr, ve_scr, bias2_scr, bias1_scr):
    s_len = o_ref.shape[1]
    half0, half1 = _half_masks()
    first_half = lax.broadcasted_iota(jnp.int32, (BLOCK, V7X_LANES), 1) < HEAD_DIM

    @pl.when((pl.program_id(0) == 0) & (pl.program_id(1) == 0))
    def _():
        qi = lax.broadcasted_iota(jnp.int32, (2 * BLOCK, 2 * BLOCK), 0) & (BLOCK - 1)
        kj = lax.broadcasted_iota(jnp.int32, (2 * BLOCK, 2 * BLOCK), 1)
        valid_two = ((kj < BLOCK) & (kj >= qi)) | ((kj >= BLOCK) & (kj - BLOCK <= qi))
        bias2_scr[...] = jnp.where(valid_two, 0.0, NEG_BIG)
        valid_one = (lax.broadcasted_iota(jnp.int32, (2 * BLOCK, BLOCK), 1)
                     <= (lax.broadcasted_iota(jnp.int32, (2 * BLOCK, BLOCK), 0) & (BLOCK - 1)))
        bias1_scr[...] = jnp.where(valid_one, 0.0, NEG_BIG)
        for g in range(N_GROUPS):
            for h, half in enumerate((half0, half1)):
                ve_scr[g, h, :, V7X_LANES:] = jnp.broadcast_to(half, (s_len, V7X_LANES))

    for g, v_ref in enumerate((v0, v1, v2)):
        v = v_ref[0]
        for h, half in enumerate((half0, half1)):
            ve_scr[g, h, :, :V7X_LANES] = v * half

    def block(q_ref, k_ref, g, q_row, k_row, n_keys, bias_ref, dst):
        q = q_ref[0, pl.ds(q_row, BLOCK), :]
        qs = jnp.concatenate([q * half0, q * half1], axis=0)
        k = k_ref[0, pl.ds(k_row, n_keys), :]
        s = _dot_nt(qs, k) + bias_ref[...]
        m = jnp.max(s, axis=-1, keepdims=True)
        p = jnp.exp2(s - m).astype(BF16)
        p_cat = jnp.concatenate([p[:BLOCK], p[BLOCK:]], axis=1)
        keys = pl.ds(k_row, n_keys)
        ve = jnp.concatenate([ve_scr[g, 0, keys, :], ve_scr[g, 1, keys, :]], axis=0)
        pv = _dot(p_cat, ve)
        num_scr[g, dst, :] = pv[:, :V7X_LANES]
        l_scr[g, dst, :] = pv[:, V7X_LANES:]
        m_scr[g, dst, :] = jnp.where(first_half, m[:BLOCK], m[BLOCK:])

    d1, d2 = DILATIONS[1], DILATIONS[2]
    stream_len = s_len // d1
    sub = d2 // d1

    for r in range(d2):
        block(q2, k2, 2, r * BLOCK, r * BLOCK, BLOCK, bias1_scr,
              pl.ds((r % d1) * stream_len + r // d1, BLOCK, stride=sub))

    for r in range(d1):
        base = r * stream_len
        block(q1, k1, 1, base, base, BLOCK, bias1_scr, pl.ds(base, BLOCK))
        for n in range(1, stream_len // BLOCK):
            block(q1, k1, 1, base + n * BLOCK, base + (n - 1) * BLOCK, 2 * BLOCK, bias2_scr,
                  pl.ds(base + n * BLOCK, BLOCK))

    def merge(n):
        rows_per = BLOCK // d1
        for r in range(d1):
            tok = pl.ds(n * BLOCK + r, rows_per, stride=d1)
            rows = pl.ds(r * stream_len + n * rows_per, rows_per)
            idx = (tok, rows, rows)
            ms = [m_scr[g, idx[g], :] for g in range(N_GROUPS)]
            top = jnp.maximum(ms[0], jnp.maximum(ms[1], ms[2]))
            ws = [jnp.exp2(mg - top) for mg in ms]
            nums = [ws[g] * num_scr[g, idx[g], :] for g in range(N_GROUPS)]
            dens = [ws[g] * l_scr[g, idx[g], :] for g in range(N_GROUPS)]
            num = nums[0] + nums[1] + nums[2]
            den = dens[0] + dens[1] + dens[2]
            o_ref[0, tok, :] = num / den

    block(q0, k0, 0, 0, 0, BLOCK, bias1_scr, pl.ds(0, BLOCK))
    for n in range(1, s_len // BLOCK):
        block(q0, k0, 0, n * BLOCK, (n - 1) * BLOCK, 2 * BLOCK, bias2_scr, pl.ds(n * BLOCK, BLOCK))
        merge(n - 1)
    merge(s_len // BLOCK - 1)


def _attn(qkv0, qkv1, qkv2):
    b, s, _ = qkv0.shape
    n_pair = GROUP_WIDTH // V7X_LANES

    def part(k):
        return pl.BlockSpec((1, s, V7X_LANES), lambda i, p, k=k: (i, 0, k * n_pair + p))

    specs = [part(0), part(1), part(2)] * N_GROUPS
    stat = pltpu.VMEM((N_GROUPS, s, V7X_LANES), F32)
    scratch = [
        stat, stat, stat,
        pltpu.VMEM((N_GROUPS, 2, s, 2 * V7X_LANES), BF16),
        pltpu.VMEM((2 * BLOCK, 2 * BLOCK), F32),
        pltpu.VMEM((2 * BLOCK, BLOCK), F32),
    ]
    return pl.pallas_call(
        _attn_kernel,
        grid=(b, n_pair),
        in_specs=specs,
        out_specs=pl.BlockSpec((1, s, V7X_LANES), lambda i, p: (i, 0, p)),
        out_shape=jax.ShapeDtypeStruct((b, s, GROUP_WIDTH), F32),
        scratch_shapes=scratch,
        compiler_params=_params("arbitrary", "arbitrary"),
        name="dilated_attn",
    )(qkv0, qkv0, qkv0, qkv1, qkv1, qkv1, qkv2, qkv2, qkv2)


def _mem_scores(qm, kv_ref):
    rows = qm.shape[0]
    half0, half1 = _half_masks()
    scores = []
    for p in range(MEM_WIDTH // V7X_LANES):
        k = kv_ref[0, 0, :, V7X_LANES * p:V7X_LANES * (p + 1)]
        for c in range(rows // MEM_ROWS):
            q = qm[MEM_ROWS * c:MEM_ROWS * (c + 1), V7X_LANES * p:V7X_LANES * (p + 1)]
            scores.append(_dot_nt(jnp.concatenate([q * half0, q * half1], axis=0), k))
    return scores


def _mem_softmax_pv(scores, kv_ref):
    half0, half1 = _half_masks()
    n_pairs = MEM_WIDTH // V7X_LANES
    per_pair = len(scores) // n_pairs
    outs = []
    for p in range(n_pairs):
        v = kv_ref[0, 0, :, MEM_WIDTH + V7X_LANES * p:MEM_WIDTH + V7X_LANES * (p + 1)]
        ve = jnp.concatenate(
            [jnp.concatenate([v * half, jnp.broadcast_to(half, (N_MEM, V7X_LANES))], axis=1)
             for half in (half0, half1)], axis=0)
        chunks = []
        for s in scores[per_pair * p:per_pair * (p + 1)]:
            e = jnp.exp2(s - jnp.max(s, axis=-1, keepdims=True)).astype(BF16)
            pv = _dot(jnp.concatenate([e[:MEM_ROWS], e[MEM_ROWS:]], axis=1), ve)
            chunks.append(pv[:, :V7X_LANES] / pv[:, V7X_LANES:])
        outs.append(jnp.concatenate(chunks, axis=0))
    return jnp.concatenate(outs, axis=1)


def _tail_kernel(x_ref, mix_ref, qm0_ref, z0_ref, kv0_ref, wo0_ref, g1_ref, w1_ref, cw_ref,
                 kv1_ref, wo1_ref, gf_ref, out_ref,
                 a_scr, y_scr, h1_scr, hb_scr, h1_old, *, tiles_per_seq):
    tm = x_ref.shape[1]
    halo = V7X_SUBLANES
    n_out = D_MODEL // COL_CHUNK
    step = pl.program_id(0)

    @pl.when(step == 0)
    def _():
        h1_scr[...] = jnp.zeros(h1_scr.shape, F32)
        hb_scr[...] = jnp.zeros(hb_scr.shape, BF16)
        a_scr[0:halo, :] = jnp.zeros((halo, CONV_WIDTH), F32)

    h1_old[...] = h1_scr[...]
    seq_start = (jnp.maximum(step - 1, 0) % tiles_per_seq) == 0
    a_scr[0:halo, :] = jnp.where(seq_start, 0.0, a_scr[0:halo, :])

    def proj1(c0, width=COL_CHUNK):
        return _dot(hb_scr[...], w1_ref[:, c0:c0 + width])

    def conv_chunk(j):
        cs = slice(COL_CHUNK * j, COL_CHUNK * (j + 1))
        a_now = proj1(B_CG + COL_CHUNK * j) * proj1(B_U + COL_CHUNK * j)
        a_scr[halo:halo + tm, cs] = a_now
        conv = (cw_ref[0:1, cs] * a_scr[halo - 2:halo - 2 + tm, cs]
                + cw_ref[1:2, cs] * a_scr[halo - 1:halo - 1 + tm, cs]
                + cw_ref[2:3, cs] * a_now)
        mix1 = proj1(B_BG + COL_CHUNK * j) * conv
        y_scr[:, cs] = (mix1 * _silu(proj1(B_Z + COL_CHUNK * j))).astype(BF16)

    scores0 = _mem_scores(qm0_ref[0], kv0_ref)
    qm1 = proj1(B_QM, MEM_WIDTH).astype(BF16)
    z_mem = proj1(B_Z + CONV_WIDTH, MEM_WIDTH)
    conv_chunk(0)
    scores1 = _mem_scores(qm1, kv1_ref)
    mem0 = _mem_softmax_pv(scores0, kv0_ref)
    z0 = z0_ref[0]
    y0 = jnp.concatenate(
        [mix_ref[0] * _silu(z0[:, :GROUP_WIDTH]), mem0 * _silu(z0[:, GROUP_WIDTH:])], axis=1).astype(BF16)
    conv_chunk(1)
    mem1 = _mem_softmax_pv(scores1, kv1_ref)
    y_scr[:, CONV_WIDTH:BRANCH_B] = (mem1 * _silu(z_mem)).astype(BF16)
    for j in range(n_out):
        cs = slice(COL_CHUNK * j, COL_CHUNK * (j + 1))
        h1_scr[:, cs] = x_ref[0, :, cs] + _dot(y0, wo0_ref[:, cs])
    conv_chunk(2)
    conv_chunk(3)
    a_scr[0:halo, :] = a_scr[tm:tm + halo, :]
    h1 = h1_scr[...]
    ms = jnp.mean(h1 * h1, axis=-1, keepdims=True)
    hb_scr[...] = (h1 * lax.rsqrt(ms + EPS) * g1_ref[...]).astype(BF16)

    y1 = y_scr[...]
    ssq = jnp.zeros((tm, 1), F32)
    for j in range(n_out):
        cs = slice(COL_CHUNK * j, COL_CHUNK * (j + 1))
        h2 = h1_old[:, cs] + _dot(y1, wo1_ref[:, cs])
        ssq = ssq + jnp.sum(h2 * h2, axis=-1, keepdims=True)
        out_ref[0, :, cs] = h2
    scale = lax.rsqrt(ssq * (1.0 / D_MODEL) + EPS)
    out_ref[0] = out_ref[0] * scale * gf_ref[...]


def _tail(x, mix, qm0, z0, kv, wo0, norm_g1, w1, conv_w, wo1, final_g):
    b, s, _ = x.shape
    tm = ROW_TILE
    per_seq = s // tm
    n_tiles = b * per_seq

    def tile_a(j):
        return jnp.minimum(j, n_tiles - 1)

    def tile_b(j):
        return jnp.maximum(j - 1, 0)

    def row(width, tile):
        return pl.BlockSpec((1, tm, width), lambda j: (tile(j) // per_seq, tile(j) % per_seq, 0))

    def kv_spec(layer, tile):
        return pl.BlockSpec((1, 1, N_MEM, 2 * MEM_WIDTH), lambda j: (layer, tile(j) // per_seq, 0, 0))

    const = lambda shape: pl.BlockSpec(shape, lambda j: (0, 0))
    tile_f32 = pltpu.VMEM((tm, D_MODEL), F32)
    tile_bf16 = pltpu.VMEM((tm, D_MODEL), BF16)
    return pl.pallas_call(
        functools.partial(_tail_kernel, tiles_per_seq=per_seq),
        grid=(n_tiles + 1,),
        in_specs=[
            row(D_MODEL, tile_a), row(GROUP_WIDTH, tile_a), row(MEM_WIDTH, tile_a), row(BRANCH_A, tile_a),
            kv_spec(0, tile_a),
            _resident(wo0.shape, lambda j: (0, 0)),
            const((1, D_MODEL)),
            _resident(w1.shape, lambda j: (0, 0)),
            const((3, CONV_WIDTH)),
            kv_spec(1, tile_b),
            _resident(wo1.shape, lambda j: (0, 0)),
            const((1, D_MODEL)),
        ],
        out_specs=row(D_MODEL, tile_b),
        out_shape=jax.ShapeDtypeStruct((b, s, D_MODEL), F32),
        scratch_shapes=[
            pltpu.VMEM((tm + V7X_SUBLANES, CONV_WIDTH), F32),
            pltpu.VMEM((tm, BRANCH_B), BF16),
            tile_f32, tile_bf16, tile_f32,
        ],
        compiler_params=_params("arbitrary"),
        name="tail",
    )(x, mix, qm0, z0, kv, wo0, norm_g1, w1, conv_w, kv, wo1, final_g)


def kernel(x, mem, positions, norm_g, mem_norm_g, w_mem_kv, attn_w_in, attn_w_out, conv_w_in, conv_w,
           conv_w_out, final_g):
    b, s, _ = x.shape
    w0 = (attn_w_in[0] * _A_SCALE).astype(BF16)
    w1 = (conv_w_in[0] * _B_SCALE).astype(BF16)
    wo0 = attn_w_out[0].astype(BF16)
    wo1 = conv_w_out[0].astype(BF16)
    w_kv = w_mem_kv.astype(BF16)

    inv_freq = ROPE_THETA ** (-jnp.arange(ROT_HALF, dtype=F32) * (2.0 / ROT_DIM))
    freq_rows = jnp.broadcast_to(inv_freq[:, None], (ROT_HALF, V7X_LANES))
    pos_rows = positions.reshape(b, 1, s)

    kv = _mem_kv(mem, mem_norm_g, w_kv)
    qkv0, qm0, z0, qkv1, qkv2 = _proj0(x, norm_g[0:1], pos_rows, freq_rows, w0)
    qkv1 = qkv1.reshape(b, s, QKV_G)
    qkv2 = qkv2.reshape(b, s, QKV_G)
    mix = _attn(qkv0, qkv1, qkv2)
    return _tail(x, mix, qm0, z0, kv, wo0, norm_g[1:2], w1, conv_w[0], wo1, final_g.reshape(1, D_MODEL))
```

```python
import functools

import numpy as np
import jax
import jax.numpy as jnp
from jax import lax
from jax.experimental import pallas as pl
from jax.experimental.pallas import tpu as pltpu

F32 = jnp.float32
BF16 = jnp.bfloat16

D_MODEL = 1024
HEAD_DIM = 64
ROT_DIM = 16
ROT_HALF = ROT_DIM // 2
ROPE_THETA = 500000.0
DILATIONS = (1, 4, 16)
BLOCK = 128
GROUP_WIDTH = 512
N_GROUPS = 3
N_MEM = 256
MEM_WIDTH = 256
CONV_WIDTH = 1024
EPS = 1e-6
SCORE_SCALE = HEAD_DIM ** -0.5 * float(np.log2(np.e))

V7X_LANES = 128
V7X_SUBLANES = 8
BF16_SUBLANES = 16
ROW_TILE = 512
COL_CHUNK = 256
VMEM_LIMIT_BYTES = 56 * 1024 * 1024
NEG_BIG = -1e30
MEM_ROWS = 256

QKV_G = 3 * GROUP_WIDTH
A_PART = N_GROUPS * GROUP_WIDTH
A_QM = 3 * A_PART
A_Z = A_QM + MEM_WIDTH
BRANCH_A = GROUP_WIDTH + MEM_WIDTH
IN_A_COLS = A_Z + BRANCH_A
B_BG, B_CG, B_U = 0, CONV_WIDTH, 2 * CONV_WIDTH
B_QM = 3 * CONV_WIDTH
B_Z = B_QM + MEM_WIDTH
BRANCH_B = CONV_WIDTH + MEM_WIDTH
IN_B_COLS = B_Z + BRANCH_B


def _column_scale(n_cols, scaled):
    scale = np.ones((n_cols,), np.float32)
    for lo, hi in scaled:
        scale[lo:hi] = SCORE_SCALE
    return scale


_A_SCALE = _column_scale(IN_A_COLS, [(0, A_PART), (A_QM, A_QM + MEM_WIDTH)])


def _params(*sem):
    return pltpu.CompilerParams(dimension_semantics=sem, vmem_limit_bytes=VMEM_LIMIT_BYTES)


def _resident(shape, index_map):
    return pl.BlockSpec(shape, index_map, pipeline_mode=pl.Buffered(1))


def _silu(z):
    return z * jax.nn.sigmoid(z)


def _dot(a, b):
    return jnp.dot(a, b, preferred_element_type=F32)


def _dot_nt(a, b):
    return lax.dot_general(a, b, (((1,), (1,)), ((), ())), preferred_element_type=F32)


def _half_masks():
    lane = lax.broadcasted_iota(jnp.int32, (1, V7X_LANES), 1)
    return (jnp.where(lane < HEAD_DIM, 1.0, 0.0).astype(BF16),
            jnp.where(lane < HEAD_DIM, 0.0, 1.0).astype(BF16))


def _memkv_kernel(mem_ref, g_ref, w_ref, kv_ref):
    w = w_ref[0].astype(BF16)
    for c in range(mem_ref.shape[0] // ROW_TILE):
        rows = slice(ROW_TILE * c, ROW_TILE * (c + 1))
        m = mem_ref[rows, :]
        ms = jnp.mean(m * m, axis=-1, keepdims=True)
        mn = (m * lax.rsqrt(ms + EPS) * g_ref[0]).astype(BF16)
        kv_ref[0, rows, :] = _dot(mn, w).astype(BF16)


def _mem_kv(mem, mem_norm_g, w_mem_kv):
    b = mem.shape[0]
    depth = w_mem_kv.shape[0]
    rows = b * N_MEM
    kv = pl.pallas_call(
        _memkv_kernel,
        grid=(depth,),
        in_specs=[
            _resident((rows, D_MODEL), lambda l: (0, 0)),
            pl.BlockSpec((1, 1, D_MODEL), lambda l: (l, 0, 0)),
            pl.BlockSpec((1, D_MODEL, 2 * MEM_WIDTH), lambda l: (l, 0, 0)),
        ],
        out_specs=pl.BlockSpec((1, rows, 2 * MEM_WIDTH), lambda l: (l, 0, 0)),
        out_shape=jax.ShapeDtypeStruct((depth, rows, 2 * MEM_WIDTH), BF16),
        compiler_params=_params("arbitrary"),
        name="mem_kv",
    )(mem.reshape(rows, D_MODEL), mem_norm_g.reshape(depth, 1, D_MODEL), w_mem_kv)
    return kv.reshape(depth, b, N_MEM, 2 * MEM_WIDTH)


def _rope_tables(pos_ref, freq_ref, tbl_scr):
    rows = tbl_scr.shape[1]
    freq = freq_ref[...]
    one = jnp.ones((V7X_SUBLANES, V7X_LANES), F32)
    zero = jnp.zeros((V7X_SUBLANES, V7X_LANES), F32)
    groups = V7X_LANES // V7X_SUBLANES
    per_head = HEAD_DIM // V7X_SUBLANES

    def lane_rows(first, second, other):
        pieces = []
        for i in range(groups):
            pieces.append(first if i % per_head == 0 else second if i % per_head == 1 else other)
        return jnp.concatenate(pieces, axis=0).T

    for c in range(rows // V7X_LANES):
        sl = slice(V7X_LANES * c, V7X_LANES * (c + 1))
        ang = pos_ref[0, :, sl].astype(F32) * freq
        cs = jnp.cos(ang)
        sn = jnp.sin(ang)
        tbl_scr[0, sl, :] = lane_rows(cs, cs, one)
        tbl_scr[1, sl, :] = lane_rows(-sn, zero, zero)
        tbl_scr[2, sl, :] = lane_rows(zero, sn, zero)


def _rope(a, cs, sn_next, sn_prev):
    return (a * cs + pltpu.roll(a, V7X_LANES - ROT_HALF, 1) * sn_next
            + pltpu.roll(a, ROT_HALF, 1) * sn_prev)


def _proj0_kernel(x_ref, g_ref, pos_ref, freq_ref, w_ref, w1_ref, wo0_ref, wo1_ref,
                  o0_ref, oqm_ref, oz_ref, o1_ref, o2_ref, w1b_ref, wo0b_ref, wo1b_ref,
                  hn_scr, perm_scr, tbl_scr):
    tm = x_ref.shape[1]
    x = x_ref[0]
    ms = jnp.mean(x * x, axis=-1, keepdims=True)
    hn = x * lax.rsqrt(ms + EPS) * g_ref[...]
    hb = hn.astype(BF16)
    n_slab = D_MODEL // V7X_LANES
    for c in range(n_slab):
        hn_scr[c] = hn[:, V7X_LANES * c:V7X_LANES * (c + 1)]
    _rope_tables(pos_ref, freq_ref, tbl_scr)

    def qkv_chunk(lhs, g, j, tables):
        part, half = divmod(j, GROUP_WIDTH // COL_CHUNK)
        c0 = A_PART * part + GROUP_WIDTH * g + COL_CHUNK * half
        acc = _dot(lhs, w_ref[:, c0:c0 + COL_CHUNK])
        if part < 2:
            acc = jnp.concatenate(
                [_rope(acc[:, :V7X_LANES], *tables), _rope(acc[:, V7X_LANES:], *tables)], axis=1)
        return acc.astype(BF16)

    tables = tuple(tbl_scr[t] for t in range(3))
    for j in range(QKV_G // COL_CHUNK):
        o0_ref[0, :, COL_CHUNK * j:COL_CHUNK * (j + 1)] = qkv_chunk(hb, 0, j, tables)
    oqm_ref[0] = _dot(hb, w_ref[:, A_QM:A_QM + MEM_WIDTH]).astype(BF16)
    for j in range(BRANCH_A // COL_CHUNK):
        oz_ref[0, :, COL_CHUNK * j:COL_CHUNK * (j + 1)] = _dot(
            hb, w_ref[:, A_Z + COL_CHUNK * j:A_Z + COL_CHUNK * (j + 1)])

    for g, o_ref in ((1, o1_ref), (2, o2_ref)):
        d = DILATIONS[g]
        n = tm // d
        for r in range(d):
            for c in range(n_slab):
                perm_scr[r * n:(r + 1) * n, V7X_LANES * c:V7X_LANES * (c + 1)] = (
                    hn_scr[c, pl.ds(r, n, stride=d), :].astype(BF16))
        tables = tuple(
            jnp.concatenate([tbl_scr[t, pl.ds(r, n, stride=d), :] for r in range(d)], axis=0)
            for t in range(3))
        lhs = perm_scr[...]
        for j in range(QKV_G // COL_CHUNK):
            res = qkv_chunk(lhs, g, j, tables)
            for r in range(d):
                o_ref[0, r, :, COL_CHUNK * j:COL_CHUNK * (j + 1)] = res[r * n:(r + 1) * n]

    w1b_ref[:, :B_QM] = w1_ref[:, :B_QM].astype(BF16)
    w1b_ref[:, B_QM:B_Z] = (w1_ref[:, B_QM:B_Z] * SCORE_SCALE).astype(BF16)
    w1b_ref[:, B_Z:] = w1_ref[:, B_Z:].astype(BF16)
    wo0b_ref[...] = wo0_ref[...].astype(BF16)
    wo1b_ref[...] = wo1_ref[...].astype(BF16)


def _proj0(x, norm_g, pos_rows, freq_rows, w0, w1, wo0, wo1):
    b, s, _ = x.shape
    tm = ROW_TILE
    per_seq = s // tm
    n_steps = b * per_seq
    row = lambda width: pl.BlockSpec((1, tm, width), lambda i, j: (i, j, 0))
    stream = lambda d: pl.BlockSpec((1, d, tm // d, QKV_G), lambda i, j: (i, 0, j, 0))

    def slab(w, steps):
        rows = w.shape[0] // steps
        assert rows * steps == w.shape[0] and rows % BF16_SUBLANES == 0
        return pl.BlockSpec((rows, w.shape[1]), lambda i, j: (jnp.minimum(i * per_seq + j, steps - 1), 0))

    slabs = [slab(w1, n_steps), slab(wo0, n_steps // 2), slab(wo1, n_steps // 2)]
    d1, d2 = DILATIONS[1], DILATIONS[2]
    return pl.pallas_call(
        _proj0_kernel,
        grid=(b, per_seq),
        in_specs=[
            row(D_MODEL),
            pl.BlockSpec((1, D_MODEL), lambda i, j: (0, 0)),
            pl.BlockSpec((1, 1, tm), lambda i, j: (i, 0, j)),
            pl.BlockSpec((V7X_SUBLANES, V7X_LANES), lambda i, j: (0, 0)),
            _resident((D_MODEL, IN_A_COLS), lambda i, j: (0, 0)),
        ] + slabs,
        out_specs=[row(QKV_G), row(MEM_WIDTH), row(BRANCH_A), stream(d1), stream(d2)] + slabs,
        out_shape=[
            jax.ShapeDtypeStruct((b, s, QKV_G), BF16),
            jax.ShapeDtypeStruct((b, s, MEM_WIDTH), BF16),
            jax.ShapeDtypeStruct((b, s, BRANCH_A), F32),
            jax.ShapeDtypeStruct((b, d1, s // d1, QKV_G), BF16),
            jax.ShapeDtypeStruct((b, d2, s // d2, QKV_G), BF16),
            jax.ShapeDtypeStruct(w1.shape, BF16),
            jax.ShapeDtypeStruct(wo0.shape, BF16),
            jax.ShapeDtypeStruct(wo1.shape, BF16),
        ],
        scratch_shapes=[
            pltpu.VMEM((D_MODEL // V7X_LANES, tm, V7X_LANES), F32),
            pltpu.VMEM((tm, D_MODEL), BF16),
            pltpu.VMEM((3, tm, V7X_LANES), F32),
        ],
        compiler_params=_params("arbitrary", "arbitrary"),
        name="proj0",
    )(x, norm_g, pos_rows, freq_rows, w0, w1, wo0, wo1)


def _attn_kernel(q0, k0, v0, q1, k1, v1, q2, k2, v2, o_ref,
                 num_scr, m_scr, l_scr, ve_scr, bias2_scr, bias1_scr):
    s_len = o_ref.shape[1]
    half0, half1 = _half_masks()
    first_half = lax.broadcasted_iota(jnp.int32, (BLOCK, V7X_LANES), 1) < HEAD_DIM

    @pl.when((pl.program_id(0) == 0) & (pl.program_id(1) == 0))
    def _():
        qi = lax.broadcasted_iota(jnp.int32, (2 * BLOCK, 2 * BLOCK), 0) & (BLOCK - 1)
        kj = lax.broadcasted_iota(jnp.int32, (2 * BLOCK, 2 * BLOCK), 1)
        valid_two = ((kj < BLOCK) & (kj >= qi)) | ((kj >= BLOCK) & (kj - BLOCK <= qi))
        bias2_scr[...] = jnp.where(valid_two, 0.0, NEG_BIG)
        valid_one = (lax.broadcasted_iota(jnp.int32, (2 * BLOCK, BLOCK), 1)
                     <= (lax.broadcasted_iota(jnp.int32, (2 * BLOCK, BLOCK), 0) & (BLOCK - 1)))
        bias1_scr[...] = jnp.where(valid_one, 0.0, NEG_BIG)
        for g in range(N_GROUPS):
            for h, half in enumerate((half0, half1)):
                ve_scr[g, h, :, V7X_LANES:] = jnp.broadcast_to(half, (s_len, V7X_LANES))

    for g, v_ref in enumerate((v0, v1, v2)):
        v = v_ref[0]
        for h, half in enumerate((half0, half1)):
            ve_scr[g, h, :, :V7X_LANES] = v * half

    def block(q_ref, k_ref, g, q_row, k_row, n_keys, bias_ref, dst):
        q = q_ref[0, pl.ds(q_row, BLOCK), :]
        qs = jnp.concatenate([q * half0, q * half1], axis=0)
        k = k_ref[0, pl.ds(k_row, n_keys), :]
        s = _dot_nt(qs, k) + bias_ref[...]
        m = jnp.max(s, axis=-1, keepdims=True)
        p = jnp.exp2(s - m).astype(BF16)
        p_cat = jnp.concatenate([p[:BLOCK], p[BLOCK:]], axis=1)
        keys = pl.ds(k_row, n_keys)
        ve = jnp.concatenate([ve_scr[g, 0, keys, :], ve_scr[g, 1, keys, :]], axis=0)
        pv = _dot(p_cat, ve)
        num_scr[g, dst, :] = pv[:, :V7X_LANES]
        l_scr[g, dst, :] = pv[:, V7X_LANES:]
        m_scr[g, dst, :] = jnp.where(first_half, m[:BLOCK], m[BLOCK:])

    d1, d2 = DILATIONS[1], DILATIONS[2]
    stream_len = s_len // d1
    sub = d2 // d1

    for r in range(d2):
        block(q2, k2, 2, r * BLOCK, r * BLOCK, BLOCK, bias1_scr,
              pl.ds((r % d1) * stream_len + r // d1, BLOCK, stride=sub))

    for r in range(d1):
        base = r * stream_len
        block(q1, k1, 1, base, base, BLOCK, bias1_scr, pl.ds(base, BLOCK))
        for n in range(1, stream_len // BLOCK):
            block(q1, k1, 1, base + n * BLOCK, base + (n - 1) * BLOCK, 2 * BLOCK, bias2_scr,
                  pl.ds(base + n * BLOCK, BLOCK))

    def merge(n):
        rows_per = BLOCK // d1
        for r in range(d1):
            tok = pl.ds(n * BLOCK + r, rows_per, stride=d1)
            rows = pl.ds(r * stream_len + n * rows_per, rows_per)
            idx = (tok, rows, rows)
            ms = [m_scr[g, idx[g], :] for g in range(N_GROUPS)]
            top = jnp.maximum(ms[0], jnp.maximum(ms[1], ms[2]))
            ws = [jnp.exp2(mg - top) for mg in ms]
            nums = [ws[g] * num_scr[g, idx[g], :] for g in range(N_GROUPS)]
            dens = [ws[g] * l_scr[g, idx[g], :] for g in range(N_GROUPS)]
            num = nums[0] + nums[1] + nums[2]
            den = dens[0] + dens[1] + dens[2]
            o_ref[0, tok, :] = num / den

    block(q0, k0, 0, 0, 0, BLOCK, bias1_scr, pl.ds(0, BLOCK))
    for n in range(1, s_len // BLOCK):
        block(q0, k0, 0, n * BLOCK, (n - 1) * BLOCK, 2 * BLOCK, bias2_scr, pl.ds(n * BLOCK, BLOCK))
        merge(n - 1)
    merge(s_len // BLOCK - 1)


def _attn(qkv0, qkv1, qkv2):
    b, s, _ = qkv0.shape
    n_pair = GROUP_WIDTH // V7X_LANES

    def part(k):
        return pl.BlockSpec((1, s, V7X_LANES), lambda i, p, k=k: (i, 0, k * n_pair + p))

    specs = [part(0), part(1), part(2)] * N_GROUPS
    stat = pltpu.VMEM((N_GROUPS, s, V7X_LANES), F32)
    scratch = [
        stat, stat, stat,
        pltpu.VMEM((N_GROUPS, 2, s, 2 * V7X_LANES), BF16),
        pltpu.VMEM((2 * BLOCK, 2 * BLOCK), F32),
        pltpu.VMEM((2 * BLOCK, BLOCK), F32),
    ]
    return pl.pallas_call(
        _attn_kernel,
        grid=(b, n_pair),
        in_specs=specs,
        out_specs=pl.BlockSpec((1, s, V7X_LANES), lambda i, p: (i, 0, p)),
        out_shape=jax.ShapeDtypeStruct((b, s, GROUP_WIDTH), F32),
        scratch_shapes=scratch,
        compiler_params=_params("arbitrary", "arbitrary"),
        name="dilated_attn",
    )(qkv0, qkv0, qkv0, qkv1, qkv1, qkv1, qkv2, qkv2, qkv2)


def _mem_scores(qm, kv_ref):
    rows = qm.shape[0]
    half0, half1 = _half_masks()
    scores = []
    for p in range(MEM_WIDTH // V7X_LANES):
        k = kv_ref[0, 0, :, V7X_LANES * p:V7X_LANES * (p + 1)]
        for c in range(rows // MEM_ROWS):
            q = qm[MEM_ROWS * c:MEM_ROWS * (c + 1), V7X_LANES * p:V7X_LANES * (p + 1)]
            scores.append(_dot_nt(jnp.concatenate([q * half0, q * half1], axis=0), k))
    return scores


def _mem_softmax_pv(scores, kv_ref):
    half0, half1 = _half_masks()
    n_pairs = MEM_WIDTH // V7X_LANES
    per_pair = len(scores) // n_pairs
    outs = []
    for p in range(n_pairs):
        v = kv_ref[0, 0, :, MEM_WIDTH + V7X_LANES * p:MEM_WIDTH + V7X_LANES * (p + 1)]
        ve = jnp.concatenate(
            [jnp.concatenate([v * half, jnp.broadcast_to(half, (N_MEM, V7X_LANES))], axis=1)
             for half in (half0, half1)], axis=0)
        chunks = []
        for s in scores[per_pair * p:per_pair * (p + 1)]:
            e = jnp.exp2(s - jnp.max(s, axis=-1, keepdims=True)).astype(BF16)
            pv = _dot(jnp.concatenate([e[:MEM_ROWS], e[MEM_ROWS:]], axis=1), ve)
            chunks.append(pv[:, :V7X_LANES] / pv[:, V7X_LANES:])
        outs.append(jnp.concatenate(chunks, axis=0))
    return jnp.concatenate(outs, axis=1)


def _tail_kernel(x_ref, mix_ref, qm0_ref, z0_ref, kv0_ref, wo0_ref, g1_ref, w1_ref, cw_ref,
                 kv1_ref, wo1_ref, gf_ref, out_ref,
                 a_scr, y_scr, h1_scr, hb_scr, h1_old, *, tiles_per_seq):
    tm = x_ref.shape[1]
    halo = V7X_SUBLANES
    n_out = D_MODEL // COL_CHUNK
    step = pl.program_id(0)

    @pl.when(step == 0)
    def _():
        h1_scr[...] = jnp.zeros(h1_scr.shape, F32)
        hb_scr[...] = jnp.zeros(hb_scr.shape, BF16)
        a_scr[0:halo, :] = jnp.zeros((halo, CONV_WIDTH), F32)

    h1_old[...] = h1_scr[...]
    seq_start = (jnp.maximum(step - 1, 0) % tiles_per_seq) == 0
    a_scr[0:halo, :] = jnp.where(seq_start, 0.0, a_scr[0:halo, :])

    def proj1(c0, width=COL_CHUNK):
        return _dot(hb_scr[...], w1_ref[:, c0:c0 + width])

    def conv_chunk(j):
        cs = slice(COL_CHUNK * j, COL_CHUNK * (j + 1))
        a_now = proj1(B_CG + COL_CHUNK * j) * proj1(B_U + COL_CHUNK * j)
        a_scr[halo:halo + tm, cs] = a_now
        conv = (cw_ref[0:1, cs] * a_scr[halo - 2:halo - 2 + tm, cs]
                + cw_ref[1:2, cs] * a_scr[halo - 1:halo - 1 + tm, cs]
                + cw_ref[2:3, cs] * a_now)
        mix1 = proj1(B_BG + COL_CHUNK * j) * conv
        y_scr[:, cs] = (mix1 * _silu(proj1(B_Z + COL_CHUNK * j))).astype(BF16)

    scores0 = _mem_scores(qm0_ref[0], kv0_ref)
    qm1 = proj1(B_QM, MEM_WIDTH).astype(BF16)
    z_mem = proj1(B_Z + CONV_WIDTH, MEM_WIDTH)
    conv_chunk(0)
    scores1 = _mem_scores(qm1, kv1_ref)
    mem0 = _mem_softmax_pv(scores0, kv0_ref)
    z0 = z0_ref[0]
    y0 = jnp.concatenate(
        [mix_ref[0] * _silu(z0[:, :GROUP_WIDTH]), mem0 * _silu(z0[:, GROUP_WIDTH:])], axis=1).astype(BF16)
    conv_chunk(1)
    mem1 = _mem_softmax_pv(scores1, kv1_ref)
    y_scr[:, CONV_WIDTH:BRANCH_B] = (mem1 * _silu(z_mem)).astype(BF16)
    for j in range(n_out):
        cs = slice(COL_CHUNK * j, COL_CHUNK * (j + 1))
        h1_scr[:, cs] = x_ref[0, :, cs] + _dot(y0, wo0_ref[:, cs])
    conv_chunk(2)
    conv_chunk(3)
    a_scr[0:halo, :] = a_scr[tm:tm + halo, :]
    h1 = h1_scr[...]
    ms = jnp.mean(h1 * h1, axis=-1, keepdims=True)
    hb_scr[...] = (h1 * lax.rsqrt(ms + EPS) * g1_ref[...]).astype(BF16)

    y1 = y_scr[...]
    ssq = jnp.zeros((tm, 1), F32)
    for j in range(n_out):
        cs = slice(COL_CHUNK * j, COL_CHUNK * (j + 1))
        h2 = h1_old[:, cs] + _dot(y1, wo1_ref[:, cs])
        ssq = ssq + jnp.sum(h2 * h2, axis=-1, keepdims=True)
        out_ref[0, :, cs] = h2
    scale = lax.rsqrt(ssq * (1.0 / D_MODEL) + EPS)
    out_ref[0] = out_ref[0] * scale * gf_ref[...]


def _tail(x, mix, qm0, z0, kv, wo0, norm_g1, w1, conv_w, wo1, final_g):
    b, s, _ = x.shape
    tm = ROW_TILE
    per_seq = s // tm
    n_tiles = b * per_seq

    def tile_a(j):
        return jnp.minimum(j, n_tiles - 1)

    def tile_b(j):
        return jnp.maximum(j - 1, 0)

    def row(width, tile):
        return pl.BlockSpec((1, tm, width), lambda j: (tile(j) // per_seq, tile(j) % per_seq, 0))

    def kv_spec(layer, tile):
        return pl.BlockSpec((1, 1, N_MEM, 2 * MEM_WIDTH), lambda j: (layer, tile(j) // per_seq, 0, 0))

    const = lambda shape: pl.BlockSpec(shape, lambda j: (0, 0))
    tile_f32 = pltpu.VMEM((tm, D_MODEL), F32)
    tile_bf16 = pltpu.VMEM((tm, D_MODEL), BF16)
    return pl.pallas_call(
        functools.partial(_tail_kernel, tiles_per_seq=per_seq),
        grid=(n_tiles + 1,),
        in_specs=[
            row(D_MODEL, tile_a), row(GROUP_WIDTH, tile_a), row(MEM_WIDTH, tile_a), row(BRANCH_A, tile_a),
            kv_spec(0, tile_a),
            _resident(wo0.shape, lambda j: (0, 0)),
            const((1, D_MODEL)),
            _resident(w1.shape, lambda j: (0, 0)),
            const((3, CONV_WIDTH)),
            kv_spec(1, tile_b),
            _resident(wo1.shape, lambda j: (0, 0)),
            const((1, D_MODEL)),
        ],
        out_specs=row(D_MODEL, tile_b),
        out_shape=jax.ShapeDtypeStruct((b, s, D_MODEL), F32),
        scratch_shapes=[
            pltpu.VMEM((tm + V7X_SUBLANES, CONV_WIDTH), F32),
            pltpu.VMEM((tm, BRANCH_B), BF16),
            tile_f32, tile_bf16, tile_f32,
        ],
        compiler_params=_params("arbitrary"),
        name="tail",
    )(x, mix, qm0, z0, kv, wo0, norm_g1, w1, conv_w, kv, wo1, final_g)


def kernel(x, mem, positions, norm_g, mem_norm_g, w_mem_kv, attn_w_in, attn_w_out, conv_w_in, conv_w,
           conv_w_out, final_g):
    b, s, _ = x.shape
    w0 = (attn_w_in[0] * _A_SCALE).astype(BF16)

    inv_freq = ROPE_THETA ** (-jnp.arange(ROT_HALF, dtype=F32) * (2.0 / ROT_DIM))
    freq_rows = jnp.broadcast_to(inv_freq[:, None], (ROT_HALF, V7X_LANES))
    pos_rows = positions.reshape(b, 1, s)

    kv = _mem_kv(mem, mem_norm_g, w_mem_kv)
    qkv0, qm0, z0, qkv1, qkv2, w1, wo0, wo1 = _proj0(
        x, norm_g[0:1], pos_rows, freq_rows, w0, conv_w_in[0], attn_w_out[0], conv_w_out[0])
    qkv1 = qkv1.reshape(b, s, QKV_G)
    qkv2 = qkv2.reshape(b, s, QKV_G)
    mix = _attn(qkv0, qkv1, qkv2)
    return _tail(x, mix, qm0, z0, kv, wo0, norm_g[1:2], w1, conv_w[0], wo1, final_g.reshape(1, D_MODEL))
```

```python
import functools

import numpy as np
import jax
import jax.numpy as jnp
from jax import lax
from jax.experimental import pallas as pl
from jax.experimental.pallas import tpu as pltpu

F32 = jnp.float32
BF16 = jnp.bfloat16

D_MODEL = 1024
HEAD_DIM = 64
ROT_DIM = 16
ROT_HALF = ROT_DIM // 2
ROPE_THETA = 500000.0
DILATIONS = (1, 4, 16)
BLOCK = 128
GROUP_WIDTH = 512
N_GROUPS = 3
N_MEM = 256
MEM_WIDTH = 256
CONV_WIDTH = 1024
EPS = 1e-6
SCORE_SCALE = HEAD_DIM ** -0.5 * float(np.log2(np.e))

V7X_LANES = 128
V7X_SUBLANES = 8
BF16_SUBLANES = 16
ROW_TILE = 512
COL_CHUNK = 256
VMEM_LIMIT_BYTES = 56 * 1024 * 1024
NEG_BIG = -1e30
MEM_ROWS = 256

QKV_G = 3 * GROUP_WIDTH
A_PART = N_GROUPS * GROUP_WIDTH
A_QM = 3 * A_PART
A_Z = A_QM + MEM_WIDTH
BRANCH_A = GROUP_WIDTH + MEM_WIDTH
IN_A_COLS = A_Z + BRANCH_A
B_BG, B_CG, B_U = 0, CONV_WIDTH, 2 * CONV_WIDTH
B_QM = 3 * CONV_WIDTH
B_Z = B_QM + MEM_WIDTH
BRANCH_B = CONV_WIDTH + MEM_WIDTH
IN_B_COLS = B_Z + BRANCH_B


def _column_scale(n_cols, scaled):
    scale = np.ones((n_cols,), np.float32)
    for lo, hi in scaled:
        scale[lo:hi] = SCORE_SCALE
    return scale


_A_SCALE = _column_scale(IN_A_COLS, [(0, A_PART), (A_QM, A_QM + MEM_WIDTH)])


def _params(*sem):
    return pltpu.CompilerParams(dimension_semantics=sem, vmem_limit_bytes=VMEM_LIMIT_BYTES)


def _resident(shape, index_map):
    return pl.BlockSpec(shape, index_map, pipeline_mode=pl.Buffered(1))


def _silu(z):
    return z * jax.nn.sigmoid(z)


def _dot(a, b):
    return jnp.dot(a, b, preferred_element_type=F32)


def _dot_nt(a, b):
    return lax.dot_general(a, b, (((1,), (1,)), ((), ())), preferred_element_type=F32)


def _half_masks():
    lane = lax.broadcasted_iota(jnp.int32, (1, V7X_LANES), 1)
    return (jnp.where(lane < HEAD_DIM, 1.0, 0.0).astype(BF16),
            jnp.where(lane < HEAD_DIM, 0.0, 1.0).astype(BF16))


def _memkv_kernel(mem_ref, g_ref, w_ref, kv_ref):
    w = w_ref[0].astype(BF16)
    for c in range(mem_ref.shape[0] // ROW_TILE):
        rows = slice(ROW_TILE * c, ROW_TILE * (c + 1))
        m = mem_ref[rows, :]
        ms = jnp.mean(m * m, axis=-1, keepdims=True)
        mn = (m * lax.rsqrt(ms + EPS) * g_ref[0]).astype(BF16)
        kv_ref[0, rows, :] = _dot(mn, w).astype(BF16)


def _mem_kv(mem, mem_norm_g, w_mem_kv):
    b = mem.shape[0]
    depth = w_mem_kv.shape[0]
    rows = b * N_MEM
    kv = pl.pallas_call(
        _memkv_kernel,
        grid=(depth,),
        in_specs=[
            _resident((rows, D_MODEL), lambda l: (0, 0)),
            pl.BlockSpec((1, 1, D_MODEL), lambda l: (l, 0, 0)),
            pl.BlockSpec((1, D_MODEL, 2 * MEM_WIDTH), lambda l: (l, 0, 0)),
        ],
        out_specs=pl.BlockSpec((1, rows, 2 * MEM_WIDTH), lambda l: (l, 0, 0)),
        out_shape=jax.ShapeDtypeStruct((depth, rows, 2 * MEM_WIDTH), BF16),
        compiler_params=_params("arbitrary"),
        name="mem_kv",
    )(mem.reshape(rows, D_MODEL), mem_norm_g.reshape(depth, 1, D_MODEL), w_mem_kv)
    return kv.reshape(depth, b, N_MEM, 2 * MEM_WIDTH)


def _rope_tables(pos_ref, freq_ref, tbl_scr):
    rows = tbl_scr.shape[1]
    freq = freq_ref[...]
    one = jnp.ones((V7X_SUBLANES, V7X_LANES), F32)
    zero = jnp.zeros((V7X_SUBLANES, V7X_LANES), F32)
    groups = V7X_LANES // V7X_SUBLANES
    per_head = HEAD_DIM // V7X_SUBLANES

    def lane_rows(first, second, other):
        pieces = []
        for i in range(groups):
            pieces.append(first if i % per_head == 0 else second if i % per_head == 1 else other)
        return jnp.concatenate(pieces, axis=0).T

    for c in range(rows // V7X_LANES):
        sl = slice(V7X_LANES * c, V7X_LANES * (c + 1))
        ang = pos_ref[0, :, sl].astype(F32) * freq
        cs = jnp.cos(ang)
        sn = jnp.sin(ang)
        tbl_scr[0, sl, :] = lane_rows(cs, cs, one)
        tbl_scr[1, sl, :] = lane_rows(-sn, zero, zero)
        tbl_scr[2, sl, :] = lane_rows(zero, sn, zero)


def _rope(a, cs, sn_next, sn_prev):
    return (a * cs + pltpu.roll(a, V7X_LANES - ROT_HALF, 1) * sn_next
            + pltpu.roll(a, ROT_HALF, 1) * sn_prev)


def _proj0_kernel(x_ref, g_ref, pos_ref, freq_ref, w_ref, w1_ref, wo0_ref, wo1_ref,
                  o0_ref, oqm_ref, oz_ref, o1_ref, o2_ref, w1b_ref, wo0b_ref, wo1b_ref,
                  hn_scr, perm_scr, tbl_scr):
    tm = x_ref.shape[1]
    x = x_ref[0]
    ms = jnp.mean(x * x, axis=-1, keepdims=True)
    hn = x * lax.rsqrt(ms + EPS) * g_ref[...]
    hb = hn.astype(BF16)
    n_slab = D_MODEL // V7X_LANES
    for c in range(n_slab):
        hn_scr[c] = hn[:, V7X_LANES * c:V7X_LANES * (c + 1)]
    _rope_tables(pos_ref, freq_ref, tbl_scr)

    def qkv_chunk(lhs, g, j, tables):
        part, half = divmod(j, GROUP_WIDTH // COL_CHUNK)
        c0 = A_PART * part + GROUP_WIDTH * g + COL_CHUNK * half
        acc = _dot(lhs, w_ref[:, c0:c0 + COL_CHUNK].astype(BF16))
        if part == 0:
            acc = acc * SCORE_SCALE
        if part < 2:
            acc = jnp.concatenate(
                [_rope(acc[:, :V7X_LANES], *tables), _rope(acc[:, V7X_LANES:], *tables)], axis=1)
        return acc.astype(BF16)

    tables = tuple(tbl_scr[t] for t in range(3))
    for j in range(QKV_G // COL_CHUNK):
        o0_ref[0, :, COL_CHUNK * j:COL_CHUNK * (j + 1)] = qkv_chunk(hb, 0, j, tables)
    oqm_ref[0] = (_dot(hb, w_ref[:, A_QM:A_QM + MEM_WIDTH].astype(BF16)) * SCORE_SCALE).astype(BF16)
    for j in range(BRANCH_A // COL_CHUNK):
        oz_ref[0, :, COL_CHUNK * j:COL_CHUNK * (j + 1)] = _dot(
            hb, w_ref[:, A_Z + COL_CHUNK * j:A_Z + COL_CHUNK * (j + 1)].astype(BF16))

    for g, o_ref in ((1, o1_ref), (2, o2_ref)):
        d = DILATIONS[g]
        n = tm // d
        for r in range(d):
            for c in range(n_slab):
                perm_scr[r * n:(r + 1) * n, V7X_LANES * c:V7X_LANES * (c + 1)] = (
                    hn_scr[c, pl.ds(r, n, stride=d), :].astype(BF16))
        tables = tuple(
            jnp.concatenate([tbl_scr[t, pl.ds(r, n, stride=d), :] for r in range(d)], axis=0)
            for t in range(3))
        lhs = perm_scr[...]
        for j in range(QKV_G // COL_CHUNK):
            res = qkv_chunk(lhs, g, j, tables)
            for r in range(d):
                o_ref[0, r, :, COL_CHUNK * j:COL_CHUNK * (j + 1)] = res[r * n:(r + 1) * n]

    w1b_ref[:, :B_QM] = w1_ref[:, :B_QM].astype(BF16)
    w1b_ref[:, B_QM:B_Z] = (w1_ref[:, B_QM:B_Z] * SCORE_SCALE).astype(BF16)
    w1b_ref[:, B_Z:] = w1_ref[:, B_Z:].astype(BF16)
    wo0b_ref[...] = wo0_ref[...].astype(BF16)
    wo1b_ref[...] = wo1_ref[...].astype(BF16)


def _proj0(x, norm_g, pos_rows, freq_rows, w0, w1, wo0, wo1):
    b, s, _ = x.shape
    tm = ROW_TILE
    per_seq = s // tm
    n_steps = b * per_seq
    row = lambda width: pl.BlockSpec((1, tm, width), lambda i, j: (i, j, 0))
    stream = lambda d: pl.BlockSpec((1, d, tm // d, QKV_G), lambda i, j: (i, 0, j, 0))

    def slab(w, steps):
        rows = w.shape[0] // steps
        assert rows * steps == w.shape[0] and rows % BF16_SUBLANES == 0
        return pl.BlockSpec((rows, w.shape[1]), lambda i, j: (jnp.minimum(i * per_seq + j, steps - 1), 0))

    slabs = [slab(w1, n_steps), slab(wo0, n_steps // 2), slab(wo1, n_steps // 2)]
    d1, d2 = DILATIONS[1], DILATIONS[2]
    return pl.pallas_call(
        _proj0_kernel,
        grid=(b, per_seq),
        in_specs=[
            row(D_MODEL),
            pl.BlockSpec((1, D_MODEL), lambda i, j: (0, 0)),
            pl.BlockSpec((1, 1, tm), lambda i, j: (i, 0, j)),
            pl.BlockSpec((V7X_SUBLANES, V7X_LANES), lambda i, j: (0, 0)),
            _resident((D_MODEL, IN_A_COLS), lambda i, j: (0, 0)),
        ] + slabs,
        out_specs=[row(QKV_G), row(MEM_WIDTH), row(BRANCH_A), stream(d1), stream(d2)] + slabs,
        out_shape=[
            jax.ShapeDtypeStruct((b, s, QKV_G), BF16),
            jax.ShapeDtypeStruct((b, s, MEM_WIDTH), BF16),
            jax.ShapeDtypeStruct((b, s, BRANCH_A), F32),
            jax.ShapeDtypeStruct((b, d1, s // d1, QKV_G), BF16),
            jax.ShapeDtypeStruct((b, d2, s // d2, QKV_G), BF16),
            jax.ShapeDtypeStruct(w1.shape, BF16),
            jax.ShapeDtypeStruct(wo0.shape, BF16),
            jax.ShapeDtypeStruct(wo1.shape, BF16),
        ],
        scratch_shapes=[
            pltpu.VMEM((D_MODEL // V7X_LANES, tm, V7X_LANES), F32),
            pltpu.VMEM((tm, D_MODEL), BF16),
            pltpu.VMEM((3, tm, V7X_LANES), F32),
        ],
        compiler_params=_params("arbitrary", "arbitrary"),
        name="proj0",
    )(x, norm_g, pos_rows, freq_rows, w0, w1, wo0, wo1)


def _attn_kernel(q0, k0, v0, q1, k1, v1, q2, k2, v2, o_ref,
                 num_scr, m_scr, l_scr, ve_scr, bias2_scr, bias1_scr):
    s_len = o_ref.shape[1]
    half0, half1 = _half_masks()
    first_half = lax.broadcasted_iota(jnp.int32, (BLOCK, V7X_LANES), 1) < HEAD_DIM

    @pl.when((pl.program_id(0) == 0) & (pl.program_id(1) == 0))
    def _():
        qi = lax.broadcasted_iota(jnp.int32, (2 * BLOCK, 2 * BLOCK), 0) & (BLOCK - 1)
        kj = lax.broadcasted_iota(jnp.int32, (2 * BLOCK, 2 * BLOCK), 1)
        valid_two = ((kj < BLOCK) & (kj >= qi)) | ((kj >= BLOCK) & (kj - BLOCK <= qi))
        bias2_scr[...] = jnp.where(valid_two, 0.0, NEG_BIG)
        valid_one = (lax.broadcasted_iota(jnp.int32, (2 * BLOCK, BLOCK), 1)
                     <= (lax.broadcasted_iota(jnp.int32, (2 * BLOCK, BLOCK), 0) & (BLOCK - 1)))
        bias1_scr[...] = jnp.where(valid_one, 0.0, NEG_BIG)
        for g in range(N_GROUPS):
            for h, half in enumerate((half0, half1)):
                ve_scr[g, h, :, V7X_LANES:] = jnp.broadcast_to(half, (s_len, V7X_LANES))

    for g, v_ref in enumerate((v0, v1, v2)):
        v = v_ref[0]
        for h, half in enumerate((half0, half1)):
            ve_scr[g, h, :, :V7X_LANES] = v * half

    def block(q_ref, k_ref, g, q_row, k_row, n_keys, bias_ref, dst):
        q = q_ref[0, pl.ds(q_row, BLOCK), :]
        qs = jnp.concatenate([q * half0, q * half1], axis=0)
        k = k_ref[0, pl.ds(k_row, n_keys), :]
        s = _dot_nt(qs, k) + bias_ref[...]
        m = jnp.max(s, axis=-1, keepdims=True)
        p = jnp.exp2(s - m).astype(BF16)
        p_cat = jnp.concatenate([p[:BLOCK], p[BLOCK:]], axis=1)
        keys = pl.ds(k_row, n_keys)
        ve = jnp.concatenate([ve_scr[g, 0, keys, :], ve_scr[g, 1, keys, :]], axis=0)
        pv = _dot(p_cat, ve)
        num_scr[g, dst, :] = pv[:, :V7X_LANES]
        l_scr[g, dst, :] = pv[:, V7X_LANES:]
        m_scr[g, dst, :] = jnp.where(first_half, m[:BLOCK], m[BLOCK:])

    d1, d2 = DILATIONS[1], DILATIONS[2]
    stream_len = s_len // d1
    sub = d2 // d1

    for r in range(d2):
        block(q2, k2, 2, r * BLOCK, r * BLOCK, BLOCK, bias1_scr,
              pl.ds((r % d1) * stream_len + r // d1, BLOCK, stride=sub))

    for r in range(d1):
        base = r * stream_len
        block(q1, k1, 1, base, base, BLOCK, bias1_scr, pl.ds(base, BLOCK))
        for n in range(1, stream_len // BLOCK):
            block(q1, k1, 1, base + n * BLOCK, base + (n - 1) * BLOCK, 2 * BLOCK, bias2_scr,
                  pl.ds(base + n * BLOCK, BLOCK))

    def merge(n):
        rows_per = BLOCK // d1
        for r in range(d1):
            tok = pl.ds(n * BLOCK + r, rows_per, stride=d1)
            rows = pl.ds(r * stream_len + n * rows_per, rows_per)
            idx = (tok, rows, rows)
            ms = [m_scr[g, idx[g], :] for g in range(N_GROUPS)]
            top = jnp.maximum(ms[0], jnp.maximum(ms[1], ms[2]))
            ws = [jnp.exp2(mg - top) for mg in ms]
            nums = [ws[g] * num_scr[g, idx[g], :] for g in range(N_GROUPS)]
            dens = [ws[g] * l_scr[g, idx[g], :] for g in range(N_GROUPS)]
            num = nums[0] + nums[1] + nums[2]
            den = dens[0] + dens[1] + dens[2]
            o_ref[0, tok, :] = num / den

    block(q0, k0, 0, 0, 0, BLOCK, bias1_scr, pl.ds(0, BLOCK))
    for n in range(1, s_len // BLOCK):
        block(q0, k0, 0, n * BLOCK, (n - 1) * BLOCK, 2 * BLOCK, bias2_scr, pl.ds(n * BLOCK, BLOCK))
        merge(n - 1)
    merge(s_len // BLOCK - 1)


def _attn(qkv0, qkv1, qkv2):
    b, s, _ = qkv0.shape
    n_pair = GROUP_WIDTH // V7X_LANES

    def part(k):
        return pl.BlockSpec((1, s, V7X_LANES), lambda i, p, k=k: (i, 0, k * n_pair + p))

    specs = [part(0), part(1), part(2)] * N_GROUPS
    stat = pltpu.VMEM((N_GROUPS, s, V7X_LANES), F32)
    scratch = [
        stat, stat, stat,
        pltpu.VMEM((N_GROUPS, 2, s, 2 * V7X_LANES), BF16),
        pltpu.VMEM((2 * BLOCK, 2 * BLOCK), F32),
        pltpu.VMEM((2 * BLOCK, BLOCK), F32),
    ]
    return pl.pallas_call(
        _attn_kernel,
        grid=(b, n_pair),
        in_specs=specs,
        out_specs=pl.BlockSpec((1, s, V7X_LANES), lambda i, p: (i, 0, p)),
        out_shape=jax.ShapeDtypeStruct((b, s, GROUP_WIDTH), F32),
        scratch_shapes=scratch,
        compiler_params=_params("arbitrary", "arbitrary"),
        name="dilated_attn",
    )(qkv0, qkv0, qkv0, qkv1, qkv1, qkv1, qkv2, qkv2, qkv2)


def _mem_scores(qm, kv_ref):
    rows = qm.shape[0]
    half0, half1 = _half_masks()
    scores = []
    for p in range(MEM_WIDTH // V7X_LANES):
        k = kv_ref[0, 0, :, V7X_LANES * p:V7X_LANES * (p + 1)]
        for c in range(rows // MEM_ROWS):
            q = qm[MEM_ROWS * c:MEM_ROWS * (c + 1), V7X_LANES * p:V7X_LANES * (p + 1)]
            scores.append(_dot_nt(jnp.concatenate([q * half0, q * half1], axis=0), k))
    return scores


def _mem_softmax_pv(scores, kv_ref):
    half0, half1 = _half_masks()
    n_pairs = MEM_WIDTH // V7X_LANES
    per_pair = len(scores) // n_pairs
    outs = []
    for p in range(n_pairs):
        v = kv_ref[0, 0, :, MEM_WIDTH + V7X_LANES * p:MEM_WIDTH + V7X_LANES * (p + 1)]
        ve = jnp.concatenate(
            [jnp.concatenate([v * half, jnp.broadcast_to(half, (N_MEM, V7X_LANES))], axis=1)
             for half in (half0, half1)], axis=0)
        chunks = []
        for s in scores[per_pair * p:per_pair * (p + 1)]:
            e = jnp.exp2(s - jnp.max(s, axis=-1, keepdims=True)).astype(BF16)
            pv = _dot(jnp.concatenate([e[:MEM_ROWS], e[MEM_ROWS:]], axis=1), ve)
            chunks.append(pv[:, :V7X_LANES] / pv[:, V7X_LANES:])
        outs.append(jnp.concatenate(chunks, axis=0))
    return jnp.concatenate(outs, axis=1)


def _tail_kernel(x_ref, mix_ref, qm0_ref, z0_ref, kv0_ref, wo0_ref, g1_ref, w1_ref, cw_ref,
                 kv1_ref, wo1_ref, gf_ref, out_ref,
                 a_scr, y_scr, h1_scr, hb_scr, h1_old, *, tiles_per_seq):
    tm = x_ref.shape[1]
    halo = V7X_SUBLANES
    n_out = D_MODEL // COL_CHUNK
    step = pl.program_id(0)

    @pl.when(step == 0)
    def _():
        h1_scr[...] = jnp.zeros(h1_scr.shape, F32)
        hb_scr[...] = jnp.zeros(hb_scr.shape, BF16)
        a_scr[0:halo, :] = jnp.zeros((halo, CONV_WIDTH), F32)

    h1_old[...] = h1_scr[...]
    seq_start = (jnp.maximum(step - 1, 0) % tiles_per_seq) == 0
    a_scr[0:halo, :] = jnp.where(seq_start, 0.0, a_scr[0:halo, :])

    def proj1(c0, width=COL_CHUNK):
        return _dot(hb_scr[...], w1_ref[:, c0:c0 + width])

    def conv_chunk(j):
        cs = slice(COL_CHUNK * j, COL_CHUNK * (j + 1))
        a_now = proj1(B_CG + COL_CHUNK * j) * proj1(B_U + COL_CHUNK * j)
        a_scr[halo:halo + tm, cs] = a_now
        conv = (cw_ref[0:1, cs] * a_scr[halo - 2:halo - 2 + tm, cs]
                + cw_ref[1:2, cs] * a_scr[halo - 1:halo - 1 + tm, cs]
                + cw_ref[2:3, cs] * a_now)
        mix1 = proj1(B_BG + COL_CHUNK * j) * conv
        y_scr[:, cs] = (mix1 * _silu(proj1(B_Z + COL_CHUNK * j))).astype(BF16)

    scores0 = _mem_scores(qm0_ref[0], kv0_ref)
    qm1 = proj1(B_QM, MEM_WIDTH).astype(BF16)
    z_mem = proj1(B_Z + CONV_WIDTH, MEM_WIDTH)
    conv_chunk(0)
    scores1 = _mem_scores(qm1, kv1_ref)
    mem0 = _mem_softmax_pv(scores0, kv0_ref)
    z0 = z0_ref[0]
    y0 = jnp.concatenate(
        [mix_ref[0] * _silu(z0[:, :GROUP_WIDTH]), mem0 * _silu(z0[:, GROUP_WIDTH:])], axis=1).astype(BF16)
    conv_chunk(1)
    mem1 = _mem_softmax_pv(scores1, kv1_ref)
    y_scr[:, CONV_WIDTH:BRANCH_B] = (mem1 * _silu(z_mem)).astype(BF16)
    for j in range(n_out):
        cs = slice(COL_CHUNK * j, COL_CHUNK * (j + 1))
        h1_scr[:, cs] = x_ref[0, :, cs] + _dot(y0, wo0_ref[:, cs])
    conv_chunk(2)
    conv_chunk(3)
    a_scr[0:halo, :] = a_scr[tm:tm + halo, :]
    h1 = h1_scr[...]
    ms = jnp.mean(h1 * h1, axis=-1, keepdims=True)
    hb_scr[...] = (h1 * lax.rsqrt(ms + EPS) * g1_ref[...]).astype(BF16)

    y1 = y_scr[...]
    ssq = jnp.zeros((tm, 1), F32)
    for j in range(n_out):
        cs = slice(COL_CHUNK * j, COL_CHUNK * (j + 1))
        h2 = h1_old[:, cs] + _dot(y1, wo1_ref[:, cs])
        ssq = ssq + jnp.sum(h2 * h2, axis=-1, keepdims=True)
        out_ref[0, :, cs] = h2
    scale = lax.rsqrt(ssq * (1.0 / D_MODEL) + EPS)
    out_ref[0] = out_ref[0] * scale * gf_ref[...]


def _tail(x, mix, qm0, z0, kv, wo0, norm_g1, w1, conv_w, wo1, final_g):
    b, s, _ = x.shape
    tm = ROW_TILE
    per_seq = s // tm
    n_tiles = b * per_seq

    def tile_a(j):
        return jnp.minimum(j, n_tiles - 1)

    def tile_b(j):
        return jnp.maximum(j - 1, 0)

    def row(width, tile):
        return pl.BlockSpec((1, tm, width), lambda j: (tile(j) // per_seq, tile(j) % per_seq, 0))

    def kv_spec(layer, tile):
        return pl.BlockSpec((1, 1, N_MEM, 2 * MEM_WIDTH), lambda j: (layer, tile(j) // per_seq, 0, 0))

    const = lambda shape: pl.BlockSpec(shape, lambda j: (0, 0))
    tile_f32 = pltpu.VMEM((tm, D_MODEL), F32)
    tile_bf16 = pltpu.VMEM((tm, D_MODEL), BF16)
    return pl.pallas_call(
        functools.partial(_tail_kernel, tiles_per_seq=per_seq),
        grid=(n_tiles + 1,),
        in_specs=[
            row(D_MODEL, tile_a), row(GROUP_WIDTH, tile_a), row(MEM_WIDTH, tile_a), row(BRANCH_A, tile_a),
            kv_spec(0, tile_a),
            _resident(wo0.shape, lambda j: (0, 0)),
            const((1, D_MODEL)),
            _resident(w1.shape, lambda j: (0, 0)),
            const((3, CONV_WIDTH)),
            kv_spec(1, tile_b),
            _resident(wo1.shape, lambda j: (0, 0)),
            const((1, D_MODEL)),
        ],
        out_specs=row(D_MODEL, tile_b),
        out_shape=jax.ShapeDtypeStruct((b, s, D_MODEL), F32),
        scratch_shapes=[
            pltpu.VMEM((tm + V7X_SUBLANES, CONV_WIDTH), F32),
            pltpu.VMEM((tm, BRANCH_B), BF16),
            tile_f32, tile_bf16, tile_f32,
        ],
        compiler_params=_params("arbitrary"),
        name="tail",
    )(x, mix, qm0, z0, kv, wo0, norm_g1, w1, conv_w, kv, wo1, final_g)


def kernel(x, mem, positions, norm_g, mem_norm_g, w_mem_kv, attn_w_in, attn_w_out, conv_w_in, conv_w,
           conv_w_out, final_g):
    b, s, _ = x.shape

    inv_freq = ROPE_THETA ** (-jnp.arange(ROT_HALF, dtype=F32) * (2.0 / ROT_DIM))
    freq_rows = jnp.broadcast_to(inv_freq[:, None], (ROT_HALF, V7X_LANES))
    pos_rows = positions.reshape(b, 1, s)

    kv = _mem_kv(mem, mem_norm_g, w_mem_kv)
    qkv0, qm0, z0, qkv1, qkv2, w1, wo0, wo1 = _proj0(
        x, norm_g[0:1], pos_rows, freq_rows, attn_w_in[0], conv_w_in[0], attn_w_out[0], conv_w_out[0])
    qkv1 = qkv1.reshape(b, s, QKV_G)
    qkv2 = qkv2.reshape(b, s, QKV_G)
    mix = _attn(qkv0, qkv1, qkv2)
    return _tail(x, mix, qm0, z0, kv, wo0, norm_g[1:2], w1, conv_w[0], wo1, final_g.reshape(1, D_MODEL))
```

```python
import functools

import numpy as np
import jax
import jax.numpy as jnp
from jax import lax
from jax.experimental import pallas as pl
from jax.experimental.pallas import tpu as pltpu

F32 = jnp.float32
BF16 = jnp.bfloat16

D_MODEL = 1024
HEAD_DIM = 64
ROT_DIM = 16
ROT_HALF = ROT_DIM // 2
ROPE_THETA = 500000.0
DILATIONS = (1, 4, 16)
BLOCK = 128
GROUP_WIDTH = 512
N_GROUPS = 3
N_MEM = 256
MEM_WIDTH = 256
CONV_WIDTH = 1024
EPS = 1e-6
SCORE_SCALE = HEAD_DIM ** -0.5 * float(np.log2(np.e))

V7X_LANES = 128
V7X_SUBLANES = 8
BF16_SUBLANES = 16
ROW_TILE = 512
COL_CHUNK = 256
VMEM_LIMIT_BYTES = 56 * 1024 * 1024
NEG_BIG = -1e30
MEM_ROWS = 256

QKV_G = 3 * GROUP_WIDTH
A_PART = N_GROUPS * GROUP_WIDTH
A_QM = 3 * A_PART
A_Z = A_QM + MEM_WIDTH
BRANCH_A = GROUP_WIDTH + MEM_WIDTH
IN_A_COLS = A_Z + BRANCH_A
B_BG, B_CG, B_U = 0, CONV_WIDTH, 2 * CONV_WIDTH
B_QM = 3 * CONV_WIDTH
B_Z = B_QM + MEM_WIDTH
BRANCH_B = CONV_WIDTH + MEM_WIDTH
IN_B_COLS = B_Z + BRANCH_B


def _column_scale(n_cols, scaled):
    scale = np.ones((n_cols,), np.float32)
    for lo, hi in scaled:
        scale[lo:hi] = SCORE_SCALE
    return scale


_A_SCALE = _column_scale(IN_A_COLS, [(0, A_PART), (A_QM, A_QM + MEM_WIDTH)])


def _params(*sem):
    return pltpu.CompilerParams(dimension_semantics=sem, vmem_limit_bytes=VMEM_LIMIT_BYTES)


def _resident(shape, index_map):
    return pl.BlockSpec(shape, index_map, pipeline_mode=pl.Buffered(1))


def _silu(z):
    return z * jax.nn.sigmoid(z)


def _dot(a, b):
    return jnp.dot(a, b, preferred_element_type=F32)


def _dot_nt(a, b):
    return lax.dot_general(a, b, (((1,), (1,)), ((), ())), preferred_element_type=F32)


def _half_masks():
    lane = lax.broadcasted_iota(jnp.int32, (1, V7X_LANES), 1)
    return (jnp.where(lane < HEAD_DIM, 1.0, 0.0).astype(BF16),
            jnp.where(lane < HEAD_DIM, 0.0, 1.0).astype(BF16))


def _memkv_kernel(mem_ref, g_ref, w_ref, kv_ref):
    w = w_ref[0].astype(BF16)
    for c in range(mem_ref.shape[0] // ROW_TILE):
        rows = slice(ROW_TILE * c, ROW_TILE * (c + 1))
        m = mem_ref[rows, :]
        ms = jnp.mean(m * m, axis=-1, keepdims=True)
        mn = (m * lax.rsqrt(ms + EPS) * g_ref[0]).astype(BF16)
        kv_ref[0, rows, :] = _dot(mn, w).astype(BF16)


def _mem_kv(mem, mem_norm_g, w_mem_kv):
    b = mem.shape[0]
    depth = w_mem_kv.shape[0]
    rows = b * N_MEM
    kv = pl.pallas_call(
        _memkv_kernel,
        grid=(depth,),
        in_specs=[
            _resident((rows, D_MODEL), lambda l: (0, 0)),
            pl.BlockSpec((1, 1, D_MODEL), lambda l: (l, 0, 0)),
            pl.BlockSpec((1, D_MODEL, 2 * MEM_WIDTH), lambda l: (l, 0, 0)),
        ],
        out_specs=pl.BlockSpec((1, rows, 2 * MEM_WIDTH), lambda l: (l, 0, 0)),
        out_shape=jax.ShapeDtypeStruct((depth, rows, 2 * MEM_WIDTH), BF16),
        compiler_params=_params("arbitrary"),
        name="mem_kv",
    )(mem.reshape(rows, D_MODEL), mem_norm_g.reshape(depth, 1, D_MODEL), w_mem_kv)
    return kv.reshape(depth, b, N_MEM, 2 * MEM_WIDTH)


def _rope_tables(pos_ref, freq_ref, tbl_scr):
    rows = tbl_scr.shape[1]
    freq = freq_ref[...]
    one = jnp.ones((V7X_SUBLANES, V7X_LANES), F32)
    zero = jnp.zeros((V7X_SUBLANES, V7X_LANES), F32)
    groups = V7X_LANES // V7X_SUBLANES
    per_head = HEAD_DIM // V7X_SUBLANES

    def lane_rows(first, second, other):
        pieces = []
        for i in range(groups):
            pieces.append(first if i % per_head == 0 else second if i % per_head == 1 else other)
        return jnp.concatenate(pieces, axis=0).T

    for c in range(rows // V7X_LANES):
        sl = slice(V7X_LANES * c, V7X_LANES * (c + 1))
        ang = pos_ref[0, :, sl].astype(F32) * freq
        cs = jnp.cos(ang)
        sn = jnp.sin(ang)
        tbl_scr[0, sl, :] = lane_rows(cs, cs, one)
        tbl_scr[1, sl, :] = lane_rows(-sn, zero, zero)
        tbl_scr[2, sl, :] = lane_rows(zero, sn, zero)


def _rope(a, cs, sn_next, sn_prev):
    return (a * cs + pltpu.roll(a, V7X_LANES - ROT_HALF, 1) * sn_next
            + pltpu.roll(a, ROT_HALF, 1) * sn_prev)


def _proj0_kernel(x_ref, g_ref, pos_ref, freq_ref, w_ref, w1_ref, wo0_ref, wo1_ref,
                  o0_ref, oqm_ref, oz_ref, o1_ref, o2_ref, w1b_ref, wo0b_ref, wo1b_ref,
                  hn_scr, perm_scr, tbl_scr):
    tm = x_ref.shape[1]
    x = x_ref[0]
    ms = jnp.mean(x * x, axis=-1, keepdims=True)
    hn = x * lax.rsqrt(ms + EPS) * g_ref[...]
    hb = hn.astype(BF16)
    n_slab = D_MODEL // V7X_LANES
    for c in range(n_slab):
        hn_scr[c] = hn[:, V7X_LANES * c:V7X_LANES * (c + 1)]
    _rope_tables(pos_ref, freq_ref, tbl_scr)

    def qkv_chunk(lhs, g, j, tables):
        part, half = divmod(j, GROUP_WIDTH // COL_CHUNK)
        c0 = A_PART * part + GROUP_WIDTH * g + COL_CHUNK * half
        acc = _dot(lhs, w_ref[:, c0:c0 + COL_CHUNK])
        if part < 2:
            acc = jnp.concatenate(
                [_rope(acc[:, :V7X_LANES], *tables), _rope(acc[:, V7X_LANES:], *tables)], axis=1)
        return acc.astype(BF16)

    tables = tuple(tbl_scr[t] for t in range(3))
    for j in range(QKV_G // COL_CHUNK):
        o0_ref[0, :, COL_CHUNK * j:COL_CHUNK * (j + 1)] = qkv_chunk(hb, 0, j, tables)
    oqm_ref[0] = _dot(hb, w_ref[:, A_QM:A_QM + MEM_WIDTH]).astype(BF16)
    for j in range(BRANCH_A // COL_CHUNK):
        oz_ref[0, :, COL_CHUNK * j:COL_CHUNK * (j + 1)] = _dot(
            hb, w_ref[:, A_Z + COL_CHUNK * j:A_Z + COL_CHUNK * (j + 1)])

    for g, o_ref in ((1, o1_ref), (2, o2_ref)):
        d = DILATIONS[g]
        n = tm // d
        for r in range(d):
            for c in range(n_slab):
                perm_scr[r * n:(r + 1) * n, V7X_LANES * c:V7X_LANES * (c + 1)] = (
                    hn_scr[c, pl.ds(r, n, stride=d), :].astype(BF16))
        tables = tuple(
            jnp.concatenate([tbl_scr[t, pl.ds(r, n, stride=d), :] for r in range(d)], axis=0)
            for t in range(3))
        lhs = perm_scr[...]
        for j in range(QKV_G // COL_CHUNK):
            res = qkv_chunk(lhs, g, j, tables)
            for r in range(d):
                o_ref[0, r, :, COL_CHUNK * j:COL_CHUNK * (j + 1)] = res[r * n:(r + 1) * n]

    w1b_ref[:, :B_QM] = w1_ref[:, :B_QM].astype(BF16)
    w1b_ref[:, B_QM:B_Z] = (w1_ref[:, B_QM:B_Z] * SCORE_SCALE).astype(BF16)
    w1b_ref[:, B_Z:] = w1_ref[:, B_Z:].astype(BF16)
    wo0b_ref[...] = wo0_ref[...].astype(BF16)
    wo1b_ref[...] = wo1_ref[...].astype(BF16)


def _proj0(x, norm_g, pos_rows, freq_rows, w0, w1, wo0, wo1):
    b, s, _ = x.shape
    tm = ROW_TILE
    per_seq = s // tm
    n_steps = b * per_seq
    row = lambda width: pl.BlockSpec((1, tm, width), lambda i, j: (i, j, 0))
    stream = lambda d: pl.BlockSpec((1, d, tm // d, QKV_G), lambda i, j: (i, 0, j, 0))

    def slab(w, steps):
        rows = w.shape[0] // steps
        assert rows * steps == w.shape[0] and rows % BF16_SUBLANES == 0
        return pl.BlockSpec((rows, w.shape[1]), lambda i, j: (jnp.minimum(i * per_seq + j, steps - 1), 0))

    slabs = [slab(w1, n_steps), slab(wo0, n_steps // 2), slab(wo1, n_steps // 2)]
    d1, d2 = DILATIONS[1], DILATIONS[2]
    return pl.pallas_call(
        _proj0_kernel,
        grid=(b, per_seq),
        in_specs=[
            row(D_MODEL),
            pl.BlockSpec((1, D_MODEL), lambda i, j: (0, 0)),
            pl.BlockSpec((1, 1, tm), lambda i, j: (i, 0, j)),
            pl.BlockSpec((V7X_SUBLANES, V7X_LANES), lambda i, j: (0, 0)),
            _resident((D_MODEL, IN_A_COLS), lambda i, j: (0, 0)),
        ] + slabs,
        out_specs=[row(QKV_G), row(MEM_WIDTH), row(BRANCH_A), stream(d1), stream(d2)] + slabs,
        out_shape=[
            jax.ShapeDtypeStruct((b, s, QKV_G), BF16),
            jax.ShapeDtypeStruct((b, s, MEM_WIDTH), BF16),
            jax.ShapeDtypeStruct((b, s, BRANCH_A), F32),
            jax.ShapeDtypeStruct((b, d1, s // d1, QKV_G), BF16),
            jax.ShapeDtypeStruct((b, d2, s // d2, QKV_G), BF16),
            jax.ShapeDtypeStruct(w1.shape, BF16),
            jax.ShapeDtypeStruct(wo0.shape, BF16),
            jax.ShapeDtypeStruct(wo1.shape, BF16),
        ],
        scratch_shapes=[
            pltpu.VMEM((D_MODEL // V7X_LANES, tm, V7X_LANES), F32),
            pltpu.VMEM((tm, D_MODEL), BF16),
            pltpu.VMEM((3, tm, V7X_LANES), F32),
        ],
        compiler_params=_params("arbitrary", "arbitrary"),
        name="proj0",
    )(x, norm_g, pos_rows, freq_rows, w0, w1, wo0, wo1)


def _attn_kernel(q0, k0, v0, q1, k1, v1, q2, k2, v2, o_ref,
                 num_scr, m_scr, l_scr, ve_scr, bias2_scr, bias1_scr):
    s_len = o_ref.shape[1]
    half0, half1 = _half_masks()
    first_half = lax.broadcasted_iota(jnp.int32, (BLOCK, V7X_LANES), 1) < HEAD_DIM

    @pl.when((pl.program_id(0) == 0) & (pl.program_id(1) == 0))
    def _():
        qi = lax.broadcasted_iota(jnp.int32, (2 * BLOCK, 2 * BLOCK), 0) & (BLOCK - 1)
        kj = lax.broadcasted_iota(jnp.int32, (2 * BLOCK, 2 * BLOCK), 1)
        valid_two = ((kj < BLOCK) & (kj >= qi)) | ((kj >= BLOCK) & (kj - BLOCK <= qi))
        bias2_scr[...] = jnp.where(valid_two, 0.0, NEG_BIG)
        valid_one = (lax.broadcasted_iota(jnp.int32, (2 * BLOCK, BLOCK), 1)
                     <= (lax.broadcasted_iota(jnp.int32, (2 * BLOCK, BLOCK), 0) & (BLOCK - 1)))
        bias1_scr[...] = jnp.where(valid_one, 0.0, NEG_BIG)
        for g in range(N_GROUPS):
            for h, half in enumerate((half0, half1)):
                ve_scr[g, h, :, V7X_LANES:] = jnp.broadcast_to(half, (s_len, V7X_LANES))

    for g, v_ref in enumerate((v0, v1, v2)):
        v = v_ref[0]
        for h, half in enumerate((half0, half1)):
            ve_scr[g, h, :, :V7X_LANES] = v * half

    def block(q_ref, k_ref, g, q_row, k_row, n_keys, bias_ref, dst):
        q = q_ref[0, pl.ds(q_row, BLOCK), :]
        qs = jnp.concatenate([q * half0, q * half1], axis=0)
        k = k_ref[0, pl.ds(k_row, n_keys), :]
        s = _dot_nt(qs, k) + bias_ref[...]
        m = jnp.max(s, axis=-1, keepdims=True)
        p = jnp.exp2(s - m).astype(BF16)
        p_cat = jnp.concatenate([p[:BLOCK], p[BLOCK:]], axis=1)
        keys = pl.ds(k_row, n_keys)
        ve = jnp.concatenate([ve_scr[g, 0, keys, :], ve_scr[g, 1, keys, :]], axis=0)
        pv = _dot(p_cat, ve)
        num_scr[g, dst, :] = pv[:, :V7X_LANES]
        l_scr[g, dst, :] = pv[:, V7X_LANES:]
        m_scr[g, dst, :] = jnp.where(first_half, m[:BLOCK], m[BLOCK:])

    d1, d2 = DILATIONS[1], DILATIONS[2]
    stream_len = s_len // d1
    sub = d2 // d1

    for r in range(d2):
        block(q2, k2, 2, r * BLOCK, r * BLOCK, BLOCK, bias1_scr,
              pl.ds((r % d1) * stream_len + r // d1, BLOCK, stride=sub))

    for r in range(d1):
        base = r * stream_len
        block(q1, k1, 1, base, base, BLOCK, bias1_scr, pl.ds(base, BLOCK))
        for n in range(1, stream_len // BLOCK):
            block(q1, k1, 1, base + n * BLOCK, base + (n - 1) * BLOCK, 2 * BLOCK, bias2_scr,
                  pl.ds(base + n * BLOCK, BLOCK))

    def merge(n):
        rows_per = BLOCK // d1
        for r in range(d1):
            tok = pl.ds(n * BLOCK + r, rows_per, stride=d1)
            rows = pl.ds(r * stream_len + n * rows_per, rows_per)
            idx = (tok, rows, rows)
            ms = [m_scr[g, idx[g], :] for g in range(N_GROUPS)]
            top = jnp.maximum(ms[0], jnp.maximum(ms[1], ms[2]))
            ws = [jnp.exp2(mg - top) for mg in ms]
            nums = [ws[g] * num_scr[g, idx[g], :] for g in range(N_GROUPS)]
            dens = [ws[g] * l_scr[g, idx[g], :] for g in range(N_GROUPS)]
            num = nums[0] + nums[1] + nums[2]
            den = dens[0] + dens[1] + dens[2]
            o_ref[0, tok, :] = num / den

    block(q0, k0, 0, 0, 0, BLOCK, bias1_scr, pl.ds(0, BLOCK))
    for n in range(1, s_len // BLOCK):
        block(q0, k0, 0, n * BLOCK, (n - 1) * BLOCK, 2 * BLOCK, bias2_scr, pl.ds(n * BLOCK, BLOCK))
        merge(n - 1)
    merge(s_len // BLOCK - 1)


def _attn(qkv0, qkv1, qkv2):
    b, s, _ = qkv0.shape
    n_pair = GROUP_WIDTH // V7X_LANES

    def part(k):
        return pl.BlockSpec((1, s, V7X_LANES), lambda i, p, k=k: (i, 0, k * n_pair + p))

    specs = [part(0), part(1), part(2)] * N_GROUPS
    stat = pltpu.VMEM((N_GROUPS, s, V7X_LANES), F32)
    scratch = [
        stat, stat, stat,
        pltpu.VMEM((N_GROUPS, 2, s, 2 * V7X_LANES), BF16),
        pltpu.VMEM((2 * BLOCK, 2 * BLOCK), F32),
        pltpu.VMEM((2 * BLOCK, BLOCK), F32),
    ]
    return pl.pallas_call(
        _attn_kernel,
        grid=(b, n_pair),
        in_specs=specs,
        out_specs=pl.BlockSpec((1, s, V7X_LANES), lambda i, p: (i, 0, p)),
        out_shape=jax.ShapeDtypeStruct((b, s, GROUP_WIDTH), F32),
        scratch_shapes=scratch,
        compiler_params=_params("arbitrary", "arbitrary"),
        name="dilated_attn",
    )(qkv0, qkv0, qkv0, qkv1, qkv1, qkv1, qkv2, qkv2, qkv2)


def _mem_scores(qm, kv_ref):
    rows = qm.shape[0]
    half0, half1 = _half_masks()
    scores = []
    for p in range(MEM_WIDTH // V7X_LANES):
        k = kv_ref[0, 0, :, V7X_LANES * p:V7X_LANES * (p + 1)]
        for c in range(rows // MEM_ROWS):
            q = qm[MEM_ROWS * c:MEM_ROWS * (c + 1), V7X_LANES * p:V7X_LANES * (p + 1)]
            scores.append(_dot_nt(jnp.concatenate([q * half0, q * half1], axis=0), k))
    return scores


def _mem_softmax_pv(scores, kv_ref):
    half0, half1 = _half_masks()
    n_pairs = MEM_WIDTH // V7X_LANES
    per_pair = len(scores) // n_pairs
    outs = []
    for p in range(n_pairs):
        v = kv_ref[0, 0, :, MEM_WIDTH + V7X_LANES * p:MEM_WIDTH + V7X_LANES * (p + 1)]
        ve = jnp.concatenate(
            [jnp.concatenate([v * half, jnp.broadcast_to(half, (N_MEM, V7X_LANES))], axis=1)
             for half in (half0, half1)], axis=0)
        chunks = []
        for s in scores[per_pair * p:per_pair * (p + 1)]:
            e = jnp.exp2(s - jnp.max(s, axis=-1, keepdims=True)).astype(BF16)
            pv = _dot(jnp.concatenate([e[:MEM_ROWS], e[MEM_ROWS:]], axis=1), ve)
            chunks.append(pv[:, :V7X_LANES] / pv[:, V7X_LANES:])
        outs.append(jnp.concatenate(chunks, axis=0))
    return jnp.concatenate(outs, axis=1)


def _tail_kernel(x_ref, mix_ref, qm0_ref, z0_ref, kv0_ref, wo0_ref, g1_ref, w1_ref, cw_ref,
                 kv1_ref, wo1_ref, gf_ref, out_ref,
                 a_scr, y_scr, h1_scr, hb_scr, *, tiles_per_seq):
    tm = x_ref.shape[1]
    halo = V7X_SUBLANES
    n_out = D_MODEL // COL_CHUNK
    step = pl.program_id(0)

    @pl.when(step == 0)
    def _():
        h1_scr[...] = jnp.zeros(h1_scr.shape, F32)
        a_scr[0:halo, :] = jnp.zeros((halo, CONV_WIDTH), F32)

    scores0 = _mem_scores(qm0_ref[0], kv0_ref)

    h1 = h1_scr[...]
    ms = jnp.mean(h1 * h1, axis=-1, keepdims=True)
    hb_scr[...] = (h1 * lax.rsqrt(ms + EPS) * g1_ref[...]).astype(BF16)
    seq_start = (jnp.maximum(step - 1, 0) % tiles_per_seq) == 0
    a_scr[0:halo, :] = jnp.where(seq_start, 0.0, a_scr[0:halo, :])

    def proj1(c0, width=COL_CHUNK):
        return _dot(hb_scr[...], w1_ref[:, c0:c0 + width])

    def conv_chunk(j):
        cs = slice(COL_CHUNK * j, COL_CHUNK * (j + 1))
        a_now = proj1(B_CG + COL_CHUNK * j) * proj1(B_U + COL_CHUNK * j)
        a_scr[halo:halo + tm, cs] = a_now
        conv = (cw_ref[0:1, cs] * a_scr[halo - 2:halo - 2 + tm, cs]
                + cw_ref[1:2, cs] * a_scr[halo - 1:halo - 1 + tm, cs]
                + cw_ref[2:3, cs] * a_now)
        mix1 = proj1(B_BG + COL_CHUNK * j) * conv
        y_scr[:, cs] = (mix1 * _silu(proj1(B_Z + COL_CHUNK * j))).astype(BF16)

    qm1 = proj1(B_QM, MEM_WIDTH).astype(BF16)
    z_mem = proj1(B_Z + CONV_WIDTH, MEM_WIDTH)
    conv_chunk(0)
    scores1 = _mem_scores(qm1, kv1_ref)
    mem0 = _mem_softmax_pv(scores0, kv0_ref)
    z0 = z0_ref[0]
    y0 = jnp.concatenate(
        [mix_ref[0] * _silu(z0[:, :GROUP_WIDTH]), mem0 * _silu(z0[:, GROUP_WIDTH:])], axis=1).astype(BF16)
    conv_chunk(1)
    mem1 = _mem_softmax_pv(scores1, kv1_ref)
    y_scr[:, CONV_WIDTH:BRANCH_B] = (mem1 * _silu(z_mem)).astype(BF16)
    conv_chunk(2)
    conv_chunk(3)
    a_scr[0:halo, :] = a_scr[tm:tm + halo, :]

    y1 = y_scr[...]
    ssq = jnp.zeros((tm, 1), F32)
    for j in range(n_out):
        cs = slice(COL_CHUNK * j, COL_CHUNK * (j + 1))
        h2 = h1_scr[:, cs] + _dot(y1, wo1_ref[:, cs])
        ssq = ssq + jnp.sum(h2 * h2, axis=-1, keepdims=True)
        out_ref[0, :, cs] = h2
    scale = lax.rsqrt(ssq * (1.0 / D_MODEL) + EPS)
    out_ref[0] = out_ref[0] * scale * gf_ref[...]

    for j in range(n_out):
        cs = slice(COL_CHUNK * j, COL_CHUNK * (j + 1))
        h1_scr[:, cs] = x_ref[0, :, cs] + _dot(y0, wo0_ref[:, cs])


def _tail(x, mix, qm0, z0, kv, wo0, norm_g1, w1, conv_w, wo1, final_g):
    b, s, _ = x.shape
    tm = ROW_TILE
    per_seq = s // tm
    n_tiles = b * per_seq

    def tile_a(j):
        return jnp.minimum(j, n_tiles - 1)

    def tile_b(j):
        return jnp.maximum(j - 1, 0)

    def row(width, tile):
        return pl.BlockSpec((1, tm, width), lambda j: (tile(j) // per_seq, tile(j) % per_seq, 0))

    def kv_spec(layer, tile):
        return pl.BlockSpec((1, 1, N_MEM, 2 * MEM_WIDTH), lambda j: (layer, tile(j) // per_seq, 0, 0))

    const = lambda shape: pl.BlockSpec(shape, lambda j: (0, 0))
    tile_f32 = pltpu.VMEM((tm, D_MODEL), F32)
    tile_bf16 = pltpu.VMEM((tm, D_MODEL), BF16)
    return pl.pallas_call(
        functools.partial(_tail_kernel, tiles_per_seq=per_seq),
        grid=(n_tiles + 1,),
        in_specs=[
            row(D_MODEL, tile_a), row(GROUP_WIDTH, tile_a), row(MEM_WIDTH, tile_a), row(BRANCH_A, tile_a),
            kv_spec(0, tile_a),
            _resident(wo0.shape, lambda j: (0, 0)),
            const((1, D_MODEL)),
            _resident(w1.shape, lambda j: (0, 0)),
            const((3, CONV_WIDTH)),
            kv_spec(1, tile_b),
            _resident(wo1.shape, lambda j: (0, 0)),
            const((1, D_MODEL)),
        ],
        out_specs=row(D_MODEL, tile_b),
        out_shape=jax.ShapeDtypeStruct((b, s, D_MODEL), F32),
        scratch_shapes=[
            pltpu.VMEM((tm + V7X_SUBLANES, CONV_WIDTH), F32),
            pltpu.VMEM((tm, BRANCH_B), BF16),
            tile_f32, tile_bf16,
        ],
        compiler_params=_params("arbitrary"),
        name="tail",
    )(x, mix, qm0, z0, kv, wo0, norm_g1, w1, conv_w, kv, wo1, final_g)


def kernel(x, mem, positions, norm_g, mem_norm_g, w_mem_kv, attn_w_in, attn_w_out, conv_w_in, conv_w,
           conv_w_out, final_g):
    b, s, _ = x.shape
    w0 = (attn_w_in[0] * _A_SCALE).astype(BF16)

    inv_freq = ROPE_THETA ** (-jnp.arange(ROT_HALF, dtype=F32) * (2.0 / ROT_DIM))
    freq_rows = jnp.broadcast_to(inv_freq[:, None], (ROT_HALF, V7X_LANES))
    pos_rows = positions.reshape(b, 1, s)

    kv = _mem_kv(mem, mem_norm_g, w_mem_kv)
    qkv0, qm0, z0, qkv1, qkv2, w1, wo0, wo1 = _proj0(
        x, norm_g[0:1], pos_rows, freq_rows, w0, conv_w_in[0], attn_w_out[0], conv_w_out[0])
    qkv1 = qkv1.reshape(b, s, QKV_G)
    qkv2 = qkv2.reshape(b, s, QKV_G)
    mix = _attn(qkv0, qkv1, qkv2)
    return _tail(x, mix, qm0, z0, kv, wo0, norm_g[1:2], w1, conv_w[0], wo1, final_g.reshape(1, D_MODEL))
```

```python
import functools

import numpy as np
import jax
import jax.numpy as jnp
from jax import lax
from jax.experimental import pallas as pl
from jax.experimental.pallas import tpu as pltpu

F32 = jnp.float32
BF16 = jnp.bfloat16

D_MODEL = 1024
HEAD_DIM = 64
ROT_DIM = 16
ROT_HALF = ROT_DIM // 2
ROPE_THETA = 500000.0
DILATIONS = (1, 4, 16)
BLOCK = 128
GROUP_WIDTH = 512
N_GROUPS = 3
N_MEM = 256
MEM_WIDTH = 256
CONV_WIDTH = 1024
EPS = 1e-6
SCORE_SCALE = HEAD_DIM ** -0.5 * float(np.log2(np.e))

V7X_LANES = 128
V7X_SUBLANES = 8
BF16_SUBLANES = 16
ROW_TILE = 512
COL_CHUNK = 256
VMEM_LIMIT_BYTES = 56 * 1024 * 1024
NEG_BIG = -1e30
MEM_ROWS = 256

QKV_G = 3 * GROUP_WIDTH
A_PART = N_GROUPS * GROUP_WIDTH
A_QM = 3 * A_PART
A_Z = A_QM + MEM_WIDTH
BRANCH_A = GROUP_WIDTH + MEM_WIDTH
IN_A_COLS = A_Z + BRANCH_A
B_BG, B_CG, B_U = 0, CONV_WIDTH, 2 * CONV_WIDTH
B_QM = 3 * CONV_WIDTH
B_Z = B_QM + MEM_WIDTH
BRANCH_B = CONV_WIDTH + MEM_WIDTH
IN_B_COLS = B_Z + BRANCH_B


def _column_scale(n_cols, scaled):
    scale = np.ones((n_cols,), np.float32)
    for lo, hi in scaled:
        scale[lo:hi] = SCORE_SCALE
    return scale


_A_SCALE = _column_scale(IN_A_COLS, [(0, A_PART), (A_QM, A_QM + MEM_WIDTH)])


def _params(*sem):
    return pltpu.CompilerParams(dimension_semantics=sem, vmem_limit_bytes=VMEM_LIMIT_BYTES)


def _resident(shape, index_map):
    return pl.BlockSpec(shape, index_map, pipeline_mode=pl.Buffered(1))


def _silu(z):
    return z * jax.nn.sigmoid(z)


def _dot(a, b):
    return jnp.dot(a, b, preferred_element_type=F32)


def _dot_nt(a, b):
    return lax.dot_general(a, b, (((1,), (1,)), ((), ())), preferred_element_type=F32)


def _half_masks():
    lane = lax.broadcasted_iota(jnp.int32, (1, V7X_LANES), 1)
    return (jnp.where(lane < HEAD_DIM, 1.0, 0.0).astype(BF16),
            jnp.where(lane < HEAD_DIM, 0.0, 1.0).astype(BF16))


def _memkv_kernel(mem_ref, g_ref, w_ref, wa_ref, scale_ref, kv_ref, wab_ref):
    w = w_ref[0].astype(BF16)
    for c in range(mem_ref.shape[0] // ROW_TILE):
        rows = slice(ROW_TILE * c, ROW_TILE * (c + 1))
        m = mem_ref[rows, :]
        ms = jnp.mean(m * m, axis=-1, keepdims=True)
        mn = (m * lax.rsqrt(ms + EPS) * g_ref[0]).astype(BF16)
        kv_ref[0, rows, :] = _dot(mn, w).astype(BF16)
    for c in range(wa_ref.shape[0] // V7X_LANES):
        rows = slice(V7X_LANES * c, V7X_LANES * (c + 1))
        wab_ref[rows, :] = (wa_ref[rows, :] * scale_ref[...]).astype(BF16)


def _mem_kv(mem, mem_norm_g, w_mem_kv, w_attn_in):
    b = mem.shape[0]
    depth = w_mem_kv.shape[0]
    rows = b * N_MEM
    slab = pl.BlockSpec((D_MODEL // depth, IN_A_COLS), lambda l: (l, 0))
    kv, w0 = pl.pallas_call(
        _memkv_kernel,
        grid=(depth,),
        in_specs=[
            _resident((rows, D_MODEL), lambda l: (0, 0)),
            pl.BlockSpec((1, 1, D_MODEL), lambda l: (l, 0, 0)),
            pl.BlockSpec((1, D_MODEL, 2 * MEM_WIDTH), lambda l: (l, 0, 0)),
            slab,
            pl.BlockSpec((1, IN_A_COLS), lambda l: (0, 0)),
        ],
        out_specs=[pl.BlockSpec((1, rows, 2 * MEM_WIDTH), lambda l: (l, 0, 0)), slab],
        out_shape=[jax.ShapeDtypeStruct((depth, rows, 2 * MEM_WIDTH), BF16),
                   jax.ShapeDtypeStruct((D_MODEL, IN_A_COLS), BF16)],
        compiler_params=_params("arbitrary"),
        name="mem_kv",
    )(mem.reshape(rows, D_MODEL), mem_norm_g.reshape(depth, 1, D_MODEL), w_mem_kv,
      w_attn_in, _A_SCALE.reshape(1, IN_A_COLS))
    return kv.reshape(depth, b, N_MEM, 2 * MEM_WIDTH), w0


def _rope_tables(pos_ref, freq_ref, tbl_scr):
    rows = tbl_scr.shape[1]
    freq = freq_ref[...]
    one = jnp.ones((V7X_SUBLANES, V7X_LANES), F32)
    zero = jnp.zeros((V7X_SUBLANES, V7X_LANES), F32)
    groups = V7X_LANES // V7X_SUBLANES
    per_head = HEAD_DIM // V7X_SUBLANES

    def lane_rows(first, second, other):
        pieces = []
        for i in range(groups):
            pieces.append(first if i % per_head == 0 else second if i % per_head == 1 else other)
        return jnp.concatenate(pieces, axis=0).T

    for c in range(rows // V7X_LANES):
        sl = slice(V7X_LANES * c, V7X_LANES * (c + 1))
        ang = pos_ref[0, :, sl].astype(F32) * freq
        cs = jnp.cos(ang)
        sn = jnp.sin(ang)
        tbl_scr[0, sl, :] = lane_rows(cs, cs, one)
        tbl_scr[1, sl, :] = lane_rows(-sn, zero, zero)
        tbl_scr[2, sl, :] = lane_rows(zero, sn, zero)


def _rope(a, cs, sn_next, sn_prev):
    return (a * cs + pltpu.roll(a, V7X_LANES - ROT_HALF, 1) * sn_next
            + pltpu.roll(a, ROT_HALF, 1) * sn_prev)


def _proj0_kernel(x_ref, g_ref, pos_ref, freq_ref, w_ref, w1_ref, wo0_ref, wo1_ref,
                  o0_ref, oqm_ref, oz_ref, o1_ref, o2_ref, w1b_ref, wo0b_ref, wo1b_ref,
                  hn_scr, perm_scr, tbl_scr):
    tm = x_ref.shape[1]
    x = x_ref[0]
    ms = jnp.mean(x * x, axis=-1, keepdims=True)
    hn = x * lax.rsqrt(ms + EPS) * g_ref[...]
    hb = hn.astype(BF16)
    n_slab = D_MODEL // V7X_LANES
    for c in range(n_slab):
        hn_scr[c] = hn[:, V7X_LANES * c:V7X_LANES * (c + 1)]
    _rope_tables(pos_ref, freq_ref, tbl_scr)

    def qkv_chunk(lhs, g, j, tables):
        part, half = divmod(j, GROUP_WIDTH // COL_CHUNK)
        c0 = A_PART * part + GROUP_WIDTH * g + COL_CHUNK * half
        acc = _dot(lhs, w_ref[:, c0:c0 + COL_CHUNK])
        if part < 2:
            acc = jnp.concatenate(
                [_rope(acc[:, :V7X_LANES], *tables), _rope(acc[:, V7X_LANES:], *tables)], axis=1)
        return acc.astype(BF16)

    tables = tuple(tbl_scr[t] for t in range(3))
    for j in range(QKV_G // COL_CHUNK):
        o0_ref[0, :, COL_CHUNK * j:COL_CHUNK * (j + 1)] = qkv_chunk(hb, 0, j, tables)
    oqm_ref[0] = _dot(hb, w_ref[:, A_QM:A_QM + MEM_WIDTH]).astype(BF16)
    for j in range(BRANCH_A // COL_CHUNK):
        oz_ref[0, :, COL_CHUNK * j:COL_CHUNK * (j + 1)] = _dot(
            hb, w_ref[:, A_Z + COL_CHUNK * j:A_Z + COL_CHUNK * (j + 1)])

    for g, o_ref in ((1, o1_ref), (2, o2_ref)):
        d = DILATIONS[g]
        n = tm // d
        for r in range(d):
            for c in range(n_slab):
                perm_scr[r * n:(r + 1) * n, V7X_LANES * c:V7X_LANES * (c + 1)] = (
                    hn_scr[c, pl.ds(r, n, stride=d), :].astype(BF16))
        tables = tuple(
            jnp.concatenate([tbl_scr[t, pl.ds(r, n, stride=d), :] for r in range(d)], axis=0)
            for t in range(3))
        lhs = perm_scr[...]
        for j in range(QKV_G // COL_CHUNK):
            res = qkv_chunk(lhs, g, j, tables)
            for r in range(d):
                o_ref[0, r, :, COL_CHUNK * j:COL_CHUNK * (j + 1)] = res[r * n:(r + 1) * n]

    w1b_ref[:, :B_QM] = w1_ref[:, :B_QM].astype(BF16)
    w1b_ref[:, B_QM:B_Z] = (w1_ref[:, B_QM:B_Z] * SCORE_SCALE).astype(BF16)
    w1b_ref[:, B_Z:] = w1_ref[:, B_Z:].astype(BF16)
    wo0b_ref[...] = wo0_ref[...].astype(BF16)
    wo1b_ref[...] = wo1_ref[...].astype(BF16)


def _proj0(x, norm_g, pos_rows, freq_rows, w0, w1, wo0, wo1):
    b, s, _ = x.shape
    tm = ROW_TILE
    per_seq = s // tm
    n_steps = b * per_seq
    row = lambda width: pl.BlockSpec((1, tm, width), lambda i, j: (i, j, 0))
    stream = lambda d: pl.BlockSpec((1, d, tm // d, QKV_G), lambda i, j: (i, 0, j, 0))

    def slab(w, steps):
        rows = w.shape[0] // steps
        assert rows * steps == w.shape[0] and rows % BF16_SUBLANES == 0
        return pl.BlockSpec((rows, w.shape[1]), lambda i, j: (jnp.minimum(i * per_seq + j, steps - 1), 0))

    slabs = [slab(w1, n_steps), slab(wo0, n_steps // 2), slab(wo1, n_steps // 2)]
    d1, d2 = DILATIONS[1], DILATIONS[2]
    return pl.pallas_call(
        _proj0_kernel,
        grid=(b, per_seq),
        in_specs=[
            row(D_MODEL),
            pl.BlockSpec((1, D_MODEL), lambda i, j: (0, 0)),
            pl.BlockSpec((1, 1, tm), lambda i, j: (i, 0, j)),
            pl.BlockSpec((V7X_SUBLANES, V7X_LANES), lambda i, j: (0, 0)),
            _resident((D_MODEL, IN_A_COLS), lambda i, j: (0, 0)),
        ] + slabs,
        out_specs=[row(QKV_G), row(MEM_WIDTH), row(BRANCH_A), stream(d1), stream(d2)] + slabs,
        out_shape=[
            jax.ShapeDtypeStruct((b, s, QKV_G), BF16),
            jax.ShapeDtypeStruct((b, s, MEM_WIDTH), BF16),
            jax.ShapeDtypeStruct((b, s, BRANCH_A), F32),
            jax.ShapeDtypeStruct((b, d1, s // d1, QKV_G), BF16),
            jax.ShapeDtypeStruct((b, d2, s // d2, QKV_G), BF16),
            jax.ShapeDtypeStruct(w1.shape, BF16),
            jax.ShapeDtypeStruct(wo0.shape, BF16),
            jax.ShapeDtypeStruct(wo1.shape, BF16),
        ],
        scratch_shapes=[
            pltpu.VMEM((D_MODEL // V7X_LANES, tm, V7X_LANES), F32),
            pltpu.VMEM((tm, D_MODEL), BF16),
            pltpu.VMEM((3, tm, V7X_LANES), F32),
        ],
        compiler_params=_params("arbitrary", "arbitrary"),
        name="proj0",
    )(x, norm_g, pos_rows, freq_rows, w0, w1, wo0, wo1)


def _attn_kernel(q0, k0, v0, q1, k1, v1, q2, k2, v2, o_ref,
                 num_scr, m_scr, l_scr, ve_scr, bias2_scr, bias1_scr):
    s_len = o_ref.shape[1]
    half0, half1 = _half_masks()
    first_half = lax.broadcasted_iota(jnp.int32, (BLOCK, V7X_LANES), 1) < HEAD_DIM

    @pl.when((pl.program_id(0) == 0) & (pl.program_id(1) == 0))
    def _():
        qi = lax.broadcasted_iota(jnp.int32, (2 * BLOCK, 2 * BLOCK), 0) & (BLOCK - 1)
        kj = lax.broadcasted_iota(jnp.int32, (2 * BLOCK, 2 * BLOCK), 1)
        valid_two = ((kj < BLOCK) & (kj >= qi)) | ((kj >= BLOCK) & (kj - BLOCK <= qi))
        bias2_scr[...] = jnp.where(valid_two, 0.0, NEG_BIG)
        valid_one = (lax.broadcasted_iota(jnp.int32, (2 * BLOCK, BLOCK), 1)
                     <= (lax.broadcasted_iota(jnp.int32, (2 * BLOCK, BLOCK), 0) & (BLOCK - 1)))
        bias1_scr[...] = jnp.where(valid_one, 0.0, NEG_BIG)
        for g in range(N_GROUPS):
            for h, half in enumerate((half0, half1)):
                ve_scr[g, h, :, V7X_LANES:] = jnp.broadcast_to(half, (s_len, V7X_LANES))

    for g, v_ref in enumerate((v0, v1, v2)):
        v = v_ref[0]
        for h, half in enumerate((half0, half1)):
            ve_scr[g, h, :, :V7X_LANES] = v * half

    def block(q_ref, k_ref, g, q_row, k_row, n_keys, bias_ref, dst):
        q = q_ref[0, pl.ds(q_row, BLOCK), :]
        qs = jnp.concatenate([q * half0, q * half1], axis=0)
        k = k_ref[0, pl.ds(k_row, n_keys), :]
        s = _dot_nt(qs, k) + bias_ref[...]
        m = jnp.max(s, axis=-1, keepdims=True)
        p = jnp.exp2(s - m).astype(BF16)
        p_cat = jnp.concatenate([p[:BLOCK], p[BLOCK:]], axis=1)
        keys = pl.ds(k_row, n_keys)
        ve = jnp.concatenate([ve_scr[g, 0, keys, :], ve_scr[g, 1, keys, :]], axis=0)
        pv = _dot(p_cat, ve)
        num_scr[g, dst, :] = pv[:, :V7X_LANES]
        l_scr[g, dst, :] = pv[:, V7X_LANES:]
        m_scr[g, dst, :] = jnp.where(first_half, m[:BLOCK], m[BLOCK:])

    d1, d2 = DILATIONS[1], DILATIONS[2]
    stream_len = s_len // d1
    sub = d2 // d1

    for r in range(d2):
        block(q2, k2, 2, r * BLOCK, r * BLOCK, BLOCK, bias1_scr,
              pl.ds((r % d1) * stream_len + r // d1, BLOCK, stride=sub))

    for r in range(d1):
        base = r * stream_len
        block(q1, k1, 1, base, base, BLOCK, bias1_scr, pl.ds(base, BLOCK))
        for n in range(1, stream_len // BLOCK):
            block(q1, k1, 1, base + n * BLOCK, base + (n - 1) * BLOCK, 2 * BLOCK, bias2_scr,
                  pl.ds(base + n * BLOCK, BLOCK))

    def merge(n):
        rows_per = BLOCK // d1
        for r in range(d1):
            tok = pl.ds(n * BLOCK + r, rows_per, stride=d1)
            rows = pl.ds(r * stream_len + n * rows_per, rows_per)
            idx = (tok, rows, rows)
            ms = [m_scr[g, idx[g], :] for g in range(N_GROUPS)]
            top = jnp.maximum(ms[0], jnp.maximum(ms[1], ms[2]))
            ws = [jnp.exp2(mg - top) for mg in ms]
            nums = [ws[g] * num_scr[g, idx[g], :] for g in range(N_GROUPS)]
            dens = [ws[g] * l_scr[g, idx[g], :] for g in range(N_GROUPS)]
            num = nums[0] + nums[1] + nums[2]
            den = dens[0] + dens[1] + dens[2]
            o_ref[0, tok, :] = num / den

    block(q0, k0, 0, 0, 0, BLOCK, bias1_scr, pl.ds(0, BLOCK))
    for n in range(1, s_len // BLOCK):
        block(q0, k0, 0, n * BLOCK, (n - 1) * BLOCK, 2 * BLOCK, bias2_scr, pl.ds(n * BLOCK, BLOCK))
        merge(n - 1)
    merge(s_len // BLOCK - 1)


def _attn(qkv0, qkv1, qkv2):
    b, s, _ = qkv0.shape
    n_pair = GROUP_WIDTH // V7X_LANES

    def part(k):
        return pl.BlockSpec((1, s, V7X_LANES), lambda i, p, k=k: (i, 0, k * n_pair + p))

    specs = [part(0), part(1), part(2)] * N_GROUPS
    stat = pltpu.VMEM((N_GROUPS, s, V7X_LANES), F32)
    scratch = [
        stat, stat, stat,
        pltpu.VMEM((N_GROUPS, 2, s, 2 * V7X_LANES), BF16),
        pltpu.VMEM((2 * BLOCK, 2 * BLOCK), F32),
        pltpu.VMEM((2 * BLOCK, BLOCK), F32),
    ]
    return pl.pallas_call(
        _attn_kernel,
        grid=(b, n_pair),
        in_specs=specs,
        out_specs=pl.BlockSpec((1, s, V7X_LANES), lambda i, p: (i, 0, p)),
        out_shape=jax.ShapeDtypeStruct((b, s, GROUP_WIDTH), F32),
        scratch_shapes=scratch,
        compiler_params=_params("arbitrary", "arbitrary"),
        name="dilated_attn",
    )(qkv0, qkv0, qkv0, qkv1, qkv1, qkv1, qkv2, qkv2, qkv2)


def _mem_scores(qm, kv_ref):
    rows = qm.shape[0]
    half0, half1 = _half_masks()
    scores = []
    for p in range(MEM_WIDTH // V7X_LANES):
        k = kv_ref[0, 0, :, V7X_LANES * p:V7X_LANES * (p + 1)]
        for c in range(rows // MEM_ROWS):
            q = qm[MEM_ROWS * c:MEM_ROWS * (c + 1), V7X_LANES * p:V7X_LANES * (p + 1)]
            scores.append(_dot_nt(jnp.concatenate([q * half0, q * half1], axis=0), k))
    return scores


def _mem_softmax_pv(scores, kv_ref):
    half0, half1 = _half_masks()
    n_pairs = MEM_WIDTH // V7X_LANES
    per_pair = len(scores) // n_pairs
    outs = []
    for p in range(n_pairs):
        v = kv_ref[0, 0, :, MEM_WIDTH + V7X_LANES * p:MEM_WIDTH + V7X_LANES * (p + 1)]
        ve = jnp.concatenate(
            [jnp.concatenate([v * half, jnp.broadcast_to(half, (N_MEM, V7X_LANES))], axis=1)
             for half in (half0, half1)], axis=0)
        chunks = []
        for s in scores[per_pair * p:per_pair * (p + 1)]:
            e = jnp.exp2(s - jnp.max(s, axis=-1, keepdims=True)).astype(BF16)
            pv = _dot(jnp.concatenate([e[:MEM_ROWS], e[MEM_ROWS:]], axis=1), ve)
            chunks.append(pv[:, :V7X_LANES] / pv[:, V7X_LANES:])
        outs.append(jnp.concatenate(chunks, axis=0))
    return jnp.concatenate(outs, axis=1)


def _tail_kernel(x_ref, mix_ref, qm0_ref, z0_ref, kv0_ref, wo0_ref, g1_ref, w1_ref, cw_ref,
                 kv1_ref, wo1_ref, gf_ref, out_ref,
                 a_scr, y_scr, h1_scr, hb_scr, *, tiles_per_seq):
    tm = x_ref.shape[1]
    halo = V7X_SUBLANES
    n_out = D_MODEL // COL_CHUNK
    step = pl.program_id(0)

    @pl.when(step == 0)
    def _():
        h1_scr[...] = jnp.zeros(h1_scr.shape, F32)
        a_scr[0:halo, :] = jnp.zeros((halo, CONV_WIDTH), F32)

    scores0 = _mem_scores(qm0_ref[0], kv0_ref)

    h1 = h1_scr[...]
    ms = jnp.mean(h1 * h1, axis=-1, keepdims=True)
    hb_scr[...] = (h1 * lax.rsqrt(ms + EPS) * g1_ref[...]).astype(BF16)
    seq_start = (jnp.maximum(step - 1, 0) % tiles_per_seq) == 0
    a_scr[0:halo, :] = jnp.where(seq_start, 0.0, a_scr[0:halo, :])

    def proj1(c0, width=COL_CHUNK):
        return _dot(hb_scr[...], w1_ref[:, c0:c0 + width])

    def conv_chunk(j):
        cs = slice(COL_CHUNK * j, COL_CHUNK * (j + 1))
        a_now = proj1(B_CG + COL_CHUNK * j) * proj1(B_U + COL_CHUNK * j)
        a_scr[halo:halo + tm, cs] = a_now
        conv = (cw_ref[0:1, cs] * a_scr[halo - 2:halo - 2 + tm, cs]
                + cw_ref[1:2, cs] * a_scr[halo - 1:halo - 1 + tm, cs]
                + cw_ref[2:3, cs] * a_now)
        mix1 = proj1(B_BG + COL_CHUNK * j) * conv
        y_scr[:, cs] = (mix1 * _silu(proj1(B_Z + COL_CHUNK * j))).astype(BF16)

    qm1 = proj1(B_QM, MEM_WIDTH).astype(BF16)
    z_mem = proj1(B_Z + CONV_WIDTH, MEM_WIDTH)
    conv_chunk(0)
    scores1 = _mem_scores(qm1, kv1_ref)
    mem0 = _mem_softmax_pv(scores0, kv0_ref)
    z0 = z0_ref[0]
    y0 = jnp.concatenate(
        [mix_ref[0] * _silu(z0[:, :GROUP_WIDTH]), mem0 * _silu(z0[:, GROUP_WIDTH:])], axis=1).astype(BF16)
    conv_chunk(1)
    mem1 = _mem_softmax_pv(scores1, kv1_ref)
    y_scr[:, CONV_WIDTH:BRANCH_B] = (mem1 * _silu(z_mem)).astype(BF16)
    conv_chunk(2)
    conv_chunk(3)
    a_scr[0:halo, :] = a_scr[tm:tm + halo, :]

    y1 = y_scr[...]
    ssq = jnp.zeros((tm, 1), F32)
    for j in range(n_out):
        cs = slice(COL_CHUNK * j, COL_CHUNK * (j + 1))
        h2 = h1_scr[:, cs] + _dot(y1, wo1_ref[:, cs])
        ssq = ssq + jnp.sum(h2 * h2, axis=-1, keepdims=True)
        out_ref[0, :, cs] = h2
    scale = lax.rsqrt(ssq * (1.0 / D_MODEL) + EPS)
    out_ref[0] = out_ref[0] * scale * gf_ref[...]

    for j in range(n_out):
        cs = slice(COL_CHUNK * j, COL_CHUNK * (j + 1))
        h1_scr[:, cs] = x_ref[0, :, cs] + _dot(y0, wo0_ref[:, cs])


def _tail(x, mix, qm0, z0, kv, wo0, norm_g1, w1, conv_w, wo1, final_g):
    b, s, _ = x.shape
    tm = ROW_TILE
    per_seq = s // tm
    n_tiles = b * per_seq

    def tile_a(j):
        return jnp.minimum(j, n_tiles - 1)

    def tile_b(j):
        return jnp.maximum(j - 1, 0)

    def row(width, tile):
        return pl.BlockSpec((1, tm, width), lambda j: (tile(j) // per_seq, tile(j) % per_seq, 0))

    def kv_spec(layer, tile):
        return pl.BlockSpec((1, 1, N_MEM, 2 * MEM_WIDTH), lambda j: (layer, tile(j) // per_seq, 0, 0))

    const = lambda shape: pl.BlockSpec(shape, lambda j: (0, 0))
    tile_f32 = pltpu.VMEM((tm, D_MODEL), F32)
    tile_bf16 = pltpu.VMEM((tm, D_MODEL), BF16)
    return pl.pallas_call(
        functools.partial(_tail_kernel, tiles_per_seq=per_seq),
        grid=(n_tiles + 1,),
        in_specs=[
            row(D_MODEL, tile_a), row(GROUP_WIDTH, tile_a), row(MEM_WIDTH, tile_a), row(BRANCH_A, tile_a),
            kv_spec(0, tile_a),
            _resident(wo0.shape, lambda j: (0, 0)),
            const((1, D_MODEL)),
            _resident(w1.shape, lambda j: (0, 0)),
            const((3, CONV_WIDTH)),
            kv_spec(1, tile_b),
            _resident(wo1.shape, lambda j: (0, 0)),
            const((1, D_MODEL)),
        ],
        out_specs=row(D_MODEL, tile_b),
        out_shape=jax.ShapeDtypeStruct((b, s, D_MODEL), F32),
        scratch_shapes=[
            pltpu.VMEM((tm + V7X_SUBLANES, CONV_WIDTH), F32),
            pltpu.VMEM((tm, BRANCH_B), BF16),
            tile_f32, tile_bf16,
        ],
        compiler_params=_params("arbitrary"),
        name="tail",
    )(x, mix, qm0, z0, kv, wo0, norm_g1, w1, conv_w, kv, wo1, final_g)


def kernel(x, mem, positions, norm_g, mem_norm_g, w_mem_kv, attn_w_in, attn_w_out, conv_w_in, conv_w,
           conv_w_out, final_g):
    b, s, _ = x.shape

    inv_freq = ROPE_THETA ** (-jnp.arange(ROT_HALF, dtype=F32) * (2.0 / ROT_DIM))
    freq_rows = jnp.broadcast_to(inv_freq[:, None], (ROT_HALF, V7X_LANES))
    pos_rows = positions.reshape(b, 1, s)

    kv, w0 = _mem_kv(mem, mem_norm_g, w_mem_kv, attn_w_in[0])
    qkv0, qm0, z0, qkv1, qkv2, w1, wo0, wo1 = _proj0(
        x, norm_g[0:1], pos_rows, freq_rows, w0, conv_w_in[0], attn_w_out[0], conv_w_out[0])
    qkv1 = qkv1.reshape(b, s, QKV_G)
    qkv2 = qkv2.reshape(b, s, QKV_G)
    mix = _attn(qkv0, qkv1, qkv2)
    return _tail(x, mix, qm0, z0, kv, wo0, norm_g[1:2], w1, conv_w[0], wo1, final_g.reshape(1, D_MODEL))
```

```python
import functools

import numpy as np
import jax
import jax.numpy as jnp
from jax import lax
from jax.experimental import pallas as pl
from jax.experimental.pallas import tpu as pltpu

F32 = jnp.float32
BF16 = jnp.bfloat16

D_MODEL = 1024
HEAD_DIM = 64
ROT_DIM = 16
ROT_HALF = ROT_DIM // 2
ROPE_THETA = 500000.0
DILATIONS = (1, 4, 16)
BLOCK = 128
GROUP_WIDTH = 512
N_GROUPS = 3
N_MEM = 256
MEM_WIDTH = 256
CONV_WIDTH = 1024
EPS = 1e-6
SCORE_SCALE = HEAD_DIM ** -0.5 * float(np.log2(np.e))

V7X_LANES = 128
V7X_SUBLANES = 8
BF16_SUBLANES = 16
ROW_TILE = 512
COL_CHUNK = 256
VMEM_LIMIT_BYTES = 56 * 1024 * 1024
NEG_BIG = -1e30
MEM_ROWS = 256

QKV_G = 3 * GROUP_WIDTH
A_PART = N_GROUPS * GROUP_WIDTH
A_QM = 3 * A_PART
A_Z = A_QM + MEM_WIDTH
BRANCH_A = GROUP_WIDTH + MEM_WIDTH
IN_A_COLS = A_Z + BRANCH_A
B_BG, B_CG, B_U = 0, CONV_WIDTH, 2 * CONV_WIDTH
B_QM = 3 * CONV_WIDTH
B_Z = B_QM + MEM_WIDTH
BRANCH_B = CONV_WIDTH + MEM_WIDTH
IN_B_COLS = B_Z + BRANCH_B


def _column_scale(n_cols, scaled):
    scale = np.ones((n_cols,), np.float32)
    for lo, hi in scaled:
        scale[lo:hi] = SCORE_SCALE
    return scale


_A_SCALE = _column_scale(IN_A_COLS, [(0, A_PART), (A_QM, A_QM + MEM_WIDTH)])


def _params(*sem):
    return pltpu.CompilerParams(dimension_semantics=sem, vmem_limit_bytes=VMEM_LIMIT_BYTES)


def _resident(shape, index_map):
    return pl.BlockSpec(shape, index_map, pipeline_mode=pl.Buffered(1))


def _silu(z):
    return z * jax.nn.sigmoid(z)


def _dot(a, b):
    return jnp.dot(a, b, preferred_element_type=F32)


def _dot_nt(a, b):
    return lax.dot_general(a, b, (((1,), (1,)), ((), ())), preferred_element_type=F32)


def _half_masks():
    lane = lax.broadcasted_iota(jnp.int32, (1, V7X_LANES), 1)
    return (jnp.where(lane < HEAD_DIM, 1.0, 0.0).astype(BF16),
            jnp.where(lane < HEAD_DIM, 0.0, 1.0).astype(BF16))


def _memkv_kernel(mem_ref, g_ref, w_ref, wa_ref, scale_ref, kv_ref, wab_ref):
    w = w_ref[0].astype(BF16)
    for c in range(mem_ref.shape[0] // ROW_TILE):
        rows = slice(ROW_TILE * c, ROW_TILE * (c + 1))
        m = mem_ref[rows, :]
        ms = jnp.mean(m * m, axis=-1, keepdims=True)
        mn = (m * lax.rsqrt(ms + EPS) * g_ref[0]).astype(BF16)
        kv_ref[0, rows, :] = _dot(mn, w).astype(BF16)
    for c in range(wa_ref.shape[0] // V7X_LANES):
        rows = slice(V7X_LANES * c, V7X_LANES * (c + 1))
        wab_ref[rows, :] = (wa_ref[rows, :] * scale_ref[...]).astype(BF16)


def _mem_kv(mem, mem_norm_g, w_mem_kv, w_attn_in):
    b = mem.shape[0]
    depth = w_mem_kv.shape[0]
    rows = b * N_MEM
    slab = pl.BlockSpec((D_MODEL // depth, IN_A_COLS), lambda l: (l, 0))
    kv, w0 = pl.pallas_call(
        _memkv_kernel,
        grid=(depth,),
        in_specs=[
            _resident((rows, D_MODEL), lambda l: (0, 0)),
            pl.BlockSpec((1, 1, D_MODEL), lambda l: (l, 0, 0)),
            pl.BlockSpec((1, D_MODEL, 2 * MEM_WIDTH), lambda l: (l, 0, 0)),
            slab,
            pl.BlockSpec((1, IN_A_COLS), lambda l: (0, 0)),
        ],
        out_specs=[pl.BlockSpec((1, rows, 2 * MEM_WIDTH), lambda l: (l, 0, 0)), slab],
        out_shape=[jax.ShapeDtypeStruct((depth, rows, 2 * MEM_WIDTH), BF16),
                   jax.ShapeDtypeStruct((D_MODEL, IN_A_COLS), BF16)],
        compiler_params=_params("arbitrary"),
        name="mem_kv",
    )(mem.reshape(rows, D_MODEL), mem_norm_g.reshape(depth, 1, D_MODEL), w_mem_kv,
      w_attn_in, _A_SCALE.reshape(1, IN_A_COLS))
    return kv.reshape(depth, b, N_MEM, 2 * MEM_WIDTH), w0


def _rope_tables(pos_ref, freq_ref, tbl_scr):
    rows = tbl_scr.shape[1]
    freq = freq_ref[...]
    one = jnp.ones((V7X_SUBLANES, V7X_LANES), F32)
    zero = jnp.zeros((V7X_SUBLANES, V7X_LANES), F32)
    groups = V7X_LANES // V7X_SUBLANES
    per_head = HEAD_DIM // V7X_SUBLANES

    def lane_rows(first, second, other):
        pieces = []
        for i in range(groups):
            pieces.append(first if i % per_head == 0 else second if i % per_head == 1 else other)
        return jnp.concatenate(pieces, axis=0).T

    for c in range(rows // V7X_LANES):
        sl = slice(V7X_LANES * c, V7X_LANES * (c + 1))
        ang = pos_ref[0, :, sl].astype(F32) * freq
        cs = jnp.cos(ang)
        sn = jnp.sin(ang)
        tbl_scr[0, sl, :] = lane_rows(cs, cs, one)
        tbl_scr[1, sl, :] = lane_rows(-sn, zero, zero)
        tbl_scr[2, sl, :] = lane_rows(zero, sn, zero)


def _rope(a, cs, sn_next, sn_prev):
    return (a * cs + pltpu.roll(a, V7X_LANES - ROT_HALF, 1) * sn_next
            + pltpu.roll(a, ROT_HALF, 1) * sn_prev)


def _proj0_kernel(x_ref, g_ref, pos_ref, freq_ref, w_ref, w1_ref, wo0_ref, wo1_ref,
                  o0_ref, oqm_ref, oz_ref, o1_ref, o2_ref, w1b_ref, wo0b_ref, wo1b_ref,
                  hn_scr, perm_scr, tbl_scr):
    tm = x_ref.shape[1]
    x = x_ref[0]
    ms = jnp.mean(x * x, axis=-1, keepdims=True)
    hn = x * lax.rsqrt(ms + EPS) * g_ref[...]
    hb = hn.astype(BF16)
    n_slab = D_MODEL // V7X_LANES
    for c in range(n_slab):
        hn_scr[c] = hn[:, V7X_LANES * c:V7X_LANES * (c + 1)]
    _rope_tables(pos_ref, freq_ref, tbl_scr)

    def qkv_chunk(lhs, g, j, tables):
        part, half = divmod(j, GROUP_WIDTH // COL_CHUNK)
        c0 = A_PART * part + GROUP_WIDTH * g + COL_CHUNK * half
        acc = _dot(lhs, w_ref[:, c0:c0 + COL_CHUNK])
        if part < 2:
            acc = jnp.concatenate(
                [_rope(acc[:, :V7X_LANES], *tables), _rope(acc[:, V7X_LANES:], *tables)], axis=1)
        return acc.astype(BF16)

    tables = tuple(tbl_scr[t] for t in range(3))
    for j in range(QKV_G // COL_CHUNK):
        o0_ref[0, :, COL_CHUNK * j:COL_CHUNK * (j + 1)] = qkv_chunk(hb, 0, j, tables)
    oqm_ref[0] = _dot(hb, w_ref[:, A_QM:A_QM + MEM_WIDTH]).astype(BF16)
    for j in range(BRANCH_A // COL_CHUNK):
        oz_ref[0, :, COL_CHUNK * j:COL_CHUNK * (j + 1)] = _dot(
            hb, w_ref[:, A_Z + COL_CHUNK * j:A_Z + COL_CHUNK * (j + 1)])

    for g, o_ref in ((1, o1_ref), (2, o2_ref)):
        d = DILATIONS[g]
        n = tm // d
        for r in range(d):
            for c in range(n_slab):
                perm_scr[r * n:(r + 1) * n, V7X_LANES * c:V7X_LANES * (c + 1)] = (
                    hn_scr[c, pl.ds(r, n, stride=d), :].astype(BF16))
        tables = tuple(
            jnp.concatenate([tbl_scr[t, pl.ds(r, n, stride=d), :] for r in range(d)], axis=0)
            for t in range(3))
        lhs = perm_scr[...]
        for j in range(QKV_G // COL_CHUNK):
            res = qkv_chunk(lhs, g, j, tables)
            for r in range(d):
                o_ref[0, r, :, COL_CHUNK * j:COL_CHUNK * (j + 1)] = res[r * n:(r + 1) * n]

    w1b_ref[:, :B_QM] = w1_ref[:, :B_QM].astype(BF16)
    w1b_ref[:, B_QM:B_Z] = (w1_ref[:, B_QM:B_Z] * SCORE_SCALE).astype(BF16)
    w1b_ref[:, B_Z:] = w1_ref[:, B_Z:].astype(BF16)
    wo0b_ref[...] = wo0_ref[...].astype(BF16)
    wo1b_ref[...] = wo1_ref[...].astype(BF16)


def _proj0(x, norm_g, pos_rows, freq_rows, w0, w1, wo0, wo1):
    b, s, _ = x.shape
    tm = ROW_TILE
    per_seq = s // tm
    n_steps = b * per_seq
    row = lambda width: pl.BlockSpec((1, tm, width), lambda i, j: (i, j, 0))
    stream = lambda d: pl.BlockSpec((1, d, tm // d, QKV_G), lambda i, j: (i, 0, j, 0))

    def slab(w, steps):
        rows = w.shape[0] // steps
        assert rows * steps == w.shape[0] and rows % BF16_SUBLANES == 0
        return pl.BlockSpec((rows, w.shape[1]), lambda i, j: (jnp.minimum(i * per_seq + j, steps - 1), 0))

    slabs = [slab(w1, n_steps), slab(wo0, n_steps // 2), slab(wo1, n_steps // 2)]
    d1, d2 = DILATIONS[1], DILATIONS[2]
    return pl.pallas_call(
        _proj0_kernel,
        grid=(b, per_seq),
        in_specs=[
            row(D_MODEL),
            pl.BlockSpec((1, D_MODEL), lambda i, j: (0, 0)),
            pl.BlockSpec((1, 1, tm), lambda i, j: (i, 0, j)),
            pl.BlockSpec((V7X_SUBLANES, V7X_LANES), lambda i, j: (0, 0)),
            _resident((D_MODEL, IN_A_COLS), lambda i, j: (0, 0)),
        ] + slabs,
        out_specs=[row(QKV_G), row(MEM_WIDTH), row(BRANCH_A), stream(d1), stream(d2)] + slabs,
        out_shape=[
            jax.ShapeDtypeStruct((b, s, QKV_G), BF16),
            jax.ShapeDtypeStruct((b, s, MEM_WIDTH), BF16),
            jax.ShapeDtypeStruct((b, s, BRANCH_A), F32),
            jax.ShapeDtypeStruct((b, d1, s // d1, QKV_G), BF16),
            jax.ShapeDtypeStruct((b, d2, s // d2, QKV_G), BF16),
            jax.ShapeDtypeStruct(w1.shape, BF16),
            jax.ShapeDtypeStruct(wo0.shape, BF16),
            jax.ShapeDtypeStruct(wo1.shape, BF16),
        ],
        scratch_shapes=[
            pltpu.VMEM((D_MODEL // V7X_LANES, tm, V7X_LANES), F32),
            pltpu.VMEM((tm, D_MODEL), BF16),
            pltpu.VMEM((3, tm, V7X_LANES), F32),
        ],
        compiler_params=_params("arbitrary", "arbitrary"),
        name="proj0",
    )(x, norm_g, pos_rows, freq_rows, w0, w1, wo0, wo1)


def _attn_kernel(q0, k0, v0, q1, k1, v1, q2, k2, v2, o_ref,
                 num_scr, m_scr, l_scr, ve_scr, bias2_scr, bias1_scr):
    s_len = o_ref.shape[1]
    half0, half1 = _half_masks()
    first_half = lax.broadcasted_iota(jnp.int32, (BLOCK, V7X_LANES), 1) < HEAD_DIM

    @pl.when((pl.program_id(0) == 0) & (pl.program_id(1) == 0))
    def _():
        qi = lax.broadcasted_iota(jnp.int32, (2 * BLOCK, 2 * BLOCK), 0) & (BLOCK - 1)
        kj = lax.broadcasted_iota(jnp.int32, (2 * BLOCK, 2 * BLOCK), 1)
        valid_two = ((kj < BLOCK) & (kj >= qi)) | ((kj >= BLOCK) & (kj - BLOCK <= qi))
        bias2_scr[...] = jnp.where(valid_two, 0.0, NEG_BIG)
        valid_one = (lax.broadcasted_iota(jnp.int32, (2 * BLOCK, BLOCK), 1)
                     <= (lax.broadcasted_iota(jnp.int32, (2 * BLOCK, BLOCK), 0) & (BLOCK - 1)))
        bias1_scr[...] = jnp.where(valid_one, 0.0, NEG_BIG)
        for g in range(N_GROUPS):
            for h, half in enumerate((half0, half1)):
                ve_scr[g, h, :, V7X_LANES:] = jnp.broadcast_to(half, (s_len, V7X_LANES))

    for g, v_ref in enumerate((v0, v1, v2)):
        v = v_ref[0]
        for h, half in enumerate((half0, half1)):
            ve_scr[g, h, :, :V7X_LANES] = v * half

    def block(q_ref, k_ref, g, q_row, k_row, n_keys, bias_ref, dst):
        q = q_ref[0, pl.ds(q_row, BLOCK), :]
        qs = jnp.concatenate([q * half0, q * half1], axis=0)
        k = k_ref[0, pl.ds(k_row, n_keys), :]
        s = _dot_nt(qs, k) + bias_ref[...]
        m = jnp.max(s, axis=-1, keepdims=True)
        p = jnp.exp2(s - m).astype(BF16)
        p_cat = jnp.concatenate([p[:BLOCK], p[BLOCK:]], axis=1)
        keys = pl.ds(k_row, n_keys)
        ve = jnp.concatenate([ve_scr[g, 0, keys, :], ve_scr[g, 1, keys, :]], axis=0)
        pv = _dot(p_cat, ve)
        num_scr[g, dst, :] = pv[:, :V7X_LANES]
        l_scr[g, dst, :] = pv[:, V7X_LANES:]
        m_scr[g, dst, :] = jnp.where(first_half, m[:BLOCK], m[BLOCK:])

    d1, d2 = DILATIONS[1], DILATIONS[2]
    stream_len = s_len // d1
    sub = d2 // d1

    for r in range(d2):
        block(q2, k2, 2, r * BLOCK, r * BLOCK, BLOCK, bias1_scr,
              pl.ds((r % d1) * stream_len + r // d1, BLOCK, stride=sub))

    for r in range(d1):
        base = r * stream_len
        block(q1, k1, 1, base, base, BLOCK, bias1_scr, pl.ds(base, BLOCK))
        for n in range(1, stream_len // BLOCK):
            block(q1, k1, 1, base + n * BLOCK, base + (n - 1) * BLOCK, 2 * BLOCK, bias2_scr,
                  pl.ds(base + n * BLOCK, BLOCK))

    def merge(n):
        rows_per = BLOCK // d1
        for r in range(d1):
            tok = pl.ds(n * BLOCK + r, rows_per, stride=d1)
            rows = pl.ds(r * stream_len + n * rows_per, rows_per)
            idx = (tok, rows, rows)
            ms = [m_scr[g, idx[g], :] for g in range(N_GROUPS)]
            top = jnp.maximum(ms[0], jnp.maximum(ms[1], ms[2]))
            ws = [jnp.exp2(mg - top) for mg in ms]
            nums = [ws[g] * num_scr[g, idx[g], :] for g in range(N_GROUPS)]
            dens = [ws[g] * l_scr[g, idx[g], :] for g in range(N_GROUPS)]
            num = nums[0] + nums[1] + nums[2]
            den = dens[0] + dens[1] + dens[2]
            o_ref[0, tok, :] = num / den

    block(q0, k0, 0, 0, 0, BLOCK, bias1_scr, pl.ds(0, BLOCK))
    for n in range(1, s_len // BLOCK):
        block(q0, k0, 0, n * BLOCK, (n - 1) * BLOCK, 2 * BLOCK, bias2_scr, pl.ds(n * BLOCK, BLOCK))
        merge(n - 1)
    merge(s_len // BLOCK - 1)


def _attn(qkv0, qkv1, qkv2):
    b, s, _ = qkv0.shape
    n_pair = GROUP_WIDTH // V7X_LANES

    def part(k):
        return pl.BlockSpec((1, s, V7X_LANES), lambda i, p, k=k: (i, 0, k * n_pair + p))

    specs = [part(0), part(1), part(2)] * N_GROUPS
    stat = pltpu.VMEM((N_GROUPS, s, V7X_LANES), F32)
    scratch = [
        stat, stat, stat,
        pltpu.VMEM((N_GROUPS, 2, s, 2 * V7X_LANES), BF16),
        pltpu.VMEM((2 * BLOCK, 2 * BLOCK), F32),
        pltpu.VMEM((2 * BLOCK, BLOCK), F32),
    ]
    return pl.pallas_call(
        _attn_kernel,
        grid=(b, n_pair),
        in_specs=specs,
        out_specs=pl.BlockSpec((1, s, V7X_LANES), lambda i, p: (i, 0, p)),
        out_shape=jax.ShapeDtypeStruct((b, s, GROUP_WIDTH), F32),
        scratch_shapes=scratch,
        compiler_params=_params("arbitrary", "arbitrary"),
        name="dilated_attn",
    )(qkv0, qkv0, qkv0, qkv1, qkv1, qkv1, qkv2, qkv2, qkv2)


def _mem_scores(qm, kv_ref):
    rows = qm.shape[0]
    half0, half1 = _half_masks()
    scores = []
    for p in range(MEM_WIDTH // V7X_LANES):
        k = kv_ref[0, 0, :, V7X_LANES * p:V7X_LANES * (p + 1)]
        for c in range(rows // MEM_ROWS):
            q = qm[MEM_ROWS * c:MEM_ROWS * (c + 1), V7X_LANES * p:V7X_LANES * (p + 1)]
            scores.append(_dot_nt(jnp.concatenate([q * half0, q * half1], axis=0), k))
    return scores


def _mem_softmax_pv(scores, kv_ref):
    half0, half1 = _half_masks()
    n_pairs = MEM_WIDTH // V7X_LANES
    per_pair = len(scores) // n_pairs
    outs = []
    for p in range(n_pairs):
        v = kv_ref[0, 0, :, MEM_WIDTH + V7X_LANES * p:MEM_WIDTH + V7X_LANES * (p + 1)]
        ve = jnp.concatenate(
            [jnp.concatenate([v * half, jnp.broadcast_to(half, (N_MEM, V7X_LANES))], axis=1)
             for half in (half0, half1)], axis=0)
        chunks = []
        for s in scores[per_pair * p:per_pair * (p + 1)]:
            e = jnp.exp2(s - jnp.max(s, axis=-1, keepdims=True)).astype(BF16)
            pv = _dot(jnp.concatenate([e[:MEM_ROWS], e[MEM_ROWS:]], axis=1), ve)
            chunks.append(pv[:, :V7X_LANES] / pv[:, V7X_LANES:])
        outs.append(jnp.concatenate(chunks, axis=0))
    return jnp.concatenate(outs, axis=1)


def _tail_kernel(x_ref, mix_ref, qm0_ref, z0_ref, kv0_ref, wo0_ref, g1_ref, w1_ref, cw_ref,
                 kv1_ref, wo1_ref, gf_ref, out_ref,
                 a_scr, y_scr, h1_scr, hb_scr, *, tiles_per_seq):
    tm = x_ref.shape[1]
    halo = V7X_SUBLANES
    n_out = D_MODEL // COL_CHUNK
    step = pl.program_id(0)
    last = pl.num_programs(0) - 1

    def a_scores():
        return _mem_scores(qm0_ref[0], kv0_ref)

    def a_gate(scores0):
        mem0 = _mem_softmax_pv(scores0, kv0_ref)
        z0 = z0_ref[0]
        return jnp.concatenate(
            [mix_ref[0] * _silu(z0[:, :GROUP_WIDTH]), mem0 * _silu(z0[:, GROUP_WIDTH:])], axis=1).astype(BF16)

    def a_out(y0):
        for j in range(n_out):
            cs = slice(COL_CHUNK * j, COL_CHUNK * (j + 1))
            h1_scr[:, cs] = x_ref[0, :, cs] + _dot(y0, wo0_ref[:, cs])

    def b_norm():
        h1 = h1_scr[...]
        ms = jnp.mean(h1 * h1, axis=-1, keepdims=True)
        hb_scr[...] = (h1 * lax.rsqrt(ms + EPS) * g1_ref[...]).astype(BF16)
        seq_start = ((step - 1) % tiles_per_seq) == 0
        a_scr[0:halo, :] = jnp.where(seq_start, 0.0, a_scr[0:halo, :])

    def proj1(c0, width=COL_CHUNK):
        return _dot(hb_scr[...], w1_ref[:, c0:c0 + width])

    def b_conv(j):
        cs = slice(COL_CHUNK * j, COL_CHUNK * (j + 1))
        a_now = proj1(B_CG + COL_CHUNK * j) * proj1(B_U + COL_CHUNK * j)
        a_scr[halo:halo + tm, cs] = a_now
        conv = (cw_ref[0:1, cs] * a_scr[halo - 2:halo - 2 + tm, cs]
                + cw_ref[1:2, cs] * a_scr[halo - 1:halo - 1 + tm, cs]
                + cw_ref[2:3, cs] * a_now)
        mix1 = proj1(B_BG + COL_CHUNK * j) * conv
        y_scr[:, cs] = (mix1 * _silu(proj1(B_Z + COL_CHUNK * j))).astype(BF16)

    def b_mem_gate(scores1, z_mem):
        mem1 = _mem_softmax_pv(scores1, kv1_ref)
        y_scr[:, CONV_WIDTH:BRANCH_B] = (mem1 * _silu(z_mem)).astype(BF16)

    def b_out():
        a_scr[0:halo, :] = a_scr[tm:tm + halo, :]
        y1 = y_scr[...]
        ssq = jnp.zeros((tm, 1), F32)
        for j in range(n_out):
            cs = slice(COL_CHUNK * j, COL_CHUNK * (j + 1))
            h2 = h1_scr[:, cs] + _dot(y1, wo1_ref[:, cs])
            ssq = ssq + jnp.sum(h2 * h2, axis=-1, keepdims=True)
            out_ref[0, :, cs] = h2
        scale = lax.rsqrt(ssq * (1.0 / D_MODEL) + EPS)
        out_ref[0] = out_ref[0] * scale * gf_ref[...]

    def stage_b(a_hooks):
        b_norm()
        qm1 = proj1(B_QM, MEM_WIDTH).astype(BF16)
        z_mem = proj1(B_Z + CONV_WIDTH, MEM_WIDTH)
        b_conv(0)
        scores1 = _mem_scores(qm1, kv1_ref)
        a_hooks[0]()
        b_conv(1)
        b_mem_gate(scores1, z_mem)
        b_conv(2)
        b_conv(3)
        b_out()
        a_hooks[1]()

    @pl.when(step == 0)
    def _():
        a_scr[0:halo, :] = jnp.zeros((halo, CONV_WIDTH), F32)
        a_out(a_gate(a_scores()))

    @pl.when((step > 0) & (step < last))
    def _():
        state = {"scores": a_scores()}

        def gate():
            state["y0"] = a_gate(state["scores"])

        stage_b((gate, lambda: a_out(state["y0"])))

    @pl.when(step == last)
    def _():
        stage_b((lambda: None, lambda: None))


def _tail(x, mix, qm0, z0, kv, wo0, norm_g1, w1, conv_w, wo1, final_g):
    b, s, _ = x.shape
    tm = ROW_TILE
    per_seq = s // tm
    n_tiles = b * per_seq

    def tile_a(j):
        return jnp.minimum(j, n_tiles - 1)

    def tile_b(j):
        return jnp.maximum(j - 1, 0)

    def row(width, tile):
        return pl.BlockSpec((1, tm, width), lambda j: (tile(j) // per_seq, tile(j) % per_seq, 0))

    def kv_spec(layer, tile):
        return pl.BlockSpec((1, 1, N_MEM, 2 * MEM_WIDTH), lambda j: (layer, tile(j) // per_seq, 0, 0))

    const = lambda shape: pl.BlockSpec(shape, lambda j: (0, 0))
    tile_f32 = pltpu.VMEM((tm, D_MODEL), F32)
    tile_bf16 = pltpu.VMEM((tm, D_MODEL), BF16)
    return pl.pallas_call(
        functools.partial(_tail_kernel, tiles_per_seq=per_seq),
        grid=(n_tiles + 1,),
        in_specs=[
            row(D_MODEL, tile_a), row(GROUP_WIDTH, tile_a), row(MEM_WIDTH, tile_a), row(BRANCH_A, tile_a),
            kv_spec(0, tile_a),
            _resident(wo0.shape, lambda j: (0, 0)),
            const((1, D_MODEL)),
            _resident(w1.shape, lambda j: (0, 0)),
            const((3, CONV_WIDTH)),
            kv_spec(1, tile_b),
            _resident(wo1.shape, lambda j: (0, 0)),
            const((1, D_MODEL)),
        ],
        out_specs=row(D_MODEL, tile_b),
        out_shape=jax.ShapeDtypeStruct((b, s, D_MODEL), F32),
        scratch_shapes=[
            pltpu.VMEM((tm + V7X_SUBLANES, CONV_WIDTH), F32),
            pltpu.VMEM((tm, BRANCH_B), BF16),
            tile_f32, tile_bf16,
        ],
        compiler_params=_params("arbitrary"),
        name="tail",
    )(x, mix, qm0, z0, kv, wo0, norm_g1, w1, conv_w, kv, wo1, final_g)


def kernel(x, mem, positions, norm_g, mem_norm_g, w_mem_kv, attn_w_in, attn_w_out, conv_w_in, conv_w,
           conv_w_out, final_g):
    b, s, _ = x.shape

    inv_freq = ROPE_THETA ** (-jnp.arange(ROT_HALF, dtype=F32) * (2.0 / ROT_DIM))
    freq_rows = jnp.broadcast_to(inv_freq[:, None], (ROT_HALF, V7X_LANES))
    pos_rows = positions.reshape(b, 1, s)

    kv, w0 = _mem_kv(mem, mem_norm_g, w_mem_kv, attn_w_in[0])
    qkv0, qm0, z0, qkv1, qkv2, w1, wo0, wo1 = _proj0(
        x, norm_g[0:1], pos_rows, freq_rows, w0, conv_w_in[0], attn_w_out[0], conv_w_out[0])
    qkv1 = qkv1.reshape(b, s, QKV_G)
    qkv2 = qkv2.reshape(b, s, QKV_G)
    mix = _attn(qkv0, qkv1, qkv2)
    return _tail(x, mix, qm0, z0, kv, wo0, norm_g[1:2], w1, conv_w[0], wo1, final_g.reshape(1, D_MODEL))
```

```python
import functools

import numpy as np
import jax
import jax.numpy as jnp
from jax import lax
from jax.experimental import pallas as pl
from jax.experimental.pallas import tpu as pltpu

F32 = jnp.float32
BF16 = jnp.bfloat16

D_MODEL = 1024
HEAD_DIM = 64
ROT_DIM = 16
ROT_HALF = ROT_DIM // 2
ROPE_THETA = 500000.0
DILATIONS = (1, 4, 16)
BLOCK = 128
GROUP_WIDTH = 512
N_GROUPS = 3
N_MEM = 256
MEM_WIDTH = 256
CONV_WIDTH = 1024
EPS = 1e-6
SCORE_SCALE = HEAD_DIM ** -0.5 * float(np.log2(np.e))

V7X_LANES = 128
V7X_SUBLANES = 8
BF16_SUBLANES = 16
ROW_TILE = 512
COL_CHUNK = 256
VMEM_LIMIT_BYTES = 56 * 1024 * 1024
NEG_BIG = -1e30
MEM_ROWS = 256

QKV_G = 3 * GROUP_WIDTH
A_PART = N_GROUPS * GROUP_WIDTH
A_QM = 3 * A_PART
A_Z = A_QM + MEM_WIDTH
BRANCH_A = GROUP_WIDTH + MEM_WIDTH
IN_A_COLS = A_Z + BRANCH_A
B_BG, B_CG, B_U = 0, CONV_WIDTH, 2 * CONV_WIDTH
B_QM = 3 * CONV_WIDTH
B_Z = B_QM + MEM_WIDTH
BRANCH_B = CONV_WIDTH + MEM_WIDTH
IN_B_COLS = B_Z + BRANCH_B


def _column_scale(n_cols, scaled):
    scale = np.ones((n_cols,), np.float32)
    for lo, hi in scaled:
        scale[lo:hi] = SCORE_SCALE
    return scale


_A_SCALE = _column_scale(IN_A_COLS, [(0, A_PART), (A_QM, A_QM + MEM_WIDTH)])


def _params(*sem):
    return pltpu.CompilerParams(dimension_semantics=sem, vmem_limit_bytes=VMEM_LIMIT_BYTES)


def _resident(shape, index_map):
    return pl.BlockSpec(shape, index_map, pipeline_mode=pl.Buffered(1))


def _silu(z):
    return z * jax.nn.sigmoid(z)


def _dot(a, b):
    return jnp.dot(a, b, preferred_element_type=F32)


def _dot_nt(a, b):
    return lax.dot_general(a, b, (((1,), (1,)), ((), ())), preferred_element_type=F32)


def _half_masks():
    lane = lax.broadcasted_iota(jnp.int32, (1, V7X_LANES), 1)
    return (jnp.where(lane < HEAD_DIM, 1.0, 0.0).astype(BF16),
            jnp.where(lane < HEAD_DIM, 0.0, 1.0).astype(BF16))


def _memkv_kernel(mem_ref, g_ref, w_ref, wa_ref, scale_ref, kv_ref, wab_ref):
    w = w_ref[0].astype(BF16)
    for c in range(mem_ref.shape[0] // ROW_TILE):
        rows = slice(ROW_TILE * c, ROW_TILE * (c + 1))
        m = mem_ref[rows, :]
        ms = jnp.mean(m * m, axis=-1, keepdims=True)
        mn = (m * lax.rsqrt(ms + EPS) * g_ref[0]).astype(BF16)
        kv_ref[0, rows, :] = _dot(mn, w).astype(BF16)
    for c in range(wa_ref.shape[0] // V7X_LANES):
        rows = slice(V7X_LANES * c, V7X_LANES * (c + 1))
        wab_ref[rows, :] = (wa_ref[rows, :] * scale_ref[...]).astype(BF16)


def _mem_kv(mem, mem_norm_g, w_mem_kv, w_attn_in):
    b = mem.shape[0]
    depth = w_mem_kv.shape[0]
    rows = b * N_MEM
    slab = pl.BlockSpec((D_MODEL // depth, IN_A_COLS), lambda l: (l, 0))
    kv, w0 = pl.pallas_call(
        _memkv_kernel,
        grid=(depth,),
        in_specs=[
            _resident((rows, D_MODEL), lambda l: (0, 0)),
            pl.BlockSpec((1, 1, D_MODEL), lambda l: (l, 0, 0)),
            pl.BlockSpec((1, D_MODEL, 2 * MEM_WIDTH), lambda l: (l, 0, 0)),
            slab,
            pl.BlockSpec((1, IN_A_COLS), lambda l: (0, 0)),
        ],
        out_specs=[pl.BlockSpec((1, rows, 2 * MEM_WIDTH), lambda l: (l, 0, 0)), slab],
        out_shape=[jax.ShapeDtypeStruct((depth, rows, 2 * MEM_WIDTH), BF16),
                   jax.ShapeDtypeStruct((D_MODEL, IN_A_COLS), BF16)],
        compiler_params=_params("arbitrary"),
        name="mem_kv",
    )(mem.reshape(rows, D_MODEL), mem_norm_g.reshape(depth, 1, D_MODEL), w_mem_kv,
      w_attn_in, _A_SCALE.reshape(1, IN_A_COLS))
    return kv.reshape(depth, b, N_MEM, 2 * MEM_WIDTH), w0


def _rope_tables(pos_ref, freq_ref, tbl_scr):
    rows = tbl_scr.shape[1]
    freq = freq_ref[...]
    one = jnp.ones((V7X_SUBLANES, V7X_LANES), F32)
    zero = jnp.zeros((V7X_SUBLANES, V7X_LANES), F32)
    groups = V7X_LANES // V7X_SUBLANES
    per_head = HEAD_DIM // V7X_SUBLANES

    def lane_rows(first, second, other):
        pieces = []
        for i in range(groups):
            pieces.append(first if i % per_head == 0 else second if i % per_head == 1 else other)
        return jnp.concatenate(pieces, axis=0).T

    for c in range(rows // V7X_LANES):
        sl = slice(V7X_LANES * c, V7X_LANES * (c + 1))
        ang = pos_ref[0, :, sl].astype(F32) * freq
        cs = jnp.cos(ang)
        sn = jnp.sin(ang)
        tbl_scr[0, sl, :] = lane_rows(cs, cs, one)
        tbl_scr[1, sl, :] = lane_rows(-sn, zero, zero)
        tbl_scr[2, sl, :] = lane_rows(zero, sn, zero)


def _rope(a, cs, sn_next, sn_prev):
    return (a * cs + pltpu.roll(a, V7X_LANES - ROT_HALF, 1) * sn_next
            + pltpu.roll(a, ROT_HALF, 1) * sn_prev)


def _proj0_kernel(x_ref, g_ref, pos_ref, freq_ref, w_ref, w1_ref, wo0_ref, wo1_ref,
                  o0_ref, oqm_ref, oz_ref, o1_ref, o2_ref, w1b_ref, wo0b_ref, wo1b_ref,
                  hn_scr, perm_scr, tbl_scr):
    tm = x_ref.shape[1]
    x = x_ref[0]
    ms = jnp.mean(x * x, axis=-1, keepdims=True)
    hn = x * lax.rsqrt(ms + EPS) * g_ref[...]
    hb = hn.astype(BF16)
    n_slab = D_MODEL // V7X_LANES
    for c in range(n_slab):
        hn_scr[c] = hn[:, V7X_LANES * c:V7X_LANES * (c + 1)]
    _rope_tables(pos_ref, freq_ref, tbl_scr)

    def qkv_chunk(lhs, g, j, tables):
        part, half = divmod(j, GROUP_WIDTH // COL_CHUNK)
        c0 = A_PART * part + GROUP_WIDTH * g + COL_CHUNK * half
        acc = _dot(lhs, w_ref[:, c0:c0 + COL_CHUNK])
        if part < 2:
            acc = jnp.concatenate(
                [_rope(acc[:, :V7X_LANES], *tables), _rope(acc[:, V7X_LANES:], *tables)], axis=1)
        return acc.astype(BF16)

    tables = tuple(tbl_scr[t] for t in range(3))
    for j in range(QKV_G // COL_CHUNK):
        o0_ref[0, :, COL_CHUNK * j:COL_CHUNK * (j + 1)] = qkv_chunk(hb, 0, j, tables)
    oqm_ref[0] = _dot(hb, w_ref[:, A_QM:A_QM + MEM_WIDTH]).astype(BF16)
    for j in range(BRANCH_A // COL_CHUNK):
        oz_ref[0, :, COL_CHUNK * j:COL_CHUNK * (j + 1)] = _dot(
            hb, w_ref[:, A_Z + COL_CHUNK * j:A_Z + COL_CHUNK * (j + 1)])

    for g, o_ref in ((1, o1_ref), (2, o2_ref)):
        d = DILATIONS[g]
        n = tm // d
        for r in range(d):
            for c in range(n_slab):
                perm_scr[r * n:(r + 1) * n, V7X_LANES * c:V7X_LANES * (c + 1)] = (
                    hn_scr[c, pl.ds(r, n, stride=d), :].astype(BF16))
        tables = tuple(
            jnp.concatenate([tbl_scr[t, pl.ds(r, n, stride=d), :] for r in range(d)], axis=0)
            for t in range(3))
        lhs = perm_scr[...]
        for j in range(QKV_G // COL_CHUNK):
            res = qkv_chunk(lhs, g, j, tables)
            for r in range(d):
                o_ref[0, r, :, COL_CHUNK * j:COL_CHUNK * (j + 1)] = res[r * n:(r + 1) * n]

    w1b_ref[:, :B_QM] = w1_ref[:, :B_QM].astype(BF16)
    w1b_ref[:, B_QM:B_Z] = (w1_ref[:, B_QM:B_Z] * SCORE_SCALE).astype(BF16)
    w1b_ref[:, B_Z:] = w1_ref[:, B_Z:].astype(BF16)
    wo0b_ref[...] = wo0_ref[...].astype(BF16)
    wo1b_ref[...] = wo1_ref[...].astype(BF16)


def _proj0(x, norm_g, pos_rows, freq_rows, w0, w1, wo0, wo1):
    b, s, _ = x.shape
    tm = ROW_TILE
    per_seq = s // tm
    n_steps = b * per_seq
    row = lambda width: pl.BlockSpec((1, tm, width), lambda i, j: (i, j, 0))
    stream = lambda d: pl.BlockSpec((1, d, tm // d, QKV_G), lambda i, j: (i, 0, j, 0))

    def slab(w, steps):
        rows = w.shape[0] // steps
        assert rows * steps == w.shape[0] and rows % BF16_SUBLANES == 0
        return pl.BlockSpec((rows, w.shape[1]), lambda i, j: (jnp.minimum(i * per_seq + j, steps - 1), 0))

    slabs = [slab(w1, n_steps), slab(wo0, n_steps // 2), slab(wo1, n_steps // 2)]
    d1, d2 = DILATIONS[1], DILATIONS[2]
    return pl.pallas_call(
        _proj0_kernel,
        grid=(b, per_seq),
        in_specs=[
            row(D_MODEL),
            pl.BlockSpec((1, D_MODEL), lambda i, j: (0, 0)),
            pl.BlockSpec((1, 1, tm), lambda i, j: (i, 0, j)),
            pl.BlockSpec((V7X_SUBLANES, V7X_LANES), lambda i, j: (0, 0)),
            _resident((D_MODEL, IN_A_COLS), lambda i, j: (0, 0)),
        ] + slabs,
        out_specs=[row(QKV_G), row(MEM_WIDTH), row(BRANCH_A), stream(d1), stream(d2)] + slabs,
        out_shape=[
            jax.ShapeDtypeStruct((b, s, QKV_G), BF16),
            jax.ShapeDtypeStruct((b, s, MEM_WIDTH), BF16),
            jax.ShapeDtypeStruct((b, s, BRANCH_A), F32),
            jax.ShapeDtypeStruct((b, d1, s // d1, QKV_G), BF16),
            jax.ShapeDtypeStruct((b, d2, s // d2, QKV_G), BF16),
            jax.ShapeDtypeStruct(w1.shape, BF16),
            jax.ShapeDtypeStruct(wo0.shape, BF16),
            jax.ShapeDtypeStruct(wo1.shape, BF16),
        ],
        scratch_shapes=[
            pltpu.VMEM((D_MODEL // V7X_LANES, tm, V7X_LANES), F32),
            pltpu.VMEM((tm, D_MODEL), BF16),
            pltpu.VMEM((3, tm, V7X_LANES), F32),
        ],
        compiler_params=_params("arbitrary", "arbitrary"),
        name="proj0",
    )(x, norm_g, pos_rows, freq_rows, w0, w1, wo0, wo1)


def _attn_kernel(q0, k0, v0, q1, k1, v1, q2, k2, v2, o_ref,
                 num_scr, m_scr, l_scr, ve_scr, bias2_scr, bias1_scr):
    s_len = o_ref.shape[1]
    half0, half1 = _half_masks()
    first_half = lax.broadcasted_iota(jnp.int32, (BLOCK, V7X_LANES), 1) < HEAD_DIM

    @pl.when((pl.program_id(0) == 0) & (pl.program_id(1) == 0))
    def _():
        qi = lax.broadcasted_iota(jnp.int32, (2 * BLOCK, 2 * BLOCK), 0) & (BLOCK - 1)
        kj = lax.broadcasted_iota(jnp.int32, (2 * BLOCK, 2 * BLOCK), 1)
        valid_two = ((kj < BLOCK) & (kj >= qi)) | ((kj >= BLOCK) & (kj - BLOCK <= qi))
        bias2_scr[...] = jnp.where(valid_two, 0.0, NEG_BIG)
        valid_one = (lax.broadcasted_iota(jnp.int32, (2 * BLOCK, BLOCK), 1)
                     <= (lax.broadcasted_iota(jnp.int32, (2 * BLOCK, BLOCK), 0) & (BLOCK - 1)))
        bias1_scr[...] = jnp.where(valid_one, 0.0, NEG_BIG)

    for g, v_ref in enumerate((v0, v1, v2)):
        v = v_ref[0]
        for h, half in enumerate((half0, half1)):
            ve_scr[g, h] = v * half

    def block(q_ref, k_ref, g, q_row, k_row, n_keys, bias_ref, dst):
        q = q_ref[0, pl.ds(q_row, BLOCK), :]
        qs = jnp.concatenate([q * half0, q * half1], axis=0)
        k = k_ref[0, pl.ds(k_row, n_keys), :]
        s = _dot_nt(qs, k) + bias_ref[...]
        m = jnp.max(s, axis=-1, keepdims=True)
        p = jnp.exp2(s - m).astype(BF16)
        p_cat = jnp.concatenate([p[:BLOCK], p[BLOCK:]], axis=1)
        keys = pl.ds(k_row, n_keys)
        ve = jnp.concatenate(
            [jnp.concatenate([ve_scr[g, h, keys, :], jnp.broadcast_to(half, (n_keys, V7X_LANES))], axis=1)
             for h, half in enumerate((half0, half1))], axis=0)
        pv = _dot(p_cat, ve)
        num_scr[g, dst, :] = pv[:, :V7X_LANES]
        l_scr[g, dst, :] = pv[:, V7X_LANES:]
        m_scr[g, dst, :] = jnp.where(first_half, m[:BLOCK], m[BLOCK:])

    d1, d2 = DILATIONS[1], DILATIONS[2]
    stream_len = s_len // d1
    sub = d2 // d1

    for r in range(d2):
        block(q2, k2, 2, r * BLOCK, r * BLOCK, BLOCK, bias1_scr,
              pl.ds((r % d1) * stream_len + r // d1, BLOCK, stride=sub))

    for r in range(d1):
        base = r * stream_len
        block(q1, k1, 1, base, base, BLOCK, bias1_scr, pl.ds(base, BLOCK))
        for n in range(1, stream_len // BLOCK):
            block(q1, k1, 1, base + n * BLOCK, base + (n - 1) * BLOCK, 2 * BLOCK, bias2_scr,
                  pl.ds(base + n * BLOCK, BLOCK))

    def merge(n):
        rows_per = BLOCK // d1
        for r in range(d1):
            tok = pl.ds(n * BLOCK + r, rows_per, stride=d1)
            rows = pl.ds(r * stream_len + n * rows_per, rows_per)
            idx = (tok, rows, rows)
            ms = [m_scr[g, idx[g], :] for g in range(N_GROUPS)]
            top = jnp.maximum(ms[0], jnp.maximum(ms[1], ms[2]))
            ws = [jnp.exp2(mg - top) for mg in ms]
            nums = [ws[g] * num_scr[g, idx[g], :] for g in range(N_GROUPS)]
            dens = [ws[g] * l_scr[g, idx[g], :] for g in range(N_GROUPS)]
            num = nums[0] + nums[1] + nums[2]
            den = dens[0] + dens[1] + dens[2]
            o_ref[0, tok, :] = num / den

    block(q0, k0, 0, 0, 0, BLOCK, bias1_scr, pl.ds(0, BLOCK))
    for n in range(1, s_len // BLOCK):
        block(q0, k0, 0, n * BLOCK, (n - 1) * BLOCK, 2 * BLOCK, bias2_scr, pl.ds(n * BLOCK, BLOCK))
        merge(n - 1)
    merge(s_len // BLOCK - 1)


def _attn(qkv0, qkv1, qkv2):
    b, s, _ = qkv0.shape
    n_pair = GROUP_WIDTH // V7X_LANES

    def part(k):
        return pl.BlockSpec((1, s, V7X_LANES), lambda i, p, k=k: (i, 0, k * n_pair + p))

    specs = [part(0), part(1), part(2)] * N_GROUPS
    stat = pltpu.VMEM((N_GROUPS, s, V7X_LANES), F32)
    scratch = [
        stat, stat, stat,
        pltpu.VMEM((N_GROUPS, 2, s, V7X_LANES), BF16),
        pltpu.VMEM((2 * BLOCK, 2 * BLOCK), F32),
        pltpu.VMEM((2 * BLOCK, BLOCK), F32),
    ]
    return pl.pallas_call(
        _attn_kernel,
        grid=(b, n_pair),
        in_specs=specs,
        out_specs=pl.BlockSpec((1, s, V7X_LANES), lambda i, p: (i, 0, p)),
        out_shape=jax.ShapeDtypeStruct((b, s, GROUP_WIDTH), F32),
        scratch_shapes=scratch,
        compiler_params=_params("arbitrary", "arbitrary"),
        name="dilated_attn",
    )(qkv0, qkv0, qkv0, qkv1, qkv1, qkv1, qkv2, qkv2, qkv2)


def _mem_scores(qm, kv_ref):
    rows = qm.shape[0]
    half0, half1 = _half_masks()
    scores = []
    for p in range(MEM_WIDTH // V7X_LANES):
        k = kv_ref[0, 0, :, V7X_LANES * p:V7X_LANES * (p + 1)]
        for c in range(rows // MEM_ROWS):
            q = qm[MEM_ROWS * c:MEM_ROWS * (c + 1), V7X_LANES * p:V7X_LANES * (p + 1)]
            scores.append(_dot_nt(jnp.concatenate([q * half0, q * half1], axis=0), k))
    return scores


def _mem_softmax_pv(scores, kv_ref):
    half0, half1 = _half_masks()
    n_pairs = MEM_WIDTH // V7X_LANES
    per_pair = len(scores) // n_pairs
    outs = []
    for p in range(n_pairs):
        v = kv_ref[0, 0, :, MEM_WIDTH + V7X_LANES * p:MEM_WIDTH + V7X_LANES * (p + 1)]
        ve = jnp.concatenate(
            [jnp.concatenate([v * half, jnp.broadcast_to(half, (N_MEM, V7X_LANES))], axis=1)
             for half in (half0, half1)], axis=0)
        chunks = []
        for s in scores[per_pair * p:per_pair * (p + 1)]:
            e = jnp.exp2(s - jnp.max(s, axis=-1, keepdims=True)).astype(BF16)
            pv = _dot(jnp.concatenate([e[:MEM_ROWS], e[MEM_ROWS:]], axis=1), ve)
            chunks.append(pv[:, :V7X_LANES] / pv[:, V7X_LANES:])
        outs.append(jnp.concatenate(chunks, axis=0))
    return jnp.concatenate(outs, axis=1)


def _tail_kernel(x_ref, mix_ref, qm0_ref, z0_ref, kv0_ref, wo0_ref, g1_ref, w1_ref, cw_ref,
                 kv1_ref, wo1_ref, gf_ref, out_ref,
                 a_scr, y_scr, h1_scr, hb_scr, *, tiles_per_seq):
    tm = x_ref.shape[1]
    halo = V7X_SUBLANES
    n_out = D_MODEL // COL_CHUNK
    step = pl.program_id(0)

    @pl.when(step == 0)
    def _():
        h1_scr[...] = jnp.zeros(h1_scr.shape, F32)
        a_scr[0:halo, :] = jnp.zeros((halo, CONV_WIDTH), F32)

    scores0 = _mem_scores(qm0_ref[0], kv0_ref)

    h1 = h1_scr[...]
    ms = jnp.mean(h1 * h1, axis=-1, keepdims=True)
    hb_scr[...] = (h1 * lax.rsqrt(ms + EPS) * g1_ref[...]).astype(BF16)
    seq_start = (jnp.maximum(step - 1, 0) % tiles_per_seq) == 0
    a_scr[0:halo, :] = jnp.where(seq_start, 0.0, a_scr[0:halo, :])

    def proj1(c0, width=COL_CHUNK):
        return _dot(hb_scr[...], w1_ref[:, c0:c0 + width])

    def conv_chunk(j):
        cs = slice(COL_CHUNK * j, COL_CHUNK * (j + 1))
        a_now = proj1(B_CG + COL_CHUNK * j) * proj1(B_U + COL_CHUNK * j)
        a_scr[halo:halo + tm, cs] = a_now
        conv = (cw_ref[0:1, cs] * a_scr[halo - 2:halo - 2 + tm, cs]
                + cw_ref[1:2, cs] * a_scr[halo - 1:halo - 1 + tm, cs]
                + cw_ref[2:3, cs] * a_now)
        mix1 = proj1(B_BG + COL_CHUNK * j) * conv
        y_scr[:, cs] = (mix1 * _silu(proj1(B_Z + COL_CHUNK * j))).astype(BF16)

    qm1 = proj1(B_QM, MEM_WIDTH).astype(BF16)
    z_mem = proj1(B_Z + CONV_WIDTH, MEM_WIDTH)
    conv_chunk(0)
    scores1 = _mem_scores(qm1, kv1_ref)
    mem0 = _mem_softmax_pv(scores0, kv0_ref)
    z0 = z0_ref[0]
    y0 = jnp.concatenate(
        [mix_ref[0] * _silu(z0[:, :GROUP_WIDTH]), mem0 * _silu(z0[:, GROUP_WIDTH:])], axis=1).astype(BF16)
    conv_chunk(1)
    mem1 = _mem_softmax_pv(scores1, kv1_ref)
    y_scr[:, CONV_WIDTH:BRANCH_B] = (mem1 * _silu(z_mem)).astype(BF16)
    conv_chunk(2)
    conv_chunk(3)
    a_scr[0:halo, :] = a_scr[tm:tm + halo, :]

    y1 = y_scr[...]
    ssq = jnp.zeros((tm, 1), F32)
    for j in range(n_out):
        cs = slice(COL_CHUNK * j, COL_CHUNK * (j + 1))
        h2 = h1_scr[:, cs] + _dot(y1, wo1_ref[:, cs])
        ssq = ssq + jnp.sum(h2 * h2, axis=-1, keepdims=True)
        out_ref[0, :, cs] = h2
    scale = lax.rsqrt(ssq * (1.0 / D_MODEL) + EPS)
    out_ref[0] = out_ref[0] * scale * gf_ref[...]

    for j in range(n_out):
        cs = slice(COL_CHUNK * j, COL_CHUNK * (j + 1))
        h1_scr[:, cs] = x_ref[0, :, cs] + _dot(y0, wo0_ref[:, cs])


def _tail(x, mix, qm0, z0, kv, wo0, norm_g1, w1, conv_w, wo1, final_g):
    b, s, _ = x.shape
    tm = ROW_TILE
    per_seq = s // tm
    n_tiles = b * per_seq

    def tile_a(j):
        return jnp.minimum(j, n_tiles - 1)

    def tile_b(j):
        return jnp.maximum(j - 1, 0)

    def row(width, tile):
        return pl.BlockSpec((1, tm, width), lambda j: (tile(j) // per_seq, tile(j) % per_seq, 0))

    def kv_spec(layer, tile):
        return pl.BlockSpec((1, 1, N_MEM, 2 * MEM_WIDTH), lambda j: (layer, tile(j) // per_seq, 0, 0))

    const = lambda shape: pl.BlockSpec(shape, lambda j: (0, 0))
    tile_f32 = pltpu.VMEM((tm, D_MODEL), F32)
    tile_bf16 = pltpu.VMEM((tm, D_MODEL), BF16)
    return pl.pallas_call(
        functools.partial(_tail_kernel, tiles_per_seq=per_seq),
        grid=(n_tiles + 1,),
        in_specs=[
            row(D_MODEL, tile_a), row(GROUP_WIDTH, tile_a), row(MEM_WIDTH, tile_a), row(BRANCH_A, tile_a),
            kv_spec(0, tile_a),
            _resident(wo0.shape, lambda j: (0, 0)),
            const((1, D_MODEL)),
            _resident(w1.shape, lambda j: (0, 0)),
            const((3, CONV_WIDTH)),
            kv_spec(1, tile_b),
            _resident(wo1.shape, lambda j: (0, 0)),
            const((1, D_MODEL)),
        ],
        out_specs=row(D_MODEL, tile_b),
        out_shape=jax.ShapeDtypeStruct((b, s, D_MODEL), F32),
        scratch_shapes=[
            pltpu.VMEM((tm + V7X_SUBLANES, CONV_WIDTH), F32),
            pltpu.VMEM((tm, BRANCH_B), BF16),
            tile_f32, tile_bf16,
        ],
        compiler_params=_params("arbitrary"),
        name="tail",
    )(x, mix, qm0, z0, kv, wo0, norm_g1, w1, conv_w, kv, wo1, final_g)


def kernel(x, mem, positions, norm_g, mem_norm_g, w_mem_kv, attn_w_in, attn_w_out, conv_w_in, conv_w,
           conv_w_out, final_g):
    b, s, _ = x.shape

    inv_freq = ROPE_THETA ** (-jnp.arange(ROT_HALF, dtype=F32) * (2.0 / ROT_DIM))
    freq_rows = jnp.broadcast_to(inv_freq[:, None], (ROT_HALF, V7X_LANES))
    pos_rows = positions.reshape(b, 1, s)

    kv, w0 = _mem_kv(mem, mem_norm_g, w_mem_kv, attn_w_in[0])
    qkv0, qm0, z0, qkv1, qkv2, w1, wo0, wo1 = _proj0(
        x, norm_g[0:1], pos_rows, freq_rows, w0, conv_w_in[0], attn_w_out[0], conv_w_out[0])
    qkv1 = qkv1.reshape(b, s, QKV_G)
    qkv2 = qkv2.reshape(b, s, QKV_G)
    mix = _attn(qkv0, qkv1, qkv2)
    return _tail(x, mix, qm0, z0, kv, wo0, norm_g[1:2], w1, conv_w[0], wo1, final_g.reshape(1, D_MODEL))
```

```python
import functools

import numpy as np
import jax
import jax.numpy as jnp
from jax import lax
from jax.experimental import pallas as pl
from jax.experimental.pallas import tpu as pltpu

F32 = jnp.float32
BF16 = jnp.bfloat16

D_MODEL = 1024
HEAD_DIM = 64
ROT_DIM = 16
ROT_HALF = ROT_DIM // 2
ROPE_THETA = 500000.0
DILATIONS = (1, 4, 16)
BLOCK = 128
GROUP_WIDTH = 512
N_GROUPS = 3
N_MEM = 256
MEM_WIDTH = 256
CONV_WIDTH = 1024
EPS = 1e-6
SCORE_SCALE = HEAD_DIM ** -0.5 * float(np.log2(np.e))

V7X_LANES = 128
V7X_SUBLANES = 8
BF16_SUBLANES = 16
ROW_TILE = 512
COL_CHUNK = 256
VMEM_LIMIT_BYTES = 56 * 1024 * 1024
NEG_BIG = -1e30
MEM_ROWS = 256

QKV_G = 3 * GROUP_WIDTH
A_PART = N_GROUPS * GROUP_WIDTH
A_QM = 3 * A_PART
A_Z = A_QM + MEM_WIDTH
BRANCH_A = GROUP_WIDTH + MEM_WIDTH
IN_A_COLS = A_Z + BRANCH_A
B_BG, B_CG, B_U = 0, CONV_WIDTH, 2 * CONV_WIDTH
B_QM = 3 * CONV_WIDTH
B_Z = B_QM + MEM_WIDTH
BRANCH_B = CONV_WIDTH + MEM_WIDTH
IN_B_COLS = B_Z + BRANCH_B


def _column_scale(n_cols, scaled):
    scale = np.ones((n_cols,), np.float32)
    for lo, hi in scaled:
        scale[lo:hi] = SCORE_SCALE
    return scale


_A_SCALE = _column_scale(IN_A_COLS, [(0, A_PART), (A_QM, A_QM + MEM_WIDTH)])


def _params(*sem):
    return pltpu.CompilerParams(dimension_semantics=sem, vmem_limit_bytes=VMEM_LIMIT_BYTES)


def _resident(shape, index_map):
    return pl.BlockSpec(shape, index_map, pipeline_mode=pl.Buffered(1))


def _silu(z):
    return z * jax.nn.sigmoid(z)


def _dot(a, b):
    return jnp.dot(a, b, preferred_element_type=F32)


def _dot_nt(a, b):
    return lax.dot_general(a, b, (((1,), (1,)), ((), ())), preferred_element_type=F32)


def _half_masks():
    lane = lax.broadcasted_iota(jnp.int32, (1, V7X_LANES), 1)
    return (jnp.where(lane < HEAD_DIM, 1.0, 0.0).astype(BF16),
            jnp.where(lane < HEAD_DIM, 0.0, 1.0).astype(BF16))


def _memkv_kernel(mem_ref, g_ref, w_ref, wa_ref, scale_ref, kv_ref, wab_ref):
    w = w_ref[0].astype(BF16)
    for c in range(mem_ref.shape[0] // ROW_TILE):
        rows = slice(ROW_TILE * c, ROW_TILE * (c + 1))
        m = mem_ref[rows, :]
        ms = jnp.mean(m * m, axis=-1, keepdims=True)
        mn = (m * lax.rsqrt(ms + EPS) * g_ref[0]).astype(BF16)
        kv_ref[0, rows, :] = _dot(mn, w).astype(BF16)
    for c in range(wa_ref.shape[0] // V7X_LANES):
        rows = slice(V7X_LANES * c, V7X_LANES * (c + 1))
        wab_ref[rows, :] = (wa_ref[rows, :] * scale_ref[...]).astype(BF16)


def _mem_kv(mem, mem_norm_g, w_mem_kv, w_attn_in):
    b = mem.shape[0]
    depth = w_mem_kv.shape[0]
    rows = b * N_MEM
    slab = pl.BlockSpec((D_MODEL // depth, IN_A_COLS), lambda l: (l, 0))
    kv, w0 = pl.pallas_call(
        _memkv_kernel,
        grid=(depth,),
        in_specs=[
            _resident((rows, D_MODEL), lambda l: (0, 0)),
            pl.BlockSpec((1, 1, D_MODEL), lambda l: (l, 0, 0)),
            pl.BlockSpec((1, D_MODEL, 2 * MEM_WIDTH), lambda l: (l, 0, 0)),
            slab,
            pl.BlockSpec((1, IN_A_COLS), lambda l: (0, 0)),
        ],
        out_specs=[pl.BlockSpec((1, rows, 2 * MEM_WIDTH), lambda l: (l, 0, 0)), slab],
        out_shape=[jax.ShapeDtypeStruct((depth, rows, 2 * MEM_WIDTH), BF16),
                   jax.ShapeDtypeStruct((D_MODEL, IN_A_COLS), BF16)],
        compiler_params=_params("arbitrary"),
        name="mem_kv",
    )(mem.reshape(rows, D_MODEL), mem_norm_g.reshape(depth, 1, D_MODEL), w_mem_kv,
      w_attn_in, _A_SCALE.reshape(1, IN_A_COLS))
    return kv.reshape(depth, b, N_MEM, 2 * MEM_WIDTH), w0


def _rope_tables(pos_ref, freq_ref, tbl_scr):
    rows = tbl_scr.shape[1]
    freq = freq_ref[...]
    one = jnp.ones((V7X_SUBLANES, V7X_LANES), F32)
    zero = jnp.zeros((V7X_SUBLANES, V7X_LANES), F32)
    groups = V7X_LANES // V7X_SUBLANES
    per_head = HEAD_DIM // V7X_SUBLANES

    def lane_rows(first, second, other):
        pieces = []
        for i in range(groups):
            pieces.append(first if i % per_head == 0 else second if i % per_head == 1 else other)
        return jnp.concatenate(pieces, axis=0).T

    for c in range(rows // V7X_LANES):
        sl = slice(V7X_LANES * c, V7X_LANES * (c + 1))
        ang = pos_ref[0, :, sl].astype(F32) * freq
        cs = jnp.cos(ang)
        sn = jnp.sin(ang)
        tbl_scr[0, sl, :] = lane_rows(cs, cs, one)
        tbl_scr[1, sl, :] = lane_rows(-sn, zero, zero)
        tbl_scr[2, sl, :] = lane_rows(zero, sn, zero)


def _rope(a, cs, sn_next, sn_prev):
    return (a * cs + pltpu.roll(a, V7X_LANES - ROT_HALF, 1) * sn_next
            + pltpu.roll(a, ROT_HALF, 1) * sn_prev)


def _proj0_kernel(x_ref, g_ref, pos_ref, freq_ref, w_ref, w1_ref, wo0_ref, wo1_ref,
                  o0_ref, oqm_ref, oz_ref, o1_ref, o2_ref, w1b_ref, wo0b_ref, wo1b_ref,
                  hn_scr, perm_scr, tbl_scr):
    tm = x_ref.shape[1]
    x = x_ref[0]
    ms = jnp.mean(x * x, axis=-1, keepdims=True)
    hn = x * lax.rsqrt(ms + EPS) * g_ref[...]
    hb = hn.astype(BF16)
    n_slab = D_MODEL // V7X_LANES
    for c in range(n_slab):
        hn_scr[c] = hn[:, V7X_LANES * c:V7X_LANES * (c + 1)]
    _rope_tables(pos_ref, freq_ref, tbl_scr)

    def qkv_chunk(lhs, g, j, tables):
        part, half = divmod(j, GROUP_WIDTH // COL_CHUNK)
        c0 = A_PART * part + GROUP_WIDTH * g + COL_CHUNK * half
        acc = _dot(lhs, w_ref[:, c0:c0 + COL_CHUNK])
        if part < 2:
            acc = jnp.concatenate(
                [_rope(acc[:, :V7X_LANES], *tables), _rope(acc[:, V7X_LANES:], *tables)], axis=1)
        return acc.astype(BF16)

    tables = tuple(tbl_scr[t] for t in range(3))
    for j in range(QKV_G // COL_CHUNK):
        res = qkv_chunk(hb, 0, j, tables)
        for e in range(COL_CHUNK // V7X_LANES):
            o0_ref[0, 2 * j + e] = res[:, V7X_LANES * e:V7X_LANES * (e + 1)]
    oqm_ref[0] = _dot(hb, w_ref[:, A_QM:A_QM + MEM_WIDTH]).astype(BF16)
    for j in range(BRANCH_A // COL_CHUNK):
        oz_ref[0, :, COL_CHUNK * j:COL_CHUNK * (j + 1)] = _dot(
            hb, w_ref[:, A_Z + COL_CHUNK * j:A_Z + COL_CHUNK * (j + 1)])

    for g, o_ref in ((1, o1_ref), (2, o2_ref)):
        d = DILATIONS[g]
        n = tm // d
        for r in range(d):
            for c in range(n_slab):
                perm_scr[r * n:(r + 1) * n, V7X_LANES * c:V7X_LANES * (c + 1)] = (
                    hn_scr[c, pl.ds(r, n, stride=d), :].astype(BF16))
        tables = tuple(
            jnp.concatenate([tbl_scr[t, pl.ds(r, n, stride=d), :] for r in range(d)], axis=0)
            for t in range(3))
        lhs = perm_scr[...]
        for j in range(QKV_G // COL_CHUNK):
            res = qkv_chunk(lhs, g, j, tables)
            for r in range(d):
                for e in range(COL_CHUNK // V7X_LANES):
                    o_ref[0, 2 * j + e, r] = res[r * n:(r + 1) * n, V7X_LANES * e:V7X_LANES * (e + 1)]

    w1b_ref[:, :B_QM] = w1_ref[:, :B_QM].astype(BF16)
    w1b_ref[:, B_QM:B_Z] = (w1_ref[:, B_QM:B_Z] * SCORE_SCALE).astype(BF16)
    w1b_ref[:, B_Z:] = w1_ref[:, B_Z:].astype(BF16)
    wo0b_ref[...] = wo0_ref[...].astype(BF16)
    wo1b_ref[...] = wo1_ref[...].astype(BF16)


def _proj0(x, norm_g, pos_rows, freq_rows, w0, w1, wo0, wo1):
    b, s, _ = x.shape
    tm = ROW_TILE
    per_seq = s // tm
    n_steps = b * per_seq
    row = lambda width: pl.BlockSpec((1, tm, width), lambda i, j: (i, j, 0))
    n_slab = QKV_G // V7X_LANES
    natural = pl.BlockSpec((1, n_slab, tm, V7X_LANES), lambda i, j: (i, 0, j, 0))
    stream = lambda d: pl.BlockSpec((1, n_slab, d, tm // d, V7X_LANES), lambda i, j: (i, 0, 0, j, 0))

    def slab(w, steps):
        rows = w.shape[0] // steps
        assert rows * steps == w.shape[0] and rows % BF16_SUBLANES == 0
        return pl.BlockSpec((rows, w.shape[1]), lambda i, j: (jnp.minimum(i * per_seq + j, steps - 1), 0))

    slabs = [slab(w1, n_steps), slab(wo0, n_steps // 2), slab(wo1, n_steps // 2)]
    d1, d2 = DILATIONS[1], DILATIONS[2]
    return pl.pallas_call(
        _proj0_kernel,
        grid=(b, per_seq),
        in_specs=[
            row(D_MODEL),
            pl.BlockSpec((1, D_MODEL), lambda i, j: (0, 0)),
            pl.BlockSpec((1, 1, tm), lambda i, j: (i, 0, j)),
            pl.BlockSpec((V7X_SUBLANES, V7X_LANES), lambda i, j: (0, 0)),
            _resident((D_MODEL, IN_A_COLS), lambda i, j: (0, 0)),
        ] + slabs,
        out_specs=[natural, row(MEM_WIDTH), row(BRANCH_A), stream(d1), stream(d2)] + slabs,
        out_shape=[
            jax.ShapeDtypeStruct((b, n_slab, s, V7X_LANES), BF16),
            jax.ShapeDtypeStruct((b, s, MEM_WIDTH), BF16),
            jax.ShapeDtypeStruct((b, s, BRANCH_A), F32),
            jax.ShapeDtypeStruct((b, n_slab, d1, s // d1, V7X_LANES), BF16),
            jax.ShapeDtypeStruct((b, n_slab, d2, s // d2, V7X_LANES), BF16),
            jax.ShapeDtypeStruct(w1.shape, BF16),
            jax.ShapeDtypeStruct(wo0.shape, BF16),
            jax.ShapeDtypeStruct(wo1.shape, BF16),
        ],
        scratch_shapes=[
            pltpu.VMEM((D_MODEL // V7X_LANES, tm, V7X_LANES), F32),
            pltpu.VMEM((tm, D_MODEL), BF16),
            pltpu.VMEM((3, tm, V7X_LANES), F32),
        ],
        compiler_params=_params("arbitrary", "arbitrary"),
        name="proj0",
    )(x, norm_g, pos_rows, freq_rows, w0, w1, wo0, wo1)


def _attn_kernel(q0, k0, v0, q1, k1, v1, q2, k2, v2, o_ref,
                 num_scr, m_scr, l_scr, ve_scr, bias2_scr, bias1_scr):
    s_len = o_ref.shape[1]
    half0, half1 = _half_masks()
    first_half = lax.broadcasted_iota(jnp.int32, (BLOCK, V7X_LANES), 1) < HEAD_DIM

    @pl.when((pl.program_id(0) == 0) & (pl.program_id(1) == 0))
    def _():
        qi = lax.broadcasted_iota(jnp.int32, (2 * BLOCK, 2 * BLOCK), 0) & (BLOCK - 1)
        kj = lax.broadcasted_iota(jnp.int32, (2 * BLOCK, 2 * BLOCK), 1)
        valid_two = ((kj < BLOCK) & (kj >= qi)) | ((kj >= BLOCK) & (kj - BLOCK <= qi))
        bias2_scr[...] = jnp.where(valid_two, 0.0, NEG_BIG)
        valid_one = (lax.broadcasted_iota(jnp.int32, (2 * BLOCK, BLOCK), 1)
                     <= (lax.broadcasted_iota(jnp.int32, (2 * BLOCK, BLOCK), 0) & (BLOCK - 1)))
        bias1_scr[...] = jnp.where(valid_one, 0.0, NEG_BIG)

    for g, v_ref in enumerate((v0, v1, v2)):
        v = v_ref[...]
        for h, half in enumerate((half0, half1)):
            ve_scr[g, h] = v * half

    def block(q_ref, k_ref, g, q_row, k_row, n_keys, bias_ref, dst):
        q = q_ref[pl.ds(q_row, BLOCK), :]
        qs = jnp.concatenate([q * half0, q * half1], axis=0)
        k = k_ref[pl.ds(k_row, n_keys), :]
        s = _dot_nt(qs, k) + bias_ref[...]
        m = jnp.max(s, axis=-1, keepdims=True)
        p = jnp.exp2(s - m).astype(BF16)
        p_cat = jnp.concatenate([p[:BLOCK], p[BLOCK:]], axis=1)
        keys = pl.ds(k_row, n_keys)
        ve = jnp.concatenate(
            [jnp.concatenate([ve_scr[g, h, keys, :], jnp.broadcast_to(half, (n_keys, V7X_LANES))], axis=1)
             for h, half in enumerate((half0, half1))], axis=0)
        pv = _dot(p_cat, ve)
        num_scr[g, dst, :] = pv[:, :V7X_LANES]
        l_scr[g, dst, :] = pv[:, V7X_LANES:]
        m_scr[g, dst, :] = jnp.where(first_half, m[:BLOCK], m[BLOCK:])

    d1, d2 = DILATIONS[1], DILATIONS[2]
    stream_len = s_len // d1
    sub = d2 // d1

    for r in range(d2):
        block(q2, k2, 2, r * BLOCK, r * BLOCK, BLOCK, bias1_scr,
              pl.ds((r % d1) * stream_len + r // d1, BLOCK, stride=sub))

    for r in range(d1):
        base = r * stream_len
        block(q1, k1, 1, base, base, BLOCK, bias1_scr, pl.ds(base, BLOCK))
        for n in range(1, stream_len // BLOCK):
            block(q1, k1, 1, base + n * BLOCK, base + (n - 1) * BLOCK, 2 * BLOCK, bias2_scr,
                  pl.ds(base + n * BLOCK, BLOCK))

    def merge(n):
        rows_per = BLOCK // d1
        for r in range(d1):
            tok = pl.ds(n * BLOCK + r, rows_per, stride=d1)
            rows = pl.ds(r * stream_len + n * rows_per, rows_per)
            idx = (tok, rows, rows)
            ms = [m_scr[g, idx[g], :] for g in range(N_GROUPS)]
            top = jnp.maximum(ms[0], jnp.maximum(ms[1], ms[2]))
            ws = [jnp.exp2(mg - top) for mg in ms]
            nums = [ws[g] * num_scr[g, idx[g], :] for g in range(N_GROUPS)]
            dens = [ws[g] * l_scr[g, idx[g], :] for g in range(N_GROUPS)]
            num = nums[0] + nums[1] + nums[2]
            den = dens[0] + dens[1] + dens[2]
            o_ref[0, tok, :] = num / den

    block(q0, k0, 0, 0, 0, BLOCK, bias1_scr, pl.ds(0, BLOCK))
    for n in range(1, s_len // BLOCK):
        block(q0, k0, 0, n * BLOCK, (n - 1) * BLOCK, 2 * BLOCK, bias2_scr, pl.ds(n * BLOCK, BLOCK))
        merge(n - 1)
    merge(s_len // BLOCK - 1)


def _attn(qkv0, qkv1, qkv2):
    b, _, s, _ = qkv0.shape
    n_pair = GROUP_WIDTH // V7X_LANES

    def part(k):
        return pl.BlockSpec((None, None, s, V7X_LANES), lambda i, p, k=k: (i, k * n_pair + p, 0, 0))

    specs = [part(0), part(1), part(2)] * N_GROUPS
    stat = pltpu.VMEM((N_GROUPS, s, V7X_LANES), F32)
    scratch = [
        stat, stat, stat,
        pltpu.VMEM((N_GROUPS, 2, s, V7X_LANES), BF16),
        pltpu.VMEM((2 * BLOCK, 2 * BLOCK), F32),
        pltpu.VMEM((2 * BLOCK, BLOCK), F32),
    ]
    return pl.pallas_call(
        _attn_kernel,
        grid=(b, n_pair),
        in_specs=specs,
        out_specs=pl.BlockSpec((1, s, V7X_LANES), lambda i, p: (i, 0, p)),
        out_shape=jax.ShapeDtypeStruct((b, s, GROUP_WIDTH), F32),
        scratch_shapes=scratch,
        compiler_params=_params("arbitrary", "arbitrary"),
        name="dilated_attn",
    )(qkv0, qkv0, qkv0, qkv1, qkv1, qkv1, qkv2, qkv2, qkv2)


def _mem_scores(qm, kv_ref):
    rows = qm.shape[0]
    half0, half1 = _half_masks()
    scores = []
    for p in range(MEM_WIDTH // V7X_LANES):
        k = kv_ref[0, 0, :, V7X_LANES * p:V7X_LANES * (p + 1)]
        for c in range(rows // MEM_ROWS):
            q = qm[MEM_ROWS * c:MEM_ROWS * (c + 1), V7X_LANES * p:V7X_LANES * (p + 1)]
            scores.append(_dot_nt(jnp.concatenate([q * half0, q * half1], axis=0), k))
    return scores


def _mem_softmax_pv(scores, kv_ref):
    half0, half1 = _half_masks()
    n_pairs = MEM_WIDTH // V7X_LANES
    per_pair = len(scores) // n_pairs
    outs = []
    for p in range(n_pairs):
        v = kv_ref[0, 0, :, MEM_WIDTH + V7X_LANES * p:MEM_WIDTH + V7X_LANES * (p + 1)]
        ve = jnp.concatenate(
            [jnp.concatenate([v * half, jnp.broadcast_to(half, (N_MEM, V7X_LANES))], axis=1)
             for half in (half0, half1)], axis=0)
        chunks = []
        for s in scores[per_pair * p:per_pair * (p + 1)]:
            e = jnp.exp2(s - jnp.max(s, axis=-1, keepdims=True)).astype(BF16)
            pv = _dot(jnp.concatenate([e[:MEM_ROWS], e[MEM_ROWS:]], axis=1), ve)
            chunks.append(pv[:, :V7X_LANES] / pv[:, V7X_LANES:])
        outs.append(jnp.concatenate(chunks, axis=0))
    return jnp.concatenate(outs, axis=1)


def _tail_kernel(x_ref, mix_ref, qm0_ref, z0_ref, kv0_ref, wo0_ref, g1_ref, w1_ref, cw_ref,
                 kv1_ref, wo1_ref, gf_ref, out_ref,
                 a_scr, y_scr, h1_scr, hb_scr, *, tiles_per_seq):
    tm = x_ref.shape[1]
    halo = V7X_SUBLANES
    n_out = D_MODEL // COL_CHUNK
    step = pl.program_id(0)

    @pl.when(step == 0)
    def _():
        h1_scr[...] = jnp.zeros(h1_scr.shape, F32)
        a_scr[0:halo, :] = jnp.zeros((halo, CONV_WIDTH), F32)

    scores0 = _mem_scores(qm0_ref[0], kv0_ref)

    h1 = h1_scr[...]
    ms = jnp.mean(h1 * h1, axis=-1, keepdims=True)
    hb_scr[...] = (h1 * lax.rsqrt(ms + EPS) * g1_ref[...]).astype(BF16)
    seq_start = (jnp.maximum(step - 1, 0) % tiles_per_seq) == 0
    a_scr[0:halo, :] = jnp.where(seq_start, 0.0, a_scr[0:halo, :])

    def proj1(c0, width=COL_CHUNK):
        return _dot(hb_scr[...], w1_ref[:, c0:c0 + width])

    def conv_chunk(j):
        cs = slice(COL_CHUNK * j, COL_CHUNK * (j + 1))
        a_now = proj1(B_CG + COL_CHUNK * j) * proj1(B_U + COL_CHUNK * j)
        a_scr[halo:halo + tm, cs] = a_now
        conv = (cw_ref[0:1, cs] * a_scr[halo - 2:halo - 2 + tm, cs]
                + cw_ref[1:2, cs] * a_scr[halo - 1:halo - 1 + tm, cs]
                + cw_ref[2:3, cs] * a_now)
        mix1 = proj1(B_BG + COL_CHUNK * j) * conv
        y_scr[:, cs] = (mix1 * _silu(proj1(B_Z + COL_CHUNK * j))).astype(BF16)

    qm1 = proj1(B_QM, MEM_WIDTH).astype(BF16)
    z_mem = proj1(B_Z + CONV_WIDTH, MEM_WIDTH)
    conv_chunk(0)
    scores1 = _mem_scores(qm1, kv1_ref)
    mem0 = _mem_softmax_pv(scores0, kv0_ref)
    z0 = z0_ref[0]
    y0 = jnp.concatenate(
        [mix_ref[0] * _silu(z0[:, :GROUP_WIDTH]), mem0 * _silu(z0[:, GROUP_WIDTH:])], axis=1).astype(BF16)
    conv_chunk(1)
    mem1 = _mem_softmax_pv(scores1, kv1_ref)
    y_scr[:, CONV_WIDTH:BRANCH_B] = (mem1 * _silu(z_mem)).astype(BF16)
    conv_chunk(2)
    conv_chunk(3)
    a_scr[0:halo, :] = a_scr[tm:tm + halo, :]

    y1 = y_scr[...]
    ssq = jnp.zeros((tm, 1), F32)
    for j in range(n_out):
        cs = slice(COL_CHUNK * j, COL_CHUNK * (j + 1))
        h2 = h1_scr[:, cs] + _dot(y1, wo1_ref[:, cs])
        ssq = ssq + jnp.sum(h2 * h2, axis=-1, keepdims=True)
        out_ref[0, :, cs] = h2
    scale = lax.rsqrt(ssq * (1.0 / D_MODEL) + EPS)
    out_ref[0] = out_ref[0] * scale * gf_ref[...]

    for j in range(n_out):
        cs = slice(COL_CHUNK * j, COL_CHUNK * (j + 1))
        h1_scr[:, cs] = x_ref[0, :, cs] + _dot(y0, wo0_ref[:, cs])


def _tail(x, mix, qm0, z0, kv, wo0, norm_g1, w1, conv_w, wo1, final_g):
    b, s, _ = x.shape
    tm = ROW_TILE
    per_seq = s // tm
    n_tiles = b * per_seq

    def tile_a(j):
        return jnp.minimum(j, n_tiles - 1)

    def tile_b(j):
        return jnp.maximum(j - 1, 0)

    def row(width, tile):
        return pl.BlockSpec((1, tm, width), lambda j: (tile(j) // per_seq, tile(j) % per_seq, 0))

    def kv_spec(layer, tile):
        return pl.BlockSpec((1, 1, N_MEM, 2 * MEM_WIDTH), lambda j: (layer, tile(j) // per_seq, 0, 0))

    const = lambda shape: pl.BlockSpec(shape, lambda j: (0, 0))
    tile_f32 = pltpu.VMEM((tm, D_MODEL), F32)
    tile_bf16 = pltpu.VMEM((tm, D_MODEL), BF16)
    return pl.pallas_call(
        functools.partial(_tail_kernel, tiles_per_seq=per_seq),
        grid=(n_tiles + 1,),
        in_specs=[
            row(D_MODEL, tile_a), row(GROUP_WIDTH, tile_a), row(MEM_WIDTH, tile_a), row(BRANCH_A, tile_a),
            kv_spec(0, tile_a),
            _resident(wo0.shape, lambda j: (0, 0)),
            const((1, D_MODEL)),
            _resident(w1.shape, lambda j: (0, 0)),
            const((3, CONV_WIDTH)),
            kv_spec(1, tile_b),
            _resident(wo1.shape, lambda j: (0, 0)),
            const((1, D_MODEL)),
        ],
        out_specs=row(D_MODEL, tile_b),
        out_shape=jax.ShapeDtypeStruct((b, s, D_MODEL), F32),
        scratch_shapes=[
            pltpu.VMEM((tm + V7X_SUBLANES, CONV_WIDTH), F32),
            pltpu.VMEM((tm, BRANCH_B), BF16),
            tile_f32, tile_bf16,
        ],
        compiler_params=_params("arbitrary"),
        name="tail",
    )(x, mix, qm0, z0, kv, wo0, norm_g1, w1, conv_w, kv, wo1, final_g)


def kernel(x, mem, positions, norm_g, mem_norm_g, w_mem_kv, attn_w_in, attn_w_out, conv_w_in, conv_w,
           conv_w_out, final_g):
    b, s, _ = x.shape

    inv_freq = ROPE_THETA ** (-jnp.arange(ROT_HALF, dtype=F32) * (2.0 / ROT_DIM))
    freq_rows = jnp.broadcast_to(inv_freq[:, None], (ROT_HALF, V7X_LANES))
    pos_rows = positions.reshape(b, 1, s)

    kv, w0 = _mem_kv(mem, mem_norm_g, w_mem_kv, attn_w_in[0])
    qkv0, qm0, z0, qkv1, qkv2, w1, wo0, wo1 = _proj0(
        x, norm_g[0:1], pos_rows, freq_rows, w0, conv_w_in[0], attn_w_out[0], conv_w_out[0])
    qkv1 = qkv1.reshape(qkv0.shape)
    qkv2 = qkv2.reshape(qkv0.shape)
    mix = _attn(qkv0, qkv1, qkv2)
    return _tail(x, mix, qm0, z0, kv, wo0, norm_g[1:2], w1, conv_w[0], wo1, final_g.reshape(1, D_MODEL))
```

```python
import functools

import numpy as np
import jax
import jax.numpy as jnp
from jax import lax
from jax.experimental import pallas as pl
from jax.experimental.pallas import tpu as pltpu

F32 = jnp.float32
BF16 = jnp.bfloat16

D_MODEL = 1024
HEAD_DIM = 64
ROT_DIM = 16
ROT_HALF = ROT_DIM // 2
ROPE_THETA = 500000.0
DILATIONS = (1, 4, 16)
BLOCK = 128
GROUP_WIDTH = 512
N_GROUPS = 3
N_MEM = 256
MEM_WIDTH = 256
CONV_WIDTH = 1024
EPS = 1e-6
SCORE_SCALE = HEAD_DIM ** -0.5 * float(np.log2(np.e))

V7X_LANES = 128
V7X_SUBLANES = 8
BF16_SUBLANES = 16
ROW_TILE = 512
COL_CHUNK = 256
VMEM_LIMIT_BYTES = 56 * 1024 * 1024
NEG_BIG = -1e30
MEM_ROWS = 256

QKV_G = 3 * GROUP_WIDTH
A_PART = N_GROUPS * GROUP_WIDTH
A_QM = 3 * A_PART
A_Z = A_QM + MEM_WIDTH
BRANCH_A = GROUP_WIDTH + MEM_WIDTH
IN_A_COLS = A_Z + BRANCH_A
B_BG, B_CG, B_U = 0, CONV_WIDTH, 2 * CONV_WIDTH
B_QM = 3 * CONV_WIDTH
B_Z = B_QM + MEM_WIDTH
BRANCH_B = CONV_WIDTH + MEM_WIDTH
IN_B_COLS = B_Z + BRANCH_B


def _column_scale(n_cols, scaled):
    scale = np.ones((n_cols,), np.float32)
    for lo, hi in scaled:
        scale[lo:hi] = SCORE_SCALE
    return scale


_A_SCALE = _column_scale(IN_A_COLS, [(0, A_PART), (A_QM, A_QM + MEM_WIDTH)])


def _params(*sem):
    return pltpu.CompilerParams(dimension_semantics=sem, vmem_limit_bytes=VMEM_LIMIT_BYTES)


def _resident(shape, index_map):
    return pl.BlockSpec(shape, index_map, pipeline_mode=pl.Buffered(1))


def _silu(z):
    return z * jax.nn.sigmoid(z)


def _dot(a, b):
    return jnp.dot(a, b, preferred_element_type=F32)


def _dot_nt(a, b):
    return lax.dot_general(a, b, (((1,), (1,)), ((), ())), preferred_element_type=F32)


def _half_masks():
    lane = lax.broadcasted_iota(jnp.int32, (1, V7X_LANES), 1)
    return (jnp.where(lane < HEAD_DIM, 1.0, 0.0).astype(BF16),
            jnp.where(lane < HEAD_DIM, 0.0, 1.0).astype(BF16))


def _memkv_kernel(mem_ref, g_ref, w_ref, wa_ref, scale_ref, kv_ref, wab_ref):
    w = w_ref[0].astype(BF16)
    for c in range(mem_ref.shape[0] // ROW_TILE):
        rows = slice(ROW_TILE * c, ROW_TILE * (c + 1))
        m = mem_ref[rows, :]
        ms = jnp.mean(m * m, axis=-1, keepdims=True)
        mn = (m * lax.rsqrt(ms + EPS) * g_ref[0]).astype(BF16)
        kv_ref[0, rows, :] = _dot(mn, w).astype(BF16)
    for c in range(wa_ref.shape[0] // V7X_LANES):
        rows = slice(V7X_LANES * c, V7X_LANES * (c + 1))
        wab_ref[rows, :] = (wa_ref[rows, :] * scale_ref[...]).astype(BF16)


def _mem_kv(mem, mem_norm_g, w_mem_kv, w_attn_in):
    b = mem.shape[0]
    depth = w_mem_kv.shape[0]
    rows = b * N_MEM
    slab = pl.BlockSpec((D_MODEL // depth, IN_A_COLS), lambda l: (l, 0))
    kv, w0 = pl.pallas_call(
        _memkv_kernel,
        grid=(depth,),
        in_specs=[
            _resident((rows, D_MODEL), lambda l: (0, 0)),
            pl.BlockSpec((1, 1, D_MODEL), lambda l: (l, 0, 0)),
            pl.BlockSpec((1, D_MODEL, 2 * MEM_WIDTH), lambda l: (l, 0, 0)),
            slab,
            pl.BlockSpec((1, IN_A_COLS), lambda l: (0, 0)),
        ],
        out_specs=[pl.BlockSpec((1, rows, 2 * MEM_WIDTH), lambda l: (l, 0, 0)), slab],
        out_shape=[jax.ShapeDtypeStruct((depth, rows, 2 * MEM_WIDTH), BF16),
                   jax.ShapeDtypeStruct((D_MODEL, IN_A_COLS), BF16)],
        compiler_params=_params("arbitrary"),
        name="mem_kv",
    )(mem.reshape(rows, D_MODEL), mem_norm_g.reshape(depth, 1, D_MODEL), w_mem_kv,
      w_attn_in, _A_SCALE.reshape(1, IN_A_COLS))
    return kv.reshape(depth, b, N_MEM, 2 * MEM_WIDTH), w0


def _rope_tables(pos_ref, freq_ref, tbl_scr):
    rows = tbl_scr.shape[1]
    freq = freq_ref[...]
    one = jnp.ones((V7X_SUBLANES, V7X_LANES), F32)
    zero = jnp.zeros((V7X_SUBLANES, V7X_LANES), F32)
    groups = V7X_LANES // V7X_SUBLANES
    per_head = HEAD_DIM // V7X_SUBLANES

    def lane_rows(first, second, other):
        pieces = []
        for i in range(groups):
            pieces.append(first if i % per_head == 0 else second if i % per_head == 1 else other)
        return jnp.concatenate(pieces, axis=0).T

    for c in range(rows // V7X_LANES):
        sl = slice(V7X_LANES * c, V7X_LANES * (c + 1))
        ang = pos_ref[0, :, sl].astype(F32) * freq
        cs = jnp.cos(ang)
        sn = jnp.sin(ang)
        tbl_scr[0, sl, :] = lane_rows(cs, cs, one)
        tbl_scr[1, sl, :] = lane_rows(-sn, zero, zero)
        tbl_scr[2, sl, :] = lane_rows(zero, sn, zero)


def _rope(a, cs, sn_next, sn_prev):
    return (a * cs + pltpu.roll(a, V7X_LANES - ROT_HALF, 1) * sn_next
            + pltpu.roll(a, ROT_HALF, 1) * sn_prev)


def _proj0_kernel(x_ref, g_ref, pos_ref, freq_ref, w_ref, w1_ref, wo0_ref, wo1_ref,
                  o0_ref, oqm_ref, oz_ref, o1_ref, o2_ref, w1b_ref, wo0b_ref, wo1b_ref,
                  hn_scr, perm_scr, tbl_scr, p4_scr, tbl4_scr):
    tm = x_ref.shape[1]
    x = x_ref[0]
    ms = jnp.mean(x * x, axis=-1, keepdims=True)
    hn = x * lax.rsqrt(ms + EPS) * g_ref[...]
    hb = hn.astype(BF16)
    n_slab = D_MODEL // V7X_LANES
    for c in range(n_slab):
        hn_scr[c] = hn[:, V7X_LANES * c:V7X_LANES * (c + 1)]
    _rope_tables(pos_ref, freq_ref, tbl_scr)

    def qkv_chunk(lhs, g, j, tables):
        part, half = divmod(j, GROUP_WIDTH // COL_CHUNK)
        c0 = A_PART * part + GROUP_WIDTH * g + COL_CHUNK * half
        acc = _dot(lhs, w_ref[:, c0:c0 + COL_CHUNK])
        if part < 2:
            acc = jnp.concatenate(
                [_rope(acc[:, :V7X_LANES], *tables), _rope(acc[:, V7X_LANES:], *tables)], axis=1)
        return acc.astype(BF16)

    tables = tuple(tbl_scr[t] for t in range(3))
    for j in range(QKV_G // COL_CHUNK):
        res = qkv_chunk(hb, 0, j, tables)
        for e in range(COL_CHUNK // V7X_LANES):
            o0_ref[0, 2 * j + e] = res[:, V7X_LANES * e:V7X_LANES * (e + 1)]
    oqm_ref[0] = _dot(hb, w_ref[:, A_QM:A_QM + MEM_WIDTH]).astype(BF16)
    for j in range(BRANCH_A // COL_CHUNK):
        oz_ref[0, :, COL_CHUNK * j:COL_CHUNK * (j + 1)] = _dot(
            hb, w_ref[:, A_Z + COL_CHUNK * j:A_Z + COL_CHUNK * (j + 1)])

    d1 = DILATIONS[1]
    n1 = tm // d1
    for r in range(d1):
        rows = slice(r * n1, (r + 1) * n1)
        for c in range(n_slab):
            p4_scr[c, rows, :] = hn_scr[c, pl.ds(r, n1, stride=d1), :]
        for t in range(3):
            tbl4_scr[t, rows, :] = tbl_scr[t, pl.ds(r, n1, stride=d1), :]

    def rows_of_stream(g, r):
        if g == 1:
            return slice(r * n1, (r + 1) * n1)
        return pl.ds((r % d1) * n1 + r // d1, tm // DILATIONS[g], stride=DILATIONS[g] // d1)

    for g, o_ref in ((1, o1_ref), (2, o2_ref)):
        d = DILATIONS[g]
        n = tm // d
        for r in range(d):
            for c in range(n_slab):
                perm_scr[r * n:(r + 1) * n, V7X_LANES * c:V7X_LANES * (c + 1)] = (
                    p4_scr[c, rows_of_stream(g, r), :].astype(BF16))
        tables = tuple(
            jnp.concatenate([tbl4_scr[t, rows_of_stream(g, r), :] for r in range(d)], axis=0)
            for t in range(3))
        lhs = perm_scr[...]
        for j in range(QKV_G // COL_CHUNK):
            res = qkv_chunk(lhs, g, j, tables)
            for r in range(d):
                for e in range(COL_CHUNK // V7X_LANES):
                    o_ref[0, 2 * j + e, r] = res[r * n:(r + 1) * n, V7X_LANES * e:V7X_LANES * (e + 1)]

    w1b_ref[:, :B_QM] = w1_ref[:, :B_QM].astype(BF16)
    w1b_ref[:, B_QM:B_Z] = (w1_ref[:, B_QM:B_Z] * SCORE_SCALE).astype(BF16)
    w1b_ref[:, B_Z:] = w1_ref[:, B_Z:].astype(BF16)
    wo0b_ref[...] = wo0_ref[...].astype(BF16)
    wo1b_ref[...] = wo1_ref[...].astype(BF16)


def _proj0(x, norm_g, pos_rows, freq_rows, w0, w1, wo0, wo1):
    b, s, _ = x.shape
    tm = ROW_TILE
    per_seq = s // tm
    n_steps = b * per_seq
    row = lambda width: pl.BlockSpec((1, tm, width), lambda i, j: (i, j, 0))
    n_slab = QKV_G // V7X_LANES
    natural = pl.BlockSpec((1, n_slab, tm, V7X_LANES), lambda i, j: (i, 0, j, 0))
    stream = lambda d: pl.BlockSpec((1, n_slab, d, tm // d, V7X_LANES), lambda i, j: (i, 0, 0, j, 0))

    def slab(w, steps):
        rows = w.shape[0] // steps
        assert rows * steps == w.shape[0] and rows % BF16_SUBLANES == 0
        return pl.BlockSpec((rows, w.shape[1]), lambda i, j: (jnp.minimum(i * per_seq + j, steps - 1), 0))

    slabs = [slab(w1, n_steps), slab(wo0, n_steps // 2), slab(wo1, n_steps // 2)]
    d1, d2 = DILATIONS[1], DILATIONS[2]
    return pl.pallas_call(
        _proj0_kernel,
        grid=(b, per_seq),
        in_specs=[
            row(D_MODEL),
            pl.BlockSpec((1, D_MODEL), lambda i, j: (0, 0)),
            pl.BlockSpec((1, 1, tm), lambda i, j: (i, 0, j)),
            pl.BlockSpec((V7X_SUBLANES, V7X_LANES), lambda i, j: (0, 0)),
            _resident((D_MODEL, IN_A_COLS), lambda i, j: (0, 0)),
        ] + slabs,
        out_specs=[natural, row(MEM_WIDTH), row(BRANCH_A), stream(d1), stream(d2)] + slabs,
        out_shape=[
            jax.ShapeDtypeStruct((b, n_slab, s, V7X_LANES), BF16),
            jax.ShapeDtypeStruct((b, s, MEM_WIDTH), BF16),
            jax.ShapeDtypeStruct((b, s, BRANCH_A), F32),
            jax.ShapeDtypeStruct((b, n_slab, d1, s // d1, V7X_LANES), BF16),
            jax.ShapeDtypeStruct((b, n_slab, d2, s // d2, V7X_LANES), BF16),
            jax.ShapeDtypeStruct(w1.shape, BF16),
            jax.ShapeDtypeStruct(wo0.shape, BF16),
            jax.ShapeDtypeStruct(wo1.shape, BF16),
        ],
        scratch_shapes=[
            pltpu.VMEM((D_MODEL // V7X_LANES, tm, V7X_LANES), F32),
            pltpu.VMEM((tm, D_MODEL), BF16),
            pltpu.VMEM((3, tm, V7X_LANES), F32),
            pltpu.VMEM((D_MODEL // V7X_LANES, tm, V7X_LANES), F32),
            pltpu.VMEM((3, tm, V7X_LANES), F32),
        ],
        compiler_params=_params("arbitrary", "arbitrary"),
        name="proj0",
    )(x, norm_g, pos_rows, freq_rows, w0, w1, wo0, wo1)


def _attn_kernel(q0, k0, v0, q1, k1, v1, q2, k2, v2, o_ref,
                 num_scr, m_scr, l_scr, ve_scr, bias2_scr, bias1_scr):
    s_len = o_ref.shape[1]
    half0, half1 = _half_masks()
    first_half = lax.broadcasted_iota(jnp.int32, (BLOCK, V7X_LANES), 1) < HEAD_DIM

    @pl.when((pl.program_id(0) == 0) & (pl.program_id(1) == 0))
    def _():
        qi = lax.broadcasted_iota(jnp.int32, (2 * BLOCK, 2 * BLOCK), 0) & (BLOCK - 1)
        kj = lax.broadcasted_iota(jnp.int32, (2 * BLOCK, 2 * BLOCK), 1)
        valid_two = ((kj < BLOCK) & (kj >= qi)) | ((kj >= BLOCK) & (kj - BLOCK <= qi))
        bias2_scr[...] = jnp.where(valid_two, 0.0, NEG_BIG)
        valid_one = (lax.broadcasted_iota(jnp.int32, (2 * BLOCK, BLOCK), 1)
                     <= (lax.broadcasted_iota(jnp.int32, (2 * BLOCK, BLOCK), 0) & (BLOCK - 1)))
        bias1_scr[...] = jnp.where(valid_one, 0.0, NEG_BIG)

    for g, v_ref in enumerate((v0, v1, v2)):
        v = v_ref[...]
        for h, half in enumerate((half0, half1)):
            ve_scr[g, h] = v * half

    def block(q_ref, k_ref, g, q_row, k_row, n_keys, bias_ref, dst):
        q = q_ref[pl.ds(q_row, BLOCK), :]
        qs = jnp.concatenate([q * half0, q * half1], axis=0)
        k = k_ref[pl.ds(k_row, n_keys), :]
        s = _dot_nt(qs, k) + bias_ref[...]
        m = jnp.max(s, axis=-1, keepdims=True)
        p = jnp.exp2(s - m).astype(BF16)
        p_cat = jnp.concatenate([p[:BLOCK], p[BLOCK:]], axis=1)
        keys = pl.ds(k_row, n_keys)
        ve = jnp.concatenate(
            [jnp.concatenate([ve_scr[g, h, keys, :], jnp.broadcast_to(half, (n_keys, V7X_LANES))], axis=1)
             for h, half in enumerate((half0, half1))], axis=0)
        pv = _dot(p_cat, ve)
        num_scr[g, dst, :] = pv[:, :V7X_LANES]
        l_scr[g, dst, :] = pv[:, V7X_LANES:]
        m_scr[g, dst, :] = jnp.where(first_half, m[:BLOCK], m[BLOCK:])

    d1, d2 = DILATIONS[1], DILATIONS[2]
    stream_len = s_len // d1
    sub = d2 // d1

    for r in range(d2):
        block(q2, k2, 2, r * BLOCK, r * BLOCK, BLOCK, bias1_scr,
              pl.ds((r % d1) * stream_len + r // d1, BLOCK, stride=sub))

    for r in range(d1):
        base = r * stream_len
        block(q1, k1, 1, base, base, BLOCK, bias1_scr, pl.ds(base, BLOCK))
        for n in range(1, stream_len // BLOCK):
            block(q1, k1, 1, base + n * BLOCK, base + (n - 1) * BLOCK, 2 * BLOCK, bias2_scr,
                  pl.ds(base + n * BLOCK, BLOCK))

    def merge(n):
        rows_per = BLOCK // d1
        for r in range(d1):
            tok = pl.ds(n * BLOCK + r, rows_per, stride=d1)
            rows = pl.ds(r * stream_len + n * rows_per, rows_per)
            idx = (tok, rows, rows)
            ms = [m_scr[g, idx[g], :] for g in range(N_GROUPS)]
            top = jnp.maximum(ms[0], jnp.maximum(ms[1], ms[2]))
            ws = [jnp.exp2(mg - top) for mg in ms]
            nums = [ws[g] * num_scr[g, idx[g], :] for g in range(N_GROUPS)]
            dens = [ws[g] * l_scr[g, idx[g], :] for g in range(N_GROUPS)]
            num = nums[0] + nums[1] + nums[2]
            den = dens[0] + dens[1] + dens[2]
            o_ref[0, tok, :] = num / den

    block(q0, k0, 0, 0, 0, BLOCK, bias1_scr, pl.ds(0, BLOCK))
    for n in range(1, s_len // BLOCK):
        block(q0, k0, 0, n * BLOCK, (n - 1) * BLOCK, 2 * BLOCK, bias2_scr, pl.ds(n * BLOCK, BLOCK))
        merge(n - 1)
    merge(s_len // BLOCK - 1)


def _attn(qkv0, qkv1, qkv2):
    b, _, s, _ = qkv0.shape
    n_pair = GROUP_WIDTH // V7X_LANES

    def part(k):
        return pl.BlockSpec((None, None, s, V7X_LANES), lambda i, p, k=k: (i, k * n_pair + p, 0, 0))

    specs = [part(0), part(1), part(2)] * N_GROUPS
    stat = pltpu.VMEM((N_GROUPS, s, V7X_LANES), F32)
    scratch = [
        stat, stat, stat,
        pltpu.VMEM((N_GROUPS, 2, s, V7X_LANES), BF16),
        pltpu.VMEM((2 * BLOCK, 2 * BLOCK), F32),
        pltpu.VMEM((2 * BLOCK, BLOCK), F32),
    ]
    return pl.pallas_call(
        _attn_kernel,
        grid=(b, n_pair),
        in_specs=specs,
        out_specs=pl.BlockSpec((1, s, V7X_LANES), lambda i, p: (i, 0, p)),
        out_shape=jax.ShapeDtypeStruct((b, s, GROUP_WIDTH), F32),
        scratch_shapes=scratch,
        compiler_params=_params("arbitrary", "arbitrary"),
        name="dilated_attn",
    )(qkv0, qkv0, qkv0, qkv1, qkv1, qkv1, qkv2, qkv2, qkv2)


def _mem_scores(qm, kv_ref):
    rows = qm.shape[0]
    half0, half1 = _half_masks()
    scores = []
    for p in range(MEM_WIDTH // V7X_LANES):
        k = kv_ref[0, 0, :, V7X_LANES * p:V7X_LANES * (p + 1)]
        for c in range(rows // MEM_ROWS):
            q = qm[MEM_ROWS * c:MEM_ROWS * (c + 1), V7X_LANES * p:V7X_LANES * (p + 1)]
            scores.append(_dot_nt(jnp.concatenate([q * half0, q * half1], axis=0), k))
    return scores


def _mem_softmax_pv(scores, kv_ref):
    half0, half1 = _half_masks()
    n_pairs = MEM_WIDTH // V7X_LANES
    per_pair = len(scores) // n_pairs
    outs = []
    for p in range(n_pairs):
        v = kv_ref[0, 0, :, MEM_WIDTH + V7X_LANES * p:MEM_WIDTH + V7X_LANES * (p + 1)]
        ve = jnp.concatenate(
            [jnp.concatenate([v * half, jnp.broadcast_to(half, (N_MEM, V7X_LANES))], axis=1)
             for half in (half0, half1)], axis=0)
        chunks = []
        for s in scores[per_pair * p:per_pair * (p + 1)]:
            e = jnp.exp2(s - jnp.max(s, axis=-1, keepdims=True)).astype(BF16)
            pv = _dot(jnp.concatenate([e[:MEM_ROWS], e[MEM_ROWS:]], axis=1), ve)
            chunks.append(pv[:, :V7X_LANES] / pv[:, V7X_LANES:])
        outs.append(jnp.concatenate(chunks, axis=0))
    return jnp.concatenate(outs, axis=1)


def _tail_kernel(x_ref, mix_ref, qm0_ref, z0_ref, kv0_ref, wo0_ref, g1_ref, w1_ref, cw_ref,
                 kv1_ref, wo1_ref, gf_ref, out_ref,
                 a_scr, y_scr, h1_scr, hb_scr, *, tiles_per_seq):
    tm = x_ref.shape[1]
    halo = V7X_SUBLANES
    n_out = D_MODEL // COL_CHUNK
    step = pl.program_id(0)

    @pl.when(step == 0)
    def _():
        h1_scr[...] = jnp.zeros(h1_scr.shape, F32)
        a_scr[0:halo, :] = jnp.zeros((halo, CONV_WIDTH), F32)

    scores0 = _mem_scores(qm0_ref[0], kv0_ref)

    h1 = h1_scr[...]
    ms = jnp.mean(h1 * h1, axis=-1, keepdims=True)
    hb_scr[...] = (h1 * lax.rsqrt(ms + EPS) * g1_ref[...]).astype(BF16)
    seq_start = (jnp.maximum(step - 1, 0) % tiles_per_seq) == 0
    a_scr[0:halo, :] = jnp.where(seq_start, 0.0, a_scr[0:halo, :])

    def proj1(c0, width=COL_CHUNK):
        return _dot(hb_scr[...], w1_ref[:, c0:c0 + width])

    def conv_chunk(j):
        cs = slice(COL_CHUNK * j, COL_CHUNK * (j + 1))
        a_now = proj1(B_CG + COL_CHUNK * j) * proj1(B_U + COL_CHUNK * j)
        a_scr[halo:halo + tm, cs] = a_now
        conv = (cw_ref[0:1, cs] * a_scr[halo - 2:halo - 2 + tm, cs]
                + cw_ref[1:2, cs] * a_scr[halo - 1:halo - 1 + tm, cs]
                + cw_ref[2:3, cs] * a_now)
        mix1 = proj1(B_BG + COL_CHUNK * j) * conv
        y_scr[:, cs] = (mix1 * _silu(proj1(B_Z + COL_CHUNK * j))).astype(BF16)

    qm1 = proj1(B_QM, MEM_WIDTH).astype(BF16)
    z_mem = proj1(B_Z + CONV_WIDTH, MEM_WIDTH)
    conv_chunk(0)
    scores1 = _mem_scores(qm1, kv1_ref)
    conv_chunk(1)
    mem1 = _mem_softmax_pv(scores1, kv1_ref)
    y_scr[:, CONV_WIDTH:BRANCH_B] = (mem1 * _silu(z_mem)).astype(BF16)
    conv_chunk(2)
    conv_chunk(3)
    a_scr[0:halo, :] = a_scr[tm:tm + halo, :]
    mem0 = _mem_softmax_pv(scores0, kv0_ref)
    z0 = z0_ref[0]
    y0 = jnp.concatenate(
        [mix_ref[0] * _silu(z0[:, :GROUP_WIDTH]), mem0 * _silu(z0[:, GROUP_WIDTH:])], axis=1).astype(BF16)

    y1 = y_scr[...]
    ssq = jnp.zeros((tm, 1), F32)
    for j in range(n_out):
        cs = slice(COL_CHUNK * j, COL_CHUNK * (j + 1))
        h2 = h1_scr[:, cs] + _dot(y1, wo1_ref[:, cs])
        ssq = ssq + jnp.sum(h2 * h2, axis=-1, keepdims=True)
        out_ref[0, :, cs] = h2
    scale = lax.rsqrt(ssq * (1.0 / D_MODEL) + EPS)
    out_ref[0] = out_ref[0] * scale * gf_ref[...]

    for j in range(n_out):
        cs = slice(COL_CHUNK * j, COL_CHUNK * (j + 1))
        h1_scr[:, cs] = x_ref[0, :, cs] + _dot(y0, wo0_ref[:, cs])


def _tail(x, mix, qm0, z0, kv, wo0, norm_g1, w1, conv_w, wo1, final_g):
    b, s, _ = x.shape
    tm = ROW_TILE
    per_seq = s // tm
    n_tiles = b * per_seq

    def tile_a(j):
        return jnp.minimum(j, n_tiles - 1)

    def tile_b(j):
        return jnp.maximum(j - 1, 0)

    def row(width, tile):
        return pl.BlockSpec((1, tm, width), lambda j: (tile(j) // per_seq, tile(j) % per_seq, 0))

    def kv_spec(layer, tile):
        return pl.BlockSpec((1, 1, N_MEM, 2 * MEM_WIDTH), lambda j: (layer, tile(j) // per_seq, 0, 0))

    const = lambda shape: pl.BlockSpec(shape, lambda j: (0, 0))
    tile_f32 = pltpu.VMEM((tm, D_MODEL), F32)
    tile_bf16 = pltpu.VMEM((tm, D_MODEL), BF16)
    return pl.pallas_call(
        functools.partial(_tail_kernel, tiles_per_seq=per_seq),
        grid=(n_tiles + 1,),
        in_specs=[
            row(D_MODEL, tile_a), row(GROUP_WIDTH, tile_a), row(MEM_WIDTH, tile_a), row(BRANCH_A, tile_a),
            kv_spec(0, tile_a),
            _resident(wo0.shape, lambda j: (0, 0)),
            const((1, D_MODEL)),
            _resident(w1.shape, lambda j: (0, 0)),
            const((3, CONV_WIDTH)),
            kv_spec(1, tile_b),
            _resident(wo1.shape, lambda j: (0, 0)),
            const((1, D_MODEL)),
        ],
        out_specs=row(D_MODEL, tile_b),
        out_shape=jax.ShapeDtypeStruct((b, s, D_MODEL), F32),
        scratch_shapes=[
            pltpu.VMEM((tm + V7X_SUBLANES, CONV_WIDTH), F32),
            pltpu.VMEM((tm, BRANCH_B), BF16),
            tile_f32, tile_bf16,
        ],
        compiler_params=_params("arbitrary"),
        name="tail",
    )(x, mix, qm0, z0, kv, wo0, norm_g1, w1, conv_w, kv, wo1, final_g)


def kernel(x, mem, positions, norm_g, mem_norm_g, w_mem_kv, attn_w_in, attn_w_out, conv_w_in, conv_w,
           conv_w_out, final_g):
    b, s, _ = x.shape

    inv_freq = ROPE_THETA ** (-jnp.arange(ROT_HALF, dtype=F32) * (2.0 / ROT_DIM))
    freq_rows = jnp.broadcast_to(inv_freq[:, None], (ROT_HALF, V7X_LANES))
    pos_rows = positions.reshape(b, 1, s)

    kv, w0 = _mem_kv(mem, mem_norm_g, w_mem_kv, attn_w_in[0])
    qkv0, qm0, z0, qkv1, qkv2, w1, wo0, wo1 = _proj0(
        x, norm_g[0:1], pos_rows, freq_rows, w0, conv_w_in[0], attn_w_out[0], conv_w_out[0])
    qkv1 = qkv1.reshape(qkv0.shape)
    qkv2 = qkv2.reshape(qkv0.shape)
    mix = _attn(qkv0, qkv1, qkv2)
    return _tail(x, mix, qm0, z0, kv, wo0, norm_g[1:2], w1, conv_w[0], wo1, final_g.reshape(1, D_MODEL))
```

```python
import functools

import numpy as np
import jax
import jax.numpy as jnp
from jax import lax
from jax.experimental import pallas as pl
from jax.experimental.pallas import tpu as pltpu

F32 = jnp.float32
BF16 = jnp.bfloat16

D_MODEL = 1024
HEAD_DIM = 64
ROT_DIM = 16
ROT_HALF = ROT_DIM // 2
ROPE_THETA = 500000.0
DILATIONS = (1, 4, 16)
BLOCK = 128
GROUP_WIDTH = 512
N_GROUPS = 3
N_MEM = 256
MEM_WIDTH = 256
CONV_WIDTH = 1024
EPS = 1e-6
SCORE_SCALE = HEAD_DIM ** -0.5 * float(np.log2(np.e))

V7X_LANES = 128
V7X_SUBLANES = 8
BF16_SUBLANES = 16
ROW_TILE = 512
COL_CHUNK = 256
VMEM_LIMIT_BYTES = 56 * 1024 * 1024
NEG_BIG = -1e30
MEM_ROWS = 256
SMALL_ROWS = 16
ROW_G1, ROW_GF, ROW_CW = 0, 1, 2

QKV_G = 3 * GROUP_WIDTH
A_PART = N_GROUPS * GROUP_WIDTH
A_QM = 3 * A_PART
A_Z = A_QM + MEM_WIDTH
BRANCH_A = GROUP_WIDTH + MEM_WIDTH
IN_A_COLS = A_Z + BRANCH_A
B_BG, B_CG, B_U = 0, CONV_WIDTH, 2 * CONV_WIDTH
B_QM = 3 * CONV_WIDTH
B_Z = B_QM + MEM_WIDTH
BRANCH_B = CONV_WIDTH + MEM_WIDTH
IN_B_COLS = B_Z + BRANCH_B


def _column_scale(n_cols, scaled):
    scale = np.ones((n_cols,), np.float32)
    for lo, hi in scaled:
        scale[lo:hi] = SCORE_SCALE
    return scale


_A_SCALE = _column_scale(IN_A_COLS, [(0, A_PART), (A_QM, A_QM + MEM_WIDTH)])


def _params(*sem):
    return pltpu.CompilerParams(dimension_semantics=sem, vmem_limit_bytes=VMEM_LIMIT_BYTES)


def _resident(shape, index_map):
    return pl.BlockSpec(shape, index_map, pipeline_mode=pl.Buffered(1))


def _silu(z):
    return z * jax.nn.sigmoid(z)


def _dot(a, b):
    return jnp.dot(a, b, preferred_element_type=F32)


def _dot_nt(a, b):
    return lax.dot_general(a, b, (((1,), (1,)), ((), ())), preferred_element_type=F32)


def _half_masks():
    lane = lax.broadcasted_iota(jnp.int32, (1, V7X_LANES), 1)
    return (jnp.where(lane < HEAD_DIM, 1.0, 0.0).astype(BF16),
            jnp.where(lane < HEAD_DIM, 0.0, 1.0).astype(BF16))


def _memkv_kernel(mem_ref, g_ref, w_ref, wa_ref, scale_ref, kv_ref, wab_ref):
    w = w_ref[0].astype(BF16)
    for c in range(mem_ref.shape[0] // ROW_TILE):
        rows = slice(ROW_TILE * c, ROW_TILE * (c + 1))
        m = mem_ref[rows, :]
        ms = jnp.mean(m * m, axis=-1, keepdims=True)
        mn = (m * lax.rsqrt(ms + EPS) * g_ref[0]).astype(BF16)
        kv_ref[0, rows, :] = _dot(mn, w).astype(BF16)
    for c in range(wa_ref.shape[0] // V7X_LANES):
        rows = slice(V7X_LANES * c, V7X_LANES * (c + 1))
        wab_ref[rows, :] = (wa_ref[rows, :] * scale_ref[...]).astype(BF16)


def _mem_kv(mem, mem_norm_g, w_mem_kv, w_attn_in):
    b = mem.shape[0]
    depth = w_mem_kv.shape[0]
    rows = b * N_MEM
    slab = pl.BlockSpec((D_MODEL // depth, IN_A_COLS), lambda l: (l, 0))
    kv, w0 = pl.pallas_call(
        _memkv_kernel,
        grid=(depth,),
        in_specs=[
            _resident((rows, D_MODEL), lambda l: (0, 0)),
            pl.BlockSpec((1, 1, D_MODEL), lambda l: (l, 0, 0)),
            pl.BlockSpec((1, D_MODEL, 2 * MEM_WIDTH), lambda l: (l, 0, 0)),
            slab,
            pl.BlockSpec((1, IN_A_COLS), lambda l: (0, 0)),
        ],
        out_specs=[pl.BlockSpec((1, rows, 2 * MEM_WIDTH), lambda l: (l, 0, 0)), slab],
        out_shape=[jax.ShapeDtypeStruct((depth, rows, 2 * MEM_WIDTH), BF16),
                   jax.ShapeDtypeStruct((D_MODEL, IN_A_COLS), BF16)],
        compiler_params=_params("arbitrary"),
        name="mem_kv",
    )(mem.reshape(rows, D_MODEL), mem_norm_g.reshape(depth, 1, D_MODEL), w_mem_kv,
      w_attn_in, _A_SCALE.reshape(1, IN_A_COLS))
    return kv.reshape(depth, b, N_MEM, 2 * MEM_WIDTH), w0


def _rope_tables(pos_ref, freq_ref, tbl_scr):
    rows = tbl_scr.shape[1]
    freq = freq_ref[0:ROT_HALF, :]
    one = jnp.ones((V7X_SUBLANES, V7X_LANES), F32)
    zero = jnp.zeros((V7X_SUBLANES, V7X_LANES), F32)
    groups = V7X_LANES // V7X_SUBLANES
    per_head = HEAD_DIM // V7X_SUBLANES

    def lane_rows(first, second, other):
        pieces = []
        for i in range(groups):
            pieces.append(first if i % per_head == 0 else second if i % per_head == 1 else other)
        return jnp.concatenate(pieces, axis=0).T

    for c in range(rows // V7X_LANES):
        sl = slice(V7X_LANES * c, V7X_LANES * (c + 1))
        ang = pos_ref[0, :, sl].astype(F32) * freq
        cs = jnp.cos(ang)
        sn = jnp.sin(ang)
        tbl_scr[0, sl, :] = lane_rows(cs, cs, one)
        tbl_scr[1, sl, :] = lane_rows(-sn, zero, zero)
        tbl_scr[2, sl, :] = lane_rows(zero, sn, zero)


def _rope(a, cs, sn_next, sn_prev):
    return (a * cs + pltpu.roll(a, V7X_LANES - ROT_HALF, 1) * sn_next
            + pltpu.roll(a, ROT_HALF, 1) * sn_prev)


def _proj0_kernel(x_ref, g_ref, pos_ref, freq_ref, w_ref, w1_ref, wo0_ref, wo1_ref,
                  o0_ref, oqm_ref, oz_ref, o1_ref, o2_ref, w1b_ref, wo0b_ref, wo1b_ref,
                  hn_scr, perm_scr, tbl_scr, p4_scr, tbl4_scr):
    tm = x_ref.shape[1]
    x = x_ref[0]
    ms = jnp.mean(x * x, axis=-1, keepdims=True)
    hn = x * lax.rsqrt(ms + EPS) * g_ref[...]
    hb = hn.astype(BF16)
    n_slab = D_MODEL // V7X_LANES
    for c in range(n_slab):
        hn_scr[c] = hn[:, V7X_LANES * c:V7X_LANES * (c + 1)]
    _rope_tables(pos_ref, freq_ref, tbl_scr)

    def qkv_chunk(lhs, g, j, tables):
        part, half = divmod(j, GROUP_WIDTH // COL_CHUNK)
        c0 = A_PART * part + GROUP_WIDTH * g + COL_CHUNK * half
        acc = _dot(lhs, w_ref[:, c0:c0 + COL_CHUNK])
        if part < 2:
            acc = jnp.concatenate(
                [_rope(acc[:, :V7X_LANES], *tables), _rope(acc[:, V7X_LANES:], *tables)], axis=1)
        return acc.astype(BF16)

    tables = tuple(tbl_scr[t] for t in range(3))
    for j in range(QKV_G // COL_CHUNK):
        res = qkv_chunk(hb, 0, j, tables)
        for e in range(COL_CHUNK // V7X_LANES):
            o0_ref[0, 2 * j + e] = res[:, V7X_LANES * e:V7X_LANES * (e + 1)]
    oqm_ref[0] = _dot(hb, w_ref[:, A_QM:A_QM + MEM_WIDTH]).astype(BF16)
    for j in range(BRANCH_A // COL_CHUNK):
        oz_ref[0, :, COL_CHUNK * j:COL_CHUNK * (j + 1)] = _dot(
            hb, w_ref[:, A_Z + COL_CHUNK * j:A_Z + COL_CHUNK * (j + 1)])

    d1 = DILATIONS[1]
    n1 = tm // d1
    for r in range(d1):
        rows = slice(r * n1, (r + 1) * n1)
        for c in range(n_slab):
            p4_scr[c, rows, :] = hn_scr[c, pl.ds(r, n1, stride=d1), :]
        for t in range(3):
            tbl4_scr[t, rows, :] = tbl_scr[t, pl.ds(r, n1, stride=d1), :]

    def rows_of_stream(g, r):
        if g == 1:
            return slice(r * n1, (r + 1) * n1)
        return pl.ds((r % d1) * n1 + r // d1, tm // DILATIONS[g], stride=DILATIONS[g] // d1)

    for g, o_ref in ((1, o1_ref), (2, o2_ref)):
        d = DILATIONS[g]
        n = tm // d
        for r in range(d):
            for c in range(n_slab):
                perm_scr[r * n:(r + 1) * n, V7X_LANES * c:V7X_LANES * (c + 1)] = (
                    p4_scr[c, rows_of_stream(g, r), :].astype(BF16))
        tables = tuple(
            jnp.concatenate([tbl4_scr[t, rows_of_stream(g, r), :] for r in range(d)], axis=0)
            for t in range(3))
        lhs = perm_scr[...]
        for j in range(QKV_G // COL_CHUNK):
            res = qkv_chunk(lhs, g, j, tables)
            for r in range(d):
                for e in range(COL_CHUNK // V7X_LANES):
                    o_ref[0, 2 * j + e, r] = res[r * n:(r + 1) * n, V7X_LANES * e:V7X_LANES * (e + 1)]

    w1b_ref[:, :B_QM] = w1_ref[:, :B_QM].astype(BF16)
    w1b_ref[:, B_QM:B_Z] = (w1_ref[:, B_QM:B_Z] * SCORE_SCALE).astype(BF16)
    w1b_ref[:, B_Z:] = w1_ref[:, B_Z:].astype(BF16)
    wo0b_ref[...] = wo0_ref[...].astype(BF16)
    wo1b_ref[...] = wo1_ref[...].astype(BF16)


def _proj0(x, norm_g, pos_rows, freq_rows, w0, w1, wo0, wo1):
    b, s, _ = x.shape
    tm = ROW_TILE
    per_seq = s // tm
    n_steps = b * per_seq
    row = lambda width: pl.BlockSpec((1, tm, width), lambda i, j: (i, j, 0))
    n_slab = QKV_G // V7X_LANES
    natural = pl.BlockSpec((1, n_slab, tm, V7X_LANES), lambda i, j: (i, 0, j, 0))
    stream = lambda d: pl.BlockSpec((1, n_slab, d, tm // d, V7X_LANES), lambda i, j: (i, 0, 0, j, 0))

    def slab(w, steps):
        rows = w.shape[0] // steps
        assert rows * steps == w.shape[0] and rows % BF16_SUBLANES == 0
        return pl.BlockSpec((rows, w.shape[1]), lambda i, j: (jnp.minimum(i * per_seq + j, steps - 1), 0))

    slabs = [slab(w1, n_steps), slab(wo0, n_steps // 2), slab(wo1, n_steps // 2)]
    d1, d2 = DILATIONS[1], DILATIONS[2]
    return pl.pallas_call(
        _proj0_kernel,
        grid=(b, per_seq),
        in_specs=[
            row(D_MODEL),
            pl.BlockSpec((1, D_MODEL), lambda i, j: (0, 0)),
            pl.BlockSpec((1, 1, tm), lambda i, j: (i, 0, j)),
            pl.BlockSpec((SMALL_ROWS, V7X_LANES), lambda i, j: (0, 0)),
            _resident((D_MODEL, IN_A_COLS), lambda i, j: (0, 0)),
        ] + slabs,
        out_specs=[natural, row(MEM_WIDTH), row(BRANCH_A), stream(d1), stream(d2)] + slabs,
        out_shape=[
            jax.ShapeDtypeStruct((b, n_slab, s, V7X_LANES), BF16),
            jax.ShapeDtypeStruct((b, s, MEM_WIDTH), BF16),
            jax.ShapeDtypeStruct((b, s, BRANCH_A), F32),
            jax.ShapeDtypeStruct((b, n_slab, d1, s // d1, V7X_LANES), BF16),
            jax.ShapeDtypeStruct((b, n_slab, d2, s // d2, V7X_LANES), BF16),
            jax.ShapeDtypeStruct(w1.shape, BF16),
            jax.ShapeDtypeStruct(wo0.shape, BF16),
            jax.ShapeDtypeStruct(wo1.shape, BF16),
        ],
        scratch_shapes=[
            pltpu.VMEM((D_MODEL // V7X_LANES, tm, V7X_LANES), F32),
            pltpu.VMEM((tm, D_MODEL), BF16),
            pltpu.VMEM((3, tm, V7X_LANES), F32),
            pltpu.VMEM((D_MODEL // V7X_LANES, tm, V7X_LANES), F32),
            pltpu.VMEM((3, tm, V7X_LANES), F32),
        ],
        compiler_params=_params("arbitrary", "arbitrary"),
        name="proj0",
    )(x, norm_g, pos_rows, freq_rows, w0, w1, wo0, wo1)


def _attn_kernel(q0, k0, v0, q1, k1, v1, q2, k2, v2, o_ref,
                 num_scr, m_scr, l_scr, ve_scr, bias2_scr, bias1_scr):
    s_len = o_ref.shape[1]
    half0, half1 = _half_masks()
    first_half = lax.broadcasted_iota(jnp.int32, (BLOCK, V7X_LANES), 1) < HEAD_DIM

    @pl.when((pl.program_id(0) == 0) & (pl.program_id(1) == 0))
    def _():
        qi = lax.broadcasted_iota(jnp.int32, (2 * BLOCK, 2 * BLOCK), 0) & (BLOCK - 1)
        kj = lax.broadcasted_iota(jnp.int32, (2 * BLOCK, 2 * BLOCK), 1)
        valid_two = ((kj < BLOCK) & (kj >= qi)) | ((kj >= BLOCK) & (kj - BLOCK <= qi))
        bias2_scr[...] = jnp.where(valid_two, 0.0, NEG_BIG)
        valid_one = (lax.broadcasted_iota(jnp.int32, (2 * BLOCK, BLOCK), 1)
                     <= (lax.broadcasted_iota(jnp.int32, (2 * BLOCK, BLOCK), 0) & (BLOCK - 1)))
        bias1_scr[...] = jnp.where(valid_one, 0.0, NEG_BIG)

    for g, v_ref in enumerate((v0, v1, v2)):
        v = v_ref[...]
        for h, half in enumerate((half0, half1)):
            ve_scr[g, h] = v * half

    def block(q_ref, k_ref, g, q_row, k_row, n_keys, bias_ref, dst):
        q = q_ref[pl.ds(q_row, BLOCK), :]
        qs = jnp.concatenate([q * half0, q * half1], axis=0)
        k = k_ref[pl.ds(k_row, n_keys), :]
        s = _dot_nt(qs, k) + bias_ref[...]
        m = jnp.max(s, axis=-1, keepdims=True)
        p = jnp.exp2(s - m).astype(BF16)
        p_cat = jnp.concatenate([p[:BLOCK], p[BLOCK:]], axis=1)
        keys = pl.ds(k_row, n_keys)
        ve = jnp.concatenate(
            [jnp.concatenate([ve_scr[g, h, keys, :], jnp.broadcast_to(half, (n_keys, V7X_LANES))], axis=1)
             for h, half in enumerate((half0, half1))], axis=0)
        pv = _dot(p_cat, ve)
        num_scr[g, dst, :] = pv[:, :V7X_LANES]
        l_scr[g, dst, :] = pv[:, V7X_LANES:]
        m_scr[g, dst, :] = jnp.where(first_half, m[:BLOCK], m[BLOCK:])

    d1, d2 = DILATIONS[1], DILATIONS[2]
    stream_len = s_len // d1
    sub = d2 // d1

    for r in range(d2):
        block(q2, k2, 2, r * BLOCK, r * BLOCK, BLOCK, bias1_scr,
              pl.ds((r % d1) * stream_len + r // d1, BLOCK, stride=sub))

    for r in range(d1):
        base = r * stream_len
        block(q1, k1, 1, base, base, BLOCK, bias1_scr, pl.ds(base, BLOCK))
        for n in range(1, stream_len // BLOCK):
            block(q1, k1, 1, base + n * BLOCK, base + (n - 1) * BLOCK, 2 * BLOCK, bias2_scr,
                  pl.ds(base + n * BLOCK, BLOCK))

    def merge(n):
        rows_per = BLOCK // d1
        for r in range(d1):
            tok = pl.ds(n * BLOCK + r, rows_per, stride=d1)
            rows = pl.ds(r * stream_len + n * rows_per, rows_per)
            idx = (tok, rows, rows)
            ms = [m_scr[g, idx[g], :] for g in range(N_GROUPS)]
            top = jnp.maximum(ms[0], jnp.maximum(ms[1], ms[2]))
            ws = [jnp.exp2(mg - top) for mg in ms]
            nums = [ws[g] * num_scr[g, idx[g], :] for g in range(N_GROUPS)]
            dens = [ws[g] * l_scr[g, idx[g], :] for g in range(N_GROUPS)]
            num = nums[0] + nums[1] + nums[2]
            den = dens[0] + dens[1] + dens[2]
            o_ref[0, tok, :] = num / den

    block(q0, k0, 0, 0, 0, BLOCK, bias1_scr, pl.ds(0, BLOCK))
    for n in range(1, s_len // BLOCK):
        block(q0, k0, 0, n * BLOCK, (n - 1) * BLOCK, 2 * BLOCK, bias2_scr, pl.ds(n * BLOCK, BLOCK))
        merge(n - 1)
    merge(s_len // BLOCK - 1)


def _attn(qkv0, qkv1, qkv2):
    b, _, s, _ = qkv0.shape
    n_pair = GROUP_WIDTH // V7X_LANES

    def part(k):
        return pl.BlockSpec((None, None, s, V7X_LANES), lambda i, p, k=k: (i, k * n_pair + p, 0, 0))

    specs = [part(0), part(1), part(2)] * N_GROUPS
    stat = pltpu.VMEM((N_GROUPS, s, V7X_LANES), F32)
    scratch = [
        stat, stat, stat,
        pltpu.VMEM((N_GROUPS, 2, s, V7X_LANES), BF16),
        pltpu.VMEM((2 * BLOCK, 2 * BLOCK), F32),
        pltpu.VMEM((2 * BLOCK, BLOCK), F32),
    ]
    return pl.pallas_call(
        _attn_kernel,
        grid=(b, n_pair),
        in_specs=specs,
        out_specs=pl.BlockSpec((1, s, V7X_LANES), lambda i, p: (i, 0, p)),
        out_shape=jax.ShapeDtypeStruct((b, s, GROUP_WIDTH), F32),
        scratch_shapes=scratch,
        compiler_params=_params("arbitrary", "arbitrary"),
        name="dilated_attn",
    )(qkv0, qkv0, qkv0, qkv1, qkv1, qkv1, qkv2, qkv2, qkv2)


def _mem_scores(qm, kv_ref):
    rows = qm.shape[0]
    half0, half1 = _half_masks()
    scores = []
    for p in range(MEM_WIDTH // V7X_LANES):
        k = kv_ref[0, 0, :, V7X_LANES * p:V7X_LANES * (p + 1)]
        for c in range(rows // MEM_ROWS):
            q = qm[MEM_ROWS * c:MEM_ROWS * (c + 1), V7X_LANES * p:V7X_LANES * (p + 1)]
            scores.append(_dot_nt(jnp.concatenate([q * half0, q * half1], axis=0), k))
    return scores


def _mem_softmax_pv(scores, kv_ref):
    half0, half1 = _half_masks()
    n_pairs = MEM_WIDTH // V7X_LANES
    per_pair = len(scores) // n_pairs
    outs = []
    for p in range(n_pairs):
        v = kv_ref[0, 0, :, MEM_WIDTH + V7X_LANES * p:MEM_WIDTH + V7X_LANES * (p + 1)]
        ve = jnp.concatenate(
            [jnp.concatenate([v * half, jnp.broadcast_to(half, (N_MEM, V7X_LANES))], axis=1)
             for half in (half0, half1)], axis=0)
        chunks = []
        for s in scores[per_pair * p:per_pair * (p + 1)]:
            e = jnp.exp2(s - jnp.max(s, axis=-1, keepdims=True)).astype(BF16)
            pv = _dot(jnp.concatenate([e[:MEM_ROWS], e[MEM_ROWS:]], axis=1), ve)
            chunks.append(pv[:, :V7X_LANES] / pv[:, V7X_LANES:])
        outs.append(jnp.concatenate(chunks, axis=0))
    return jnp.concatenate(outs, axis=1)


def _tail_kernel(x_ref, mix_ref, qm0_ref, z0_ref, kv0_ref, wo0_ref, w1_ref,
                 kv1_ref, wo1_ref, rows_ref, out_ref,
                 a_scr, y_scr, h1_scr, hb_scr, *, tiles_per_seq):
    tm = x_ref.shape[1]
    halo = V7X_SUBLANES
    n_out = D_MODEL // COL_CHUNK
    step = pl.program_id(0)

    @pl.when(step == 0)
    def _():
        h1_scr[...] = jnp.zeros(h1_scr.shape, F32)
        a_scr[0:halo, :] = jnp.zeros((halo, CONV_WIDTH), F32)

    scores0 = _mem_scores(qm0_ref[0], kv0_ref)

    h1 = h1_scr[...]
    ms = jnp.mean(h1 * h1, axis=-1, keepdims=True)
    hb_scr[...] = (h1 * lax.rsqrt(ms + EPS) * rows_ref[ROW_G1:ROW_G1 + 1, :]).astype(BF16)
    seq_start = (jnp.maximum(step - 1, 0) % tiles_per_seq) == 0
    a_scr[0:halo, :] = jnp.where(seq_start, 0.0, a_scr[0:halo, :])

    def proj1(c0, width=COL_CHUNK):
        return _dot(hb_scr[...], w1_ref[:, c0:c0 + width])

    def conv_chunk(j):
        cs = slice(COL_CHUNK * j, COL_CHUNK * (j + 1))
        a_now = proj1(B_CG + COL_CHUNK * j) * proj1(B_U + COL_CHUNK * j)
        a_scr[halo:halo + tm, cs] = a_now
        conv = (rows_ref[ROW_CW:ROW_CW + 1, cs] * a_scr[halo - 2:halo - 2 + tm, cs]
                + rows_ref[ROW_CW + 1:ROW_CW + 2, cs] * a_scr[halo - 1:halo - 1 + tm, cs]
                + rows_ref[ROW_CW + 2:ROW_CW + 3, cs] * a_now)
        mix1 = proj1(B_BG + COL_CHUNK * j) * conv
        y_scr[:, cs] = (mix1 * _silu(proj1(B_Z + COL_CHUNK * j))).astype(BF16)

    qm1 = proj1(B_QM, MEM_WIDTH).astype(BF16)
    z_mem = proj1(B_Z + CONV_WIDTH, MEM_WIDTH)
    conv_chunk(0)
    scores1 = _mem_scores(qm1, kv1_ref)
    conv_chunk(1)
    mem1 = _mem_softmax_pv(scores1, kv1_ref)
    y_scr[:, CONV_WIDTH:BRANCH_B] = (mem1 * _silu(z_mem)).astype(BF16)
    conv_chunk(2)
    conv_chunk(3)
    a_scr[0:halo, :] = a_scr[tm:tm + halo, :]
    mem0 = _mem_softmax_pv(scores0, kv0_ref)
    z0 = z0_ref[0]
    y0 = jnp.concatenate(
        [mix_ref[0] * _silu(z0[:, :GROUP_WIDTH]), mem0 * _silu(z0[:, GROUP_WIDTH:])], axis=1).astype(BF16)

    y1 = y_scr[...]
    ssq = jnp.zeros((tm, 1), F32)
    for j in range(n_out):
        cs = slice(COL_CHUNK * j, COL_CHUNK * (j + 1))
        h2 = h1_scr[:, cs] + _dot(y1, wo1_ref[:, cs])
        ssq = ssq + jnp.sum(h2 * h2, axis=-1, keepdims=True)
        out_ref[0, :, cs] = h2
    scale = lax.rsqrt(ssq * (1.0 / D_MODEL) + EPS)
    out_ref[0] = out_ref[0] * scale * rows_ref[ROW_GF:ROW_GF + 1, :]

    for j in range(n_out):
        cs = slice(COL_CHUNK * j, COL_CHUNK * (j + 1))
        h1_scr[:, cs] = x_ref[0, :, cs] + _dot(y0, wo0_ref[:, cs])


def _tail(x, mix, qm0, z0, kv, wo0, w1, wo1, small_rows):
    b, s, _ = x.shape
    tm = ROW_TILE
    per_seq = s // tm
    n_tiles = b * per_seq

    def tile_a(j):
        return jnp.minimum(j, n_tiles - 1)

    def tile_b(j):
        return jnp.maximum(j - 1, 0)

    def row(width, tile):
        return pl.BlockSpec((1, tm, width), lambda j: (tile(j) // per_seq, tile(j) % per_seq, 0))

    def kv_spec(layer, tile):
        return pl.BlockSpec((1, 1, N_MEM, 2 * MEM_WIDTH), lambda j: (layer, tile(j) // per_seq, 0, 0))

    const = lambda shape: pl.BlockSpec(shape, lambda j: (0, 0))
    tile_f32 = pltpu.VMEM((tm, D_MODEL), F32)
    tile_bf16 = pltpu.VMEM((tm, D_MODEL), BF16)
    return pl.pallas_call(
        functools.partial(_tail_kernel, tiles_per_seq=per_seq),
        grid=(n_tiles + 1,),
        in_specs=[
            row(D_MODEL, tile_a), row(GROUP_WIDTH, tile_a), row(MEM_WIDTH, tile_a), row(BRANCH_A, tile_a),
            kv_spec(0, tile_a),
            _resident(wo0.shape, lambda j: (0, 0)),
            _resident(w1.shape, lambda j: (0, 0)),
            kv_spec(1, tile_b),
            _resident(wo1.shape, lambda j: (0, 0)),
            const(small_rows.shape),
        ],
        out_specs=row(D_MODEL, tile_b),
        out_shape=jax.ShapeDtypeStruct((b, s, D_MODEL), F32),
        scratch_shapes=[
            pltpu.VMEM((tm + V7X_SUBLANES, CONV_WIDTH), F32),
            pltpu.VMEM((tm, BRANCH_B), BF16),
            tile_f32, tile_bf16,
        ],
        compiler_params=_params("arbitrary"),
        name="tail",
    )(x, mix, qm0, z0, kv, wo0, w1, kv, wo1, small_rows)


def kernel(x, mem, positions, norm_g, mem_norm_g, w_mem_kv, attn_w_in, attn_w_out, conv_w_in, conv_w,
           conv_w_out, final_g):
    b, s, _ = x.shape

    inv_freq = ROPE_THETA ** (-jnp.arange(ROT_HALF, dtype=F32) * (2.0 / ROT_DIM))
    freq_rows = jnp.broadcast_to(jnp.tile(inv_freq, SMALL_ROWS // ROT_HALF)[:, None], (SMALL_ROWS, V7X_LANES))
    pos_rows = positions.reshape(b, 1, s)

    kv, w0 = _mem_kv(mem, mem_norm_g, w_mem_kv, attn_w_in[0])
    qkv0, qm0, z0, qkv1, qkv2, w1, wo0, wo1 = _proj0(
        x, norm_g[0:1], pos_rows, freq_rows, w0, conv_w_in[0], attn_w_out[0], conv_w_out[0])
    qkv1 = qkv1.reshape(qkv0.shape)
    qkv2 = qkv2.reshape(qkv0.shape)
    mix = _attn(qkv0, qkv1, qkv2)
    small_rows = jnp.concatenate([norm_g[1:2], final_g.reshape(1, D_MODEL), conv_w[0]], axis=0)
    return _tail(x, mix, qm0, z0, kv, wo0, w1, wo1, small_rows)
```

```python
import functools

import numpy as np
import jax
import jax.numpy as jnp
from jax import lax
from jax.experimental import pallas as pl
from jax.experimental.pallas import tpu as pltpu

F32 = jnp.float32
BF16 = jnp.bfloat16

D_MODEL = 1024
HEAD_DIM = 64
ROT_DIM = 16
ROT_HALF = ROT_DIM // 2
ROPE_THETA = 500000.0
DILATIONS = (1, 4, 16)
BLOCK = 128
GROUP_WIDTH = 512
N_GROUPS = 3
N_MEM = 256
MEM_WIDTH = 256
CONV_WIDTH = 1024
EPS = 1e-6
SCORE_SCALE = HEAD_DIM ** -0.5 * float(np.log2(np.e))

V7X_LANES = 128
V7X_SUBLANES = 8
BF16_SUBLANES = 16
ROW_TILE = 512
COL_CHUNK = 256
VMEM_LIMIT_BYTES = 56 * 1024 * 1024
NEG_BIG = -1e30
MEM_ROWS = 256
SMALL_ROWS = 16
ROW_G1, ROW_GF, ROW_CW = 0, 1, 2

QKV_G = 3 * GROUP_WIDTH
A_PART = N_GROUPS * GROUP_WIDTH
A_QM = 3 * A_PART
A_Z = A_QM + MEM_WIDTH
BRANCH_A = GROUP_WIDTH + MEM_WIDTH
IN_A_COLS = A_Z + BRANCH_A
B_BG, B_CG, B_U = 0, CONV_WIDTH, 2 * CONV_WIDTH
B_QM = 3 * CONV_WIDTH
B_Z = B_QM + MEM_WIDTH
BRANCH_B = CONV_WIDTH + MEM_WIDTH
IN_B_COLS = B_Z + BRANCH_B


def _column_scale(n_cols, scaled):
    scale = np.ones((n_cols,), np.float32)
    for lo, hi in scaled:
        scale[lo:hi] = SCORE_SCALE
    return scale


_A_SCALE = _column_scale(IN_A_COLS, [(0, A_PART), (A_QM, A_QM + MEM_WIDTH)])


def _params(*sem):
    return pltpu.CompilerParams(dimension_semantics=sem, vmem_limit_bytes=VMEM_LIMIT_BYTES)


def _resident(shape, index_map):
    return pl.BlockSpec(shape, index_map, pipeline_mode=pl.Buffered(1))


def _silu(z):
    return z * jax.nn.sigmoid(z)


def _dot(a, b):
    return jnp.dot(a, b, preferred_element_type=F32)


def _dot_nt(a, b):
    return lax.dot_general(a, b, (((1,), (1,)), ((), ())), preferred_element_type=F32)


def _half_masks():
    lane = lax.broadcasted_iota(jnp.int32, (1, V7X_LANES), 1)
    return (jnp.where(lane < HEAD_DIM, 1.0, 0.0).astype(BF16),
            jnp.where(lane < HEAD_DIM, 0.0, 1.0).astype(BF16))


def _memkv_kernel(mem_ref, g_ref, w_ref, wa_ref, scale_ref, kv_ref, wab_ref):
    chunk = pl.program_id(1)
    m = mem_ref[pl.ds(pl.multiple_of(chunk * ROW_TILE, ROW_TILE), ROW_TILE), :]
    ms = jnp.mean(m * m, axis=-1, keepdims=True)
    mn = (m * lax.rsqrt(ms + EPS) * g_ref[0]).astype(BF16)
    kv_ref[0] = _dot(mn, w_ref[0].astype(BF16)).astype(BF16)
    wab_ref[...] = (wa_ref[...] * scale_ref[...]).astype(BF16)


def _mem_kv(mem, mem_norm_g, w_mem_kv, w_attn_in):
    b = mem.shape[0]
    depth = w_mem_kv.shape[0]
    rows = b * N_MEM
    n_chunks = rows // ROW_TILE
    slab = pl.BlockSpec((D_MODEL // (depth * n_chunks), IN_A_COLS), lambda l, c: (l * n_chunks + c, 0))
    kv, w0 = pl.pallas_call(
        _memkv_kernel,
        grid=(depth, n_chunks),
        in_specs=[
            _resident((rows, D_MODEL), lambda l, c: (0, 0)),
            pl.BlockSpec((1, 1, D_MODEL), lambda l, c: (l, 0, 0)),
            pl.BlockSpec((1, D_MODEL, 2 * MEM_WIDTH), lambda l, c: (l, 0, 0)),
            slab,
            pl.BlockSpec((1, IN_A_COLS), lambda l, c: (0, 0)),
        ],
        out_specs=[pl.BlockSpec((1, ROW_TILE, 2 * MEM_WIDTH), lambda l, c: (l, c, 0)), slab],
        out_shape=[jax.ShapeDtypeStruct((depth, rows, 2 * MEM_WIDTH), BF16),
                   jax.ShapeDtypeStruct((D_MODEL, IN_A_COLS), BF16)],
        compiler_params=_params("arbitrary", "arbitrary"),
        name="mem_kv",
    )(mem.reshape(rows, D_MODEL), mem_norm_g.reshape(depth, 1, D_MODEL), w_mem_kv,
      w_attn_in, _A_SCALE.reshape(1, IN_A_COLS))
    return kv.reshape(depth, b, N_MEM, 2 * MEM_WIDTH), w0


def _rope_tables(pos_ref, freq_ref, tbl_scr):
    rows = tbl_scr.shape[1]
    freq = freq_ref[0:ROT_HALF, :]
    one = jnp.ones((V7X_SUBLANES, V7X_LANES), F32)
    zero = jnp.zeros((V7X_SUBLANES, V7X_LANES), F32)
    groups = V7X_LANES // V7X_SUBLANES
    per_head = HEAD_DIM // V7X_SUBLANES

    def lane_rows(first, second, other):
        pieces = []
        for i in range(groups):
            pieces.append(first if i % per_head == 0 else second if i % per_head == 1 else other)
        return jnp.concatenate(pieces, axis=0).T

    for c in range(rows // V7X_LANES):
        sl = slice(V7X_LANES * c, V7X_LANES * (c + 1))
        ang = pos_ref[0, :, sl].astype(F32) * freq
        cs = jnp.cos(ang)
        sn = jnp.sin(ang)
        tbl_scr[0, sl, :] = lane_rows(cs, cs, one)
        tbl_scr[1, sl, :] = lane_rows(-sn, zero, zero)
        tbl_scr[2, sl, :] = lane_rows(zero, sn, zero)


def _rope(a, cs, sn_next, sn_prev):
    return (a * cs + pltpu.roll(a, V7X_LANES - ROT_HALF, 1) * sn_next
            + pltpu.roll(a, ROT_HALF, 1) * sn_prev)


def _proj0_kernel(x_ref, g_ref, pos_ref, freq_ref, w_ref, w1_ref, wo0_ref, wo1_ref,
                  o0_ref, oqm_ref, oz_ref, o1_ref, o2_ref, w1b_ref, wo0b_ref, wo1b_ref,
                  hn_scr, perm_scr, tbl_scr, p4_scr, tbl4_scr):
    tm = x_ref.shape[1]
    x = x_ref[0]
    ms = jnp.mean(x * x, axis=-1, keepdims=True)
    hn = x * lax.rsqrt(ms + EPS) * g_ref[...]
    hb = hn.astype(BF16)
    n_slab = D_MODEL // V7X_LANES
    for c in range(n_slab):
        hn_scr[c] = hn[:, V7X_LANES * c:V7X_LANES * (c + 1)]
    _rope_tables(pos_ref, freq_ref, tbl_scr)

    def qkv_chunk(lhs, g, j, tables):
        part, half = divmod(j, GROUP_WIDTH // COL_CHUNK)
        c0 = A_PART * part + GROUP_WIDTH * g + COL_CHUNK * half
        acc = _dot(lhs, w_ref[:, c0:c0 + COL_CHUNK])
        if part < 2:
            acc = jnp.concatenate(
                [_rope(acc[:, :V7X_LANES], *tables), _rope(acc[:, V7X_LANES:], *tables)], axis=1)
        return acc.astype(BF16)

    tables = tuple(tbl_scr[t] for t in range(3))
    for j in range(QKV_G // COL_CHUNK):
        res = qkv_chunk(hb, 0, j, tables)
        for e in range(COL_CHUNK // V7X_LANES):
            o0_ref[0, 2 * j + e] = res[:, V7X_LANES * e:V7X_LANES * (e + 1)]
    oqm_ref[0] = _dot(hb, w_ref[:, A_QM:A_QM + MEM_WIDTH]).astype(BF16)
    for j in range(BRANCH_A // COL_CHUNK):
        oz_ref[0, :, COL_CHUNK * j:COL_CHUNK * (j + 1)] = _dot(
            hb, w_ref[:, A_Z + COL_CHUNK * j:A_Z + COL_CHUNK * (j + 1)])

    d1 = DILATIONS[1]
    n1 = tm // d1
    for r in range(d1):
        rows = slice(r * n1, (r + 1) * n1)
        for c in range(n_slab):
            p4_scr[c, rows, :] = hn_scr[c, pl.ds(r, n1, stride=d1), :]
        for t in range(3):
            tbl4_scr[t, rows, :] = tbl_scr[t, pl.ds(r, n1, stride=d1), :]

    def rows_of_stream(g, r):
        if g == 1:
            return slice(r * n1, (r + 1) * n1)
        return pl.ds((r % d1) * n1 + r // d1, tm // DILATIONS[g], stride=DILATIONS[g] // d1)

    for g, o_ref in ((1, o1_ref), (2, o2_ref)):
        d = DILATIONS[g]
        n = tm // d
        for r in range(d):
            for c in range(n_slab):
                perm_scr[r * n:(r + 1) * n, V7X_LANES * c:V7X_LANES * (c + 1)] = (
                    p4_scr[c, rows_of_stream(g, r), :].astype(BF16))
        tables = tuple(
            jnp.concatenate([tbl4_scr[t, rows_of_stream(g, r), :] for r in range(d)], axis=0)
            for t in range(3))
        lhs = perm_scr[...]
        for j in range(QKV_G // COL_CHUNK):
            res = qkv_chunk(lhs, g, j, tables)
            for r in range(d):
                for e in range(COL_CHUNK // V7X_LANES):
                    o_ref[0, 2 * j + e, r] = res[r * n:(r + 1) * n, V7X_LANES * e:V7X_LANES * (e + 1)]

    w1b_ref[:, :B_QM] = w1_ref[:, :B_QM].astype(BF16)
    w1b_ref[:, B_QM:B_Z] = (w1_ref[:, B_QM:B_Z] * SCORE_SCALE).astype(BF16)
    w1b_ref[:, B_Z:] = w1_ref[:, B_Z:].astype(BF16)
    wo0b_ref[...] = wo0_ref[...].astype(BF16)
    wo1b_ref[...] = wo1_ref[...].astype(BF16)


def _proj0(x, norm_g, pos_rows, freq_rows, w0, w1, wo0, wo1):
    b, s, _ = x.shape
    tm = ROW_TILE
    per_seq = s // tm
    n_steps = b * per_seq
    row = lambda width: pl.BlockSpec((1, tm, width), lambda i, j: (i, j, 0))
    n_slab = QKV_G // V7X_LANES
    natural = pl.BlockSpec((1, n_slab, tm, V7X_LANES), lambda i, j: (i, 0, j, 0))
    stream = lambda d: pl.BlockSpec((1, n_slab, d, tm // d, V7X_LANES), lambda i, j: (i, 0, 0, j, 0))

    def slab(w, steps):
        rows = w.shape[0] // steps
        assert rows * steps == w.shape[0] and rows % BF16_SUBLANES == 0
        return pl.BlockSpec((rows, w.shape[1]), lambda i, j: (jnp.minimum(i * per_seq + j, steps - 1), 0))

    slabs = [slab(w1, n_steps), slab(wo0, n_steps // 2), slab(wo1, n_steps // 2)]
    d1, d2 = DILATIONS[1], DILATIONS[2]
    return pl.pallas_call(
        _proj0_kernel,
        grid=(b, per_seq),
        in_specs=[
            row(D_MODEL),
            pl.BlockSpec((1, D_MODEL), lambda i, j: (0, 0)),
            pl.BlockSpec((1, 1, tm), lambda i, j: (i, 0, j)),
            pl.BlockSpec((SMALL_ROWS, V7X_LANES), lambda i, j: (0, 0)),
            _resident((D_MODEL, IN_A_COLS), lambda i, j: (0, 0)),
        ] + slabs,
        out_specs=[natural, row(MEM_WIDTH), row(BRANCH_A), stream(d1), stream(d2)] + slabs,
        out_shape=[
            jax.ShapeDtypeStruct((b, n_slab, s, V7X_LANES), BF16),
            jax.ShapeDtypeStruct((b, s, MEM_WIDTH), BF16),
            jax.ShapeDtypeStruct((b, s, BRANCH_A), F32),
            jax.ShapeDtypeStruct((b, n_slab, d1, s // d1, V7X_LANES), BF16),
            jax.ShapeDtypeStruct((b, n_slab, d2, s // d2, V7X_LANES), BF16),
            jax.ShapeDtypeStruct(w1.shape, BF16),
            jax.ShapeDtypeStruct(wo0.shape, BF16),
            jax.ShapeDtypeStruct(wo1.shape, BF16),
        ],
        scratch_shapes=[
            pltpu.VMEM((D_MODEL // V7X_LANES, tm, V7X_LANES), F32),
            pltpu.VMEM((tm, D_MODEL), BF16),
            pltpu.VMEM((3, tm, V7X_LANES), F32),
            pltpu.VMEM((D_MODEL // V7X_LANES, tm, V7X_LANES), F32),
            pltpu.VMEM((3, tm, V7X_LANES), F32),
        ],
        compiler_params=_params("arbitrary", "arbitrary"),
        name="proj0",
    )(x, norm_g, pos_rows, freq_rows, w0, w1, wo0, wo1)


def _attn_kernel(q0, k0, v0, q1, k1, v1, q2, k2, v2, o_ref,
                 num_scr, m_scr, l_scr, ve_scr, bias2_scr, bias1_scr):
    s_len = o_ref.shape[1]
    half0, half1 = _half_masks()
    first_half = lax.broadcasted_iota(jnp.int32, (BLOCK, V7X_LANES), 1) < HEAD_DIM

    @pl.when((pl.program_id(0) == 0) & (pl.program_id(1) == 0))
    def _():
        qi = lax.broadcasted_iota(jnp.int32, (2 * BLOCK, 2 * BLOCK), 0) & (BLOCK - 1)
        kj = lax.broadcasted_iota(jnp.int32, (2 * BLOCK, 2 * BLOCK), 1)
        valid_two = ((kj < BLOCK) & (kj >= qi)) | ((kj >= BLOCK) & (kj - BLOCK <= qi))
        bias2_scr[...] = jnp.where(valid_two, 0.0, NEG_BIG)
        valid_one = (lax.broadcasted_iota(jnp.int32, (2 * BLOCK, BLOCK), 1)
                     <= (lax.broadcasted_iota(jnp.int32, (2 * BLOCK, BLOCK), 0) & (BLOCK - 1)))
        bias1_scr[...] = jnp.where(valid_one, 0.0, NEG_BIG)

    for g, v_ref in enumerate((v0, v1, v2)):
        v = v_ref[...]
        for h, half in enumerate((half0, half1)):
            ve_scr[g, h] = v * half

    def block(q_ref, k_ref, g, q_row, k_row, n_keys, bias_ref, dst):
        q = q_ref[pl.ds(q_row, BLOCK), :]
        qs = jnp.concatenate([q * half0, q * half1], axis=0)
        k = k_ref[pl.ds(k_row, n_keys), :]
        s = _dot_nt(qs, k) + bias_ref[...]
        m = jnp.max(s, axis=-1, keepdims=True)
        p = jnp.exp2(s - m).astype(BF16)
        p_cat = jnp.concatenate([p[:BLOCK], p[BLOCK:]], axis=1)
        keys = pl.ds(k_row, n_keys)
        ve = jnp.concatenate(
            [jnp.concatenate([ve_scr[g, h, keys, :], jnp.broadcast_to(half, (n_keys, V7X_LANES))], axis=1)
             for h, half in enumerate((half0, half1))], axis=0)
        pv = _dot(p_cat, ve)
        num_scr[g, dst, :] = pv[:, :V7X_LANES]
        l_scr[g, dst, :] = pv[:, V7X_LANES:]
        m_scr[g, dst, :] = jnp.where(first_half, m[:BLOCK], m[BLOCK:])

    d1, d2 = DILATIONS[1], DILATIONS[2]
    stream_len = s_len // d1
    sub = d2 // d1

    for r in range(d2):
        block(q2, k2, 2, r * BLOCK, r * BLOCK, BLOCK, bias1_scr,
              pl.ds((r % d1) * stream_len + r // d1, BLOCK, stride=sub))

    for r in range(d1):
        base = r * stream_len
        block(q1, k1, 1, base, base, BLOCK, bias1_scr, pl.ds(base, BLOCK))
        for n in range(1, stream_len // BLOCK):
            block(q1, k1, 1, base + n * BLOCK, base + (n - 1) * BLOCK, 2 * BLOCK, bias2_scr,
                  pl.ds(base + n * BLOCK, BLOCK))

    def merge(n):
        rows_per = BLOCK // d1
        for r in range(d1):
            tok = pl.ds(n * BLOCK + r, rows_per, stride=d1)
            rows = pl.ds(r * stream_len + n * rows_per, rows_per)
            idx = (tok, rows, rows)
            ms = [m_scr[g, idx[g], :] for g in range(N_GROUPS)]
            top = jnp.maximum(ms[0], jnp.maximum(ms[1], ms[2]))
            ws = [jnp.exp2(mg - top) for mg in ms]
            nums = [ws[g] * num_scr[g, idx[g], :] for g in range(N_GROUPS)]
            dens = [ws[g] * l_scr[g, idx[g], :] for g in range(N_GROUPS)]
            num = nums[0] + nums[1] + nums[2]
            den = dens[0] + dens[1] + dens[2]
            o_ref[0, tok, :] = num / den

    block(q0, k0, 0, 0, 0, BLOCK, bias1_scr, pl.ds(0, BLOCK))
    for n in range(1, s_len // BLOCK):
        block(q0, k0, 0, n * BLOCK, (n - 1) * BLOCK, 2 * BLOCK, bias2_scr, pl.ds(n * BLOCK, BLOCK))
        merge(n - 1)
    merge(s_len // BLOCK - 1)


def _attn(qkv0, qkv1, qkv2):
    b, _, s, _ = qkv0.shape
    n_pair = GROUP_WIDTH // V7X_LANES

    def part(k):
        return pl.BlockSpec((None, None, s, V7X_LANES), lambda i, p, k=k: (i, k * n_pair + p, 0, 0))

    specs = [part(0), part(1), part(2)] * N_GROUPS
    stat = pltpu.VMEM((N_GROUPS, s, V7X_LANES), F32)
    scratch = [
        stat, stat, stat,
        pltpu.VMEM((N_GROUPS, 2, s, V7X_LANES), BF16),
        pltpu.VMEM((2 * BLOCK, 2 * BLOCK), F32),
        pltpu.VMEM((2 * BLOCK, BLOCK), F32),
    ]
    return pl.pallas_call(
        _attn_kernel,
        grid=(b, n_pair),
        in_specs=specs,
        out_specs=pl.BlockSpec((1, s, V7X_LANES), lambda i, p: (i, 0, p)),
        out_shape=jax.ShapeDtypeStruct((b, s, GROUP_WIDTH), F32),
        scratch_shapes=scratch,
        compiler_params=_params("arbitrary", "arbitrary"),
        name="dilated_attn",
    )(qkv0, qkv0, qkv0, qkv1, qkv1, qkv1, qkv2, qkv2, qkv2)


def _mem_scores(qm, kv_ref):
    rows = qm.shape[0]
    half0, half1 = _half_masks()
    scores = []
    for p in range(MEM_WIDTH // V7X_LANES):
        k = kv_ref[0, 0, :, V7X_LANES * p:V7X_LANES * (p + 1)]
        for c in range(rows // MEM_ROWS):
            q = qm[MEM_ROWS * c:MEM_ROWS * (c + 1), V7X_LANES * p:V7X_LANES * (p + 1)]
            scores.append(_dot_nt(jnp.concatenate([q * half0, q * half1], axis=0), k))
    return scores


def _mem_softmax_pv(scores, kv_ref):
    half0, half1 = _half_masks()
    n_pairs = MEM_WIDTH // V7X_LANES
    per_pair = len(scores) // n_pairs
    outs = []
    for p in range(n_pairs):
        v = kv_ref[0, 0, :, MEM_WIDTH + V7X_LANES * p:MEM_WIDTH + V7X_LANES * (p + 1)]
        ve = jnp.concatenate(
            [jnp.concatenate([v * half, jnp.broadcast_to(half, (N_MEM, V7X_LANES))], axis=1)
             for half in (half0, half1)], axis=0)
        chunks = []
        for s in scores[per_pair * p:per_pair * (p + 1)]:
            e = jnp.exp2(s - jnp.max(s, axis=-1, keepdims=True)).astype(BF16)
            pv = _dot(jnp.concatenate([e[:MEM_ROWS], e[MEM_ROWS:]], axis=1), ve)
            chunks.append(pv[:, :V7X_LANES] / pv[:, V7X_LANES:])
        outs.append(jnp.concatenate(chunks, axis=0))
    return jnp.concatenate(outs, axis=1)


def _tail_kernel(x_ref, mix_ref, qm0_ref, z0_ref, kv0_ref, wo0_ref, w1_ref,
                 kv1_ref, wo1_ref, rows_ref, out_ref,
                 a_scr, y_scr, h1_scr, hb_scr, *, tiles_per_seq):
    tm = x_ref.shape[1]
    halo = V7X_SUBLANES
    n_out = D_MODEL // COL_CHUNK
    step = pl.program_id(0)

    @pl.when(step == 0)
    def _():
        h1_scr[...] = jnp.zeros(h1_scr.shape, F32)
        a_scr[0:halo, :] = jnp.zeros((halo, CONV_WIDTH), F32)

    scores0 = _mem_scores(qm0_ref[0], kv0_ref)

    h1 = h1_scr[...]
    ms = jnp.mean(h1 * h1, axis=-1, keepdims=True)
    hb_scr[...] = (h1 * lax.rsqrt(ms + EPS) * rows_ref[ROW_G1:ROW_G1 + 1, :]).astype(BF16)
    seq_start = (jnp.maximum(step - 1, 0) % tiles_per_seq) == 0
    a_scr[0:halo, :] = jnp.where(seq_start, 0.0, a_scr[0:halo, :])

    def proj1(c0, width=COL_CHUNK):
        return _dot(hb_scr[...], w1_ref[:, c0:c0 + width])

    def conv_chunk(j):
        cs = slice(COL_CHUNK * j, COL_CHUNK * (j + 1))
        a_now = proj1(B_CG + COL_CHUNK * j) * proj1(B_U + COL_CHUNK * j)
        a_scr[halo:halo + tm, cs] = a_now
        conv = (rows_ref[ROW_CW:ROW_CW + 1, cs] * a_scr[halo - 2:halo - 2 + tm, cs]
                + rows_ref[ROW_CW + 1:ROW_CW + 2, cs] * a_scr[halo - 1:halo - 1 + tm, cs]
                + rows_ref[ROW_CW + 2:ROW_CW + 3, cs] * a_now)
        mix1 = proj1(B_BG + COL_CHUNK * j) * conv
        y_scr[:, cs] = (mix1 * _silu(proj1(B_Z + COL_CHUNK * j))).astype(BF16)

    qm1 = proj1(B_QM, MEM_WIDTH).astype(BF16)
    z_mem = proj1(B_Z + CONV_WIDTH, MEM_WIDTH)
    conv_chunk(0)
    scores1 = _mem_scores(qm1, kv1_ref)
    conv_chunk(1)
    mem1 = _mem_softmax_pv(scores1, kv1_ref)
    y_scr[:, CONV_WIDTH:BRANCH_B] = (mem1 * _silu(z_mem)).astype(BF16)
    conv_chunk(2)
    conv_chunk(3)
    a_scr[0:halo, :] = a_scr[tm:tm + halo, :]
    mem0 = _mem_softmax_pv(scores0, kv0_ref)
    z0 = z0_ref[0]
    y0 = jnp.concatenate(
        [mix_ref[0] * _silu(z0[:, :GROUP_WIDTH]), mem0 * _silu(z0[:, GROUP_WIDTH:])], axis=1).astype(BF16)

    y1 = y_scr[...]
    ssq = jnp.zeros((tm, 1), F32)
    for j in range(n_out):
        cs = slice(COL_CHUNK * j, COL_CHUNK * (j + 1))
        h2 = h1_scr[:, cs] + _dot(y1, wo1_ref[:, cs])
        ssq = ssq + jnp.sum(h2 * h2, axis=-1, keepdims=True)
        out_ref[0, :, cs] = h2
    scale = lax.rsqrt(ssq * (1.0 / D_MODEL) + EPS)
    out_ref[0] = out_ref[0] * scale * rows_ref[ROW_GF:ROW_GF + 1, :]

    for j in range(n_out):
        cs = slice(COL_CHUNK * j, COL_CHUNK * (j + 1))
        h1_scr[:, cs] = x_ref[0, :, cs] + _dot(y0, wo0_ref[:, cs])


def _tail(x, mix, qm0, z0, kv, wo0, w1, wo1, small_rows):
    b, s, _ = x.shape
    tm = ROW_TILE
    per_seq = s // tm
    n_tiles = b * per_seq

    def tile_a(j):
        return jnp.minimum(j, n_tiles - 1)

    def tile_b(j):
        return jnp.maximum(j - 1, 0)

    def row(width, tile):
        return pl.BlockSpec((1, tm, width), lambda j: (tile(j) // per_seq, tile(j) % per_seq, 0))

    def kv_spec(layer, tile):
        return pl.BlockSpec((1, 1, N_MEM, 2 * MEM_WIDTH), lambda j: (layer, tile(j) // per_seq, 0, 0))

    const = lambda shape: pl.BlockSpec(shape, lambda j: (0, 0))
    tile_f32 = pltpu.VMEM((tm, D_MODEL), F32)
    tile_bf16 = pltpu.VMEM((tm, D_MODEL), BF16)
    return pl.pallas_call(
        functools.partial(_tail_kernel, tiles_per_seq=per_seq),
        grid=(n_tiles + 1,),
        in_specs=[
            row(D_MODEL, tile_a), row(GROUP_WIDTH, tile_a), row(MEM_WIDTH, tile_a), row(BRANCH_A, tile_a),
            kv_spec(0, tile_a),
            _resident(wo0.shape, lambda j: (0, 0)),
            _resident(w1.shape, lambda j: (0, 0)),
            kv_spec(1, tile_b),
            _resident(wo1.shape, lambda j: (0, 0)),
            const(small_rows.shape),
        ],
        out_specs=row(D_MODEL, tile_b),
        out_shape=jax.ShapeDtypeStruct((b, s, D_MODEL), F32),
        scratch_shapes=[
            pltpu.VMEM((tm + V7X_SUBLANES, CONV_WIDTH), F32),
            pltpu.VMEM((tm, BRANCH_B), BF16),
            tile_f32, tile_bf16,
        ],
        compiler_params=_params("arbitrary"),
        name="tail",
    )(x, mix, qm0, z0, kv, wo0, w1, kv, wo1, small_rows)


def kernel(x, mem, positions, norm_g, mem_norm_g, w_mem_kv, attn_w_in, attn_w_out, conv_w_in, conv_w,
           conv_w_out, final_g):
    b, s, _ = x.shape

    inv_freq = ROPE_THETA ** (-jnp.arange(ROT_HALF, dtype=F32) * (2.0 / ROT_DIM))
    freq_rows = jnp.broadcast_to(jnp.tile(inv_freq, SMALL_ROWS // ROT_HALF)[:, None], (SMALL_ROWS, V7X_LANES))
    pos_rows = positions.reshape(b, 1, s)

    kv, w0 = _mem_kv(mem, mem_norm_g, w_mem_kv, attn_w_in[0])
    qkv0, qm0, z0, qkv1, qkv2, w1, wo0, wo1 = _proj0(
        x, norm_g[0:1], pos_rows, freq_rows, w0, conv_w_in[0], attn_w_out[0], conv_w_out[0])
    qkv1 = qkv1.reshape(qkv0.shape)
    qkv2 = qkv2.reshape(qkv0.shape)
    mix = _attn(qkv0, qkv1, qkv2)
    small_rows = jnp.concatenate([norm_g[1:2], final_g.reshape(1, D_MODEL), conv_w[0]], axis=0)
    return _tail(x, mix, qm0, z0, kv, wo0, w1, wo1, small_rows)
```

```python
import functools

import numpy as np
import jax
import jax.numpy as jnp
from jax import lax
from jax.experimental import pallas as pl
from jax.experimental.pallas import tpu as pltpu

F32 = jnp.float32
BF16 = jnp.bfloat16

D_MODEL = 1024
HEAD_DIM = 64
ROT_DIM = 16
ROT_HALF = ROT_DIM // 2
ROPE_THETA = 500000.0
DILATIONS = (1, 4, 16)
BLOCK = 128
GROUP_WIDTH = 512
N_GROUPS = 3
N_MEM = 256
MEM_WIDTH = 256
CONV_WIDTH = 1024
EPS = 1e-6
SCORE_SCALE = HEAD_DIM ** -0.5 * float(np.log2(np.e))

V7X_LANES = 128
V7X_SUBLANES = 8
BF16_SUBLANES = 16
ROW_TILE = 512
COL_CHUNK = 256
VMEM_LIMIT_BYTES = 56 * 1024 * 1024
NEG_BIG = -1e30
MEM_ROWS = 256
SMALL_ROWS = 16
ROW_G1, ROW_GF, ROW_CW = 0, 1, 2

QKV_G = 3 * GROUP_WIDTH
A_PART = N_GROUPS * GROUP_WIDTH
A_QM = 3 * A_PART
A_Z = A_QM + MEM_WIDTH
BRANCH_A = GROUP_WIDTH + MEM_WIDTH
IN_A_COLS = A_Z + BRANCH_A
B_BG, B_CG, B_U = 0, CONV_WIDTH, 2 * CONV_WIDTH
B_QM = 3 * CONV_WIDTH
B_Z = B_QM + MEM_WIDTH
BRANCH_B = CONV_WIDTH + MEM_WIDTH
IN_B_COLS = B_Z + BRANCH_B


def _column_scale(n_cols, scaled):
    scale = np.ones((n_cols,), np.float32)
    for lo, hi in scaled:
        scale[lo:hi] = SCORE_SCALE
    return scale


_A_SCALE = _column_scale(IN_A_COLS, [(0, A_PART), (A_QM, A_QM + MEM_WIDTH)])


def _params(*sem):
    return pltpu.CompilerParams(dimension_semantics=sem, vmem_limit_bytes=VMEM_LIMIT_BYTES)


def _resident(shape, index_map):
    return pl.BlockSpec(shape, index_map, pipeline_mode=pl.Buffered(1))


def _silu(z):
    return z * jax.nn.sigmoid(z)


def _dot(a, b):
    return jnp.dot(a, b, preferred_element_type=F32)


def _dot_nt(a, b):
    return lax.dot_general(a, b, (((1,), (1,)), ((), ())), preferred_element_type=F32)


def _half_masks():
    lane = lax.broadcasted_iota(jnp.int32, (1, V7X_LANES), 1)
    return (jnp.where(lane < HEAD_DIM, 1.0, 0.0).astype(BF16),
            jnp.where(lane < HEAD_DIM, 0.0, 1.0).astype(BF16))


def _memkv_kernel(mem_ref, g_ref, w_ref, wa_ref, scale_ref, kv_ref, wab_ref):
    w = w_ref[0].astype(BF16)
    for c in range(mem_ref.shape[0] // ROW_TILE):
        rows = slice(ROW_TILE * c, ROW_TILE * (c + 1))
        m = mem_ref[rows, :]
        ms = jnp.mean(m * m, axis=-1, keepdims=True)
        mn = (m * lax.rsqrt(ms + EPS) * g_ref[0]).astype(BF16)
        kv_ref[0, rows, :] = _dot(mn, w).astype(BF16)
    for c in range(wa_ref.shape[0] // V7X_LANES):
        rows = slice(V7X_LANES * c, V7X_LANES * (c + 1))
        wab_ref[rows, :] = (wa_ref[rows, :] * scale_ref[...]).astype(BF16)


def _mem_kv(mem, mem_norm_g, w_mem_kv, w_attn_in):
    b = mem.shape[0]
    depth = w_mem_kv.shape[0]
    rows = b * N_MEM
    slab = pl.BlockSpec((D_MODEL // depth, IN_A_COLS), lambda l: (l, 0))
    kv, w0 = pl.pallas_call(
        _memkv_kernel,
        grid=(depth,),
        in_specs=[
            _resident((rows, D_MODEL), lambda l: (0, 0)),
            pl.BlockSpec((1, 1, D_MODEL), lambda l: (l, 0, 0)),
            pl.BlockSpec((1, D_MODEL, 2 * MEM_WIDTH), lambda l: (l, 0, 0)),
            slab,
            pl.BlockSpec((1, IN_A_COLS), lambda l: (0, 0)),
        ],
        out_specs=[pl.BlockSpec((1, rows, 2 * MEM_WIDTH), lambda l: (l, 0, 0)), slab],
        out_shape=[jax.ShapeDtypeStruct((depth, rows, 2 * MEM_WIDTH), BF16),
                   jax.ShapeDtypeStruct((D_MODEL, IN_A_COLS), BF16)],
        compiler_params=_params("arbitrary"),
        name="mem_kv",
    )(mem.reshape(rows, D_MODEL), mem_norm_g.reshape(depth, 1, D_MODEL), w_mem_kv,
      w_attn_in, _A_SCALE.reshape(1, IN_A_COLS))
    return kv.reshape(depth, b, N_MEM, 2 * MEM_WIDTH), w0


def _rope_tables(pos_ref, freq_ref, tbl_scr):
    rows = tbl_scr.shape[1]
    freq = freq_ref[0:ROT_HALF, :]
    one = jnp.ones((V7X_SUBLANES, V7X_LANES), F32)
    zero = jnp.zeros((V7X_SUBLANES, V7X_LANES), F32)
    groups = V7X_LANES // V7X_SUBLANES
    per_head = HEAD_DIM // V7X_SUBLANES

    def lane_rows(first, second, other):
        pieces = []
        for i in range(groups):
            pieces.append(first if i % per_head == 0 else second if i % per_head == 1 else other)
        return jnp.concatenate(pieces, axis=0).T

    for c in range(rows // V7X_LANES):
        sl = slice(V7X_LANES * c, V7X_LANES * (c + 1))
        ang = pos_ref[0, :, sl].astype(F32) * freq
        cs = jnp.cos(ang)
        sn = jnp.sin(ang)
        tbl_scr[0, sl, :] = lane_rows(cs, cs, one)
        tbl_scr[1, sl, :] = lane_rows(-sn, zero, zero)
        tbl_scr[2, sl, :] = lane_rows(zero, sn, zero)


def _rope(a, cs, sn_next, sn_prev):
    return (a * cs + pltpu.roll(a, V7X_LANES - ROT_HALF, 1) * sn_next
            + pltpu.roll(a, ROT_HALF, 1) * sn_prev)


def _proj0_kernel(x_ref, g_ref, pos_ref, freq_ref, w_ref, w1_ref, wo0_ref, wo1_ref,
                  o0_ref, oqm_ref, oz_ref, o1_ref, o2_ref, w1b_ref, wo0b_ref, wo1b_ref,
                  hn_scr, perm_scr, tbl_scr, p4_scr, tbl4_scr):
    tm = x_ref.shape[1]
    x = x_ref[0]
    ms = jnp.mean(x * x, axis=-1, keepdims=True)
    hn = x * lax.rsqrt(ms + EPS) * g_ref[...]
    hb = hn.astype(BF16)
    n_slab = D_MODEL // V7X_LANES
    for c in range(n_slab):
        hn_scr[c] = hn[:, V7X_LANES * c:V7X_LANES * (c + 1)]
    _rope_tables(pos_ref, freq_ref, tbl_scr)

    def qkv_chunk(lhs, g, j, tables):
        part, half = divmod(j, GROUP_WIDTH // COL_CHUNK)
        c0 = A_PART * part + GROUP_WIDTH * g + COL_CHUNK * half
        acc = _dot(lhs, w_ref[:, c0:c0 + COL_CHUNK])
        if part < 2:
            acc = jnp.concatenate(
                [_rope(acc[:, :V7X_LANES], *tables), _rope(acc[:, V7X_LANES:], *tables)], axis=1)
        return acc.astype(BF16)

    tables = tuple(tbl_scr[t] for t in range(3))
    for j in range(QKV_G // COL_CHUNK):
        res = qkv_chunk(hb, 0, j, tables)
        for e in range(COL_CHUNK // V7X_LANES):
            o0_ref[0, 2 * j + e] = res[:, V7X_LANES * e:V7X_LANES * (e + 1)]
    oqm_ref[0] = _dot(hb, w_ref[:, A_QM:A_QM + MEM_WIDTH]).astype(BF16)
    for j in range(BRANCH_A // COL_CHUNK):
        oz_ref[0, :, COL_CHUNK * j:COL_CHUNK * (j + 1)] = _dot(
            hb, w_ref[:, A_Z + COL_CHUNK * j:A_Z + COL_CHUNK * (j + 1)]).astype(BF16)

    d1 = DILATIONS[1]
    n1 = tm // d1
    for r in range(d1):
        rows = slice(r * n1, (r + 1) * n1)
        for c in range(n_slab):
            p4_scr[c, rows, :] = hn_scr[c, pl.ds(r, n1, stride=d1), :]
        for t in range(3):
            tbl4_scr[t, rows, :] = tbl_scr[t, pl.ds(r, n1, stride=d1), :]

    def rows_of_stream(g, r):
        if g == 1:
            return slice(r * n1, (r + 1) * n1)
        return pl.ds((r % d1) * n1 + r // d1, tm // DILATIONS[g], stride=DILATIONS[g] // d1)

    for g, o_ref in ((1, o1_ref), (2, o2_ref)):
        d = DILATIONS[g]
        n = tm // d
        for r in range(d):
            for c in range(n_slab):
                perm_scr[r * n:(r + 1) * n, V7X_LANES * c:V7X_LANES * (c + 1)] = (
                    p4_scr[c, rows_of_stream(g, r), :].astype(BF16))
        tables = tuple(
            jnp.concatenate([tbl4_scr[t, rows_of_stream(g, r), :] for r in range(d)], axis=0)
            for t in range(3))
        lhs = perm_scr[...]
        for j in range(QKV_G // COL_CHUNK):
            res = qkv_chunk(lhs, g, j, tables)
            for r in range(d):
                for e in range(COL_CHUNK // V7X_LANES):
                    o_ref[0, 2 * j + e, r] = res[r * n:(r + 1) * n, V7X_LANES * e:V7X_LANES * (e + 1)]

    w1b_ref[:, :B_QM] = w1_ref[:, :B_QM].astype(BF16)
    w1b_ref[:, B_QM:B_Z] = (w1_ref[:, B_QM:B_Z] * SCORE_SCALE).astype(BF16)
    w1b_ref[:, B_Z:] = w1_ref[:, B_Z:].astype(BF16)
    wo0b_ref[...] = wo0_ref[...].astype(BF16)
    wo1b_ref[...] = wo1_ref[...].astype(BF16)


def _proj0(x, norm_g, pos_rows, freq_rows, w0, w1, wo0, wo1):
    b, s, _ = x.shape
    tm = ROW_TILE
    per_seq = s // tm
    n_steps = b * per_seq
    row = lambda width: pl.BlockSpec((1, tm, width), lambda i, j: (i, j, 0))
    n_slab = QKV_G // V7X_LANES
    natural = pl.BlockSpec((1, n_slab, tm, V7X_LANES), lambda i, j: (i, 0, j, 0))
    stream = lambda d: pl.BlockSpec((1, n_slab, d, tm // d, V7X_LANES), lambda i, j: (i, 0, 0, j, 0))

    def slab(w, steps):
        rows = w.shape[0] // steps
        assert rows * steps == w.shape[0] and rows % BF16_SUBLANES == 0
        return pl.BlockSpec((rows, w.shape[1]), lambda i, j: (jnp.minimum(i * per_seq + j, steps - 1), 0))

    slabs = [slab(w1, n_steps), slab(wo0, n_steps // 2), slab(wo1, n_steps // 2)]
    d1, d2 = DILATIONS[1], DILATIONS[2]
    return pl.pallas_call(
        _proj0_kernel,
        grid=(b, per_seq),
        in_specs=[
            row(D_MODEL),
            pl.BlockSpec((1, D_MODEL), lambda i, j: (0, 0)),
            pl.BlockSpec((1, 1, tm), lambda i, j: (i, 0, j)),
            pl.BlockSpec((SMALL_ROWS, V7X_LANES), lambda i, j: (0, 0)),
            _resident((D_MODEL, IN_A_COLS), lambda i, j: (0, 0)),
        ] + slabs,
        out_specs=[natural, row(MEM_WIDTH), row(BRANCH_A), stream(d1), stream(d2)] + slabs,
        out_shape=[
            jax.ShapeDtypeStruct((b, n_slab, s, V7X_LANES), BF16),
            jax.ShapeDtypeStruct((b, s, MEM_WIDTH), BF16),
            jax.ShapeDtypeStruct((b, s, BRANCH_A), BF16),
            jax.ShapeDtypeStruct((b, n_slab, d1, s // d1, V7X_LANES), BF16),
            jax.ShapeDtypeStruct((b, n_slab, d2, s // d2, V7X_LANES), BF16),
            jax.ShapeDtypeStruct(w1.shape, BF16),
            jax.ShapeDtypeStruct(wo0.shape, BF16),
            jax.ShapeDtypeStruct(wo1.shape, BF16),
        ],
        scratch_shapes=[
            pltpu.VMEM((D_MODEL // V7X_LANES, tm, V7X_LANES), F32),
            pltpu.VMEM((tm, D_MODEL), BF16),
            pltpu.VMEM((3, tm, V7X_LANES), F32),
            pltpu.VMEM((D_MODEL // V7X_LANES, tm, V7X_LANES), F32),
            pltpu.VMEM((3, tm, V7X_LANES), F32),
        ],
        compiler_params=_params("arbitrary", "arbitrary"),
        name="proj0",
    )(x, norm_g, pos_rows, freq_rows, w0, w1, wo0, wo1)


def _attn_kernel(q0, k0, v0, q1, k1, v1, q2, k2, v2, o_ref,
                 num_scr, m_scr, l_scr, ve_scr, bias2_scr, bias1_scr):
    s_len = o_ref.shape[1]
    half0, half1 = _half_masks()
    first_half = lax.broadcasted_iota(jnp.int32, (BLOCK, V7X_LANES), 1) < HEAD_DIM

    @pl.when((pl.program_id(0) == 0) & (pl.program_id(1) == 0))
    def _():
        qi = lax.broadcasted_iota(jnp.int32, (2 * BLOCK, 2 * BLOCK), 0) & (BLOCK - 1)
        kj = lax.broadcasted_iota(jnp.int32, (2 * BLOCK, 2 * BLOCK), 1)
        valid_two = ((kj < BLOCK) & (kj >= qi)) | ((kj >= BLOCK) & (kj - BLOCK <= qi))
        bias2_scr[...] = jnp.where(valid_two, 0.0, NEG_BIG)
        valid_one = (lax.broadcasted_iota(jnp.int32, (2 * BLOCK, BLOCK), 1)
                     <= (lax.broadcasted_iota(jnp.int32, (2 * BLOCK, BLOCK), 0) & (BLOCK - 1)))
        bias1_scr[...] = jnp.where(valid_one, 0.0, NEG_BIG)

    for g, v_ref in enumerate((v0, v1, v2)):
        v = v_ref[...]
        for h, half in enumerate((half0, half1)):
            ve_scr[g, h] = v * half

    def block(q_ref, k_ref, g, q_row, k_row, n_keys, bias_ref, dst):
        q = q_ref[pl.ds(q_row, BLOCK), :]
        qs = jnp.concatenate([q * half0, q * half1], axis=0)
        k = k_ref[pl.ds(k_row, n_keys), :]
        s = _dot_nt(qs, k) + bias_ref[...]
        m = jnp.max(s, axis=-1, keepdims=True)
        p = jnp.exp2(s - m).astype(BF16)
        p_cat = jnp.concatenate([p[:BLOCK], p[BLOCK:]], axis=1)
        keys = pl.ds(k_row, n_keys)
        ve = jnp.concatenate(
            [jnp.concatenate([ve_scr[g, h, keys, :], jnp.broadcast_to(half, (n_keys, V7X_LANES))], axis=1)
             for h, half in enumerate((half0, half1))], axis=0)
        pv = _dot(p_cat, ve)
        num_scr[g, dst, :] = pv[:, :V7X_LANES]
        l_scr[g, dst, :] = pv[:, V7X_LANES:]
        m_scr[g, dst, :] = jnp.where(first_half, m[:BLOCK], m[BLOCK:])

    d1, d2 = DILATIONS[1], DILATIONS[2]
    stream_len = s_len // d1
    sub = d2 // d1

    for r in range(d2):
        block(q2, k2, 2, r * BLOCK, r * BLOCK, BLOCK, bias1_scr,
              pl.ds((r % d1) * stream_len + r // d1, BLOCK, stride=sub))

    for r in range(d1):
        base = r * stream_len
        block(q1, k1, 1, base, base, BLOCK, bias1_scr, pl.ds(base, BLOCK))
        for n in range(1, stream_len // BLOCK):
            block(q1, k1, 1, base + n * BLOCK, base + (n - 1) * BLOCK, 2 * BLOCK, bias2_scr,
                  pl.ds(base + n * BLOCK, BLOCK))

    def merge(n):
        rows_per = BLOCK // d1
        for r in range(d1):
            tok = pl.ds(n * BLOCK + r, rows_per, stride=d1)
            rows = pl.ds(r * stream_len + n * rows_per, rows_per)
            idx = (tok, rows, rows)
            ms = [m_scr[g, idx[g], :] for g in range(N_GROUPS)]
            top = jnp.maximum(ms[0], jnp.maximum(ms[1], ms[2]))
            ws = [jnp.exp2(mg - top) for mg in ms]
            nums = [ws[g] * num_scr[g, idx[g], :] for g in range(N_GROUPS)]
            dens = [ws[g] * l_scr[g, idx[g], :] for g in range(N_GROUPS)]
            num = nums[0] + nums[1] + nums[2]
            den = dens[0] + dens[1] + dens[2]
            o_ref[0, tok, :] = num / den

    block(q0, k0, 0, 0, 0, BLOCK, bias1_scr, pl.ds(0, BLOCK))
    for n in range(1, s_len // BLOCK):
        block(q0, k0, 0, n * BLOCK, (n - 1) * BLOCK, 2 * BLOCK, bias2_scr, pl.ds(n * BLOCK, BLOCK))
        merge(n - 1)
    merge(s_len // BLOCK - 1)


def _attn(qkv0, qkv1, qkv2):
    b, _, s, _ = qkv0.shape
    n_pair = GROUP_WIDTH // V7X_LANES

    def part(k):
        return pl.BlockSpec((None, None, s, V7X_LANES), lambda i, p, k=k: (i, k * n_pair + p, 0, 0))

    specs = [part(0), part(1), part(2)] * N_GROUPS
    stat = pltpu.VMEM((N_GROUPS, s, V7X_LANES), F32)
    scratch = [
        stat, stat, stat,
        pltpu.VMEM((N_GROUPS, 2, s, V7X_LANES), BF16),
        pltpu.VMEM((2 * BLOCK, 2 * BLOCK), F32),
        pltpu.VMEM((2 * BLOCK, BLOCK), F32),
    ]
    return pl.pallas_call(
        _attn_kernel,
        grid=(b, n_pair),
        in_specs=specs,
        out_specs=pl.BlockSpec((1, s, V7X_LANES), lambda i, p: (i, 0, p)),
        out_shape=jax.ShapeDtypeStruct((b, s, GROUP_WIDTH), F32),
        scratch_shapes=scratch,
        compiler_params=_params("arbitrary", "arbitrary"),
        name="dilated_attn",
    )(qkv0, qkv0, qkv0, qkv1, qkv1, qkv1, qkv2, qkv2, qkv2)


def _mem_scores(qm, kv_ref):
    rows = qm.shape[0]
    half0, half1 = _half_masks()
    scores = []
    for p in range(MEM_WIDTH // V7X_LANES):
        k = kv_ref[0, 0, :, V7X_LANES * p:V7X_LANES * (p + 1)]
        for c in range(rows // MEM_ROWS):
            q = qm[MEM_ROWS * c:MEM_ROWS * (c + 1), V7X_LANES * p:V7X_LANES * (p + 1)]
            scores.append(_dot_nt(jnp.concatenate([q * half0, q * half1], axis=0), k))
    return scores


def _mem_softmax_pv(scores, kv_ref):
    half0, half1 = _half_masks()
    n_pairs = MEM_WIDTH // V7X_LANES
    per_pair = len(scores) // n_pairs
    outs = []
    for p in range(n_pairs):
        v = kv_ref[0, 0, :, MEM_WIDTH + V7X_LANES * p:MEM_WIDTH + V7X_LANES * (p + 1)]
        ve = jnp.concatenate(
            [jnp.concatenate([v * half, jnp.broadcast_to(half, (N_MEM, V7X_LANES))], axis=1)
             for half in (half0, half1)], axis=0)
        chunks = []
        for s in scores[per_pair * p:per_pair * (p + 1)]:
            e = jnp.exp2(s - jnp.max(s, axis=-1, keepdims=True)).astype(BF16)
            pv = _dot(jnp.concatenate([e[:MEM_ROWS], e[MEM_ROWS:]], axis=1), ve)
            chunks.append(pv[:, :V7X_LANES] / pv[:, V7X_LANES:])
        outs.append(jnp.concatenate(chunks, axis=0))
    return jnp.concatenate(outs, axis=1)


def _tail_kernel(x_ref, mix_ref, qm0_ref, z0_ref, kv0_ref, wo0_ref, w1_ref,
                 kv1_ref, wo1_ref, rows_ref, out_ref,
                 a_scr, y_scr, h1_scr, hb_scr, *, tiles_per_seq):
    tm = x_ref.shape[1]
    halo = V7X_SUBLANES
    n_out = D_MODEL // COL_CHUNK
    step = pl.program_id(0)

    @pl.when(step == 0)
    def _():
        h1_scr[...] = jnp.zeros(h1_scr.shape, F32)
        a_scr[0:halo, :] = jnp.zeros((halo, CONV_WIDTH), F32)

    scores0 = _mem_scores(qm0_ref[0], kv0_ref)

    h1 = h1_scr[...]
    ms = jnp.mean(h1 * h1, axis=-1, keepdims=True)
    hb_scr[...] = (h1 * lax.rsqrt(ms + EPS) * rows_ref[ROW_G1:ROW_G1 + 1, :]).astype(BF16)
    seq_start = (jnp.maximum(step - 1, 0) % tiles_per_seq) == 0
    a_scr[0:halo, :] = jnp.where(seq_start, 0.0, a_scr[0:halo, :])

    def proj1(c0, width=COL_CHUNK):
        return _dot(hb_scr[...], w1_ref[:, c0:c0 + width])

    def conv_chunk(j):
        cs = slice(COL_CHUNK * j, COL_CHUNK * (j + 1))
        a_now = proj1(B_CG + COL_CHUNK * j) * proj1(B_U + COL_CHUNK * j)
        a_scr[halo:halo + tm, cs] = a_now
        conv = (rows_ref[ROW_CW:ROW_CW + 1, cs] * a_scr[halo - 2:halo - 2 + tm, cs]
                + rows_ref[ROW_CW + 1:ROW_CW + 2, cs] * a_scr[halo - 1:halo - 1 + tm, cs]
                + rows_ref[ROW_CW + 2:ROW_CW + 3, cs] * a_now)
        mix1 = proj1(B_BG + COL_CHUNK * j) * conv
        y_scr[:, cs] = (mix1 * _silu(proj1(B_Z + COL_CHUNK * j))).astype(BF16)

    qm1 = proj1(B_QM, MEM_WIDTH).astype(BF16)
    z_mem = proj1(B_Z + CONV_WIDTH, MEM_WIDTH)
    conv_chunk(0)
    scores1 = _mem_scores(qm1, kv1_ref)
    conv_chunk(1)
    mem1 = _mem_softmax_pv(scores1, kv1_ref)
    y_scr[:, CONV_WIDTH:BRANCH_B] = (mem1 * _silu(z_mem)).astype(BF16)
    conv_chunk(2)
    conv_chunk(3)
    a_scr[0:halo, :] = a_scr[tm:tm + halo, :]
    mem0 = _mem_softmax_pv(scores0, kv0_ref)
    z0 = z0_ref[0].astype(F32)
    y0 = jnp.concatenate(
        [mix_ref[0] * _silu(z0[:, :GROUP_WIDTH]), mem0 * _silu(z0[:, GROUP_WIDTH:])], axis=1).astype(BF16)

    y1 = y_scr[...]
    ssq = jnp.zeros((tm, 1), F32)
    for j in range(n_out):
        cs = slice(COL_CHUNK * j, COL_CHUNK * (j + 1))
        h2 = h1_scr[:, cs] + _dot(y1, wo1_ref[:, cs])
        ssq = ssq + jnp.sum(h2 * h2, axis=-1, keepdims=True)
        out_ref[0, :, cs] = h2
    scale = lax.rsqrt(ssq * (1.0 / D_MODEL) + EPS)
    out_ref[0] = out_ref[0] * scale * rows_ref[ROW_GF:ROW_GF + 1, :]

    for j in range(n_out):
        cs = slice(COL_CHUNK * j, COL_CHUNK * (j + 1))
        h1_scr[:, cs] = x_ref[0, :, cs] + _dot(y0, wo0_ref[:, cs])


def _tail(x, mix, qm0, z0, kv, wo0, w1, wo1, small_rows):
    b, s, _ = x.shape
    tm = ROW_TILE
    per_seq = s // tm
    n_tiles = b * per_seq

    def tile_a(j):
        return jnp.minimum(j, n_tiles - 1)

    def tile_b(j):
        return jnp.maximum(j - 1, 0)

    def row(width, tile):
        return pl.BlockSpec((1, tm, width), lambda j: (tile(j) // per_seq, tile(j) % per_seq, 0))

    def kv_spec(layer, tile):
        return pl.BlockSpec((1, 1, N_MEM, 2 * MEM_WIDTH), lambda j: (layer, tile(j) // per_seq, 0, 0))

    const = lambda shape: pl.BlockSpec(shape, lambda j: (0, 0))
    tile_f32 = pltpu.VMEM((tm, D_MODEL), F32)
    tile_bf16 = pltpu.VMEM((tm, D_MODEL), BF16)
    return pl.pallas_call(
        functools.partial(_tail_kernel, tiles_per_seq=per_seq),
        grid=(n_tiles + 1,),
        in_specs=[
            row(D_MODEL, tile_a), row(GROUP_WIDTH, tile_a), row(MEM_WIDTH, tile_a), row(BRANCH_A, tile_a),
            kv_spec(0, tile_a),
            _resident(wo0.shape, lambda j: (0, 0)),
            _resident(w1.shape, lambda j: (0, 0)),
            kv_spec(1, tile_b),
            _resident(wo1.shape, lambda j: (0, 0)),
            const(small_rows.shape),
        ],
        out_specs=row(D_MODEL, tile_b),
        out_shape=jax.ShapeDtypeStruct((b, s, D_MODEL), F32),
        scratch_shapes=[
            pltpu.VMEM((tm + V7X_SUBLANES, CONV_WIDTH), F32),
            pltpu.VMEM((tm, BRANCH_B), BF16),
            tile_f32, tile_bf16,
        ],
        compiler_params=_params("arbitrary"),
        name="tail",
    )(x, mix, qm0, z0, kv, wo0, w1, kv, wo1, small_rows)


def kernel(x, mem, positions, norm_g, mem_norm_g, w_mem_kv, attn_w_in, attn_w_out, conv_w_in, conv_w,
           conv_w_out, final_g):
    b, s, _ = x.shape

    inv_freq = ROPE_THETA ** (-jnp.arange(ROT_HALF, dtype=F32) * (2.0 / ROT_DIM))
    freq_rows = jnp.broadcast_to(jnp.tile(inv_freq, SMALL_ROWS // ROT_HALF)[:, None], (SMALL_ROWS, V7X_LANES))
    pos_rows = positions.reshape(b, 1, s)

    kv, w0 = _mem_kv(mem, mem_norm_g, w_mem_kv, attn_w_in[0])
    qkv0, qm0, z0, qkv1, qkv2, w1, wo0, wo1 = _proj0(
        x, norm_g[0:1], pos_rows, freq_rows, w0, conv_w_in[0], attn_w_out[0], conv_w_out[0])
    qkv1 = qkv1.reshape(qkv0.shape)
    qkv2 = qkv2.reshape(qkv0.shape)
    mix = _attn(qkv0, qkv1, qkv2)
    small_rows = jnp.concatenate([norm_g[1:2], final_g.reshape(1, D_MODEL), conv_w[0]], axis=0)
    return _tail(x, mix, qm0, z0, kv, wo0, w1, wo1, small_rows)
```

```python
import functools

import numpy as np
import jax
import jax.numpy as jnp
from jax import lax
from jax.experimental import pallas as pl
from jax.experimental.pallas import tpu as pltpu

F32 = jnp.float32
BF16 = jnp.bfloat16

D_MODEL = 1024
HEAD_DIM = 64
ROT_DIM = 16
ROT_HALF = ROT_DIM // 2
ROPE_THETA = 500000.0
DILATIONS = (1, 4, 16)
BLOCK = 128
GROUP_WIDTH = 512
N_GROUPS = 3
N_MEM = 256
MEM_WIDTH = 256
CONV_WIDTH = 1024
EPS = 1e-6
SCORE_SCALE = HEAD_DIM ** -0.5 * float(np.log2(np.e))

V7X_LANES = 128
V7X_SUBLANES = 8
BF16_SUBLANES = 16
ROW_TILE = 512
COL_CHUNK = 256
VMEM_LIMIT_BYTES = 56 * 1024 * 1024
NEG_BIG = -1e30
MEM_ROWS = 256
SMALL_ROWS = 16
ROW_G1, ROW_GF, ROW_CW = 0, 1, 2

QKV_G = 3 * GROUP_WIDTH
A_PART = N_GROUPS * GROUP_WIDTH
A_QM = 3 * A_PART
A_Z = A_QM + MEM_WIDTH
BRANCH_A = GROUP_WIDTH + MEM_WIDTH
IN_A_COLS = A_Z + BRANCH_A
B_BG, B_CG, B_U = 0, CONV_WIDTH, 2 * CONV_WIDTH
B_QM = 3 * CONV_WIDTH
B_Z = B_QM + MEM_WIDTH
BRANCH_B = CONV_WIDTH + MEM_WIDTH
IN_B_COLS = B_Z + BRANCH_B


def _column_scale(n_cols, scaled):
    scale = np.ones((n_cols,), np.float32)
    for lo, hi in scaled:
        scale[lo:hi] = SCORE_SCALE
    return scale


_A_SCALE = _column_scale(IN_A_COLS, [(0, A_PART), (A_QM, A_QM + MEM_WIDTH)])


def _params(*sem):
    return pltpu.CompilerParams(dimension_semantics=sem, vmem_limit_bytes=VMEM_LIMIT_BYTES)


def _resident(shape, index_map):
    return pl.BlockSpec(shape, index_map, pipeline_mode=pl.Buffered(1))


def _silu(z):
    return z * jax.nn.sigmoid(z)


def _dot(a, b):
    return jnp.dot(a, b, preferred_element_type=F32)


def _dot_nt(a, b):
    return lax.dot_general(a, b, (((1,), (1,)), ((), ())), preferred_element_type=F32)


def _half_masks():
    lane = lax.broadcasted_iota(jnp.int32, (1, V7X_LANES), 1)
    return (jnp.where(lane < HEAD_DIM, 1.0, 0.0).astype(BF16),
            jnp.where(lane < HEAD_DIM, 0.0, 1.0).astype(BF16))


def _memkv_kernel(mem_ref, g_ref, w_ref, wa_ref, scale_ref, kv_ref, wab_ref):
    w = w_ref[0].astype(BF16)
    for c in range(mem_ref.shape[0] // ROW_TILE):
        rows = slice(ROW_TILE * c, ROW_TILE * (c + 1))
        m = mem_ref[rows, :]
        ms = jnp.mean(m * m, axis=-1, keepdims=True)
        mn = (m * lax.rsqrt(ms + EPS) * g_ref[0]).astype(BF16)
        kv_ref[0, rows, :] = _dot(mn, w).astype(BF16)
    for c in range(wa_ref.shape[0] // V7X_LANES):
        rows = slice(V7X_LANES * c, V7X_LANES * (c + 1))
        wab_ref[rows, :] = (wa_ref[rows, :] * scale_ref[...]).astype(BF16)


def _mem_kv(mem, mem_norm_g, w_mem_kv, w_attn_in):
    b = mem.shape[0]
    depth = w_mem_kv.shape[0]
    rows = b * N_MEM
    slab = pl.BlockSpec((D_MODEL // depth, IN_A_COLS), lambda l: (l, 0))
    kv, w0 = pl.pallas_call(
        _memkv_kernel,
        grid=(depth,),
        in_specs=[
            _resident((rows, D_MODEL), lambda l: (0, 0)),
            pl.BlockSpec((1, 1, D_MODEL), lambda l: (l, 0, 0)),
            pl.BlockSpec((1, D_MODEL, 2 * MEM_WIDTH), lambda l: (l, 0, 0)),
            slab,
            pl.BlockSpec((1, IN_A_COLS), lambda l: (0, 0)),
        ],
        out_specs=[pl.BlockSpec((1, rows, 2 * MEM_WIDTH), lambda l: (l, 0, 0)), slab],
        out_shape=[jax.ShapeDtypeStruct((depth, rows, 2 * MEM_WIDTH), BF16),
                   jax.ShapeDtypeStruct((D_MODEL, IN_A_COLS), BF16)],
        compiler_params=_params("arbitrary"),
        name="mem_kv",
    )(mem.reshape(rows, D_MODEL), mem_norm_g.reshape(depth, 1, D_MODEL), w_mem_kv,
      w_attn_in, _A_SCALE.reshape(1, IN_A_COLS))
    return kv.reshape(depth, b, N_MEM, 2 * MEM_WIDTH), w0


def _rope_tables(pos_ref, freq_ref, tbl_scr):
    rows = tbl_scr.shape[1]
    freq = freq_ref[0:ROT_HALF, :]
    one = jnp.ones((V7X_SUBLANES, V7X_LANES), F32)
    zero = jnp.zeros((V7X_SUBLANES, V7X_LANES), F32)
    groups = V7X_LANES // V7X_SUBLANES
    per_head = HEAD_DIM // V7X_SUBLANES

    def lane_rows(first, second, other):
        pieces = []
        for i in range(groups):
            pieces.append(first if i % per_head == 0 else second if i % per_head == 1 else other)
        return jnp.concatenate(pieces, axis=0).T

    for c in range(rows // V7X_LANES):
        sl = slice(V7X_LANES * c, V7X_LANES * (c + 1))
        ang = pos_ref[0, :, sl].astype(F32) * freq
        cs = jnp.cos(ang)
        sn = jnp.sin(ang)
        tbl_scr[0, sl, :] = lane_rows(cs, cs, one)
        tbl_scr[1, sl, :] = lane_rows(-sn, zero, zero)
        tbl_scr[2, sl, :] = lane_rows(zero, sn, zero)


def _rope(a, cs, sn_next, sn_prev):
    return (a * cs + pltpu.roll(a, V7X_LANES - ROT_HALF, 1) * sn_next
            + pltpu.roll(a, ROT_HALF, 1) * sn_prev)


def _proj0_kernel(x_ref, g_ref, pos_ref, freq_ref, w_ref, w1_ref, wo0_ref, wo1_ref,
                  o0_ref, oqm_ref, oz_ref, o1_ref, o2_ref, w1b_ref, wo0b_ref, wo1b_ref,
                  hn_scr, perm_scr, tbl_scr, p4_scr, tbl4_scr):
    tm = x_ref.shape[1]
    x = x_ref[0]
    ms = jnp.mean(x * x, axis=-1, keepdims=True)
    hn = x * lax.rsqrt(ms + EPS) * g_ref[...]
    hb = hn.astype(BF16)
    n_slab = D_MODEL // V7X_LANES
    for c in range(n_slab):
        hn_scr[c] = hn[:, V7X_LANES * c:V7X_LANES * (c + 1)]
    _rope_tables(pos_ref, freq_ref, tbl_scr)

    def qkv_chunk(lhs, g, j, tables):
        part, half = divmod(j, GROUP_WIDTH // COL_CHUNK)
        c0 = A_PART * part + GROUP_WIDTH * g + COL_CHUNK * half
        acc = _dot(lhs, w_ref[:, c0:c0 + COL_CHUNK])
        if part < 2:
            acc = jnp.concatenate(
                [_rope(acc[:, :V7X_LANES], *tables), _rope(acc[:, V7X_LANES:], *tables)], axis=1)
        return acc.astype(BF16)

    tables = tuple(tbl_scr[t] for t in range(3))
    for j in range(QKV_G // COL_CHUNK):
        res = qkv_chunk(hb, 0, j, tables)
        for e in range(COL_CHUNK // V7X_LANES):
            o0_ref[0, 2 * j + e] = res[:, V7X_LANES * e:V7X_LANES * (e + 1)]
    oqm_ref[0] = _dot(hb, w_ref[:, A_QM:A_QM + MEM_WIDTH]).astype(BF16)
    for j in range(BRANCH_A // COL_CHUNK):
        oz_ref[0, :, COL_CHUNK * j:COL_CHUNK * (j + 1)] = _dot(
            hb, w_ref[:, A_Z + COL_CHUNK * j:A_Z + COL_CHUNK * (j + 1)])

    d1 = DILATIONS[1]
    n1 = tm // d1
    for r in range(d1):
        rows = slice(r * n1, (r + 1) * n1)
        for c in range(n_slab):
            p4_scr[c, rows, :] = hn_scr[c, pl.ds(r, n1, stride=d1), :]
        for t in range(3):
            tbl4_scr[t, rows, :] = tbl_scr[t, pl.ds(r, n1, stride=d1), :]

    def rows_of_stream(g, r):
        if g == 1:
            return slice(r * n1, (r + 1) * n1)
        return pl.ds((r % d1) * n1 + r // d1, tm // DILATIONS[g], stride=DILATIONS[g] // d1)

    for g, o_ref in ((1, o1_ref), (2, o2_ref)):
        d = DILATIONS[g]
        n = tm // d
        for r in range(d):
            for c in range(n_slab):
                perm_scr[r * n:(r + 1) * n, V7X_LANES * c:V7X_LANES * (c + 1)] = (
                    p4_scr[c, rows_of_stream(g, r), :].astype(BF16))
        tables = tuple(
            jnp.concatenate([tbl4_scr[t, rows_of_stream(g, r), :] for r in range(d)], axis=0)
            for t in range(3))
        lhs = perm_scr[...]
        for j in range(QKV_G // COL_CHUNK):
            res = qkv_chunk(lhs, g, j, tables)
            for r in range(d):
                for e in range(COL_CHUNK // V7X_LANES):
                    o_ref[0, 2 * j + e, r] = res[r * n:(r + 1) * n, V7X_LANES * e:V7X_LANES * (e + 1)]

    w1b_ref[:, :B_QM] = w1_ref[:, :B_QM].astype(BF16)
    w1b_ref[:, B_QM:B_Z] = (w1_ref[:, B_QM:B_Z] * SCORE_SCALE).astype(BF16)
    w1b_ref[:, B_Z:] = w1_ref[:, B_Z:].astype(BF16)
    wo0b_ref[...] = wo0_ref[...].astype(BF16)
    wo1b_ref[...] = wo1_ref[...].astype(BF16)


def _proj0(x, norm_g, pos_rows, freq_rows, w0, w1, wo0, wo1):
    b, s, _ = x.shape
    tm = ROW_TILE
    per_seq = s // tm
    n_steps = b * per_seq
    row = lambda width: pl.BlockSpec((1, tm, width), lambda i, j: (i, j, 0))
    n_slab = QKV_G // V7X_LANES
    natural = pl.BlockSpec((1, n_slab, tm, V7X_LANES), lambda i, j: (i, 0, j, 0))
    stream = lambda d: pl.BlockSpec((1, n_slab, d, tm // d, V7X_LANES), lambda i, j: (i, 0, 0, j, 0))

    def slab(w, steps):
        rows = w.shape[0] // steps
        assert rows * steps == w.shape[0] and rows % BF16_SUBLANES == 0
        return pl.BlockSpec((rows, w.shape[1]), lambda i, j: (jnp.minimum(i * per_seq + j, steps - 1), 0))

    slabs = [slab(w1, n_steps), slab(wo0, n_steps // 2), slab(wo1, n_steps // 2)]
    d1, d2 = DILATIONS[1], DILATIONS[2]
    return pl.pallas_call(
        _proj0_kernel,
        grid=(b, per_seq),
        in_specs=[
            row(D_MODEL),
            pl.BlockSpec((1, D_MODEL), lambda i, j: (0, 0)),
            pl.BlockSpec((1, 1, tm), lambda i, j: (i, 0, j)),
            pl.BlockSpec((SMALL_ROWS, V7X_LANES), lambda i, j: (0, 0)),
            _resident((D_MODEL, IN_A_COLS), lambda i, j: (0, 0)),
        ] + slabs,
        out_specs=[natural, row(MEM_WIDTH), row(BRANCH_A), stream(d1), stream(d2)] + slabs,
        out_shape=[
            jax.ShapeDtypeStruct((b, n_slab, s, V7X_LANES), BF16),
            jax.ShapeDtypeStruct((b, s, MEM_WIDTH), BF16),
            jax.ShapeDtypeStruct((b, s, BRANCH_A), F32),
            jax.ShapeDtypeStruct((b, n_slab, d1, s // d1, V7X_LANES), BF16),
            jax.ShapeDtypeStruct((b, n_slab, d2, s // d2, V7X_LANES), BF16),
            jax.ShapeDtypeStruct(w1.shape, BF16),
            jax.ShapeDtypeStruct(wo0.shape, BF16),
            jax.ShapeDtypeStruct(wo1.shape, BF16),
        ],
        scratch_shapes=[
            pltpu.VMEM((D_MODEL // V7X_LANES, tm, V7X_LANES), F32),
            pltpu.VMEM((tm, D_MODEL), BF16),
            pltpu.VMEM((3, tm, V7X_LANES), F32),
            pltpu.VMEM((D_MODEL // V7X_LANES, tm, V7X_LANES), F32),
            pltpu.VMEM((3, tm, V7X_LANES), F32),
        ],
        compiler_params=_params("arbitrary", "arbitrary"),
        name="proj0",
    )(x, norm_g, pos_rows, freq_rows, w0, w1, wo0, wo1)


def _attn_kernel(q0, k0, v0, q1, k1, v1, q2, k2, v2, o_ref,
                 num_scr, m_scr, l_scr, ve_scr, kt_scr, bias2_scr, bias1_scr):
    s_len = o_ref.shape[1]
    half0, half1 = _half_masks()
    first_half = lax.broadcasted_iota(jnp.int32, (BLOCK, V7X_LANES), 1) < HEAD_DIM

    @pl.when((pl.program_id(0) == 0) & (pl.program_id(1) == 0))
    def _():
        qi = lax.broadcasted_iota(jnp.int32, (2 * BLOCK, 2 * BLOCK), 0) & (BLOCK - 1)
        kj = lax.broadcasted_iota(jnp.int32, (2 * BLOCK, 2 * BLOCK), 1)
        valid_two = ((kj < BLOCK) & (kj >= qi)) | ((kj >= BLOCK) & (kj - BLOCK <= qi))
        bias2_scr[...] = jnp.where(valid_two, 0.0, NEG_BIG)
        valid_one = (lax.broadcasted_iota(jnp.int32, (2 * BLOCK, BLOCK), 1)
                     <= (lax.broadcasted_iota(jnp.int32, (2 * BLOCK, BLOCK), 0) & (BLOCK - 1)))
        bias1_scr[...] = jnp.where(valid_one, 0.0, NEG_BIG)

    for g, (k_ref, v_ref) in enumerate(((k0, v0), (k1, v1), (k2, v2))):
        v = v_ref[...]
        for h, half in enumerate((half0, half1)):
            ve_scr[g, h] = v * half
        kt_scr[g] = k_ref[...].T

    def block(q_ref, k_ref, g, q_row, k_row, n_keys, bias_ref, dst):
        q = q_ref[pl.ds(q_row, BLOCK), :]
        qs = jnp.concatenate([q * half0, q * half1], axis=0)
        s = _dot(qs, kt_scr[g, :, pl.ds(k_row, n_keys)]) + bias_ref[...]
        m = jnp.max(s, axis=-1, keepdims=True)
        p = jnp.exp2(s - m).astype(BF16)
        p_cat = jnp.concatenate([p[:BLOCK], p[BLOCK:]], axis=1)
        keys = pl.ds(k_row, n_keys)
        ve = jnp.concatenate(
            [jnp.concatenate([ve_scr[g, h, keys, :], jnp.broadcast_to(half, (n_keys, V7X_LANES))], axis=1)
             for h, half in enumerate((half0, half1))], axis=0)
        pv = _dot(p_cat, ve)
        num_scr[g, dst, :] = pv[:, :V7X_LANES]
        l_scr[g, dst, :] = pv[:, V7X_LANES:]
        m_scr[g, dst, :] = jnp.where(first_half, m[:BLOCK], m[BLOCK:])

    d1, d2 = DILATIONS[1], DILATIONS[2]
    stream_len = s_len // d1
    sub = d2 // d1

    for r in range(d2):
        block(q2, k2, 2, r * BLOCK, r * BLOCK, BLOCK, bias1_scr,
              pl.ds((r % d1) * stream_len + r // d1, BLOCK, stride=sub))

    for r in range(d1):
        base = r * stream_len
        block(q1, k1, 1, base, base, BLOCK, bias1_scr, pl.ds(base, BLOCK))
        for n in range(1, stream_len // BLOCK):
            block(q1, k1, 1, base + n * BLOCK, base + (n - 1) * BLOCK, 2 * BLOCK, bias2_scr,
                  pl.ds(base + n * BLOCK, BLOCK))

    def merge(n):
        rows_per = BLOCK // d1
        for r in range(d1):
            tok = pl.ds(n * BLOCK + r, rows_per, stride=d1)
            rows = pl.ds(r * stream_len + n * rows_per, rows_per)
            idx = (tok, rows, rows)
            ms = [m_scr[g, idx[g], :] for g in range(N_GROUPS)]
            top = jnp.maximum(ms[0], jnp.maximum(ms[1], ms[2]))
            ws = [jnp.exp2(mg - top) for mg in ms]
            nums = [ws[g] * num_scr[g, idx[g], :] for g in range(N_GROUPS)]
            dens = [ws[g] * l_scr[g, idx[g], :] for g in range(N_GROUPS)]
            num = nums[0] + nums[1] + nums[2]
            den = dens[0] + dens[1] + dens[2]
            o_ref[0, tok, :] = num / den

    block(q0, k0, 0, 0, 0, BLOCK, bias1_scr, pl.ds(0, BLOCK))
    for n in range(1, s_len // BLOCK):
        block(q0, k0, 0, n * BLOCK, (n - 1) * BLOCK, 2 * BLOCK, bias2_scr, pl.ds(n * BLOCK, BLOCK))
        merge(n - 1)
    merge(s_len // BLOCK - 1)


def _attn(qkv0, qkv1, qkv2):
    b, _, s, _ = qkv0.shape
    n_pair = GROUP_WIDTH // V7X_LANES

    def part(k):
        return pl.BlockSpec((None, None, s, V7X_LANES), lambda i, p, k=k: (i, k * n_pair + p, 0, 0))

    specs = [part(0), part(1), part(2)] * N_GROUPS
    stat = pltpu.VMEM((N_GROUPS, s, V7X_LANES), F32)
    scratch = [
        stat, stat, stat,
        pltpu.VMEM((N_GROUPS, 2, s, V7X_LANES), BF16),
        pltpu.VMEM((N_GROUPS, V7X_LANES, s), BF16),
        pltpu.VMEM((2 * BLOCK, 2 * BLOCK), F32),
        pltpu.VMEM((2 * BLOCK, BLOCK), F32),
    ]
    return pl.pallas_call(
        _attn_kernel,
        grid=(b, n_pair),
        in_specs=specs,
        out_specs=pl.BlockSpec((1, s, V7X_LANES), lambda i, p: (i, 0, p)),
        out_shape=jax.ShapeDtypeStruct((b, s, GROUP_WIDTH), F32),
        scratch_shapes=scratch,
        compiler_params=_params("arbitrary", "arbitrary"),
        name="dilated_attn",
    )(qkv0, qkv0, qkv0, qkv1, qkv1, qkv1, qkv2, qkv2, qkv2)


def _mem_scores(qm, kv_ref):
    rows = qm.shape[0]
    half0, half1 = _half_masks()
    scores = []
    for p in range(MEM_WIDTH // V7X_LANES):
        k = kv_ref[0, 0, :, V7X_LANES * p:V7X_LANES * (p + 1)]
        for c in range(rows // MEM_ROWS):
            q = qm[MEM_ROWS * c:MEM_ROWS * (c + 1), V7X_LANES * p:V7X_LANES * (p + 1)]
            scores.append(_dot_nt(jnp.concatenate([q * half0, q * half1], axis=0), k))
    return scores


def _mem_softmax_pv(scores, kv_ref):
    half0, half1 = _half_masks()
    n_pairs = MEM_WIDTH // V7X_LANES
    per_pair = len(scores) // n_pairs
    outs = []
    for p in range(n_pairs):
        v = kv_ref[0, 0, :, MEM_WIDTH + V7X_LANES * p:MEM_WIDTH + V7X_LANES * (p + 1)]
        ve = jnp.concatenate(
            [jnp.concatenate([v * half, jnp.broadcast_to(half, (N_MEM, V7X_LANES))], axis=1)
             for half in (half0, half1)], axis=0)
        chunks = []
        for s in scores[per_pair * p:per_pair * (p + 1)]:
            e = jnp.exp2(s - jnp.max(s, axis=-1, keepdims=True)).astype(BF16)
            pv = _dot(jnp.concatenate([e[:MEM_ROWS], e[MEM_ROWS:]], axis=1), ve)
            chunks.append(pv[:, :V7X_LANES] / pv[:, V7X_LANES:])
        outs.append(jnp.concatenate(chunks, axis=0))
    return jnp.concatenate(outs, axis=1)


def _tail_kernel(x_ref, mix_ref, qm0_ref, z0_ref, kv0_ref, wo0_ref, w1_ref,
                 kv1_ref, wo1_ref, rows_ref, out_ref,
                 a_scr, y_scr, h1_scr, hb_scr, *, tiles_per_seq):
    tm = x_ref.shape[1]
    halo = V7X_SUBLANES
    n_out = D_MODEL // COL_CHUNK
    step = pl.program_id(0)

    @pl.when(step == 0)
    def _():
        h1_scr[...] = jnp.zeros(h1_scr.shape, F32)
        a_scr[0:halo, :] = jnp.zeros((halo, CONV_WIDTH), F32)

    scores0 = _mem_scores(qm0_ref[0], kv0_ref)

    h1 = h1_scr[...]
    ms = jnp.mean(h1 * h1, axis=-1, keepdims=True)
    hb_scr[...] = (h1 * lax.rsqrt(ms + EPS) * rows_ref[ROW_G1:ROW_G1 + 1, :]).astype(BF16)
    seq_start = (jnp.maximum(step - 1, 0) % tiles_per_seq) == 0
    a_scr[0:halo, :] = jnp.where(seq_start, 0.0, a_scr[0:halo, :])

    def proj1(c0, width=COL_CHUNK):
        return _dot(hb_scr[...], w1_ref[:, c0:c0 + width])

    def conv_chunk(j):
        cs = slice(COL_CHUNK * j, COL_CHUNK * (j + 1))
        a_now = proj1(B_CG + COL_CHUNK * j) * proj1(B_U + COL_CHUNK * j)
        a_scr[halo:halo + tm, cs] = a_now
        conv = (rows_ref[ROW_CW:ROW_CW + 1, cs] * a_scr[halo - 2:halo - 2 + tm, cs]
                + rows_ref[ROW_CW + 1:ROW_CW + 2, cs] * a_scr[halo - 1:halo - 1 + tm, cs]
                + rows_ref[ROW_CW + 2:ROW_CW + 3, cs] * a_now)
        mix1 = proj1(B_BG + COL_CHUNK * j) * conv
        y_scr[:, cs] = (mix1 * _silu(proj1(B_Z + COL_CHUNK * j))).astype(BF16)

    qm1 = proj1(B_QM, MEM_WIDTH).astype(BF16)
    z_mem = proj1(B_Z + CONV_WIDTH, MEM_WIDTH)
    conv_chunk(0)
    scores1 = _mem_scores(qm1, kv1_ref)
    conv_chunk(1)
    mem1 = _mem_softmax_pv(scores1, kv1_ref)
    y_scr[:, CONV_WIDTH:BRANCH_B] = (mem1 * _silu(z_mem)).astype(BF16)
    conv_chunk(2)
    conv_chunk(3)
    a_scr[0:halo, :] = a_scr[tm:tm + halo, :]
    mem0 = _mem_softmax_pv(scores0, kv0_ref)
    z0 = z0_ref[0]
    y0 = jnp.concatenate(
        [mix_ref[0] * _silu(z0[:, :GROUP_WIDTH]), mem0 * _silu(z0[:, GROUP_WIDTH:])], axis=1).astype(BF16)

    y1 = y_scr[...]
    ssq = jnp.zeros((tm, 1), F32)
    for j in range(n_out):
        cs = slice(COL_CHUNK * j, COL_CHUNK * (j + 1))
        h2 = h1_scr[:, cs] + _dot(y1, wo1_ref[:, cs])
        ssq = ssq + jnp.sum(h2 * h2, axis=-1, keepdims=True)
        out_ref[0, :, cs] = h2
    scale = lax.rsqrt(ssq * (1.0 / D_MODEL) + EPS)
    out_ref[0] = out_ref[0] * scale * rows_ref[ROW_GF:ROW_GF + 1, :]

    for j in range(n_out):
        cs = slice(COL_CHUNK * j, COL_CHUNK * (j + 1))
        h1_scr[:, cs] = x_ref[0, :, cs] + _dot(y0, wo0_ref[:, cs])


def _tail(x, mix, qm0, z0, kv, wo0, w1, wo1, small_rows):
    b, s, _ = x.shape
    tm = ROW_TILE
    per_seq = s // tm
    n_tiles = b * per_seq

    def tile_a(j):
        return jnp.minimum(j, n_tiles - 1)

    def tile_b(j):
        return jnp.maximum(j - 1, 0)

    def row(width, tile):
        return pl.BlockSpec((1, tm, width), lambda j: (tile(j) // per_seq, tile(j) % per_seq, 0))

    def kv_spec(layer, tile):
        return pl.BlockSpec((1, 1, N_MEM, 2 * MEM_WIDTH), lambda j: (layer, tile(j) // per_seq, 0, 0))

    const = lambda shape: pl.BlockSpec(shape, lambda j: (0, 0))
    tile_f32 = pltpu.VMEM((tm, D_MODEL), F32)
    tile_bf16 = pltpu.VMEM((tm, D_MODEL), BF16)
    return pl.pallas_call(
        functools.partial(_tail_kernel, tiles_per_seq=per_seq),
        grid=(n_tiles + 1,),
        in_specs=[
            row(D_MODEL, tile_a), row(GROUP_WIDTH, tile_a), row(MEM_WIDTH, tile_a), row(BRANCH_A, tile_a),
            kv_spec(0, tile_a),
            _resident(wo0.shape, lambda j: (0, 0)),
            _resident(w1.shape, lambda j: (0, 0)),
            kv_spec(1, tile_b),
            _resident(wo1.shape, lambda j: (0, 0)),
            const(small_rows.shape),
        ],
        out_specs=row(D_MODEL, tile_b),
        out_shape=jax.ShapeDtypeStruct((b, s, D_MODEL), F32),
        scratch_shapes=[
            pltpu.VMEM((tm + V7X_SUBLANES, CONV_WIDTH), F32),
            pltpu.VMEM((tm, BRANCH_B), BF16),
            tile_f32, tile_bf16,
        ],
        compiler_params=_params("arbitrary"),
        name="tail",
    )(x, mix, qm0, z0, kv, wo0, w1, kv, wo1, small_rows)


def kernel(x, mem, positions, norm_g, mem_norm_g, w_mem_kv, attn_w_in, attn_w_out, conv_w_in, conv_w,
           conv_w_out, final_g):
    b, s, _ = x.shape

    inv_freq = ROPE_THETA ** (-jnp.arange(ROT_HALF, dtype=F32) * (2.0 / ROT_DIM))
    freq_rows = jnp.broadcast_to(jnp.tile(inv_freq, SMALL_ROWS // ROT_HALF)[:, None], (SMALL_ROWS, V7X_LANES))
    pos_rows = positions.reshape(b, 1, s)

    kv, w0 = _mem_kv(mem, mem_norm_g, w_mem_kv, attn_w_in[0])
    qkv0, qm0, z0, qkv1, qkv2, w1, wo0, wo1 = _proj0(
        x, norm_g[0:1], pos_rows, freq_rows, w0, conv_w_in[0], attn_w_out[0], conv_w_out[0])
    qkv1 = qkv1.reshape(qkv0.shape)
    qkv2 = qkv2.reshape(qkv0.shape)
    mix = _attn(qkv0, qkv1, qkv2)
    small_rows = jnp.concatenate([norm_g[1:2], final_g.reshape(1, D_MODEL), conv_w[0]], axis=0)
    return _tail(x, mix, qm0, z0, kv, wo0, w1, wo1, small_rows)
```

```python
import functools

import numpy as np
import jax
import jax.numpy as jnp
from jax import lax
from jax.experimental import pallas as pl
from jax.experimental.pallas import tpu as pltpu

F32 = jnp.float32
BF16 = jnp.bfloat16

D_MODEL = 1024
HEAD_DIM = 64
ROT_DIM = 16
ROT_HALF = ROT_DIM // 2
ROPE_THETA = 500000.0
DILATIONS = (1, 4, 16)
BLOCK = 128
GROUP_WIDTH = 512
N_GROUPS = 3
N_MEM = 256
MEM_WIDTH = 256
CONV_WIDTH = 1024
EPS = 1e-6
SCORE_SCALE = HEAD_DIM ** -0.5 * float(np.log2(np.e))

V7X_LANES = 128
V7X_SUBLANES = 8
BF16_SUBLANES = 16
ROW_TILE = 512
COL_CHUNK = 256
VMEM_LIMIT_BYTES = 56 * 1024 * 1024
NEG_BIG = -1e30
MEM_ROWS = 256
SMALL_ROWS = 16
ROW_G1, ROW_GF, ROW_CW = 0, 1, 2

QKV_G = 3 * GROUP_WIDTH
A_PART = N_GROUPS * GROUP_WIDTH
A_QM = 3 * A_PART
A_Z = A_QM + MEM_WIDTH
BRANCH_A = GROUP_WIDTH + MEM_WIDTH
IN_A_COLS = A_Z + BRANCH_A
B_BG, B_CG, B_U = 0, CONV_WIDTH, 2 * CONV_WIDTH
B_QM = 3 * CONV_WIDTH
B_Z = B_QM + MEM_WIDTH
BRANCH_B = CONV_WIDTH + MEM_WIDTH
IN_B_COLS = B_Z + BRANCH_B


def _column_scale(n_cols, scaled):
    scale = np.ones((n_cols,), np.float32)
    for lo, hi in scaled:
        scale[lo:hi] = SCORE_SCALE
    return scale


_A_SCALE = _column_scale(IN_A_COLS, [(0, A_PART), (A_QM, A_QM + MEM_WIDTH)])


def _params(*sem):
    return pltpu.CompilerParams(dimension_semantics=sem, vmem_limit_bytes=VMEM_LIMIT_BYTES)


def _resident(shape, index_map):
    return pl.BlockSpec(shape, index_map, pipeline_mode=pl.Buffered(1))


def _silu(z):
    return z * jax.nn.sigmoid(z)


def _dot(a, b):
    return jnp.dot(a, b, preferred_element_type=F32)


def _dot_nt(a, b):
    return lax.dot_general(a, b, (((1,), (1,)), ((), ())), preferred_element_type=F32)


def _half_masks():
    lane = lax.broadcasted_iota(jnp.int32, (1, V7X_LANES), 1)
    return (jnp.where(lane < HEAD_DIM, 1.0, 0.0).astype(BF16),
            jnp.where(lane < HEAD_DIM, 0.0, 1.0).astype(BF16))


def _memkv_kernel(mem_ref, g_ref, w_ref, wa_ref, scale_ref, kv_ref, wab_ref):
    w = w_ref[0].astype(BF16)
    gain = jnp.where(pl.program_id(0) == 0, g_ref[0:1, :], g_ref[1:2, :])
    for c in range(mem_ref.shape[0] // ROW_TILE):
        rows = slice(ROW_TILE * c, ROW_TILE * (c + 1))
        m = mem_ref[rows, :]
        ms = jnp.mean(m * m, axis=-1, keepdims=True)
        mn = (m * lax.rsqrt(ms + EPS) * gain).astype(BF16)
        kv_ref[0, rows, :] = _dot(mn, w).astype(BF16)
    for c in range(wa_ref.shape[0] // V7X_LANES):
        rows = slice(V7X_LANES * c, V7X_LANES * (c + 1))
        wab_ref[rows, :] = (wa_ref[rows, :] * scale_ref[...]).astype(BF16)


def _mem_kv(mem, mem_norm_g, w_mem_kv, w_attn_in):
    b = mem.shape[0]
    depth = w_mem_kv.shape[0]
    rows = b * N_MEM
    slab = pl.BlockSpec((D_MODEL // depth, IN_A_COLS), lambda l: (l, 0))
    kv, w0 = pl.pallas_call(
        _memkv_kernel,
        grid=(depth,),
        in_specs=[
            _resident((rows, D_MODEL), lambda l: (0, 0)),
            pl.BlockSpec((depth, D_MODEL), lambda l: (0, 0)),
            pl.BlockSpec((1, D_MODEL, 2 * MEM_WIDTH), lambda l: (l, 0, 0)),
            slab,
            pl.BlockSpec((1, IN_A_COLS), lambda l: (0, 0)),
        ],
        out_specs=[pl.BlockSpec((1, rows, 2 * MEM_WIDTH), lambda l: (l, 0, 0)), slab],
        out_shape=[jax.ShapeDtypeStruct((depth, rows, 2 * MEM_WIDTH), BF16),
                   jax.ShapeDtypeStruct((D_MODEL, IN_A_COLS), BF16)],
        compiler_params=_params("arbitrary"),
        name="mem_kv",
    )(mem.reshape(rows, D_MODEL), mem_norm_g, w_mem_kv,
      w_attn_in, _A_SCALE.reshape(1, IN_A_COLS))
    return kv.reshape(depth, b, N_MEM, 2 * MEM_WIDTH), w0


def _rope_tables(pos_ref, freq_ref, tbl_scr):
    rows = tbl_scr.shape[1]
    freq = freq_ref[0:ROT_HALF, :]
    one = jnp.ones((V7X_SUBLANES, V7X_LANES), F32)
    zero = jnp.zeros((V7X_SUBLANES, V7X_LANES), F32)
    groups = V7X_LANES // V7X_SUBLANES
    per_head = HEAD_DIM // V7X_SUBLANES
    own_row = lax.broadcasted_iota(jnp.int32, (pos_ref.shape[0], V7X_LANES), 0) == pl.program_id(0)

    def lane_rows(first, second, other):
        pieces = []
        for i in range(groups):
            pieces.append(first if i % per_head == 0 else second if i % per_head == 1 else other)
        return jnp.concatenate(pieces, axis=0).T

    for c in range(rows // V7X_LANES):
        sl = slice(V7X_LANES * c, V7X_LANES * (c + 1))
        pos = jnp.sum(jnp.where(own_row, pos_ref[:, sl].astype(F32), 0.0), axis=0, keepdims=True)
        ang = pos * freq
        cs = jnp.cos(ang)
        sn = jnp.sin(ang)
        tbl_scr[0, sl, :] = lane_rows(cs, cs, one)
        tbl_scr[1, sl, :] = lane_rows(-sn, zero, zero)
        tbl_scr[2, sl, :] = lane_rows(zero, sn, zero)


def _rope(a, cs, sn_next, sn_prev):
    return (a * cs + pltpu.roll(a, V7X_LANES - ROT_HALF, 1) * sn_next
            + pltpu.roll(a, ROT_HALF, 1) * sn_prev)


def _proj0_kernel(x_ref, g_ref, pos_ref, freq_ref, w_ref, w1_ref, wo0_ref, wo1_ref,
                  o0_ref, oqm_ref, oz_ref, o1_ref, o2_ref, w1b_ref, wo0b_ref, wo1b_ref,
                  hn_scr, perm_scr, tbl_scr, p4_scr, tbl4_scr):
    tm = x_ref.shape[1]
    x = x_ref[0]
    ms = jnp.mean(x * x, axis=-1, keepdims=True)
    hn = x * lax.rsqrt(ms + EPS) * g_ref[0:1, :]
    hb = hn.astype(BF16)
    n_slab = D_MODEL // V7X_LANES
    for c in range(n_slab):
        hn_scr[c] = hn[:, V7X_LANES * c:V7X_LANES * (c + 1)]
    _rope_tables(pos_ref, freq_ref, tbl_scr)

    def qkv_chunk(lhs, g, j, tables):
        part, half = divmod(j, GROUP_WIDTH // COL_CHUNK)
        c0 = A_PART * part + GROUP_WIDTH * g + COL_CHUNK * half
        acc = _dot(lhs, w_ref[:, c0:c0 + COL_CHUNK])
        if part < 2:
            acc = jnp.concatenate(
                [_rope(acc[:, :V7X_LANES], *tables), _rope(acc[:, V7X_LANES:], *tables)], axis=1)
        return acc.astype(BF16)

    tables = tuple(tbl_scr[t] for t in range(3))
    for j in range(QKV_G // COL_CHUNK):
        res = qkv_chunk(hb, 0, j, tables)
        for e in range(COL_CHUNK // V7X_LANES):
            o0_ref[0, 2 * j + e] = res[:, V7X_LANES * e:V7X_LANES * (e + 1)]
    oqm_ref[0] = _dot(hb, w_ref[:, A_QM:A_QM + MEM_WIDTH]).astype(BF16)
    for j in range(BRANCH_A // COL_CHUNK):
        oz_ref[0, :, COL_CHUNK * j:COL_CHUNK * (j + 1)] = _dot(
            hb, w_ref[:, A_Z + COL_CHUNK * j:A_Z + COL_CHUNK * (j + 1)])

    d1 = DILATIONS[1]
    n1 = tm // d1
    for r in range(d1):
        rows = slice(r * n1, (r + 1) * n1)
        for c in range(n_slab):
            p4_scr[c, rows, :] = hn_scr[c, pl.ds(r, n1, stride=d1), :]
        for t in range(3):
            tbl4_scr[t, rows, :] = tbl_scr[t, pl.ds(r, n1, stride=d1), :]

    def rows_of_stream(g, r):
        if g == 1:
            return slice(r * n1, (r + 1) * n1)
        return pl.ds((r % d1) * n1 + r // d1, tm // DILATIONS[g], stride=DILATIONS[g] // d1)

    for g, o_ref in ((1, o1_ref), (2, o2_ref)):
        d = DILATIONS[g]
        n = tm // d
        for r in range(d):
            for c in range(n_slab):
                perm_scr[r * n:(r + 1) * n, V7X_LANES * c:V7X_LANES * (c + 1)] = (
                    p4_scr[c, rows_of_stream(g, r), :].astype(BF16))
        tables = tuple(
            jnp.concatenate([tbl4_scr[t, rows_of_stream(g, r), :] for r in range(d)], axis=0)
            for t in range(3))
        lhs = perm_scr[...]
        for j in range(QKV_G // COL_CHUNK):
            res = qkv_chunk(lhs, g, j, tables)
            for r in range(d):
                for e in range(COL_CHUNK // V7X_LANES):
                    o_ref[0, 2 * j + e, r] = res[r * n:(r + 1) * n, V7X_LANES * e:V7X_LANES * (e + 1)]

    w1b_ref[:, :B_QM] = w1_ref[:, :B_QM].astype(BF16)
    w1b_ref[:, B_QM:B_Z] = (w1_ref[:, B_QM:B_Z] * SCORE_SCALE).astype(BF16)
    w1b_ref[:, B_Z:] = w1_ref[:, B_Z:].astype(BF16)
    wo0b_ref[...] = wo0_ref[...].astype(BF16)
    wo1b_ref[...] = wo1_ref[...].astype(BF16)


def _proj0(x, norm_g, pos_rows, freq_rows, w0, w1, wo0, wo1):
    b, s, _ = x.shape
    tm = ROW_TILE
    per_seq = s // tm
    n_steps = b * per_seq
    row = lambda width: pl.BlockSpec((1, tm, width), lambda i, j: (i, j, 0))
    n_slab = QKV_G // V7X_LANES
    natural = pl.BlockSpec((1, n_slab, tm, V7X_LANES), lambda i, j: (i, 0, j, 0))
    stream = lambda d: pl.BlockSpec((1, n_slab, d, tm // d, V7X_LANES), lambda i, j: (i, 0, 0, j, 0))

    def slab(w, steps):
        rows = w.shape[0] // steps
        assert rows * steps == w.shape[0] and rows % BF16_SUBLANES == 0
        return pl.BlockSpec((rows, w.shape[1]), lambda i, j: (jnp.minimum(i * per_seq + j, steps - 1), 0))

    slabs = [slab(w1, n_steps), slab(wo0, n_steps // 2), slab(wo1, n_steps // 2)]
    d1, d2 = DILATIONS[1], DILATIONS[2]
    return pl.pallas_call(
        _proj0_kernel,
        grid=(b, per_seq),
        in_specs=[
            row(D_MODEL),
            pl.BlockSpec(norm_g.shape, lambda i, j: (0, 0)),
            pl.BlockSpec((b, tm), lambda i, j: (0, j)),
            pl.BlockSpec((SMALL_ROWS, V7X_LANES), lambda i, j: (0, 0)),
            _resident((D_MODEL, IN_A_COLS), lambda i, j: (0, 0)),
        ] + slabs,
        out_specs=[natural, row(MEM_WIDTH), row(BRANCH_A), stream(d1), stream(d2)] + slabs,
        out_shape=[
            jax.ShapeDtypeStruct((b, n_slab, s, V7X_LANES), BF16),
            jax.ShapeDtypeStruct((b, s, MEM_WIDTH), BF16),
            jax.ShapeDtypeStruct((b, s, BRANCH_A), F32),
            jax.ShapeDtypeStruct((b, n_slab, d1, s // d1, V7X_LANES), BF16),
            jax.ShapeDtypeStruct((b, n_slab, d2, s // d2, V7X_LANES), BF16),
            jax.ShapeDtypeStruct(w1.shape, BF16),
            jax.ShapeDtypeStruct(wo0.shape, BF16),
            jax.ShapeDtypeStruct(wo1.shape, BF16),
        ],
        scratch_shapes=[
            pltpu.VMEM((D_MODEL // V7X_LANES, tm, V7X_LANES), F32),
            pltpu.VMEM((tm, D_MODEL), BF16),
            pltpu.VMEM((3, tm, V7X_LANES), F32),
            pltpu.VMEM((D_MODEL // V7X_LANES, tm, V7X_LANES), F32),
            pltpu.VMEM((3, tm, V7X_LANES), F32),
        ],
        compiler_params=_params("arbitrary", "arbitrary"),
        name="proj0",
    )(x, norm_g, pos_rows, freq_rows, w0, w1, wo0, wo1)


def _attn_kernel(q0, k0, v0, q1, k1, v1, q2, k2, v2, o_ref,
                 num_scr, m_scr, l_scr, ve_scr, bias2_scr, bias1_scr):
    s_len = o_ref.shape[1]
    half0, half1 = _half_masks()
    first_half = lax.broadcasted_iota(jnp.int32, (BLOCK, V7X_LANES), 1) < HEAD_DIM

    @pl.when((pl.program_id(0) == 0) & (pl.program_id(1) == 0))
    def _():
        qi = lax.broadcasted_iota(jnp.int32, (2 * BLOCK, 2 * BLOCK), 0) & (BLOCK - 1)
        kj = lax.broadcasted_iota(jnp.int32, (2 * BLOCK, 2 * BLOCK), 1)
        valid_two = ((kj < BLOCK) & (kj >= qi)) | ((kj >= BLOCK) & (kj - BLOCK <= qi))
        bias2_scr[...] = jnp.where(valid_two, 0.0, NEG_BIG)
        valid_one = (lax.broadcasted_iota(jnp.int32, (2 * BLOCK, BLOCK), 1)
                     <= (lax.broadcasted_iota(jnp.int32, (2 * BLOCK, BLOCK), 0) & (BLOCK - 1)))
        bias1_scr[...] = jnp.where(valid_one, 0.0, NEG_BIG)

    for g, v_ref in enumerate((v0, v1, v2)):
        v = v_ref[...]
        for h, half in enumerate((half0, half1)):
            ve_scr[g, h] = v * half

    def block(q_ref, k_ref, g, q_row, k_row, n_keys, bias_ref, dst):
        q = q_ref[pl.ds(q_row, BLOCK), :]
        qs = jnp.concatenate([q * half0, q * half1], axis=0)
        k = k_ref[pl.ds(k_row, n_keys), :]
        s = _dot_nt(qs, k) + bias_ref[...]
        m = jnp.max(s, axis=-1, keepdims=True)
        p = jnp.exp2(s - m).astype(BF16)
        p_cat = jnp.concatenate([p[:BLOCK], p[BLOCK:]], axis=1)
        keys = pl.ds(k_row, n_keys)
        ve = jnp.concatenate(
            [jnp.concatenate([ve_scr[g, h, keys, :], jnp.broadcast_to(half, (n_keys, V7X_LANES))], axis=1)
             for h, half in enumerate((half0, half1))], axis=0)
        pv = _dot(p_cat, ve)
        num_scr[g, dst, :] = pv[:, :V7X_LANES]
        l_scr[g, dst, :] = pv[:, V7X_LANES:]
        m_scr[g, dst, :] = jnp.where(first_half, m[:BLOCK], m[BLOCK:])

    d1, d2 = DILATIONS[1], DILATIONS[2]
    stream_len = s_len // d1
    sub = d2 // d1

    for r in range(d2):
        block(q2, k2, 2, r * BLOCK, r * BLOCK, BLOCK, bias1_scr,
              pl.ds((r % d1) * stream_len + r // d1, BLOCK, stride=sub))

    for r in range(d1):
        base = r * stream_len
        block(q1, k1, 1, base, base, BLOCK, bias1_scr, pl.ds(base, BLOCK))
        for n in range(1, stream_len // BLOCK):
            block(q1, k1, 1, base + n * BLOCK, base + (n - 1) * BLOCK, 2 * BLOCK, bias2_scr,
                  pl.ds(base + n * BLOCK, BLOCK))

    def merge(n):
        rows_per = BLOCK // d1
        for r in range(d1):
            tok = pl.ds(n * BLOCK + r, rows_per, stride=d1)
            rows = pl.ds(r * stream_len + n * rows_per, rows_per)
            idx = (tok, rows, rows)
            ms = [m_scr[g, idx[g], :] for g in range(N_GROUPS)]
            top = jnp.maximum(ms[0], jnp.maximum(ms[1], ms[2]))
            ws = [jnp.exp2(mg - top) for mg in ms]
            nums = [ws[g] * num_scr[g, idx[g], :] for g in range(N_GROUPS)]
            dens = [ws[g] * l_scr[g, idx[g], :] for g in range(N_GROUPS)]
            num = nums[0] + nums[1] + nums[2]
            den = dens[0] + dens[1] + dens[2]
            o_ref[0, tok, :] = num / den

    block(q0, k0, 0, 0, 0, BLOCK, bias1_scr, pl.ds(0, BLOCK))
    for n in range(1, s_len // BLOCK):
        block(q0, k0, 0, n * BLOCK, (n - 1) * BLOCK, 2 * BLOCK, bias2_scr, pl.ds(n * BLOCK, BLOCK))
        merge(n - 1)
    merge(s_len // BLOCK - 1)


def _attn(qkv0, qkv1, qkv2):
    b, _, s, _ = qkv0.shape
    n_pair = GROUP_WIDTH // V7X_LANES

    def part(k):
        return pl.BlockSpec((None, None, s, V7X_LANES), lambda i, p, k=k: (i, k * n_pair + p, 0, 0))

    specs = [part(0), part(1), part(2)] * N_GROUPS
    stat = pltpu.VMEM((N_GROUPS, s, V7X_LANES), F32)
    scratch = [
        stat, stat, stat,
        pltpu.VMEM((N_GROUPS, 2, s, V7X_LANES), BF16),
        pltpu.VMEM((2 * BLOCK, 2 * BLOCK), F32),
        pltpu.VMEM((2 * BLOCK, BLOCK), F32),
    ]
    return pl.pallas_call(
        _attn_kernel,
        grid=(b, n_pair),
        in_specs=specs,
        out_specs=pl.BlockSpec((1, s, V7X_LANES), lambda i, p: (i, 0, p)),
        out_shape=jax.ShapeDtypeStruct((b, s, GROUP_WIDTH), F32),
        scratch_shapes=scratch,
        compiler_params=_params("arbitrary", "arbitrary"),
        name="dilated_attn",
    )(qkv0, qkv0, qkv0, qkv1, qkv1, qkv1, qkv2, qkv2, qkv2)


def _mem_scores(qm, kv_ref):
    rows = qm.shape[0]
    half0, half1 = _half_masks()
    scores = []
    for p in range(MEM_WIDTH // V7X_LANES):
        k = kv_ref[0, 0, :, V7X_LANES * p:V7X_LANES * (p + 1)]
        for c in range(rows // MEM_ROWS):
            q = qm[MEM_ROWS * c:MEM_ROWS * (c + 1), V7X_LANES * p:V7X_LANES * (p + 1)]
            scores.append(_dot_nt(jnp.concatenate([q * half0, q * half1], axis=0), k))
    return scores


def _mem_softmax_pv(scores, kv_ref):
    half0, half1 = _half_masks()
    n_pairs = MEM_WIDTH // V7X_LANES
    per_pair = len(scores) // n_pairs
    outs = []
    for p in range(n_pairs):
        v = kv_ref[0, 0, :, MEM_WIDTH + V7X_LANES * p:MEM_WIDTH + V7X_LANES * (p + 1)]
        ve = jnp.concatenate(
            [jnp.concatenate([v * half, jnp.broadcast_to(half, (N_MEM, V7X_LANES))], axis=1)
             for half in (half0, half1)], axis=0)
        chunks = []
        for s in scores[per_pair * p:per_pair * (p + 1)]:
            e = jnp.exp2(s - jnp.max(s, axis=-1, keepdims=True)).astype(BF16)
            pv = _dot(jnp.concatenate([e[:MEM_ROWS], e[MEM_ROWS:]], axis=1), ve)
            chunks.append(pv[:, :V7X_LANES] / pv[:, V7X_LANES:])
        outs.append(jnp.concatenate(chunks, axis=0))
    return jnp.concatenate(outs, axis=1)


def _tail_kernel(x_ref, mix_ref, qm0_ref, z0_ref, kv0_ref, wo0_ref, w1_ref,
                 kv1_ref, wo1_ref, rows_ref, out_ref,
                 a_scr, y_scr, h1_scr, hb_scr, *, tiles_per_seq):
    tm = x_ref.shape[1]
    halo = V7X_SUBLANES
    n_out = D_MODEL // COL_CHUNK
    step = pl.program_id(0)

    @pl.when(step == 0)
    def _():
        h1_scr[...] = jnp.zeros(h1_scr.shape, F32)
        a_scr[0:halo, :] = jnp.zeros((halo, CONV_WIDTH), F32)

    scores0 = _mem_scores(qm0_ref[0], kv0_ref)

    h1 = h1_scr[...]
    ms = jnp.mean(h1 * h1, axis=-1, keepdims=True)
    hb_scr[...] = (h1 * lax.rsqrt(ms + EPS) * rows_ref[ROW_G1:ROW_G1 + 1, :]).astype(BF16)
    seq_start = (jnp.maximum(step - 1, 0) % tiles_per_seq) == 0
    a_scr[0:halo, :] = jnp.where(seq_start, 0.0, a_scr[0:halo, :])

    def proj1(c0, width=COL_CHUNK):
        return _dot(hb_scr[...], w1_ref[:, c0:c0 + width])

    def conv_chunk(j):
        cs = slice(COL_CHUNK * j, COL_CHUNK * (j + 1))
        a_now = proj1(B_CG + COL_CHUNK * j) * proj1(B_U + COL_CHUNK * j)
        a_scr[halo:halo + tm, cs] = a_now
        conv = (rows_ref[ROW_CW:ROW_CW + 1, cs] * a_scr[halo - 2:halo - 2 + tm, cs]
                + rows_ref[ROW_CW + 1:ROW_CW + 2, cs] * a_scr[halo - 1:halo - 1 + tm, cs]
                + rows_ref[ROW_CW + 2:ROW_CW + 3, cs] * a_now)
        mix1 = proj1(B_BG + COL_CHUNK * j) * conv
        y_scr[:, cs] = (mix1 * _silu(proj1(B_Z + COL_CHUNK * j))).astype(BF16)

    qm1 = proj1(B_QM, MEM_WIDTH).astype(BF16)
    z_mem = proj1(B_Z + CONV_WIDTH, MEM_WIDTH)
    conv_chunk(0)
    scores1 = _mem_scores(qm1, kv1_ref)
    conv_chunk(1)
    mem1 = _mem_softmax_pv(scores1, kv1_ref)
    y_scr[:, CONV_WIDTH:BRANCH_B] = (mem1 * _silu(z_mem)).astype(BF16)
    conv_chunk(2)
    conv_chunk(3)
    a_scr[0:halo, :] = a_scr[tm:tm + halo, :]
    mem0 = _mem_softmax_pv(scores0, kv0_ref)
    z0 = z0_ref[0]
    y0 = jnp.concatenate(
        [mix_ref[0] * _silu(z0[:, :GROUP_WIDTH]), mem0 * _silu(z0[:, GROUP_WIDTH:])], axis=1).astype(BF16)

    y1 = y_scr[...]
    ssq = jnp.zeros((tm, 1), F32)
    for j in range(n_out):
        cs = slice(COL_CHUNK * j, COL_CHUNK * (j + 1))
        h2 = h1_scr[:, cs] + _dot(y1, wo1_ref[:, cs])
        ssq = ssq + jnp.sum(h2 * h2, axis=-1, keepdims=True)
        out_ref[0, :, cs] = h2
    scale = lax.rsqrt(ssq * (1.0 / D_MODEL) + EPS)
    out_ref[0] = out_ref[0] * scale * rows_ref[ROW_GF:ROW_GF + 1, :]

    for j in range(n_out):
        cs = slice(COL_CHUNK * j, COL_CHUNK * (j + 1))
        h1_scr[:, cs] = x_ref[0, :, cs] + _dot(y0, wo0_ref[:, cs])


def _tail(x, mix, qm0, z0, kv, wo0, w1, wo1, small_rows):
    b, s, _ = x.shape
    tm = ROW_TILE
    per_seq = s // tm
    n_tiles = b * per_seq

    def tile_a(j):
        return jnp.minimum(j, n_tiles - 1)

    def tile_b(j):
        return jnp.maximum(j - 1, 0)

    def row(width, tile):
        return pl.BlockSpec((1, tm, width), lambda j: (tile(j) // per_seq, tile(j) % per_seq, 0))

    def kv_spec(layer, tile):
        return pl.BlockSpec((1, 1, N_MEM, 2 * MEM_WIDTH), lambda j: (layer, tile(j) // per_seq, 0, 0))

    const = lambda shape: pl.BlockSpec(shape, lambda j: (0, 0))
    tile_f32 = pltpu.VMEM((tm, D_MODEL), F32)
    tile_bf16 = pltpu.VMEM((tm, D_MODEL), BF16)
    return pl.pallas_call(
        functools.partial(_tail_kernel, tiles_per_seq=per_seq),
        grid=(n_tiles + 1,),
        in_specs=[
            row(D_MODEL, tile_a), row(GROUP_WIDTH, tile_a), row(MEM_WIDTH, tile_a), row(BRANCH_A, tile_a),
            kv_spec(0, tile_a),
            _resident(wo0.shape, lambda j: (0, 0)),
            _resident(w1.shape, lambda j: (0, 0)),
            kv_spec(1, tile_b),
            _resident(wo1.shape, lambda j: (0, 0)),
            const(small_rows.shape),
        ],
        out_specs=row(D_MODEL, tile_b),
        out_shape=jax.ShapeDtypeStruct((b, s, D_MODEL), F32),
        scratch_shapes=[
            pltpu.VMEM((tm + V7X_SUBLANES, CONV_WIDTH), F32),
            pltpu.VMEM((tm, BRANCH_B), BF16),
            tile_f32, tile_bf16,
        ],
        compiler_params=_params("arbitrary"),
        name="tail",
    )(x, mix, qm0, z0, kv, wo0, w1, kv, wo1, small_rows)


def kernel(x, mem, positions, norm_g, mem_norm_g, w_mem_kv, attn_w_in, attn_w_out, conv_w_in, conv_w,
           conv_w_out, final_g):
    b, s, _ = x.shape

    inv_freq = ROPE_THETA ** (-jnp.arange(ROT_HALF, dtype=F32) * (2.0 / ROT_DIM))
    freq_rows = jnp.broadcast_to(jnp.tile(inv_freq, SMALL_ROWS // ROT_HALF)[:, None], (SMALL_ROWS, V7X_LANES))

    kv, w0 = _mem_kv(mem, mem_norm_g, w_mem_kv, attn_w_in[0])
    qkv0, qm0, z0, qkv1, qkv2, w1, wo0, wo1 = _proj0(
        x, norm_g, positions, freq_rows, w0, conv_w_in[0], attn_w_out[0], conv_w_out[0])
    qkv1 = qkv1.reshape(qkv0.shape)
    qkv2 = qkv2.reshape(qkv0.shape)
    mix = _attn(qkv0, qkv1, qkv2)
    small_rows = jnp.concatenate([norm_g[1:2], final_g.reshape(1, D_MODEL), conv_w[0]], axis=0)
    return _tail(x, mix, qm0, z0, kv, wo0, w1, wo1, small_rows)
```

```python
import functools

import numpy as np
import jax
import jax.numpy as jnp
from jax import lax
from jax.experimental import pallas as pl
from jax.experimental.pallas import tpu as pltpu

F32 = jnp.float32
BF16 = jnp.bfloat16

D_MODEL = 1024
HEAD_DIM = 64
ROT_DIM = 16
ROT_HALF = ROT_DIM // 2
ROPE_THETA = 500000.0
DILATIONS = (1, 4, 16)
BLOCK = 128
GROUP_WIDTH = 512
N_GROUPS = 3
N_MEM = 256
MEM_WIDTH = 256
CONV_WIDTH = 1024
EPS = 1e-6
SCORE_SCALE = HEAD_DIM ** -0.5 * float(np.log2(np.e))

V7X_LANES = 128
V7X_SUBLANES = 8
BF16_SUBLANES = 16
ROW_TILE = 512
COL_CHUNK = 256
VMEM_LIMIT_BYTES = 56 * 1024 * 1024
NEG_BIG = -1e30
MEM_ROWS = 256
SMALL_ROWS = 16
ROW_G1, ROW_GF, ROW_CW = 0, 1, 2

QKV_G = 3 * GROUP_WIDTH
A_PART = N_GROUPS * GROUP_WIDTH
A_QM = 3 * A_PART
A_Z = A_QM + MEM_WIDTH
BRANCH_A = GROUP_WIDTH + MEM_WIDTH
IN_A_COLS = A_Z + BRANCH_A
B_BG, B_CG, B_U = 0, CONV_WIDTH, 2 * CONV_WIDTH
B_QM = 3 * CONV_WIDTH
B_Z = B_QM + MEM_WIDTH
BRANCH_B = CONV_WIDTH + MEM_WIDTH
IN_B_COLS = B_Z + BRANCH_B


def _column_scale(n_cols, scaled):
    scale = np.ones((n_cols,), np.float32)
    for lo, hi in scaled:
        scale[lo:hi] = SCORE_SCALE
    return scale


_A_SCALE = _column_scale(IN_A_COLS, [(0, A_PART), (A_QM, A_QM + MEM_WIDTH)])


def _params(*sem):
    return pltpu.CompilerParams(dimension_semantics=sem, vmem_limit_bytes=VMEM_LIMIT_BYTES)


def _resident(shape, index_map):
    return pl.BlockSpec(shape, index_map, pipeline_mode=pl.Buffered(1))


def _silu(z):
    return z * jax.nn.sigmoid(z)


def _dot(a, b):
    return jnp.dot(a, b, preferred_element_type=F32)


def _dot_nt(a, b):
    return lax.dot_general(a, b, (((1,), (1,)), ((), ())), preferred_element_type=F32)


def _half_masks():
    lane = lax.broadcasted_iota(jnp.int32, (1, V7X_LANES), 1)
    return (jnp.where(lane < HEAD_DIM, 1.0, 0.0).astype(BF16),
            jnp.where(lane < HEAD_DIM, 0.0, 1.0).astype(BF16))


def _memkv_kernel(mem_ref, g_ref, w_ref, wa_ref, scale_ref, kv_ref, wab_ref):
    w = w_ref[0].astype(BF16)
    gain = jnp.where(pl.program_id(0) == 0, g_ref[0:1, :], g_ref[1:2, :])
    for c in range(mem_ref.shape[0] // ROW_TILE):
        rows = slice(ROW_TILE * c, ROW_TILE * (c + 1))
        m = mem_ref[rows, :]
        ms = jnp.mean(m * m, axis=-1, keepdims=True)
        mn = (m * lax.rsqrt(ms + EPS) * gain).astype(BF16)
        kv_ref[0, rows, :] = _dot(mn, w).astype(BF16)
    for c in range(wa_ref.shape[0] // V7X_LANES):
        rows = slice(V7X_LANES * c, V7X_LANES * (c + 1))
        wab_ref[rows, :] = (wa_ref[rows, :] * scale_ref[...]).astype(BF16)


def _mem_kv(mem, mem_norm_g, w_mem_kv, w_attn_in):
    b = mem.shape[0]
    depth = w_mem_kv.shape[0]
    assert depth == 2, "the kernel picks the layer's gain row with a two-way select"
    rows = b * N_MEM
    slab = pl.BlockSpec((D_MODEL // depth, IN_A_COLS), lambda l: (l, 0))
    kv, w0 = pl.pallas_call(
        _memkv_kernel,
        grid=(depth,),
        in_specs=[
            _resident((rows, D_MODEL), lambda l: (0, 0)),
            pl.BlockSpec((depth, D_MODEL), lambda l: (0, 0)),
            pl.BlockSpec((1, D_MODEL, 2 * MEM_WIDTH), lambda l: (l, 0, 0)),
            slab,
            pl.BlockSpec((1, IN_A_COLS), lambda l: (0, 0)),
        ],
        out_specs=[pl.BlockSpec((1, rows, 2 * MEM_WIDTH), lambda l: (l, 0, 0)), slab],
        out_shape=[jax.ShapeDtypeStruct((depth, rows, 2 * MEM_WIDTH), BF16),
                   jax.ShapeDtypeStruct((D_MODEL, IN_A_COLS), BF16)],
        compiler_params=_params("arbitrary"),
        name="mem_kv",
    )(mem.reshape(rows, D_MODEL), mem_norm_g, w_mem_kv,
      w_attn_in, _A_SCALE.reshape(1, IN_A_COLS))
    return kv.reshape(depth, b, N_MEM, 2 * MEM_WIDTH), w0


def _rope_tables(pos_ref, freq_ref, tbl_scr):
    rows = tbl_scr.shape[1]
    freq = freq_ref[0:ROT_HALF, :]
    one = jnp.ones((V7X_SUBLANES, V7X_LANES), F32)
    zero = jnp.zeros((V7X_SUBLANES, V7X_LANES), F32)
    groups = V7X_LANES // V7X_SUBLANES
    per_head = HEAD_DIM // V7X_SUBLANES
    own_row = lax.broadcasted_iota(jnp.int32, (pos_ref.shape[0], V7X_LANES), 0) == pl.program_id(0)

    def lane_rows(first, second, other):
        pieces = []
        for i in range(groups):
            pieces.append(first if i % per_head == 0 else second if i % per_head == 1 else other)
        return jnp.concatenate(pieces, axis=0).T

    for c in range(rows // V7X_LANES):
        sl = slice(V7X_LANES * c, V7X_LANES * (c + 1))
        pos = jnp.sum(jnp.where(own_row, pos_ref[:, sl].astype(F32), 0.0), axis=0, keepdims=True)
        ang = pos * freq
        cs = jnp.cos(ang)
        sn = jnp.sin(ang)
        tbl_scr[0, sl, :] = lane_rows(cs, cs, one)
        tbl_scr[1, sl, :] = lane_rows(-sn, zero, zero)
        tbl_scr[2, sl, :] = lane_rows(zero, sn, zero)


def _rope(a, cs, sn_next, sn_prev):
    return (a * cs + pltpu.roll(a, V7X_LANES - ROT_HALF, 1) * sn_next
            + pltpu.roll(a, ROT_HALF, 1) * sn_prev)


def _proj0_kernel(x_ref, g_ref, pos_ref, freq_ref, w_ref, w1_ref, wo0_ref, wo1_ref,
                  o0_ref, oqm_ref, oz_ref, o1_ref, o2_ref, w1b_ref, wo0b_ref, wo1b_ref,
                  hn_scr, perm_scr, tbl_scr, p4_scr, tbl4_scr):
    tm = x_ref.shape[1]
    x = x_ref[0]
    ms = jnp.mean(x * x, axis=-1, keepdims=True)
    hn = x * lax.rsqrt(ms + EPS) * g_ref[0:1, :]
    hb = hn.astype(BF16)
    n_slab = D_MODEL // V7X_LANES
    for c in range(n_slab):
        hn_scr[c] = hn[:, V7X_LANES * c:V7X_LANES * (c + 1)]
    _rope_tables(pos_ref, freq_ref, tbl_scr)

    def qkv_chunk(lhs, g, j, tables):
        part, half = divmod(j, GROUP_WIDTH // COL_CHUNK)
        c0 = A_PART * part + GROUP_WIDTH * g + COL_CHUNK * half
        acc = _dot(lhs, w_ref[:, c0:c0 + COL_CHUNK])
        if part < 2:
            acc = jnp.concatenate(
                [_rope(acc[:, :V7X_LANES], *tables), _rope(acc[:, V7X_LANES:], *tables)], axis=1)
        return acc.astype(BF16)

    tables = tuple(tbl_scr[t] for t in range(3))
    for j in range(QKV_G // COL_CHUNK):
        res = qkv_chunk(hb, 0, j, tables)
        for e in range(COL_CHUNK // V7X_LANES):
            o0_ref[0, 2 * j + e] = res[:, V7X_LANES * e:V7X_LANES * (e + 1)]
    oqm_ref[0] = _dot(hb, w_ref[:, A_QM:A_QM + MEM_WIDTH]).astype(BF16)
    for j in range(BRANCH_A // COL_CHUNK):
        oz_ref[0, :, COL_CHUNK * j:COL_CHUNK * (j + 1)] = _dot(
            hb, w_ref[:, A_Z + COL_CHUNK * j:A_Z + COL_CHUNK * (j + 1)])

    d1 = DILATIONS[1]
    n1 = tm // d1
    for r in range(d1):
        rows = slice(r * n1, (r + 1) * n1)
        for c in range(n_slab):
            p4_scr[c, rows, :] = hn_scr[c, pl.ds(r, n1, stride=d1), :]
        for t in range(3):
            tbl4_scr[t, rows, :] = tbl_scr[t, pl.ds(r, n1, stride=d1), :]

    def rows_of_stream(g, r):
        if g == 1:
            return slice(r * n1, (r + 1) * n1)
        return pl.ds((r % d1) * n1 + r // d1, tm // DILATIONS[g], stride=DILATIONS[g] // d1)

    for g, o_ref in ((1, o1_ref), (2, o2_ref)):
        d = DILATIONS[g]
        n = tm // d
        for r in range(d):
            for c in range(n_slab):
                perm_scr[r * n:(r + 1) * n, V7X_LANES * c:V7X_LANES * (c + 1)] = (
                    p4_scr[c, rows_of_stream(g, r), :].astype(BF16))
        tables = tuple(
            jnp.concatenate([tbl4_scr[t, rows_of_stream(g, r), :] for r in range(d)], axis=0)
            for t in range(3))
        lhs = perm_scr[...]
        for j in range(QKV_G // COL_CHUNK):
            res = qkv_chunk(lhs, g, j, tables)
            for r in range(d):
                for e in range(COL_CHUNK // V7X_LANES):
                    o_ref[0, 2 * j + e, r] = res[r * n:(r + 1) * n, V7X_LANES * e:V7X_LANES * (e + 1)]

    w1b_ref[:, :B_QM] = w1_ref[:, :B_QM].astype(BF16)
    w1b_ref[:, B_QM:B_Z] = (w1_ref[:, B_QM:B_Z] * SCORE_SCALE).astype(BF16)
    w1b_ref[:, B_Z:] = w1_ref[:, B_Z:].astype(BF16)
    wo0b_ref[...] = wo0_ref[...].astype(BF16)
    wo1b_ref[...] = wo1_ref[...].astype(BF16)


def _proj0(x, norm_g, pos_rows, freq_rows, w0, w1, wo0, wo1):
    b, s, _ = x.shape
    tm = ROW_TILE
    per_seq = s // tm
    n_steps = b * per_seq
    row = lambda width: pl.BlockSpec((1, tm, width), lambda i, j: (i, j, 0))
    n_slab = QKV_G // V7X_LANES
    natural = pl.BlockSpec((1, n_slab, tm, V7X_LANES), lambda i, j: (i, 0, j, 0))
    stream = lambda d: pl.BlockSpec((1, n_slab, d, tm // d, V7X_LANES), lambda i, j: (i, 0, 0, j, 0))

    def slab(w, steps):
        rows = w.shape[0] // steps
        assert rows * steps == w.shape[0] and rows % BF16_SUBLANES == 0
        return pl.BlockSpec((rows, w.shape[1]), lambda i, j: (jnp.minimum(i * per_seq + j, steps - 1), 0))

    slabs = [slab(w1, n_steps), slab(wo0, n_steps // 2), slab(wo1, n_steps // 2)]
    d1, d2 = DILATIONS[1], DILATIONS[2]
    return pl.pallas_call(
        _proj0_kernel,
        grid=(b, per_seq),
        in_specs=[
            row(D_MODEL),
            pl.BlockSpec(norm_g.shape, lambda i, j: (0, 0)),
            pl.BlockSpec((b, tm), lambda i, j: (0, j)),
            pl.BlockSpec((SMALL_ROWS, V7X_LANES), lambda i, j: (0, 0)),
            _resident((D_MODEL, IN_A_COLS), lambda i, j: (0, 0)),
        ] + slabs,
        out_specs=[natural, row(MEM_WIDTH), row(BRANCH_A), stream(d1), stream(d2)] + slabs,
        out_shape=[
            jax.ShapeDtypeStruct((b, n_slab, s, V7X_LANES), BF16),
            jax.ShapeDtypeStruct((b, s, MEM_WIDTH), BF16),
            jax.ShapeDtypeStruct((b, s, BRANCH_A), F32),
            jax.ShapeDtypeStruct((b, n_slab, d1, s // d1, V7X_LANES), BF16),
            jax.ShapeDtypeStruct((b, n_slab, d2, s // d2, V7X_LANES), BF16),
            jax.ShapeDtypeStruct(w1.shape, BF16),
            jax.ShapeDtypeStruct(wo0.shape, BF16),
            jax.ShapeDtypeStruct(wo1.shape, BF16),
        ],
        scratch_shapes=[
            pltpu.VMEM((D_MODEL // V7X_LANES, tm, V7X_LANES), F32),
            pltpu.VMEM((tm, D_MODEL), BF16),
            pltpu.VMEM((3, tm, V7X_LANES), F32),
            pltpu.VMEM((D_MODEL // V7X_LANES, tm, V7X_LANES), F32),
            pltpu.VMEM((3, tm, V7X_LANES), F32),
        ],
        compiler_params=_params("arbitrary", "arbitrary"),
        name="proj0",
    )(x, norm_g, pos_rows, freq_rows, w0, w1, wo0, wo1)


def _attn_kernel(q0, k0, v0, q1, k1, v1, q2, k2, v2, o_ref,
                 num_scr, m_scr, l_scr, ve_scr, bias2_scr, bias1_scr):
    s_len = o_ref.shape[1]
    half0, half1 = _half_masks()
    first_half = lax.broadcasted_iota(jnp.int32, (BLOCK, V7X_LANES), 1) < HEAD_DIM

    @pl.when((pl.program_id(0) == 0) & (pl.program_id(1) == 0))
    def _():
        qi = lax.broadcasted_iota(jnp.int32, (2 * BLOCK, 2 * BLOCK), 0) & (BLOCK - 1)
        kj = lax.broadcasted_iota(jnp.int32, (2 * BLOCK, 2 * BLOCK), 1)
        valid_two = ((kj < BLOCK) & (kj >= qi)) | ((kj >= BLOCK) & (kj - BLOCK <= qi))
        bias2_scr[...] = jnp.where(valid_two, 0.0, NEG_BIG)
        valid_one = (lax.broadcasted_iota(jnp.int32, (2 * BLOCK, BLOCK), 1)
                     <= (lax.broadcasted_iota(jnp.int32, (2 * BLOCK, BLOCK), 0) & (BLOCK - 1)))
        bias1_scr[...] = jnp.where(valid_one, 0.0, NEG_BIG)

    for g, v_ref in enumerate((v0, v1, v2)):
        v = v_ref[...]
        for h, half in enumerate((half0, half1)):
            ve_scr[g, h] = v * half

    def block(q_ref, k_ref, g, q_row, k_row, n_keys, bias_ref, dst):
        q = q_ref[pl.ds(q_row, BLOCK), :]
        qs = jnp.concatenate([q * half0, q * half1], axis=0)
        k = k_ref[pl.ds(k_row, n_keys), :]
        s = _dot_nt(qs, k) + bias_ref[...]
        m = jnp.max(s, axis=-1, keepdims=True)
        p = jnp.exp2(s - m).astype(BF16)
        p_cat = jnp.concatenate([p[:BLOCK], p[BLOCK:]], axis=1)
        keys = pl.ds(k_row, n_keys)
        ve = jnp.concatenate(
            [jnp.concatenate([ve_scr[g, h, keys, :], jnp.broadcast_to(half, (n_keys, V7X_LANES))], axis=1)
             for h, half in enumerate((half0, half1))], axis=0)
        pv = _dot(p_cat, ve)
        num_scr[g, dst, :] = pv[:, :V7X_LANES]
        l_scr[g, dst, :] = pv[:, V7X_LANES:]
        m_scr[g, dst, :] = jnp.where(first_half, m[:BLOCK], m[BLOCK:])

    d1, d2 = DILATIONS[1], DILATIONS[2]
    stream_len = s_len // d1
    sub = d2 // d1

    for r in range(d2):
        block(q2, k2, 2, r * BLOCK, r * BLOCK, BLOCK, bias1_scr,
              pl.ds((r % d1) * stream_len + r // d1, BLOCK, stride=sub))

    for r in range(d1):
        base = r * stream_len
        block(q1, k1, 1, base, base, BLOCK, bias1_scr, pl.ds(base, BLOCK))
        for n in range(1, stream_len // BLOCK):
            block(q1, k1, 1, base + n * BLOCK, base + (n - 1) * BLOCK, 2 * BLOCK, bias2_scr,
                  pl.ds(base + n * BLOCK, BLOCK))

    def merge(n):
        rows_per = BLOCK // d1
        for r in range(d1):
            tok = pl.ds(n * BLOCK + r, rows_per, stride=d1)
            rows = pl.ds(r * stream_len + n * rows_per, rows_per)
            idx = (tok, rows, rows)
            ms = [m_scr[g, idx[g], :] for g in range(N_GROUPS)]
            top = jnp.maximum(ms[0], jnp.maximum(ms[1], ms[2]))
            ws = [jnp.exp2(mg - top) for mg in ms]
            nums = [ws[g] * num_scr[g, idx[g], :] for g in range(N_GROUPS)]
            dens = [ws[g] * l_scr[g, idx[g], :] for g in range(N_GROUPS)]
            num = nums[0] + nums[1] + nums[2]
            den = dens[0] + dens[1] + dens[2]
            o_ref[0, tok, :] = num / den

    block(q0, k0, 0, 0, 0, BLOCK, bias1_scr, pl.ds(0, BLOCK))
    for n in range(1, s_len // BLOCK):
        block(q0, k0, 0, n * BLOCK, (n - 1) * BLOCK, 2 * BLOCK, bias2_scr, pl.ds(n * BLOCK, BLOCK))
        merge(n - 1)
    merge(s_len // BLOCK - 1)


def _attn(qkv0, qkv1, qkv2):
    b, _, s, _ = qkv0.shape
    n_pair = GROUP_WIDTH // V7X_LANES

    def part(k):
        return pl.BlockSpec((None, None, s, V7X_LANES), lambda i, p, k=k: (i, k * n_pair + p, 0, 0))

    specs = [part(0), part(1), part(2)] * N_GROUPS
    stat = pltpu.VMEM((N_GROUPS, s, V7X_LANES), F32)
    scratch = [
        stat, stat, stat,
        pltpu.VMEM((N_GROUPS, 2, s, V7X_LANES), BF16),
        pltpu.VMEM((2 * BLOCK, 2 * BLOCK), F32),
        pltpu.VMEM((2 * BLOCK, BLOCK), F32),
    ]
    return pl.pallas_call(
        _attn_kernel,
        grid=(b, n_pair),
        in_specs=specs,
        out_specs=pl.BlockSpec((1, s, V7X_LANES), lambda i, p: (i, 0, p)),
        out_shape=jax.ShapeDtypeStruct((b, s, GROUP_WIDTH), F32),
        scratch_shapes=scratch,
        compiler_params=_params("arbitrary", "arbitrary"),
        name="dilated_attn",
    )(qkv0, qkv0, qkv0, qkv1, qkv1, qkv1, qkv2, qkv2, qkv2)


def _mem_scores(qm, kv_ref):
    rows = qm.shape[0]
    half0, half1 = _half_masks()
    scores = []
    for p in range(MEM_WIDTH // V7X_LANES):
        k = kv_ref[0, 0, :, V7X_LANES * p:V7X_LANES * (p + 1)]
        for c in range(rows // MEM_ROWS):
            q = qm[MEM_ROWS * c:MEM_ROWS * (c + 1), V7X_LANES * p:V7X_LANES * (p + 1)]
            scores.append(_dot_nt(jnp.concatenate([q * half0, q * half1], axis=0), k))
    return scores


def _mem_softmax_pv(scores, kv_ref):
    half0, half1 = _half_masks()
    n_pairs = MEM_WIDTH // V7X_LANES
    per_pair = len(scores) // n_pairs
    outs = []
    for p in range(n_pairs):
        v = kv_ref[0, 0, :, MEM_WIDTH + V7X_LANES * p:MEM_WIDTH + V7X_LANES * (p + 1)]
        ve = jnp.concatenate(
            [jnp.concatenate([v * half, jnp.broadcast_to(half, (N_MEM, V7X_LANES))], axis=1)
             for half in (half0, half1)], axis=0)
        chunks = []
        for s in scores[per_pair * p:per_pair * (p + 1)]:
            e = jnp.exp2(s - jnp.max(s, axis=-1, keepdims=True)).astype(BF16)
            pv = _dot(jnp.concatenate([e[:MEM_ROWS], e[MEM_ROWS:]], axis=1), ve)
            chunks.append(pv[:, :V7X_LANES] / pv[:, V7X_LANES:])
        outs.append(jnp.concatenate(chunks, axis=0))
    return jnp.concatenate(outs, axis=1)


def _tail_kernel(x_ref, mix_ref, qm0_ref, z0_ref, kv0_ref, wo0_ref, w1_ref,
                 kv1_ref, wo1_ref, rows_ref, out_ref,
                 a_scr, y_scr, h1_scr, hb_scr, *, tiles_per_seq):
    tm = x_ref.shape[1]
    halo = V7X_SUBLANES
    n_out = D_MODEL // COL_CHUNK
    step = pl.program_id(0)

    @pl.when(step == 0)
    def _():
        h1_scr[...] = jnp.zeros(h1_scr.shape, F32)
        a_scr[0:halo, :] = jnp.zeros((halo, CONV_WIDTH), F32)

    scores0 = _mem_scores(qm0_ref[0], kv0_ref)

    h1 = h1_scr[...]
    ms = jnp.mean(h1 * h1, axis=-1, keepdims=True)
    hb_scr[...] = (h1 * lax.rsqrt(ms + EPS) * rows_ref[ROW_G1:ROW_G1 + 1, :]).astype(BF16)
    seq_start = (jnp.maximum(step - 1, 0) % tiles_per_seq) == 0
    a_scr[0:halo, :] = jnp.where(seq_start, 0.0, a_scr[0:halo, :])

    def proj1(c0, width=COL_CHUNK):
        return _dot(hb_scr[...], w1_ref[:, c0:c0 + width])

    def conv_chunk(j):
        cs = slice(COL_CHUNK * j, COL_CHUNK * (j + 1))
        a_now = proj1(B_CG + COL_CHUNK * j) * proj1(B_U + COL_CHUNK * j)
        a_scr[halo:halo + tm, cs] = a_now
        conv = (rows_ref[ROW_CW:ROW_CW + 1, cs] * a_scr[halo - 2:halo - 2 + tm, cs]
                + rows_ref[ROW_CW + 1:ROW_CW + 2, cs] * a_scr[halo - 1:halo - 1 + tm, cs]
                + rows_ref[ROW_CW + 2:ROW_CW + 3, cs] * a_now)
        mix1 = proj1(B_BG + COL_CHUNK * j) * conv
        y_scr[:, cs] = (mix1 * _silu(proj1(B_Z + COL_CHUNK * j))).astype(BF16)

    qm1 = proj1(B_QM, MEM_WIDTH).astype(BF16)
    z_mem = proj1(B_Z + CONV_WIDTH, MEM_WIDTH)
    conv_chunk(0)
    scores1 = _mem_scores(qm1, kv1_ref)
    conv_chunk(1)
    mem1 = _mem_softmax_pv(scores1, kv1_ref)
    y_scr[:, CONV_WIDTH:BRANCH_B] = (mem1 * _silu(z_mem)).astype(BF16)
    conv_chunk(2)
    conv_chunk(3)
    a_scr[0:halo, :] = a_scr[tm:tm + halo, :]
    mem0 = _mem_softmax_pv(scores0, kv0_ref)
    z0 = z0_ref[0]
    y0 = jnp.concatenate(
        [mix_ref[0] * _silu(z0[:, :GROUP_WIDTH]), mem0 * _silu(z0[:, GROUP_WIDTH:])], axis=1).astype(BF16)

    y1 = y_scr[...]
    ssq = jnp.zeros((tm, 1), F32)
    for j in range(n_out):
        cs = slice(COL_CHUNK * j, COL_CHUNK * (j + 1))
        h2 = h1_scr[:, cs] + _dot(y1, wo1_ref[:, cs])
        ssq = ssq + jnp.sum(h2 * h2, axis=-1, keepdims=True)
        out_ref[0, :, cs] = h2
    scale = lax.rsqrt(ssq * (1.0 / D_MODEL) + EPS)
    out_ref[0] = out_ref[0] * scale * rows_ref[ROW_GF:ROW_GF + 1, :]

    for j in range(n_out):
        cs = slice(COL_CHUNK * j, COL_CHUNK * (j + 1))
        h1_scr[:, cs] = x_ref[0, :, cs] + _dot(y0, wo0_ref[:, cs])


def _tail(x, mix, qm0, z0, kv, wo0, w1, wo1, small_rows):
    b, s, _ = x.shape
    tm = ROW_TILE
    per_seq = s // tm
    n_tiles = b * per_seq

    def tile_a(j):
        return jnp.minimum(j, n_tiles - 1)

    def tile_b(j):
        return jnp.maximum(j - 1, 0)

    def row(width, tile):
        return pl.BlockSpec((1, tm, width), lambda j: (tile(j) // per_seq, tile(j) % per_seq, 0))

    def kv_spec(layer, tile):
        return pl.BlockSpec((1, 1, N_MEM, 2 * MEM_WIDTH), lambda j: (layer, tile(j) // per_seq, 0, 0))

    const = lambda shape: pl.BlockSpec(shape, lambda j: (0, 0))
    tile_f32 = pltpu.VMEM((tm, D_MODEL), F32)
    tile_bf16 = pltpu.VMEM((tm, D_MODEL), BF16)
    return pl.pallas_call(
        functools.partial(_tail_kernel, tiles_per_seq=per_seq),
        grid=(n_tiles + 1,),
        in_specs=[
            row(D_MODEL, tile_a), row(GROUP_WIDTH, tile_a), row(MEM_WIDTH, tile_a), row(BRANCH_A, tile_a),
            kv_spec(0, tile_a),
            _resident(wo0.shape, lambda j: (0, 0)),
            _resident(w1.shape, lambda j: (0, 0)),
            kv_spec(1, tile_b),
            _resident(wo1.shape, lambda j: (0, 0)),
            const(small_rows.shape),
        ],
        out_specs=row(D_MODEL, tile_b),
        out_shape=jax.ShapeDtypeStruct((b, s, D_MODEL), F32),
        scratch_shapes=[
            pltpu.VMEM((tm + V7X_SUBLANES, CONV_WIDTH), F32),
            pltpu.VMEM((tm, BRANCH_B), BF16),
            tile_f32, tile_bf16,
        ],
        compiler_params=_params("arbitrary"),
        name="tail",
    )(x, mix, qm0, z0, kv, wo0, w1, kv, wo1, small_rows)


def kernel(x, mem, positions, norm_g, mem_norm_g, w_mem_kv, attn_w_in, attn_w_out, conv_w_in, conv_w,
           conv_w_out, final_g):
    b, s, _ = x.shape

    inv_freq = ROPE_THETA ** (-jnp.arange(ROT_HALF, dtype=F32) * (2.0 / ROT_DIM))
    freq_rows = jnp.broadcast_to(jnp.tile(inv_freq, SMALL_ROWS // ROT_HALF)[:, None], (SMALL_ROWS, V7X_LANES))

    kv, w0 = _mem_kv(mem, mem_norm_g, w_mem_kv, attn_w_in[0])
    qkv0, qm0, z0, qkv1, qkv2, w1, wo0, wo1 = _proj0(
        x, norm_g, positions, freq_rows, w0, conv_w_in[0], attn_w_out[0], conv_w_out[0])
    qkv1 = qkv1.reshape(qkv0.shape)
    qkv2 = qkv2.reshape(qkv0.shape)
    mix = _attn(qkv0, qkv1, qkv2)
    small_rows = jnp.concatenate([norm_g[1:2], final_g.reshape(1, D_MODEL), conv_w[0]], axis=0)
    return _tail(x, mix, qm0, z0, kv, wo0, w1, wo1, small_rows)
```

```python
import functools

import numpy as np
import jax
import jax.numpy as jnp
from jax import lax
from jax.experimental import pallas as pl
from jax.experimental.pallas import tpu as pltpu

F32 = jnp.float32
BF16 = jnp.bfloat16

D_MODEL = 1024
HEAD_DIM = 64
ROT_DIM = 16
ROT_HALF = ROT_DIM // 2
ROPE_THETA = 500000.0
DILATIONS = (1, 4, 16)
BLOCK = 128
GROUP_WIDTH = 512
N_GROUPS = 3
N_MEM = 256
MEM_WIDTH = 256
CONV_WIDTH = 1024
EPS = 1e-6
SCORE_SCALE = HEAD_DIM ** -0.5 * float(np.log2(np.e))

V7X_LANES = 128
V7X_SUBLANES = 8
BF16_SUBLANES = 16
ROW_TILE = 512
COL_CHUNK = 256
VMEM_LIMIT_BYTES = 56 * 1024 * 1024
NEG_BIG = -1e30
MEM_ROWS = 256
SMALL_ROWS = 16
ROW_G1, ROW_GF, ROW_CW = 0, 1, 2

QKV_G = 3 * GROUP_WIDTH
A_PART = N_GROUPS * GROUP_WIDTH
A_QM = 3 * A_PART
A_Z = A_QM + MEM_WIDTH
BRANCH_A = GROUP_WIDTH + MEM_WIDTH
IN_A_COLS = A_Z + BRANCH_A
B_BG, B_CG, B_U = 0, CONV_WIDTH, 2 * CONV_WIDTH
B_QM = 3 * CONV_WIDTH
B_Z = B_QM + MEM_WIDTH
BRANCH_B = CONV_WIDTH + MEM_WIDTH
IN_B_COLS = B_Z + BRANCH_B


def _column_scale(n_cols, scaled):
    scale = np.ones((n_cols,), np.float32)
    for lo, hi in scaled:
        scale[lo:hi] = SCORE_SCALE
    return scale


_A_SCALE = _column_scale(IN_A_COLS, [(0, A_PART), (A_QM, A_QM + MEM_WIDTH)])


def _params(*sem):
    return pltpu.CompilerParams(dimension_semantics=sem, vmem_limit_bytes=VMEM_LIMIT_BYTES)


def _resident(shape, index_map):
    return pl.BlockSpec(shape, index_map, pipeline_mode=pl.Buffered(1))


def _silu(z):
    return z * jax.nn.sigmoid(z)


def _dot(a, b):
    return jnp.dot(a, b, preferred_element_type=F32)


def _dot_nt(a, b):
    return lax.dot_general(a, b, (((1,), (1,)), ((), ())), preferred_element_type=F32)


def _half_masks():
    lane = lax.broadcasted_iota(jnp.int32, (1, V7X_LANES), 1)
    return (jnp.where(lane < HEAD_DIM, 1.0, 0.0).astype(BF16),
            jnp.where(lane < HEAD_DIM, 0.0, 1.0).astype(BF16))


def _memkv_kernel(mem_hbm, g_ref, w_ref, wa_ref, scale_ref, kv_ref, wab_ref, mem_scr, sems):
    first = pl.program_id(0) == 0
    n_chunks = mem_scr.shape[0] // ROW_TILE

    def chunk_copy(c):
        rows = pl.ds(ROW_TILE * c, ROW_TILE)
        return pltpu.make_async_copy(mem_hbm.at[rows, :], mem_scr.at[rows, :], sems.at[c])

    @pl.when(first)
    def _():
        for c in range(n_chunks):
            chunk_copy(c).start()

    for c in range(wa_ref.shape[0] // V7X_LANES):
        rows = slice(V7X_LANES * c, V7X_LANES * (c + 1))
        wab_ref[rows, :] = (wa_ref[rows, :] * scale_ref[...]).astype(BF16)

    w = w_ref[0].astype(BF16)
    gain = jnp.where(first, g_ref[0:1, :], g_ref[1:2, :])
    for c in range(n_chunks):
        @pl.when(first)
        def _():
            chunk_copy(c).wait()

        rows = slice(ROW_TILE * c, ROW_TILE * (c + 1))
        m = mem_scr[rows, :]
        ms = jnp.mean(m * m, axis=-1, keepdims=True)
        mn = (m * lax.rsqrt(ms + EPS) * gain).astype(BF16)
        kv_ref[0, rows, :] = _dot(mn, w).astype(BF16)


def _mem_kv(mem, mem_norm_g, w_mem_kv, w_attn_in):
    b = mem.shape[0]
    depth = w_mem_kv.shape[0]
    assert depth == 2, "the kernel picks the layer's gain row with a two-way select"
    rows = b * N_MEM
    slab = pl.BlockSpec((D_MODEL // depth, IN_A_COLS), lambda l: (l, 0))
    kv, w0 = pl.pallas_call(
        _memkv_kernel,
        grid=(depth,),
        in_specs=[
            pl.BlockSpec(memory_space=pl.ANY),
            pl.BlockSpec((depth, D_MODEL), lambda l: (0, 0)),
            pl.BlockSpec((1, D_MODEL, 2 * MEM_WIDTH), lambda l: (l, 0, 0)),
            slab,
            pl.BlockSpec((1, IN_A_COLS), lambda l: (0, 0)),
        ],
        out_specs=[pl.BlockSpec((1, rows, 2 * MEM_WIDTH), lambda l: (l, 0, 0)), slab],
        out_shape=[jax.ShapeDtypeStruct((depth, rows, 2 * MEM_WIDTH), BF16),
                   jax.ShapeDtypeStruct((D_MODEL, IN_A_COLS), BF16)],
        scratch_shapes=[pltpu.VMEM((rows, D_MODEL), F32),
                        pltpu.SemaphoreType.DMA((rows // ROW_TILE,))],
        compiler_params=_params("arbitrary"),
        name="mem_kv",
    )(mem.reshape(rows, D_MODEL), mem_norm_g, w_mem_kv,
      w_attn_in, _A_SCALE.reshape(1, IN_A_COLS))
    return kv.reshape(depth, b, N_MEM, 2 * MEM_WIDTH), w0


def _rope_tables(pos_ref, freq_ref, tbl_scr):
    rows = tbl_scr.shape[1]
    freq = freq_ref[0:ROT_HALF, :]
    one = jnp.ones((V7X_SUBLANES, V7X_LANES), F32)
    zero = jnp.zeros((V7X_SUBLANES, V7X_LANES), F32)
    groups = V7X_LANES // V7X_SUBLANES
    per_head = HEAD_DIM // V7X_SUBLANES
    own_row = lax.broadcasted_iota(jnp.int32, (pos_ref.shape[0], V7X_LANES), 0) == pl.program_id(0)

    def lane_rows(first, second, other):
        pieces = []
        for i in range(groups):
            pieces.append(first if i % per_head == 0 else second if i % per_head == 1 else other)
        return jnp.concatenate(pieces, axis=0).T

    for c in range(rows // V7X_LANES):
        sl = slice(V7X_LANES * c, V7X_LANES * (c + 1))
        pos = jnp.sum(jnp.where(own_row, pos_ref[:, sl].astype(F32), 0.0), axis=0, keepdims=True)
        ang = pos * freq
        cs = jnp.cos(ang)
        sn = jnp.sin(ang)
        tbl_scr[0, sl, :] = lane_rows(cs, cs, one)
        tbl_scr[1, sl, :] = lane_rows(-sn, zero, zero)
        tbl_scr[2, sl, :] = lane_rows(zero, sn, zero)


def _rope(a, cs, sn_next, sn_prev):
    return (a * cs + pltpu.roll(a, V7X_LANES - ROT_HALF, 1) * sn_next
            + pltpu.roll(a, ROT_HALF, 1) * sn_prev)


def _proj0_kernel(x_ref, g_ref, pos_ref, freq_ref, w_ref, w1_ref, wo0_ref, wo1_ref,
                  o0_ref, oqm_ref, oz_ref, o1_ref, o2_ref, w1b_ref, wo0b_ref, wo1b_ref,
                  hn_scr, perm_scr, tbl_scr, p4_scr, tbl4_scr):
    tm = x_ref.shape[1]
    x = x_ref[0]
    ms = jnp.mean(x * x, axis=-1, keepdims=True)
    hn = x * lax.rsqrt(ms + EPS) * g_ref[0:1, :]
    hb = hn.astype(BF16)
    n_slab = D_MODEL // V7X_LANES
    for c in range(n_slab):
        hn_scr[c] = hn[:, V7X_LANES * c:V7X_LANES * (c + 1)]
    _rope_tables(pos_ref, freq_ref, tbl_scr)

    def qkv_chunk(lhs, g, j, tables):
        part, half = divmod(j, GROUP_WIDTH // COL_CHUNK)
        c0 = A_PART * part + GROUP_WIDTH * g + COL_CHUNK * half
        acc = _dot(lhs, w_ref[:, c0:c0 + COL_CHUNK])
        if part < 2:
            acc = jnp.concatenate(
                [_rope(acc[:, :V7X_LANES], *tables), _rope(acc[:, V7X_LANES:], *tables)], axis=1)
        return acc.astype(BF16)

    tables = tuple(tbl_scr[t] for t in range(3))
    for j in range(QKV_G // COL_CHUNK):
        res = qkv_chunk(hb, 0, j, tables)
        for e in range(COL_CHUNK // V7X_LANES):
            o0_ref[0, 2 * j + e] = res[:, V7X_LANES * e:V7X_LANES * (e + 1)]
    oqm_ref[0] = _dot(hb, w_ref[:, A_QM:A_QM + MEM_WIDTH]).astype(BF16)
    for j in range(BRANCH_A // COL_CHUNK):
        oz_ref[0, :, COL_CHUNK * j:COL_CHUNK * (j + 1)] = _dot(
            hb, w_ref[:, A_Z + COL_CHUNK * j:A_Z + COL_CHUNK * (j + 1)])

    d1 = DILATIONS[1]
    n1 = tm // d1
    for r in range(d1):
        rows = slice(r * n1, (r + 1) * n1)
        for c in range(n_slab):
            p4_scr[c, rows, :] = hn_scr[c, pl.ds(r, n1, stride=d1), :]
        for t in range(3):
            tbl4_scr[t, rows, :] = tbl_scr[t, pl.ds(r, n1, stride=d1), :]

    def rows_of_stream(g, r):
        if g == 1:
            return slice(r * n1, (r + 1) * n1)
        return pl.ds((r % d1) * n1 + r // d1, tm // DILATIONS[g], stride=DILATIONS[g] // d1)

    for g, o_ref in ((1, o1_ref), (2, o2_ref)):
        d = DILATIONS[g]
        n = tm // d
        for r in range(d):
            for c in range(n_slab):
                perm_scr[r * n:(r + 1) * n, V7X_LANES * c:V7X_LANES * (c + 1)] = (
                    p4_scr[c, rows_of_stream(g, r), :].astype(BF16))
        tables = tuple(
            jnp.concatenate([tbl4_scr[t, rows_of_stream(g, r), :] for r in range(d)], axis=0)
            for t in range(3))
        lhs = perm_scr[...]
        for j in range(QKV_G // COL_CHUNK):
            res = qkv_chunk(lhs, g, j, tables)
            for r in range(d):
                for e in range(COL_CHUNK // V7X_LANES):
                    o_ref[0, 2 * j + e, r] = res[r * n:(r + 1) * n, V7X_LANES * e:V7X_LANES * (e + 1)]

    w1b_ref[:, :B_QM] = w1_ref[:, :B_QM].astype(BF16)
    w1b_ref[:, B_QM:B_Z] = (w1_ref[:, B_QM:B_Z] * SCORE_SCALE).astype(BF16)
    w1b_ref[:, B_Z:] = w1_ref[:, B_Z:].astype(BF16)
    wo0b_ref[...] = wo0_ref[...].astype(BF16)
    wo1b_ref[...] = wo1_ref[...].astype(BF16)


def _proj0(x, norm_g, pos_rows, freq_rows, w0, w1, wo0, wo1):
    b, s, _ = x.shape
    tm = ROW_TILE
    per_seq = s // tm
    n_steps = b * per_seq
    row = lambda width: pl.BlockSpec((1, tm, width), lambda i, j: (i, j, 0))
    n_slab = QKV_G // V7X_LANES
    natural = pl.BlockSpec((1, n_slab, tm, V7X_LANES), lambda i, j: (i, 0, j, 0))
    stream = lambda d: pl.BlockSpec((1, n_slab, d, tm // d, V7X_LANES), lambda i, j: (i, 0, 0, j, 0))

    def slab(w, steps):
        rows = w.shape[0] // steps
        assert rows * steps == w.shape[0] and rows % BF16_SUBLANES == 0
        return pl.BlockSpec((rows, w.shape[1]), lambda i, j: (jnp.minimum(i * per_seq + j, steps - 1), 0))

    slabs = [slab(w1, n_steps), slab(wo0, n_steps // 2), slab(wo1, n_steps // 2)]
    d1, d2 = DILATIONS[1], DILATIONS[2]
    return pl.pallas_call(
        _proj0_kernel,
        grid=(b, per_seq),
        in_specs=[
            row(D_MODEL),
            pl.BlockSpec(norm_g.shape, lambda i, j: (0, 0)),
            pl.BlockSpec((b, tm), lambda i, j: (0, j)),
            pl.BlockSpec((SMALL_ROWS, V7X_LANES), lambda i, j: (0, 0)),
            _resident((D_MODEL, IN_A_COLS), lambda i, j: (0, 0)),
        ] + slabs,
        out_specs=[natural, row(MEM_WIDTH), row(BRANCH_A), stream(d1), stream(d2)] + slabs,
        out_shape=[
            jax.ShapeDtypeStruct((b, n_slab, s, V7X_LANES), BF16),
            jax.ShapeDtypeStruct((b, s, MEM_WIDTH), BF16),
            jax.ShapeDtypeStruct((b, s, BRANCH_A), F32),
            jax.ShapeDtypeStruct((b, n_slab, d1, s // d1, V7X_LANES), BF16),
            jax.ShapeDtypeStruct((b, n_slab, d2, s // d2, V7X_LANES), BF16),
            jax.ShapeDtypeStruct(w1.shape, BF16),
            jax.ShapeDtypeStruct(wo0.shape, BF16),
            jax.ShapeDtypeStruct(wo1.shape, BF16),
        ],
        scratch_shapes=[
            pltpu.VMEM((D_MODEL // V7X_LANES, tm, V7X_LANES), F32),
            pltpu.VMEM((tm, D_MODEL), BF16),
            pltpu.VMEM((3, tm, V7X_LANES), F32),
            pltpu.VMEM((D_MODEL // V7X_LANES, tm, V7X_LANES), F32),
            pltpu.VMEM((3, tm, V7X_LANES), F32),
        ],
        compiler_params=_params("arbitrary", "arbitrary"),
        name="proj0",
    )(x, norm_g, pos_rows, freq_rows, w0, w1, wo0, wo1)


def _attn_kernel(q0, k0, v0, q1, k1, v1, q2, k2, v2, o_ref,
                 num_scr, m_scr, l_scr, ve_scr, bias2_scr, bias1_scr):
    s_len = o_ref.shape[1]
    half0, half1 = _half_masks()
    first_half = lax.broadcasted_iota(jnp.int32, (BLOCK, V7X_LANES), 1) < HEAD_DIM

    @pl.when((pl.program_id(0) == 0) & (pl.program_id(1) == 0))
    def _():
        qi = lax.broadcasted_iota(jnp.int32, (2 * BLOCK, 2 * BLOCK), 0) & (BLOCK - 1)
        kj = lax.broadcasted_iota(jnp.int32, (2 * BLOCK, 2 * BLOCK), 1)
        valid_two = ((kj < BLOCK) & (kj >= qi)) | ((kj >= BLOCK) & (kj - BLOCK <= qi))
        bias2_scr[...] = jnp.where(valid_two, 0.0, NEG_BIG)
        valid_one = (lax.broadcasted_iota(jnp.int32, (2 * BLOCK, BLOCK), 1)
                     <= (lax.broadcasted_iota(jnp.int32, (2 * BLOCK, BLOCK), 0) & (BLOCK - 1)))
        bias1_scr[...] = jnp.where(valid_one, 0.0, NEG_BIG)

    for g, v_ref in enumerate((v0, v1, v2)):
        v = v_ref[...]
        for h, half in enumerate((half0, half1)):
            ve_scr[g, h] = v * half

    def block(q_ref, k_ref, g, q_row, k_row, n_keys, bias_ref, dst):
        q = q_ref[pl.ds(q_row, BLOCK), :]
        qs = jnp.concatenate([q * half0, q * half1], axis=0)
        k = k_ref[pl.ds(k_row, n_keys), :]
        s = _dot_nt(qs, k) + bias_ref[...]
        m = jnp.max(s, axis=-1, keepdims=True)
        p = jnp.exp2(s - m).astype(BF16)
        p_cat = jnp.concatenate([p[:BLOCK], p[BLOCK:]], axis=1)
        keys = pl.ds(k_row, n_keys)
        ve = jnp.concatenate(
            [jnp.concatenate([ve_scr[g, h, keys, :], jnp.broadcast_to(half, (n_keys, V7X_LANES))], axis=1)
             for h, half in enumerate((half0, half1))], axis=0)
        pv = _dot(p_cat, ve)
        num_scr[g, dst, :] = pv[:, :V7X_LANES]
        l_scr[g, dst, :] = pv[:, V7X_LANES:]
        m_scr[g, dst, :] = jnp.where(first_half, m[:BLOCK], m[BLOCK:])

    d1, d2 = DILATIONS[1], DILATIONS[2]
    stream_len = s_len // d1
    sub = d2 // d1

    for r in range(d2):
        block(q2, k2, 2, r * BLOCK, r * BLOCK, BLOCK, bias1_scr,
              pl.ds((r % d1) * stream_len + r // d1, BLOCK, stride=sub))

    for r in range(d1):
        base = r * stream_len
        block(q1, k1, 1, base, base, BLOCK, bias1_scr, pl.ds(base, BLOCK))
        for n in range(1, stream_len // BLOCK):
            block(q1, k1, 1, base + n * BLOCK, base + (n - 1) * BLOCK, 2 * BLOCK, bias2_scr,
                  pl.ds(base + n * BLOCK, BLOCK))

    def merge(n):
        rows_per = BLOCK // d1
        for r in range(d1):
            tok = pl.ds(n * BLOCK + r, rows_per, stride=d1)
            rows = pl.ds(r * stream_len + n * rows_per, rows_per)
            idx = (tok, rows, rows)
            ms = [m_scr[g, idx[g], :] for g in range(N_GROUPS)]
            top = jnp.maximum(ms[0], jnp.maximum(ms[1], ms[2]))
            ws = [jnp.exp2(mg - top) for mg in ms]
            nums = [ws[g] * num_scr[g, idx[g], :] for g in range(N_GROUPS)]
            dens = [ws[g] * l_scr[g, idx[g], :] for g in range(N_GROUPS)]
            num = nums[0] + nums[1] + nums[2]
            den = dens[0] + dens[1] + dens[2]
            o_ref[0, tok, :] = num / den

    block(q0, k0, 0, 0, 0, BLOCK, bias1_scr, pl.ds(0, BLOCK))
    for n in range(1, s_len // BLOCK):
        block(q0, k0, 0, n * BLOCK, (n - 1) * BLOCK, 2 * BLOCK, bias2_scr, pl.ds(n * BLOCK, BLOCK))
        merge(n - 1)
    merge(s_len // BLOCK - 1)


def _attn(qkv0, qkv1, qkv2):
    b, _, s, _ = qkv0.shape
    n_pair = GROUP_WIDTH // V7X_LANES

    def part(k):
        return pl.BlockSpec((None, None, s, V7X_LANES), lambda i, p, k=k: (i, k * n_pair + p, 0, 0))

    specs = [part(0), part(1), part(2)] * N_GROUPS
    stat = pltpu.VMEM((N_GROUPS, s, V7X_LANES), F32)
    scratch = [
        stat, stat, stat,
        pltpu.VMEM((N_GROUPS, 2, s, V7X_LANES), BF16),
        pltpu.VMEM((2 * BLOCK, 2 * BLOCK), F32),
        pltpu.VMEM((2 * BLOCK, BLOCK), F32),
    ]
    return pl.pallas_call(
        _attn_kernel,
        grid=(b, n_pair),
        in_specs=specs,
        out_specs=pl.BlockSpec((1, s, V7X_LANES), lambda i, p: (i, 0, p)),
        out_shape=jax.ShapeDtypeStruct((b, s, GROUP_WIDTH), F32),
        scratch_shapes=scratch,
        compiler_params=_params("arbitrary", "arbitrary"),
        name="dilated_attn",
    )(qkv0, qkv0, qkv0, qkv1, qkv1, qkv1, qkv2, qkv2, qkv2)


def _mem_scores(qm, kv_ref):
    rows = qm.shape[0]
    half0, half1 = _half_masks()
    scores = []
    for p in range(MEM_WIDTH // V7X_LANES):
        k = kv_ref[0, 0, :, V7X_LANES * p:V7X_LANES * (p + 1)]
        for c in range(rows // MEM_ROWS):
            q = qm[MEM_ROWS * c:MEM_ROWS * (c + 1), V7X_LANES * p:V7X_LANES * (p + 1)]
            scores.append(_dot_nt(jnp.concatenate([q * half0, q * half1], axis=0), k))
    return scores


def _mem_softmax_pv(scores, kv_ref):
    half0, half1 = _half_masks()
    n_pairs = MEM_WIDTH // V7X_LANES
    per_pair = len(scores) // n_pairs
    outs = []
    for p in range(n_pairs):
        v = kv_ref[0, 0, :, MEM_WIDTH + V7X_LANES * p:MEM_WIDTH + V7X_LANES * (p + 1)]
        ve = jnp.concatenate(
            [jnp.concatenate([v * half, jnp.broadcast_to(half, (N_MEM, V7X_LANES))], axis=1)
             for half in (half0, half1)], axis=0)
        chunks = []
        for s in scores[per_pair * p:per_pair * (p + 1)]:
            e = jnp.exp2(s - jnp.max(s, axis=-1, keepdims=True)).astype(BF16)
            pv = _dot(jnp.concatenate([e[:MEM_ROWS], e[MEM_ROWS:]], axis=1), ve)
            chunks.append(pv[:, :V7X_LANES] / pv[:, V7X_LANES:])
        outs.append(jnp.concatenate(chunks, axis=0))
    return jnp.concatenate(outs, axis=1)


def _tail_kernel(x_ref, mix_ref, qm0_ref, z0_ref, kv0_ref, wo0_ref, w1_ref,
                 kv1_ref, wo1_ref, rows_ref, out_ref,
                 a_scr, y_scr, h1_scr, hb_scr, *, tiles_per_seq):
    tm = x_ref.shape[1]
    halo = V7X_SUBLANES
    n_out = D_MODEL // COL_CHUNK
    step = pl.program_id(0)

    @pl.when(step == 0)
    def _():
        h1_scr[...] = jnp.zeros(h1_scr.shape, F32)
        a_scr[0:halo, :] = jnp.zeros((halo, CONV_WIDTH), F32)

    scores0 = _mem_scores(qm0_ref[0], kv0_ref)

    h1 = h1_scr[...]
    ms = jnp.mean(h1 * h1, axis=-1, keepdims=True)
    hb_scr[...] = (h1 * lax.rsqrt(ms + EPS) * rows_ref[ROW_G1:ROW_G1 + 1, :]).astype(BF16)
    seq_start = (jnp.maximum(step - 1, 0) % tiles_per_seq) == 0
    a_scr[0:halo, :] = jnp.where(seq_start, 0.0, a_scr[0:halo, :])

    def proj1(c0, width=COL_CHUNK):
        return _dot(hb_scr[...], w1_ref[:, c0:c0 + width])

    def conv_chunk(j):
        cs = slice(COL_CHUNK * j, COL_CHUNK * (j + 1))
        a_now = proj1(B_CG + COL_CHUNK * j) * proj1(B_U + COL_CHUNK * j)
        a_scr[halo:halo + tm, cs] = a_now
        conv = (rows_ref[ROW_CW:ROW_CW + 1, cs] * a_scr[halo - 2:halo - 2 + tm, cs]
                + rows_ref[ROW_CW + 1:ROW_CW + 2, cs] * a_scr[halo - 1:halo - 1 + tm, cs]
                + rows_ref[ROW_CW + 2:ROW_CW + 3, cs] * a_now)
        mix1 = proj1(B_BG + COL_CHUNK * j) * conv
        y_scr[:, cs] = (mix1 * _silu(proj1(B_Z + COL_CHUNK * j))).astype(BF16)

    qm1 = proj1(B_QM, MEM_WIDTH).astype(BF16)
    z_mem = proj1(B_Z + CONV_WIDTH, MEM_WIDTH)
    conv_chunk(0)
    scores1 = _mem_scores(qm1, kv1_ref)
    conv_chunk(1)
    mem1 = _mem_softmax_pv(scores1, kv1_ref)
    y_scr[:, CONV_WIDTH:BRANCH_B] = (mem1 * _silu(z_mem)).astype(BF16)
    conv_chunk(2)
    conv_chunk(3)
    a_scr[0:halo, :] = a_scr[tm:tm + halo, :]
    mem0 = _mem_softmax_pv(scores0, kv0_ref)
    z0 = z0_ref[0]
    y0 = jnp.concatenate(
        [mix_ref[0] * _silu(z0[:, :GROUP_WIDTH]), mem0 * _silu(z0[:, GROUP_WIDTH:])], axis=1).astype(BF16)

    y1 = y_scr[...]
    ssq = jnp.zeros((tm, 1), F32)
    for j in range(n_out):
        cs = slice(COL_CHUNK * j, COL_CHUNK * (j + 1))
        h2 = h1_scr[:, cs] + _dot(y1, wo1_ref[:, cs])
        ssq = ssq + jnp.sum(h2 * h2, axis=-1, keepdims=True)
        out_ref[0, :, cs] = h2
    scale = lax.rsqrt(ssq * (1.0 / D_MODEL) + EPS)
    out_ref[0] = out_ref[0] * scale * rows_ref[ROW_GF:ROW_GF + 1, :]

    for j in range(n_out):
        cs = slice(COL_CHUNK * j, COL_CHUNK * (j + 1))
        h1_scr[:, cs] = x_ref[0, :, cs] + _dot(y0, wo0_ref[:, cs])


def _tail(x, mix, qm0, z0, kv, wo0, w1, wo1, small_rows):
    b, s, _ = x.shape
    tm = ROW_TILE
    per_seq = s // tm
    n_tiles = b * per_seq

    def tile_a(j):
        return jnp.minimum(j, n_tiles - 1)

    def tile_b(j):
        return jnp.maximum(j - 1, 0)

    def row(width, tile):
        return pl.BlockSpec((1, tm, width), lambda j: (tile(j) // per_seq, tile(j) % per_seq, 0))

    def kv_spec(layer, tile):
        return pl.BlockSpec((1, 1, N_MEM, 2 * MEM_WIDTH), lambda j: (layer, tile(j) // per_seq, 0, 0))

    const = lambda shape: pl.BlockSpec(shape, lambda j: (0, 0))
    tile_f32 = pltpu.VMEM((tm, D_MODEL), F32)
    tile_bf16 = pltpu.VMEM((tm, D_MODEL), BF16)
    return pl.pallas_call(
        functools.partial(_tail_kernel, tiles_per_seq=per_seq),
        grid=(n_tiles + 1,),
        in_specs=[
            row(D_MODEL, tile_a), row(GROUP_WIDTH, tile_a), row(MEM_WIDTH, tile_a), row(BRANCH_A, tile_a),
            kv_spec(0, tile_a),
            _resident(wo0.shape, lambda j: (0, 0)),
            _resident(w1.shape, lambda j: (0, 0)),
            kv_spec(1, tile_b),
            _resident(wo1.shape, lambda j: (0, 0)),
            const(small_rows.shape),
        ],
        out_specs=row(D_MODEL, tile_b),
        out_shape=jax.ShapeDtypeStruct((b, s, D_MODEL), F32),
        scratch_shapes=[
            pltpu.VMEM((tm + V7X_SUBLANES, CONV_WIDTH), F32),
            pltpu.VMEM((tm, BRANCH_B), BF16),
            tile_f32, tile_bf16,
        ],
        compiler_params=_params("arbitrary"),
        name="tail",
    )(x, mix, qm0, z0, kv, wo0, w1, kv, wo1, small_rows)


def kernel(x, mem, positions, norm_g, mem_norm_g, w_mem_kv, attn_w_in, attn_w_out, conv_w_in, conv_w,
           conv_w_out, final_g):
    b, s, _ = x.shape

    inv_freq = ROPE_THETA ** (-jnp.arange(ROT_HALF, dtype=F32) * (2.0 / ROT_DIM))
    freq_rows = jnp.broadcast_to(jnp.tile(inv_freq, SMALL_ROWS // ROT_HALF)[:, None], (SMALL_ROWS, V7X_LANES))

    kv, w0 = _mem_kv(mem, mem_norm_g, w_mem_kv, attn_w_in[0])
    qkv0, qm0, z0, qkv1, qkv2, w1, wo0, wo1 = _proj0(
        x, norm_g, positions, freq_rows, w0, conv_w_in[0], attn_w_out[0], conv_w_out[0])
    qkv1 = qkv1.reshape(qkv0.shape)
    qkv2 = qkv2.reshape(qkv0.shape)
    mix = _attn(qkv0, qkv1, qkv2)
    small_rows = jnp.concatenate([norm_g[1:2], final_g.reshape(1, D_MODEL), conv_w[0]], axis=0)
    return _tail(x, mix, qm0, z0, kv, wo0, w1, wo1, small_rows)
```

```python
import functools

import numpy as np
import jax
import jax.numpy as jnp
from jax import lax
from jax.experimental import pallas as pl
from jax.experimental.pallas import tpu as pltpu

F32 = jnp.float32
BF16 = jnp.bfloat16

D_MODEL = 1024
HEAD_DIM = 64
ROT_DIM = 16
ROT_HALF = ROT_DIM // 2
ROPE_THETA = 500000.0
DILATIONS = (1, 4, 16)
BLOCK = 128
GROUP_WIDTH = 512
N_GROUPS = 3
N_MEM = 256
MEM_WIDTH = 256
CONV_WIDTH = 1024
EPS = 1e-6
SCORE_SCALE = HEAD_DIM ** -0.5 * float(np.log2(np.e))

V7X_LANES = 128
V7X_SUBLANES = 8
BF16_SUBLANES = 16
ROW_TILE = 512
COL_CHUNK = 256
VMEM_LIMIT_BYTES = 56 * 1024 * 1024
NEG_BIG = -1e30
MEM_ROWS = 256
SMALL_ROWS = 16
ROW_G1, ROW_GF, ROW_CW = 0, 1, 2

QKV_G = 3 * GROUP_WIDTH
A_PART = N_GROUPS * GROUP_WIDTH
A_QM = 3 * A_PART
A_Z = A_QM + MEM_WIDTH
BRANCH_A = GROUP_WIDTH + MEM_WIDTH
IN_A_COLS = A_Z + BRANCH_A
B_BG, B_CG, B_U = 0, CONV_WIDTH, 2 * CONV_WIDTH
B_QM = 3 * CONV_WIDTH
B_Z = B_QM + MEM_WIDTH
BRANCH_B = CONV_WIDTH + MEM_WIDTH
IN_B_COLS = B_Z + BRANCH_B


def _column_scale(n_cols, scaled):
    scale = np.ones((n_cols,), np.float32)
    for lo, hi in scaled:
        scale[lo:hi] = SCORE_SCALE
    return scale


_A_SCALE = _column_scale(IN_A_COLS, [(0, A_PART), (A_QM, A_QM + MEM_WIDTH)])


def _params(*sem):
    return pltpu.CompilerParams(dimension_semantics=sem, vmem_limit_bytes=VMEM_LIMIT_BYTES)


def _resident(shape, index_map):
    return pl.BlockSpec(shape, index_map, pipeline_mode=pl.Buffered(1))


def _silu(z):
    return z * jax.nn.sigmoid(z)


def _dot(a, b):
    return jnp.dot(a, b, preferred_element_type=F32)


def _dot_nt(a, b):
    return lax.dot_general(a, b, (((1,), (1,)), ((), ())), preferred_element_type=F32)


def _half_masks():
    lane = lax.broadcasted_iota(jnp.int32, (1, V7X_LANES), 1)
    return (jnp.where(lane < HEAD_DIM, 1.0, 0.0).astype(BF16),
            jnp.where(lane < HEAD_DIM, 0.0, 1.0).astype(BF16))


def _memkv_kernel(mem_ref, g_ref, w_ref, wa_ref, scale_ref, kv_ref, wab_ref):
    w = w_ref[0].astype(BF16)
    gain = jnp.where(pl.program_id(0) == 0, g_ref[0:1, :], g_ref[1:2, :])
    for c in range(mem_ref.shape[0] // ROW_TILE):
        rows = slice(ROW_TILE * c, ROW_TILE * (c + 1))
        m = mem_ref[rows, :]
        ms = jnp.mean(m * m, axis=-1, keepdims=True)
        mn = (m * lax.rsqrt(ms + EPS) * gain).astype(BF16)
        kv_ref[0, rows, :] = _dot(mn, w).astype(BF16)
    for c in range(wa_ref.shape[0] // V7X_LANES):
        rows = slice(V7X_LANES * c, V7X_LANES * (c + 1))
        wab_ref[rows, :] = (wa_ref[rows, :] * scale_ref[...]).astype(BF16)


def _mem_kv(mem, mem_norm_g, w_mem_kv, w_attn_in):
    b = mem.shape[0]
    depth = w_mem_kv.shape[0]
    assert depth == 2, "the kernel picks the layer's gain row with a two-way select"
    rows = b * N_MEM
    slab = pl.BlockSpec((D_MODEL // depth, IN_A_COLS), lambda l: (l, 0))
    kv, w0 = pl.pallas_call(
        _memkv_kernel,
        grid=(depth,),
        in_specs=[
            _resident((rows, D_MODEL), lambda l: (0, 0)),
            pl.BlockSpec((depth, D_MODEL), lambda l: (0, 0)),
            pl.BlockSpec((1, D_MODEL, 2 * MEM_WIDTH), lambda l: (l, 0, 0)),
            slab,
            pl.BlockSpec((1, IN_A_COLS), lambda l: (0, 0)),
        ],
        out_specs=[pl.BlockSpec((1, rows, 2 * MEM_WIDTH), lambda l: (l, 0, 0)), slab],
        out_shape=[jax.ShapeDtypeStruct((depth, rows, 2 * MEM_WIDTH), BF16),
                   jax.ShapeDtypeStruct((D_MODEL, IN_A_COLS), BF16)],
        compiler_params=_params("arbitrary"),
        name="mem_kv",
    )(mem.reshape(rows, D_MODEL), mem_norm_g, w_mem_kv,
      w_attn_in, _A_SCALE.reshape(1, IN_A_COLS))
    return kv.reshape(depth, b, N_MEM, 2 * MEM_WIDTH), w0


def _rope_tables(pos_ref, freq_ref, tbl_scr):
    rows = tbl_scr.shape[1]
    freq = freq_ref[0:ROT_HALF, :]
    one = jnp.ones((V7X_SUBLANES, V7X_LANES), F32)
    zero = jnp.zeros((V7X_SUBLANES, V7X_LANES), F32)
    groups = V7X_LANES // V7X_SUBLANES
    per_head = HEAD_DIM // V7X_SUBLANES
    own_row = lax.broadcasted_iota(jnp.int32, (pos_ref.shape[0], V7X_LANES), 0) == pl.program_id(0)

    def lane_rows(first, second, other):
        pieces = []
        for i in range(groups):
            pieces.append(first if i % per_head == 0 else second if i % per_head == 1 else other)
        return jnp.concatenate(pieces, axis=0).T

    for c in range(rows // V7X_LANES):
        sl = slice(V7X_LANES * c, V7X_LANES * (c + 1))
        pos = jnp.sum(jnp.where(own_row, pos_ref[:, sl].astype(F32), 0.0), axis=0, keepdims=True)
        ang = pos * freq
        cs = jnp.cos(ang)
        sn = jnp.sin(ang)
        tbl_scr[0, sl, :] = lane_rows(cs, cs, one)
        tbl_scr[1, sl, :] = lane_rows(-sn, zero, zero)
        tbl_scr[2, sl, :] = lane_rows(zero, sn, zero)


def _rope(a, cs, sn_next, sn_prev):
    return (a * cs + pltpu.roll(a, V7X_LANES - ROT_HALF, 1) * sn_next
            + pltpu.roll(a, ROT_HALF, 1) * sn_prev)


def _proj0_kernel(x_ref, g_ref, pos_ref, freq_ref, w_ref, w1_ref, wo0_ref, wo1_ref,
                  o0_ref, oqm_ref, oz_ref, o1_ref, o2_ref, w1b_ref, wo0b_ref, wo1b_ref,
                  hn_scr, perm_scr, tbl_scr, p4_scr, tbl4_scr):
    tm = x_ref.shape[1]
    x = x_ref[0]
    ms = jnp.mean(x * x, axis=-1, keepdims=True)
    hn = x * lax.rsqrt(ms + EPS) * g_ref[0:1, :]
    hb = hn.astype(BF16)
    n_slab = D_MODEL // V7X_LANES
    for c in range(n_slab):
        hn_scr[c] = hn[:, V7X_LANES * c:V7X_LANES * (c + 1)]
    _rope_tables(pos_ref, freq_ref, tbl_scr)

    def qkv_chunk(lhs, g, j, tables):
        part, half = divmod(j, GROUP_WIDTH // COL_CHUNK)
        c0 = A_PART * part + GROUP_WIDTH * g + COL_CHUNK * half
        acc = _dot(lhs, w_ref[:, c0:c0 + COL_CHUNK])
        if part < 2:
            acc = jnp.concatenate(
                [_rope(acc[:, :V7X_LANES], *tables), _rope(acc[:, V7X_LANES:], *tables)], axis=1)
        return acc.astype(BF16)

    tables = tuple(tbl_scr[t] for t in range(3))
    for j in range(QKV_G // COL_CHUNK):
        res = qkv_chunk(hb, 0, j, tables)
        for e in range(COL_CHUNK // V7X_LANES):
            o0_ref[0, 2 * j + e] = res[:, V7X_LANES * e:V7X_LANES * (e + 1)]
    oqm_ref[0] = _dot(hb, w_ref[:, A_QM:A_QM + MEM_WIDTH]).astype(BF16)
    for j in range(BRANCH_A // COL_CHUNK):
        oz_ref[0, :, COL_CHUNK * j:COL_CHUNK * (j + 1)] = _dot(
            hb, w_ref[:, A_Z + COL_CHUNK * j:A_Z + COL_CHUNK * (j + 1)])

    d1 = DILATIONS[1]
    n1 = tm // d1
    for r in range(d1):
        rows = slice(r * n1, (r + 1) * n1)
        for c in range(n_slab):
            p4_scr[c, rows, :] = hn_scr[c, pl.ds(r, n1, stride=d1), :]
        for t in range(3):
            tbl4_scr[t, rows, :] = tbl_scr[t, pl.ds(r, n1, stride=d1), :]

    def rows_of_stream(g, r):
        if g == 1:
            return slice(r * n1, (r + 1) * n1)
        return pl.ds((r % d1) * n1 + r // d1, tm // DILATIONS[g], stride=DILATIONS[g] // d1)

    for g, o_ref in ((1, o1_ref), (2, o2_ref)):
        d = DILATIONS[g]
        n = tm // d
        for r in range(d):
            for c in range(n_slab):
                perm_scr[r * n:(r + 1) * n, V7X_LANES * c:V7X_LANES * (c + 1)] = (
                    p4_scr[c, rows_of_stream(g, r), :].astype(BF16))
        tables = tuple(
            jnp.concatenate([tbl4_scr[t, rows_of_stream(g, r), :] for r in range(d)], axis=0)
            for t in range(3))
        lhs = perm_scr[...]
        for j in range(QKV_G // COL_CHUNK):
            res = qkv_chunk(lhs, g, j, tables)
            for r in range(d):
                for e in range(COL_CHUNK // V7X_LANES):
                    o_ref[0, 2 * j + e, r] = res[r * n:(r + 1) * n, V7X_LANES * e:V7X_LANES * (e + 1)]

    w1b_ref[:, :B_QM] = w1_ref[:, :B_QM].astype(BF16)
    w1b_ref[:, B_QM:B_Z] = (w1_ref[:, B_QM:B_Z] * SCORE_SCALE).astype(BF16)
    w1b_ref[:, B_Z:] = w1_ref[:, B_Z:].astype(BF16)
    wo0b_ref[...] = wo0_ref[...].astype(BF16)
    wo1b_ref[...] = wo1_ref[...].astype(BF16)


def _proj0(x, norm_g, pos_rows, freq_rows, w0, w1, wo0, wo1):
    b, s, _ = x.shape
    tm = ROW_TILE
    per_seq = s // tm
    n_steps = b * per_seq
    row = lambda width: pl.BlockSpec((1, tm, width), lambda i, j: (i, j, 0))
    n_slab = QKV_G // V7X_LANES
    natural = pl.BlockSpec((1, n_slab, tm, V7X_LANES), lambda i, j: (i, 0, j, 0))
    stream = lambda d: pl.BlockSpec((1, n_slab, d, tm // d, V7X_LANES), lambda i, j: (i, 0, 0, j, 0))

    def slab(w, steps):
        rows = w.shape[0] // steps
        assert rows * steps == w.shape[0] and rows % BF16_SUBLANES == 0
        return pl.BlockSpec((rows, w.shape[1]), lambda i, j: (jnp.minimum(i * per_seq + j, steps - 1), 0))

    slabs = [slab(w1, n_steps), slab(wo0, n_steps // 2), slab(wo1, n_steps // 2)]
    d1, d2 = DILATIONS[1], DILATIONS[2]
    return pl.pallas_call(
        _proj0_kernel,
        grid=(b, per_seq),
        in_specs=[
            row(D_MODEL),
            pl.BlockSpec(norm_g.shape, lambda i, j: (0, 0)),
            pl.BlockSpec((b, tm), lambda i, j: (0, j)),
            pl.BlockSpec((SMALL_ROWS, V7X_LANES), lambda i, j: (0, 0)),
            _resident((D_MODEL, IN_A_COLS), lambda i, j: (0, 0)),
        ] + slabs,
        out_specs=[natural, row(MEM_WIDTH), row(BRANCH_A), stream(d1), stream(d2)] + slabs,
        out_shape=[
            jax.ShapeDtypeStruct((b, n_slab, s, V7X_LANES), BF16),
            jax.ShapeDtypeStruct((b, s, MEM_WIDTH), BF16),
            jax.ShapeDtypeStruct((b, s, BRANCH_A), F32),
            jax.ShapeDtypeStruct((b, n_slab, d1, s // d1, V7X_LANES), BF16),
            jax.ShapeDtypeStruct((b, n_slab, d2, s // d2, V7X_LANES), BF16),
            jax.ShapeDtypeStruct(w1.shape, BF16),
            jax.ShapeDtypeStruct(wo0.shape, BF16),
            jax.ShapeDtypeStruct(wo1.shape, BF16),
        ],
        scratch_shapes=[
            pltpu.VMEM((D_MODEL // V7X_LANES, tm, V7X_LANES), F32),
            pltpu.VMEM((tm, D_MODEL), BF16),
            pltpu.VMEM((3, tm, V7X_LANES), F32),
            pltpu.VMEM((D_MODEL // V7X_LANES, tm, V7X_LANES), F32),
            pltpu.VMEM((3, tm, V7X_LANES), F32),
        ],
        compiler_params=_params("arbitrary", "arbitrary"),
        name="proj0",
    )(x, norm_g, pos_rows, freq_rows, w0, w1, wo0, wo1)


def _attn_kernel(q0, k0, v0, q1, k1, v1, q2, k2, v2, o_ref,
                 num_scr, m_scr, l_scr, ve_scr, bias2_scr, bias1_scr):
    s_len = o_ref.shape[1]
    half0, half1 = _half_masks()
    first_half = lax.broadcasted_iota(jnp.int32, (BLOCK, V7X_LANES), 1) < HEAD_DIM

    @pl.when((pl.program_id(0) == 0) & (pl.program_id(1) == 0))
    def _():
        qi = lax.broadcasted_iota(jnp.int32, (2 * BLOCK, 2 * BLOCK), 0) & (BLOCK - 1)
        kj = lax.broadcasted_iota(jnp.int32, (2 * BLOCK, 2 * BLOCK), 1)
        valid_two = ((kj < BLOCK) & (kj >= qi)) | ((kj >= BLOCK) & (kj - BLOCK <= qi))
        bias2_scr[...] = jnp.where(valid_two, 0.0, NEG_BIG)
        valid_one = (lax.broadcasted_iota(jnp.int32, (2 * BLOCK, BLOCK), 1)
                     <= (lax.broadcasted_iota(jnp.int32, (2 * BLOCK, BLOCK), 0) & (BLOCK - 1)))
        bias1_scr[...] = jnp.where(valid_one, 0.0, NEG_BIG)

    for g, v_ref in enumerate((v0, v1, v2)):
        v = v_ref[...]
        for h, half in enumerate((half0, half1)):
            ve_scr[g, h] = v * half

    def block(q_ref, k_ref, g, q_row, k_row, n_keys, bias_ref, dst):
        q = q_ref[pl.ds(q_row, BLOCK), :]
        qs = jnp.concatenate([q * half0, q * half1], axis=0)
        k = k_ref[pl.ds(k_row, n_keys), :]
        s = _dot_nt(qs, k) + bias_ref[...]
        m = jnp.max(s, axis=-1, keepdims=True)
        p = jnp.exp2((s - m).astype(BF16))
        p_cat = jnp.concatenate([p[:BLOCK], p[BLOCK:]], axis=1)
        keys = pl.ds(k_row, n_keys)
        ve = jnp.concatenate(
            [jnp.concatenate([ve_scr[g, h, keys, :], jnp.broadcast_to(half, (n_keys, V7X_LANES))], axis=1)
             for h, half in enumerate((half0, half1))], axis=0)
        pv = _dot(p_cat, ve)
        num_scr[g, dst, :] = pv[:, :V7X_LANES]
        l_scr[g, dst, :] = pv[:, V7X_LANES:]
        m_scr[g, dst, :] = jnp.where(first_half, m[:BLOCK], m[BLOCK:])

    d1, d2 = DILATIONS[1], DILATIONS[2]
    stream_len = s_len // d1
    sub = d2 // d1

    for r in range(d2):
        block(q2, k2, 2, r * BLOCK, r * BLOCK, BLOCK, bias1_scr,
              pl.ds((r % d1) * stream_len + r // d1, BLOCK, stride=sub))

    for r in range(d1):
        base = r * stream_len
        block(q1, k1, 1, base, base, BLOCK, bias1_scr, pl.ds(base, BLOCK))
        for n in range(1, stream_len // BLOCK):
            block(q1, k1, 1, base + n * BLOCK, base + (n - 1) * BLOCK, 2 * BLOCK, bias2_scr,
                  pl.ds(base + n * BLOCK, BLOCK))

    def merge(n):
        rows_per = BLOCK // d1
        for r in range(d1):
            tok = pl.ds(n * BLOCK + r, rows_per, stride=d1)
            rows = pl.ds(r * stream_len + n * rows_per, rows_per)
            idx = (tok, rows, rows)
            ms = [m_scr[g, idx[g], :] for g in range(N_GROUPS)]
            top = jnp.maximum(ms[0], jnp.maximum(ms[1], ms[2]))
            ws = [jnp.exp2(mg - top) for mg in ms]
            nums = [ws[g] * num_scr[g, idx[g], :] for g in range(N_GROUPS)]
            dens = [ws[g] * l_scr[g, idx[g], :] for g in range(N_GROUPS)]
            num = nums[0] + nums[1] + nums[2]
            den = dens[0] + dens[1] + dens[2]
            o_ref[0, tok, :] = num / den

    block(q0, k0, 0, 0, 0, BLOCK, bias1_scr, pl.ds(0, BLOCK))
    for n in range(1, s_len // BLOCK):
        block(q0, k0, 0, n * BLOCK, (n - 1) * BLOCK, 2 * BLOCK, bias2_scr, pl.ds(n * BLOCK, BLOCK))
        merge(n - 1)
    merge(s_len // BLOCK - 1)


def _attn(qkv0, qkv1, qkv2):
    b, _, s, _ = qkv0.shape
    n_pair = GROUP_WIDTH // V7X_LANES

    def part(k):
        return pl.BlockSpec((None, None, s, V7X_LANES), lambda i, p, k=k: (i, k * n_pair + p, 0, 0))

    specs = [part(0), part(1), part(2)] * N_GROUPS
    stat = pltpu.VMEM((N_GROUPS, s, V7X_LANES), F32)
    scratch = [
        stat, stat, stat,
        pltpu.VMEM((N_GROUPS, 2, s, V7X_LANES), BF16),
        pltpu.VMEM((2 * BLOCK, 2 * BLOCK), F32),
        pltpu.VMEM((2 * BLOCK, BLOCK), F32),
    ]
    return pl.pallas_call(
        _attn_kernel,
        grid=(b, n_pair),
        in_specs=specs,
        out_specs=pl.BlockSpec((1, s, V7X_LANES), lambda i, p: (i, 0, p)),
        out_shape=jax.ShapeDtypeStruct((b, s, GROUP_WIDTH), F32),
        scratch_shapes=scratch,
        compiler_params=_params("arbitrary", "arbitrary"),
        name="dilated_attn",
    )(qkv0, qkv0, qkv0, qkv1, qkv1, qkv1, qkv2, qkv2, qkv2)


def _mem_scores(qm, kv_ref):
    rows = qm.shape[0]
    half0, half1 = _half_masks()
    scores = []
    for p in range(MEM_WIDTH // V7X_LANES):
        k = kv_ref[0, 0, :, V7X_LANES * p:V7X_LANES * (p + 1)]
        for c in range(rows // MEM_ROWS):
            q = qm[MEM_ROWS * c:MEM_ROWS * (c + 1), V7X_LANES * p:V7X_LANES * (p + 1)]
            scores.append(_dot_nt(jnp.concatenate([q * half0, q * half1], axis=0), k))
    return scores


def _mem_softmax_pv(scores, kv_ref):
    half0, half1 = _half_masks()
    n_pairs = MEM_WIDTH // V7X_LANES
    per_pair = len(scores) // n_pairs
    outs = []
    for p in range(n_pairs):
        v = kv_ref[0, 0, :, MEM_WIDTH + V7X_LANES * p:MEM_WIDTH + V7X_LANES * (p + 1)]
        ve = jnp.concatenate(
            [jnp.concatenate([v * half, jnp.broadcast_to(half, (N_MEM, V7X_LANES))], axis=1)
             for half in (half0, half1)], axis=0)
        chunks = []
        for s in scores[per_pair * p:per_pair * (p + 1)]:
            e = jnp.exp2(s - jnp.max(s, axis=-1, keepdims=True)).astype(BF16)
            pv = _dot(jnp.concatenate([e[:MEM_ROWS], e[MEM_ROWS:]], axis=1), ve)
            chunks.append(pv[:, :V7X_LANES] / pv[:, V7X_LANES:])
        outs.append(jnp.concatenate(chunks, axis=0))
    return jnp.concatenate(outs, axis=1)


def _tail_kernel(x_ref, mix_ref, qm0_ref, z0_ref, kv0_ref, wo0_ref, w1_ref,
                 kv1_ref, wo1_ref, rows_ref, out_ref,
                 a_scr, y_scr, h1_scr, hb_scr, *, tiles_per_seq):
    tm = x_ref.shape[1]
    halo = V7X_SUBLANES
    n_out = D_MODEL // COL_CHUNK
    step = pl.program_id(0)

    @pl.when(step == 0)
    def _():
        h1_scr[...] = jnp.zeros(h1_scr.shape, F32)
        a_scr[0:halo, :] = jnp.zeros((halo, CONV_WIDTH), F32)

    scores0 = _mem_scores(qm0_ref[0], kv0_ref)

    h1 = h1_scr[...]
    ms = jnp.mean(h1 * h1, axis=-1, keepdims=True)
    hb_scr[...] = (h1 * lax.rsqrt(ms + EPS) * rows_ref[ROW_G1:ROW_G1 + 1, :]).astype(BF16)
    seq_start = (jnp.maximum(step - 1, 0) % tiles_per_seq) == 0
    a_scr[0:halo, :] = jnp.where(seq_start, 0.0, a_scr[0:halo, :])

    def proj1(c0, width=COL_CHUNK):
        return _dot(hb_scr[...], w1_ref[:, c0:c0 + width])

    def conv_chunk(j):
        cs = slice(COL_CHUNK * j, COL_CHUNK * (j + 1))
        a_now = proj1(B_CG + COL_CHUNK * j) * proj1(B_U + COL_CHUNK * j)
        a_scr[halo:halo + tm, cs] = a_now
        conv = (rows_ref[ROW_CW:ROW_CW + 1, cs] * a_scr[halo - 2:halo - 2 + tm, cs]
                + rows_ref[ROW_CW + 1:ROW_CW + 2, cs] * a_scr[halo - 1:halo - 1 + tm, cs]
                + rows_ref[ROW_CW + 2:ROW_CW + 3, cs] * a_now)
        mix1 = proj1(B_BG + COL_CHUNK * j) * conv
        y_scr[:, cs] = (mix1 * _silu(proj1(B_Z + COL_CHUNK * j))).astype(BF16)

    qm1 = proj1(B_QM, MEM_WIDTH).astype(BF16)
    z_mem = proj1(B_Z + CONV_WIDTH, MEM_WIDTH)
    conv_chunk(0)
    scores1 = _mem_scores(qm1, kv1_ref)
    conv_chunk(1)
    mem1 = _mem_softmax_pv(scores1, kv1_ref)
    y_scr[:, CONV_WIDTH:BRANCH_B] = (mem1 * _silu(z_mem)).astype(BF16)
    conv_chunk(2)
    conv_chunk(3)
    a_scr[0:halo, :] = a_scr[tm:tm + halo, :]
    mem0 = _mem_softmax_pv(scores0, kv0_ref)
    z0 = z0_ref[0]
    y0 = jnp.concatenate(
        [mix_ref[0] * _silu(z0[:, :GROUP_WIDTH]), mem0 * _silu(z0[:, GROUP_WIDTH:])], axis=1).astype(BF16)

    y1 = y_scr[...]
    ssq = jnp.zeros((tm, 1), F32)
    for j in range(n_out):
        cs = slice(COL_CHUNK * j, COL_CHUNK * (j + 1))
        h2 = h1_scr[:, cs] + _dot(y1, wo1_ref[:, cs])
        ssq = ssq + jnp.sum(h2 * h2, axis=-1, keepdims=True)
        out_ref[0, :, cs] = h2
    scale = lax.rsqrt(ssq * (1.0 / D_MODEL) + EPS)
    out_ref[0] = out_ref[0] * scale * rows_ref[ROW_GF:ROW_GF + 1, :]

    for j in range(n_out):
        cs = slice(COL_CHUNK * j, COL_CHUNK * (j + 1))
        h1_scr[:, cs] = x_ref[0, :, cs] + _dot(y0, wo0_ref[:, cs])


def _tail(x, mix, qm0, z0, kv, wo0, w1, wo1, small_rows):
    b, s, _ = x.shape
    tm = ROW_TILE
    per_seq = s // tm
    n_tiles = b * per_seq

    def tile_a(j):
        return jnp.minimum(j, n_tiles - 1)

    def tile_b(j):
        return jnp.maximum(j - 1, 0)

    def row(width, tile):
        return pl.BlockSpec((1, tm, width), lambda j: (tile(j) // per_seq, tile(j) % per_seq, 0))

    def kv_spec(layer, tile):
        return pl.BlockSpec((1, 1, N_MEM, 2 * MEM_WIDTH), lambda j: (layer, tile(j) // per_seq, 0, 0))

    const = lambda shape: pl.BlockSpec(shape, lambda j: (0, 0))
    tile_f32 = pltpu.VMEM((tm, D_MODEL), F32)
    tile_bf16 = pltpu.VMEM((tm, D_MODEL), BF16)
    return pl.pallas_call(
        functools.partial(_tail_kernel, tiles_per_seq=per_seq),
        grid=(n_tiles + 1,),
        in_specs=[
            row(D_MODEL, tile_a), row(GROUP_WIDTH, tile_a), row(MEM_WIDTH, tile_a), row(BRANCH_A, tile_a),
            kv_spec(0, tile_a),
            _resident(wo0.shape, lambda j: (0, 0)),
            _resident(w1.shape, lambda j: (0, 0)),
            kv_spec(1, tile_b),
            _resident(wo1.shape, lambda j: (0, 0)),
            const(small_rows.shape),
        ],
        out_specs=row(D_MODEL, tile_b),
        out_shape=jax.ShapeDtypeStruct((b, s, D_MODEL), F32),
        scratch_shapes=[
            pltpu.VMEM((tm + V7X_SUBLANES, CONV_WIDTH), F32),
            pltpu.VMEM((tm, BRANCH_B), BF16),
            tile_f32, tile_bf16,
        ],
        compiler_params=_params("arbitrary"),
        name="tail",
    )(x, mix, qm0, z0, kv, wo0, w1, kv, wo1, small_rows)


def kernel(x, mem, positions, norm_g, mem_norm_g, w_mem_kv, attn_w_in, attn_w_out, conv_w_in, conv_w,
           conv_w_out, final_g):
    b, s, _ = x.shape

    inv_freq = ROPE_THETA ** (-jnp.arange(ROT_HALF, dtype=F32) * (2.0 / ROT_DIM))
    freq_rows = jnp.broadcast_to(jnp.tile(inv_freq, SMALL_ROWS // ROT_HALF)[:, None], (SMALL_ROWS, V7X_LANES))

    kv, w0 = _mem_kv(mem, mem_norm_g, w_mem_kv, attn_w_in[0])
    qkv0, qm0, z0, qkv1, qkv2, w1, wo0, wo1 = _proj0(
        x, norm_g, positions, freq_rows, w0, conv_w_in[0], attn_w_out[0], conv_w_out[0])
    qkv1 = qkv1.reshape(qkv0.shape)
    qkv2 = qkv2.reshape(qkv0.shape)
    mix = _attn(qkv0, qkv1, qkv2)
    small_rows = jnp.concatenate([norm_g[1:2], final_g.reshape(1, D_MODEL), conv_w[0]], axis=0)
    return _tail(x, mix, qm0, z0, kv, wo0, w1, wo1, small_rows)
```

```python
import functools

import numpy as np
import jax
import jax.numpy as jnp
from jax import lax
from jax.experimental import pallas as pl
from jax.experimental.pallas import tpu as pltpu

F32 = jnp.float32
BF16 = jnp.bfloat16

D_MODEL = 1024
HEAD_DIM = 64
ROT_DIM = 16
ROT_HALF = ROT_DIM // 2
ROPE_THETA = 500000.0
DILATIONS = (1, 4, 16)
BLOCK = 128
GROUP_WIDTH = 512
N_GROUPS = 3
N_MEM = 256
MEM_WIDTH = 256
CONV_WIDTH = 1024
EPS = 1e-6
SCORE_SCALE = HEAD_DIM ** -0.5 * float(np.log2(np.e))

V7X_LANES = 128
V7X_SUBLANES = 8
BF16_SUBLANES = 16
ROW_TILE = 512
COL_CHUNK = 256
VMEM_LIMIT_BYTES = 56 * 1024 * 1024
NEG_BIG = -1e30
MEM_ROWS = 256
SMALL_ROWS = 16
ROW_G1, ROW_GF, ROW_CW = 0, 1, 2

QKV_G = 3 * GROUP_WIDTH
A_PART = N_GROUPS * GROUP_WIDTH
A_QM = 3 * A_PART
A_Z = A_QM + MEM_WIDTH
BRANCH_A = GROUP_WIDTH + MEM_WIDTH
IN_A_COLS = A_Z + BRANCH_A
B_BG, B_CG, B_U = 0, CONV_WIDTH, 2 * CONV_WIDTH
B_QM = 3 * CONV_WIDTH
B_Z = B_QM + MEM_WIDTH
BRANCH_B = CONV_WIDTH + MEM_WIDTH
IN_B_COLS = B_Z + BRANCH_B


def _column_scale(n_cols, scaled):
    scale = np.ones((n_cols,), np.float32)
    for lo, hi in scaled:
        scale[lo:hi] = SCORE_SCALE
    return scale


_A_SCALE = _column_scale(IN_A_COLS, [(0, A_PART), (A_QM, A_QM + MEM_WIDTH)])


def _params(*sem):
    return pltpu.CompilerParams(dimension_semantics=sem, vmem_limit_bytes=VMEM_LIMIT_BYTES)


def _resident(shape, index_map):
    return pl.BlockSpec(shape, index_map, pipeline_mode=pl.Buffered(1))


def _silu(z):
    return z * jax.nn.sigmoid(z)


def _dot(a, b):
    return jnp.dot(a, b, preferred_element_type=F32)


def _dot_nt(a, b):
    return lax.dot_general(a, b, (((1,), (1,)), ((), ())), preferred_element_type=F32)


def _half_masks():
    lane = lax.broadcasted_iota(jnp.int32, (1, V7X_LANES), 1)
    return (jnp.where(lane < HEAD_DIM, 1.0, 0.0).astype(BF16),
            jnp.where(lane < HEAD_DIM, 0.0, 1.0).astype(BF16))


def _memkv_kernel(mem_ref, g_ref, w_ref, wa_ref, scale_ref, kv_ref, wab_ref):
    w = w_ref[0].astype(BF16)
    gain = jnp.where(pl.program_id(0) == 0, g_ref[0:1, :], g_ref[1:2, :])
    for c in range(mem_ref.shape[0] // ROW_TILE):
        rows = slice(ROW_TILE * c, ROW_TILE * (c + 1))
        m = mem_ref[rows, :]
        ms = jnp.mean(m * m, axis=-1, keepdims=True)
        mn = (m * lax.rsqrt(ms + EPS) * gain).astype(BF16)
        kv_ref[0, rows, :] = _dot(mn, w).astype(BF16)
    for c in range(wa_ref.shape[0] // V7X_LANES):
        rows = slice(V7X_LANES * c, V7X_LANES * (c + 1))
        wab_ref[rows, :] = (wa_ref[rows, :] * scale_ref[...]).astype(BF16)


def _mem_kv(mem, mem_norm_g, w_mem_kv, w_attn_in):
    b = mem.shape[0]
    depth = w_mem_kv.shape[0]
    assert depth == 2, "the kernel picks the layer's gain row with a two-way select"
    rows = b * N_MEM
    slab = pl.BlockSpec((D_MODEL // depth, IN_A_COLS), lambda l: (l, 0))
    kv, w0 = pl.pallas_call(
        _memkv_kernel,
        grid=(depth,),
        in_specs=[
            _resident((rows, D_MODEL), lambda l: (0, 0)),
            pl.BlockSpec((depth, D_MODEL), lambda l: (0, 0)),
            pl.BlockSpec((1, D_MODEL, 2 * MEM_WIDTH), lambda l: (l, 0, 0)),
            slab,
            pl.BlockSpec((1, IN_A_COLS), lambda l: (0, 0)),
        ],
        out_specs=[pl.BlockSpec((1, rows, 2 * MEM_WIDTH), lambda l: (l, 0, 0)), slab],
        out_shape=[jax.ShapeDtypeStruct((depth, rows, 2 * MEM_WIDTH), BF16),
                   jax.ShapeDtypeStruct((D_MODEL, IN_A_COLS), BF16)],
        compiler_params=_params("arbitrary"),
        name="mem_kv",
    )(mem.reshape(rows, D_MODEL), mem_norm_g, w_mem_kv,
      w_attn_in, _A_SCALE.reshape(1, IN_A_COLS))
    return kv.reshape(depth, b, N_MEM, 2 * MEM_WIDTH), w0


def _rope_tables(pos_ref, freq_ref, tbl_scr):
    rows = tbl_scr.shape[1]
    freq = freq_ref[0:ROT_HALF, :]
    one = jnp.ones((V7X_SUBLANES, V7X_LANES), F32)
    zero = jnp.zeros((V7X_SUBLANES, V7X_LANES), F32)
    groups = V7X_LANES // V7X_SUBLANES
    per_head = HEAD_DIM // V7X_SUBLANES
    own_row = lax.broadcasted_iota(jnp.int32, (pos_ref.shape[0], V7X_LANES), 0) == pl.program_id(0)

    def lane_rows(first, second, other):
        pieces = []
        for i in range(groups):
            pieces.append(first if i % per_head == 0 else second if i % per_head == 1 else other)
        return jnp.concatenate(pieces, axis=0).T

    for c in range(rows // V7X_LANES):
        sl = slice(V7X_LANES * c, V7X_LANES * (c + 1))
        pos = jnp.sum(jnp.where(own_row, pos_ref[:, sl].astype(F32), 0.0), axis=0, keepdims=True)
        ang = pos * freq
        cs = jnp.cos(ang)
        sn = jnp.sin(ang)
        tbl_scr[0, sl, :] = lane_rows(cs, cs, one)
        tbl_scr[1, sl, :] = lane_rows(-sn, zero, zero)
        tbl_scr[2, sl, :] = lane_rows(zero, sn, zero)


def _rope(a, cs, sn_next, sn_prev):
    return (a * cs + pltpu.roll(a, V7X_LANES - ROT_HALF, 1) * sn_next
            + pltpu.roll(a, ROT_HALF, 1) * sn_prev)


def _proj0_kernel(x_ref, g_ref, pos_ref, freq_ref, w_ref, w1_ref, wo0_ref, wo1_ref,
                  o0_ref, oqm_ref, oz_ref, o1_ref, o2_ref, w1b_ref, wo0b_ref, wo1b_ref,
                  hn_scr, perm_scr, tbl_scr, p4_scr, tbl4_scr):
    tm = x_ref.shape[1]
    x = x_ref[0]
    ms = jnp.mean(x * x, axis=-1, keepdims=True)
    hn = x * lax.rsqrt(ms + EPS) * g_ref[0:1, :]
    hb = hn.astype(BF16)
    n_slab = D_MODEL // V7X_LANES
    for c in range(n_slab):
        hn_scr[c] = hn[:, V7X_LANES * c:V7X_LANES * (c + 1)]
    _rope_tables(pos_ref, freq_ref, tbl_scr)

    def qkv_chunk(lhs, g, j, tables):
        part, half = divmod(j, GROUP_WIDTH // COL_CHUNK)
        c0 = A_PART * part + GROUP_WIDTH * g + COL_CHUNK * half
        acc = _dot(lhs, w_ref[:, c0:c0 + COL_CHUNK])
        if part < 2:
            acc = jnp.concatenate(
                [_rope(acc[:, :V7X_LANES], *tables), _rope(acc[:, V7X_LANES:], *tables)], axis=1)
        return acc.astype(BF16)

    tables = tuple(tbl_scr[t] for t in range(3))
    for j in range(QKV_G // COL_CHUNK):
        res = qkv_chunk(hb, 0, j, tables)
        for e in range(COL_CHUNK // V7X_LANES):
            o0_ref[0, 2 * j + e] = res[:, V7X_LANES * e:V7X_LANES * (e + 1)]
    oqm_ref[0] = _dot(hb, w_ref[:, A_QM:A_QM + MEM_WIDTH]).astype(BF16)
    for j in range(BRANCH_A // COL_CHUNK):
        oz_ref[0, :, COL_CHUNK * j:COL_CHUNK * (j + 1)] = _dot(
            hb, w_ref[:, A_Z + COL_CHUNK * j:A_Z + COL_CHUNK * (j + 1)])

    d1 = DILATIONS[1]
    n1 = tm // d1
    for r in range(d1):
        rows = slice(r * n1, (r + 1) * n1)
        for c in range(n_slab):
            p4_scr[c, rows, :] = hn_scr[c, pl.ds(r, n1, stride=d1), :]
        for t in range(3):
            tbl4_scr[t, rows, :] = tbl_scr[t, pl.ds(r, n1, stride=d1), :]

    def rows_of_stream(g, r):
        if g == 1:
            return slice(r * n1, (r + 1) * n1)
        return pl.ds((r % d1) * n1 + r // d1, tm // DILATIONS[g], stride=DILATIONS[g] // d1)

    for g, o_ref in ((1, o1_ref), (2, o2_ref)):
        d = DILATIONS[g]
        n = tm // d
        for r in range(d):
            for c in range(n_slab):
                perm_scr[r * n:(r + 1) * n, V7X_LANES * c:V7X_LANES * (c + 1)] = (
                    p4_scr[c, rows_of_stream(g, r), :].astype(BF16))
        tables = tuple(
            jnp.concatenate([tbl4_scr[t, rows_of_stream(g, r), :] for r in range(d)], axis=0)
            for t in range(3))
        lhs = perm_scr[...]
        for j in range(QKV_G // COL_CHUNK):
            res = qkv_chunk(lhs, g, j, tables)
            for r in range(d):
                for e in range(COL_CHUNK // V7X_LANES):
                    o_ref[0, 2 * j + e, r] = res[r * n:(r + 1) * n, V7X_LANES * e:V7X_LANES * (e + 1)]

    w1b_ref[:, :B_QM] = w1_ref[:, :B_QM].astype(BF16)
    w1b_ref[:, B_QM:B_Z] = (w1_ref[:, B_QM:B_Z] * SCORE_SCALE).astype(BF16)
    w1b_ref[:, B_Z:] = w1_ref[:, B_Z:].astype(BF16)
    wo0b_ref[...] = wo0_ref[...].astype(BF16)
    wo1b_ref[...] = wo1_ref[...].astype(BF16)


def _proj0(x, norm_g, pos_rows, freq_rows, w0, w1, wo0, wo1):
    b, s, _ = x.shape
    tm = ROW_TILE
    per_seq = s // tm
    n_steps = b * per_seq
    row = lambda width: pl.BlockSpec((1, tm, width), lambda i, j: (i, j, 0))
    n_slab = QKV_G // V7X_LANES
    natural = pl.BlockSpec((1, n_slab, tm, V7X_LANES), lambda i, j: (i, 0, j, 0))
    stream = lambda d: pl.BlockSpec((1, n_slab, d, tm // d, V7X_LANES), lambda i, j: (i, 0, 0, j, 0))

    def slab(w, steps):
        rows = w.shape[0] // steps
        assert rows * steps == w.shape[0] and rows % BF16_SUBLANES == 0
        return pl.BlockSpec((rows, w.shape[1]), lambda i, j: (jnp.minimum(i * per_seq + j, steps - 1), 0))

    slabs = [slab(w1, n_steps), slab(wo0, n_steps // 2), slab(wo1, n_steps // 2)]
    d1, d2 = DILATIONS[1], DILATIONS[2]
    return pl.pallas_call(
        _proj0_kernel,
        grid=(b, per_seq),
        in_specs=[
            row(D_MODEL),
            pl.BlockSpec(norm_g.shape, lambda i, j: (0, 0)),
            pl.BlockSpec((b, tm), lambda i, j: (0, j)),
            pl.BlockSpec((SMALL_ROWS, V7X_LANES), lambda i, j: (0, 0)),
            _resident((D_MODEL, IN_A_COLS), lambda i, j: (0, 0)),
        ] + slabs,
        out_specs=[natural, row(MEM_WIDTH), row(BRANCH_A), stream(d1), stream(d2)] + slabs,
        out_shape=[
            jax.ShapeDtypeStruct((b, n_slab, s, V7X_LANES), BF16),
            jax.ShapeDtypeStruct((b, s, MEM_WIDTH), BF16),
            jax.ShapeDtypeStruct((b, s, BRANCH_A), F32),
            jax.ShapeDtypeStruct((b, n_slab, d1, s // d1, V7X_LANES), BF16),
            jax.ShapeDtypeStruct((b, n_slab, d2, s // d2, V7X_LANES), BF16),
            jax.ShapeDtypeStruct(w1.shape, BF16),
            jax.ShapeDtypeStruct(wo0.shape, BF16),
            jax.ShapeDtypeStruct(wo1.shape, BF16),
        ],
        scratch_shapes=[
            pltpu.VMEM((D_MODEL // V7X_LANES, tm, V7X_LANES), F32),
            pltpu.VMEM((tm, D_MODEL), BF16),
            pltpu.VMEM((3, tm, V7X_LANES), F32),
            pltpu.VMEM((D_MODEL // V7X_LANES, tm, V7X_LANES), F32),
            pltpu.VMEM((3, tm, V7X_LANES), F32),
        ],
        compiler_params=_params("arbitrary", "arbitrary"),
        name="proj0",
    )(x, norm_g, pos_rows, freq_rows, w0, w1, wo0, wo1)


def _attn_kernel(q0, k0, v0, q1, k1, v1, q2, k2, v2, o_ref,
                 num_scr, m_scr, l_scr, ve_scr, bias2_scr, bias1_scr):
    s_len = o_ref.shape[1]
    half0, half1 = _half_masks()
    first_half = lax.broadcasted_iota(jnp.int32, (BLOCK, V7X_LANES), 1) < HEAD_DIM

    @pl.when((pl.program_id(0) == 0) & (pl.program_id(1) == 0))
    def _():
        qi = lax.broadcasted_iota(jnp.int32, (2 * BLOCK, 2 * BLOCK), 0) & (BLOCK - 1)
        kj = lax.broadcasted_iota(jnp.int32, (2 * BLOCK, 2 * BLOCK), 1)
        valid_two = ((kj < BLOCK) & (kj >= qi)) | ((kj >= BLOCK) & (kj - BLOCK <= qi))
        bias2_scr[...] = jnp.where(valid_two, 0.0, NEG_BIG)
        valid_one = (lax.broadcasted_iota(jnp.int32, (2 * BLOCK, BLOCK), 1)
                     <= (lax.broadcasted_iota(jnp.int32, (2 * BLOCK, BLOCK), 0) & (BLOCK - 1)))
        bias1_scr[...] = jnp.where(valid_one, 0.0, NEG_BIG)

    for g, v_ref in enumerate((v0, v1, v2)):
        v = v_ref[...]
        for h, half in enumerate((half0, half1)):
            ve_scr[g, h] = v * half

    def block(q_ref, k_ref, g, q_row, k_row, n_keys, bias_ref, dst):
        q = q_ref[pl.ds(q_row, BLOCK), :]
        qs = jnp.concatenate([q * half0, q * half1], axis=0)
        k = k_ref[pl.ds(k_row, n_keys), :]
        s = _dot_nt(qs, k) + bias_ref[...]
        m = jnp.max(s, axis=-1, keepdims=True)
        p = jnp.exp2((s - m).astype(BF16))
        p_cat = jnp.concatenate([p[:BLOCK], p[BLOCK:]], axis=1)
        keys = pl.ds(k_row, n_keys)
        ve = jnp.concatenate(
            [jnp.concatenate([ve_scr[g, h, keys, :], jnp.broadcast_to(half, (n_keys, V7X_LANES))], axis=1)
             for h, half in enumerate((half0, half1))], axis=0)
        pv = _dot(p_cat, ve)
        num_scr[g, dst, :] = pv[:, :V7X_LANES]
        l_scr[g, dst, :] = pv[:, V7X_LANES:]
        m_scr[g, dst, :] = jnp.where(first_half, m[:BLOCK], m[BLOCK:])

    d1, d2 = DILATIONS[1], DILATIONS[2]
    stream_len = s_len // d1
    sub = d2 // d1

    for r in range(d2):
        block(q2, k2, 2, r * BLOCK, r * BLOCK, BLOCK, bias1_scr,
              pl.ds((r % d1) * stream_len + r // d1, BLOCK, stride=sub))

    for r in range(d1):
        base = r * stream_len
        block(q1, k1, 1, base, base, BLOCK, bias1_scr, pl.ds(base, BLOCK))
        for n in range(1, stream_len // BLOCK):
            block(q1, k1, 1, base + n * BLOCK, base + (n - 1) * BLOCK, 2 * BLOCK, bias2_scr,
                  pl.ds(base + n * BLOCK, BLOCK))

    def merge(n):
        rows_per = BLOCK // d1
        for r in range(d1):
            tok = pl.ds(n * BLOCK + r, rows_per, stride=d1)
            rows = pl.ds(r * stream_len + n * rows_per, rows_per)
            idx = (tok, rows, rows)
            ms = [m_scr[g, idx[g], :] for g in range(N_GROUPS)]
            top = jnp.maximum(ms[0], jnp.maximum(ms[1], ms[2]))
            ws = [jnp.exp2(mg - top) for mg in ms]
            nums = [ws[g] * num_scr[g, idx[g], :] for g in range(N_GROUPS)]
            dens = [ws[g] * l_scr[g, idx[g], :] for g in range(N_GROUPS)]
            num = nums[0] + nums[1] + nums[2]
            den = dens[0] + dens[1] + dens[2]
            o_ref[0, tok, :] = num / den

    block(q0, k0, 0, 0, 0, BLOCK, bias1_scr, pl.ds(0, BLOCK))
    for n in range(1, s_len // BLOCK):
        block(q0, k0, 0, n * BLOCK, (n - 1) * BLOCK, 2 * BLOCK, bias2_scr, pl.ds(n * BLOCK, BLOCK))
        merge(n - 1)
    merge(s_len // BLOCK - 1)


def _attn(qkv0, qkv1, qkv2):
    b, _, s, _ = qkv0.shape
    n_pair = GROUP_WIDTH // V7X_LANES

    def part(k):
        return pl.BlockSpec((None, None, s, V7X_LANES), lambda i, p, k=k: (i, k * n_pair + p, 0, 0))

    specs = [part(0), part(1), part(2)] * N_GROUPS
    stat = pltpu.VMEM((N_GROUPS, s, V7X_LANES), F32)
    scratch = [
        stat, stat, stat,
        pltpu.VMEM((N_GROUPS, 2, s, V7X_LANES), BF16),
        pltpu.VMEM((2 * BLOCK, 2 * BLOCK), F32),
        pltpu.VMEM((2 * BLOCK, BLOCK), F32),
    ]
    return pl.pallas_call(
        _attn_kernel,
        grid=(b, n_pair),
        in_specs=specs,
        out_specs=pl.BlockSpec((1, s, V7X_LANES), lambda i, p: (i, 0, p)),
        out_shape=jax.ShapeDtypeStruct((b, s, GROUP_WIDTH), F32),
        scratch_shapes=scratch,
        compiler_params=_params("arbitrary", "arbitrary"),
        name="dilated_attn",
    )(qkv0, qkv0, qkv0, qkv1, qkv1, qkv1, qkv2, qkv2, qkv2)


def _mem_scores(qm, kv_ref):
    rows = qm.shape[0]
    half0, half1 = _half_masks()
    scores = []
    for p in range(MEM_WIDTH // V7X_LANES):
        k = kv_ref[0, 0, :, V7X_LANES * p:V7X_LANES * (p + 1)]
        for c in range(rows // MEM_ROWS):
            q = qm[MEM_ROWS * c:MEM_ROWS * (c + 1), V7X_LANES * p:V7X_LANES * (p + 1)]
            scores.append(_dot_nt(jnp.concatenate([q * half0, q * half1], axis=0), k))
    return scores


def _mem_softmax_pv(scores, kv_ref):
    half0, half1 = _half_masks()
    n_pairs = MEM_WIDTH // V7X_LANES
    per_pair = len(scores) // n_pairs
    outs = []
    for p in range(n_pairs):
        v = kv_ref[0, 0, :, MEM_WIDTH + V7X_LANES * p:MEM_WIDTH + V7X_LANES * (p + 1)]
        ve = jnp.concatenate(
            [jnp.concatenate([v * half, jnp.broadcast_to(half, (N_MEM, V7X_LANES))], axis=1)
             for half in (half0, half1)], axis=0)
        chunks = []
        for s in scores[per_pair * p:per_pair * (p + 1)]:
            e = jnp.exp2((s - jnp.max(s, axis=-1, keepdims=True)).astype(BF16))
            pv = _dot(jnp.concatenate([e[:MEM_ROWS], e[MEM_ROWS:]], axis=1), ve)
            chunks.append(pv[:, :V7X_LANES] / pv[:, V7X_LANES:])
        outs.append(jnp.concatenate(chunks, axis=0))
    return jnp.concatenate(outs, axis=1)


def _tail_kernel(x_ref, mix_ref, qm0_ref, z0_ref, kv0_ref, wo0_ref, w1_ref,
                 kv1_ref, wo1_ref, rows_ref, out_ref,
                 a_scr, y_scr, h1_scr, hb_scr, *, tiles_per_seq):
    tm = x_ref.shape[1]
    halo = V7X_SUBLANES
    n_out = D_MODEL // COL_CHUNK
    step = pl.program_id(0)

    @pl.when(step == 0)
    def _():
        h1_scr[...] = jnp.zeros(h1_scr.shape, F32)
        a_scr[0:halo, :] = jnp.zeros((halo, CONV_WIDTH), F32)

    scores0 = _mem_scores(qm0_ref[0], kv0_ref)

    h1 = h1_scr[...]
    ms = jnp.mean(h1 * h1, axis=-1, keepdims=True)
    hb_scr[...] = (h1 * lax.rsqrt(ms + EPS) * rows_ref[ROW_G1:ROW_G1 + 1, :]).astype(BF16)
    seq_start = (jnp.maximum(step - 1, 0) % tiles_per_seq) == 0
    a_scr[0:halo, :] = jnp.where(seq_start, 0.0, a_scr[0:halo, :])

    def proj1(c0, width=COL_CHUNK):
        return _dot(hb_scr[...], w1_ref[:, c0:c0 + width])

    def conv_chunk(j):
        cs = slice(COL_CHUNK * j, COL_CHUNK * (j + 1))
        a_now = proj1(B_CG + COL_CHUNK * j) * proj1(B_U + COL_CHUNK * j)
        a_scr[halo:halo + tm, cs] = a_now
        conv = (rows_ref[ROW_CW:ROW_CW + 1, cs] * a_scr[halo - 2:halo - 2 + tm, cs]
                + rows_ref[ROW_CW + 1:ROW_CW + 2, cs] * a_scr[halo - 1:halo - 1 + tm, cs]
                + rows_ref[ROW_CW + 2:ROW_CW + 3, cs] * a_now)
        mix1 = proj1(B_BG + COL_CHUNK * j) * conv
        y_scr[:, cs] = (mix1 * _silu(proj1(B_Z + COL_CHUNK * j))).astype(BF16)

    qm1 = proj1(B_QM, MEM_WIDTH).astype(BF16)
    z_mem = proj1(B_Z + CONV_WIDTH, MEM_WIDTH)
    conv_chunk(0)
    scores1 = _mem_scores(qm1, kv1_ref)
    conv_chunk(1)
    mem1 = _mem_softmax_pv(scores1, kv1_ref)
    y_scr[:, CONV_WIDTH:BRANCH_B] = (mem1 * _silu(z_mem)).astype(BF16)
    conv_chunk(2)
    conv_chunk(3)
    a_scr[0:halo, :] = a_scr[tm:tm + halo, :]
    mem0 = _mem_softmax_pv(scores0, kv0_ref)
    z0 = z0_ref[0]
    y0 = jnp.concatenate(
        [mix_ref[0] * _silu(z0[:, :GROUP_WIDTH]), mem0 * _silu(z0[:, GROUP_WIDTH:])], axis=1).astype(BF16)

    y1 = y_scr[...]
    ssq = jnp.zeros((tm, 1), F32)
    for j in range(n_out):
        cs = slice(COL_CHUNK * j, COL_CHUNK * (j + 1))
        h2 = h1_scr[:, cs] + _dot(y1, wo1_ref[:, cs])
        ssq = ssq + jnp.sum(h2 * h2, axis=-1, keepdims=True)
        out_ref[0, :, cs] = h2
    scale = lax.rsqrt(ssq * (1.0 / D_MODEL) + EPS)
    out_ref[0] = out_ref[0] * scale * rows_ref[ROW_GF:ROW_GF + 1, :]

    for j in range(n_out):
        cs = slice(COL_CHUNK * j, COL_CHUNK * (j + 1))
        h1_scr[:, cs] = x_ref[0, :, cs] + _dot(y0, wo0_ref[:, cs])


def _tail(x, mix, qm0, z0, kv, wo0, w1, wo1, small_rows):
    b, s, _ = x.shape
    tm = ROW_TILE
    per_seq = s // tm
    n_tiles = b * per_seq

    def tile_a(j):
        return jnp.minimum(j, n_tiles - 1)

    def tile_b(j):
        return jnp.maximum(j - 1, 0)

    def row(width, tile):
        return pl.BlockSpec((1, tm, width), lambda j: (tile(j) // per_seq, tile(j) % per_seq, 0))

    def kv_spec(layer, tile):
        return pl.BlockSpec((1, 1, N_MEM, 2 * MEM_WIDTH), lambda j: (layer, tile(j) // per_seq, 0, 0))

    const = lambda shape: pl.BlockSpec(shape, lambda j: (0, 0))
    tile_f32 = pltpu.VMEM((tm, D_MODEL), F32)
    tile_bf16 = pltpu.VMEM((tm, D_MODEL), BF16)
    return pl.pallas_call(
        functools.partial(_tail_kernel, tiles_per_seq=per_seq),
        grid=(n_tiles + 1,),
        in_specs=[
            row(D_MODEL, tile_a), row(GROUP_WIDTH, tile_a), row(MEM_WIDTH, tile_a), row(BRANCH_A, tile_a),
            kv_spec(0, tile_a),
            _resident(wo0.shape, lambda j: (0, 0)),
            _resident(w1.shape, lambda j: (0, 0)),
            kv_spec(1, tile_b),
            _resident(wo1.shape, lambda j: (0, 0)),
            const(small_rows.shape),
        ],
        out_specs=row(D_MODEL, tile_b),
        out_shape=jax.ShapeDtypeStruct((b, s, D_MODEL), F32),
        scratch_shapes=[
            pltpu.VMEM((tm + V7X_SUBLANES, CONV_WIDTH), F32),
            pltpu.VMEM((tm, BRANCH_B), BF16),
            tile_f32, tile_bf16,
        ],
        compiler_params=_params("arbitrary"),
        name="tail",
    )(x, mix, qm0, z0, kv, wo0, w1, kv, wo1, small_rows)


def kernel(x, mem, positions, norm_g, mem_norm_g, w_mem_kv, attn_w_in, attn_w_out, conv_w_in, conv_w,
           conv_w_out, final_g):
    b, s, _ = x.shape

    inv_freq = ROPE_THETA ** (-jnp.arange(ROT_HALF, dtype=F32) * (2.0 / ROT_DIM))
    freq_rows = jnp.broadcast_to(jnp.tile(inv_freq, SMALL_ROWS // ROT_HALF)[:, None], (SMALL_ROWS, V7X_LANES))

    kv, w0 = _mem_kv(mem, mem_norm_g, w_mem_kv, attn_w_in[0])
    qkv0, qm0, z0, qkv1, qkv2, w1, wo0, wo1 = _proj0(
        x, norm_g, positions, freq_rows, w0, conv_w_in[0], attn_w_out[0], conv_w_out[0])
    qkv1 = qkv1.reshape(qkv0.shape)
    qkv2 = qkv2.reshape(qkv0.shape)
    mix = _attn(qkv0, qkv1, qkv2)
    small_rows = jnp.concatenate([norm_g[1:2], final_g.reshape(1, D_MODEL), conv_w[0]], axis=0)
    return _tail(x, mix, qm0, z0, kv, wo0, w1, wo1, small_rows)
```

```python
import functools

import numpy as np
import jax
import jax.numpy as jnp
from jax import lax
from jax.experimental import pallas as pl
from jax.experimental.pallas import tpu as pltpu

F32 = jnp.float32
BF16 = jnp.bfloat16

D_MODEL = 1024
HEAD_DIM = 64
ROT_DIM = 16
ROT_HALF = ROT_DIM // 2
ROPE_THETA = 500000.0
DILATIONS = (1, 4, 16)
BLOCK = 128
GROUP_WIDTH = 512
N_GROUPS = 3
N_MEM = 256
MEM_WIDTH = 256
CONV_WIDTH = 1024
EPS = 1e-6
SCORE_SCALE = HEAD_DIM ** -0.5 * float(np.log2(np.e))

V7X_LANES = 128
V7X_SUBLANES = 8
BF16_SUBLANES = 16
ROW_TILE = 512
COL_CHUNK = 256
VMEM_LIMIT_BYTES = 56 * 1024 * 1024
NEG_BIG = -1e30
MEM_ROWS = 256
SMALL_ROWS = 16
ROW_G1, ROW_GF, ROW_CW = 0, 1, 2

QKV_G = 3 * GROUP_WIDTH
A_PART = N_GROUPS * GROUP_WIDTH
A_QM = 3 * A_PART
A_Z = A_QM + MEM_WIDTH
BRANCH_A = GROUP_WIDTH + MEM_WIDTH
IN_A_COLS = A_Z + BRANCH_A
B_BG, B_CG, B_U = 0, CONV_WIDTH, 2 * CONV_WIDTH
B_QM = 3 * CONV_WIDTH
B_Z = B_QM + MEM_WIDTH
BRANCH_B = CONV_WIDTH + MEM_WIDTH
IN_B_COLS = B_Z + BRANCH_B


def _column_scale(n_cols, scaled):
    scale = np.ones((n_cols,), np.float32)
    for lo, hi in scaled:
        scale[lo:hi] = SCORE_SCALE
    return scale


_A_SCALE = _column_scale(IN_A_COLS, [(0, A_PART), (A_QM, A_QM + MEM_WIDTH)])


def _params(*sem):
    return pltpu.CompilerParams(dimension_semantics=sem, vmem_limit_bytes=VMEM_LIMIT_BYTES)


def _resident(shape, index_map):
    return pl.BlockSpec(shape, index_map, pipeline_mode=pl.Buffered(1))


def _silu(z):
    return z * jax.nn.sigmoid(z)


def _dot(a, b):
    return jnp.dot(a, b, preferred_element_type=F32)


def _dot_nt(a, b):
    return lax.dot_general(a, b, (((1,), (1,)), ((), ())), preferred_element_type=F32)


def _half_masks():
    lane = lax.broadcasted_iota(jnp.int32, (1, V7X_LANES), 1)
    return (jnp.where(lane < HEAD_DIM, 1.0, 0.0).astype(BF16),
            jnp.where(lane < HEAD_DIM, 0.0, 1.0).astype(BF16))


def _memkv_kernel(mem_ref, g_ref, w_ref, wa_ref, scale_ref, kv_ref, wab_ref):
    w = w_ref[0].astype(BF16)
    gain = jnp.where(pl.program_id(0) == 0, g_ref[0:1, :], g_ref[1:2, :])
    for c in range(mem_ref.shape[0] // ROW_TILE):
        rows = slice(ROW_TILE * c, ROW_TILE * (c + 1))
        m = mem_ref[rows, :]
        ms = jnp.mean(m * m, axis=-1, keepdims=True)
        mn = (m * lax.rsqrt(ms + EPS) * gain).astype(BF16)
        kv_ref[0, rows, :] = _dot(mn, w).astype(BF16)
    for c in range(wa_ref.shape[0] // V7X_LANES):
        rows = slice(V7X_LANES * c, V7X_LANES * (c + 1))
        wab_ref[rows, :] = (wa_ref[rows, :] * scale_ref[...]).astype(BF16)


def _mem_kv(mem, mem_norm_g, w_mem_kv, w_attn_in):
    b = mem.shape[0]
    depth = w_mem_kv.shape[0]
    assert depth == 2, "the kernel picks the layer's gain row with a two-way select"
    rows = b * N_MEM
    slab = pl.BlockSpec((D_MODEL // depth, IN_A_COLS), lambda l: (l, 0))
    kv, w0 = pl.pallas_call(
        _memkv_kernel,
        grid=(depth,),
        in_specs=[
            _resident((rows, D_MODEL), lambda l: (0, 0)),
            pl.BlockSpec((depth, D_MODEL), lambda l: (0, 0)),
            pl.BlockSpec((1, D_MODEL, 2 * MEM_WIDTH), lambda l: (l, 0, 0)),
            slab,
            pl.BlockSpec((1, IN_A_COLS), lambda l: (0, 0)),
        ],
        out_specs=[pl.BlockSpec((1, rows, 2 * MEM_WIDTH), lambda l: (l, 0, 0)), slab],
        out_shape=[jax.ShapeDtypeStruct((depth, rows, 2 * MEM_WIDTH), BF16),
                   jax.ShapeDtypeStruct((D_MODEL, IN_A_COLS), BF16)],
        compiler_params=_params("arbitrary"),
        name="mem_kv",
    )(mem.reshape(rows, D_MODEL), mem_norm_g, w_mem_kv,
      w_attn_in, _A_SCALE.reshape(1, IN_A_COLS))
    return kv.reshape(depth, b, N_MEM, 2 * MEM_WIDTH), w0


def _rope_tables(pos_ref, freq_ref, tbl_scr):
    rows = tbl_scr.shape[1]
    freq = freq_ref[0:ROT_HALF, :]
    one = jnp.ones((V7X_SUBLANES, V7X_LANES), F32)
    zero = jnp.zeros((V7X_SUBLANES, V7X_LANES), F32)
    groups = V7X_LANES // V7X_SUBLANES
    per_head = HEAD_DIM // V7X_SUBLANES
    own_row = lax.broadcasted_iota(jnp.int32, (pos_ref.shape[0], V7X_LANES), 0) == pl.program_id(0)

    def lane_rows(first, second, other):
        pieces = []
        for i in range(groups):
            pieces.append(first if i % per_head == 0 else second if i % per_head == 1 else other)
        return jnp.concatenate(pieces, axis=0).T

    for c in range(rows // V7X_LANES):
        sl = slice(V7X_LANES * c, V7X_LANES * (c + 1))
        pos = jnp.sum(jnp.where(own_row, pos_ref[:, sl].astype(F32), 0.0), axis=0, keepdims=True)
        ang = pos * freq
        cs = jnp.cos(ang)
        sn = jnp.sin(ang)
        tbl_scr[0, sl, :] = lane_rows(cs, cs, one)
        tbl_scr[1, sl, :] = lane_rows(-sn, zero, zero)
        tbl_scr[2, sl, :] = lane_rows(zero, sn, zero)


def _rope(a, cs, sn_next, sn_prev):
    return (a * cs + pltpu.roll(a, V7X_LANES - ROT_HALF, 1) * sn_next
            + pltpu.roll(a, ROT_HALF, 1) * sn_prev)


def _proj0_kernel(x_ref, g_ref, pos_ref, freq_ref, w_ref, w1_ref, wo0_ref, wo1_ref,
                  o0_ref, oqm_ref, oz_ref, o1_ref, o2_ref, w1b_ref, wo0b_ref, wo1b_ref,
                  hn_scr, perm_scr, tbl_scr, p4_scr, tbl4_scr):
    tm = x_ref.shape[1]
    x = x_ref[0]
    ms = jnp.mean(x * x, axis=-1, keepdims=True)
    hn = x * lax.rsqrt(ms + EPS) * g_ref[0:1, :]
    hb = hn.astype(BF16)
    n_slab = D_MODEL // V7X_LANES
    for c in range(n_slab):
        hn_scr[c] = hn[:, V7X_LANES * c:V7X_LANES * (c + 1)]
    _rope_tables(pos_ref, freq_ref, tbl_scr)

    def qkv_chunk(lhs, g, j, tables):
        part, half = divmod(j, GROUP_WIDTH // COL_CHUNK)
        c0 = A_PART * part + GROUP_WIDTH * g + COL_CHUNK * half
        acc = _dot(lhs, w_ref[:, c0:c0 + COL_CHUNK])
        if part < 2:
            acc = jnp.concatenate(
                [_rope(acc[:, :V7X_LANES], *tables), _rope(acc[:, V7X_LANES:], *tables)], axis=1)
        return acc.astype(BF16)

    tables = tuple(tbl_scr[t] for t in range(3))
    for j in range(QKV_G // COL_CHUNK):
        res = qkv_chunk(hb, 0, j, tables)
        for e in range(COL_CHUNK // V7X_LANES):
            o0_ref[0, 2 * j + e] = res[:, V7X_LANES * e:V7X_LANES * (e + 1)]
    oqm_ref[0] = _dot(hb, w_ref[:, A_QM:A_QM + MEM_WIDTH]).astype(BF16)
    for j in range(BRANCH_A // COL_CHUNK):
        oz_ref[0, :, COL_CHUNK * j:COL_CHUNK * (j + 1)] = _dot(
            hb, w_ref[:, A_Z + COL_CHUNK * j:A_Z + COL_CHUNK * (j + 1)])

    d1 = DILATIONS[1]
    n1 = tm // d1
    for r in range(d1):
        rows = slice(r * n1, (r + 1) * n1)
        for c in range(n_slab):
            p4_scr[c, rows, :] = hn_scr[c, pl.ds(r, n1, stride=d1), :]
        for t in range(3):
            tbl4_scr[t, rows, :] = tbl_scr[t, pl.ds(r, n1, stride=d1), :]

    def rows_of_stream(g, r):
        if g == 1:
            return slice(r * n1, (r + 1) * n1)
        return pl.ds((r % d1) * n1 + r // d1, tm // DILATIONS[g], stride=DILATIONS[g] // d1)

    for g, o_ref in ((1, o1_ref), (2, o2_ref)):
        d = DILATIONS[g]
        n = tm // d
        for r in range(d):
            for c in range(n_slab):
                perm_scr[r * n:(r + 1) * n, V7X_LANES * c:V7X_LANES * (c + 1)] = (
                    p4_scr[c, rows_of_stream(g, r), :].astype(BF16))
        tables = tuple(
            jnp.concatenate([tbl4_scr[t, rows_of_stream(g, r), :] for r in range(d)], axis=0)
            for t in range(3))
        lhs = perm_scr[...]
        for j in range(QKV_G // COL_CHUNK):
            res = qkv_chunk(lhs, g, j, tables)
            for r in range(d):
                for e in range(COL_CHUNK // V7X_LANES):
                    o_ref[0, 2 * j + e, r] = res[r * n:(r + 1) * n, V7X_LANES * e:V7X_LANES * (e + 1)]

    w1b_ref[:, :B_QM] = w1_ref[:, :B_QM].astype(BF16)
    w1b_ref[:, B_QM:B_Z] = (w1_ref[:, B_QM:B_Z] * SCORE_SCALE).astype(BF16)
    w1b_ref[:, B_Z:] = w1_ref[:, B_Z:].astype(BF16)
    wo0b_ref[...] = wo0_ref[...].astype(BF16)
    wo1b_ref[...] = wo1_ref[...].astype(BF16)


def _proj0(x, norm_g, pos_rows, freq_rows, w0, w1, wo0, wo1):
    b, s, _ = x.shape
    tm = ROW_TILE
    per_seq = s // tm
    n_steps = b * per_seq
    row = lambda width: pl.BlockSpec((1, tm, width), lambda i, j: (i, j, 0))
    n_slab = QKV_G // V7X_LANES
    natural = pl.BlockSpec((1, n_slab, tm, V7X_LANES), lambda i, j: (i, 0, j, 0))
    stream = lambda d: pl.BlockSpec((1, n_slab, d, tm // d, V7X_LANES), lambda i, j: (i, 0, 0, j, 0))

    def slab(w, steps):
        rows = w.shape[0] // steps
        assert rows * steps == w.shape[0] and rows % BF16_SUBLANES == 0
        return pl.BlockSpec((rows, w.shape[1]), lambda i, j: (jnp.minimum(i * per_seq + j, steps - 1), 0))

    slabs = [slab(w1, n_steps), slab(wo0, n_steps // 2), slab(wo1, n_steps // 2)]
    d1, d2 = DILATIONS[1], DILATIONS[2]
    return pl.pallas_call(
        _proj0_kernel,
        grid=(b, per_seq),
        in_specs=[
            row(D_MODEL),
            pl.BlockSpec(norm_g.shape, lambda i, j: (0, 0)),
            pl.BlockSpec((b, tm), lambda i, j: (0, j)),
            pl.BlockSpec((SMALL_ROWS, V7X_LANES), lambda i, j: (0, 0)),
            _resident((D_MODEL, IN_A_COLS), lambda i, j: (0, 0)),
        ] + slabs,
        out_specs=[natural, row(MEM_WIDTH), row(BRANCH_A), stream(d1), stream(d2)] + slabs,
        out_shape=[
            jax.ShapeDtypeStruct((b, n_slab, s, V7X_LANES), BF16),
            jax.ShapeDtypeStruct((b, s, MEM_WIDTH), BF16),
            jax.ShapeDtypeStruct((b, s, BRANCH_A), F32),
            jax.ShapeDtypeStruct((b, n_slab, d1, s // d1, V7X_LANES), BF16),
            jax.ShapeDtypeStruct((b, n_slab, d2, s // d2, V7X_LANES), BF16),
            jax.ShapeDtypeStruct(w1.shape, BF16),
            jax.ShapeDtypeStruct(wo0.shape, BF16),
            jax.ShapeDtypeStruct(wo1.shape, BF16),
        ],
        scratch_shapes=[
            pltpu.VMEM((D_MODEL // V7X_LANES, tm, V7X_LANES), F32),
            pltpu.VMEM((tm, D_MODEL), BF16),
            pltpu.VMEM((3, tm, V7X_LANES), F32),
            pltpu.VMEM((D_MODEL // V7X_LANES, tm, V7X_LANES), F32),
            pltpu.VMEM((3, tm, V7X_LANES), F32),
        ],
        compiler_params=_params("arbitrary", "arbitrary"),
        name="proj0",
    )(x, norm_g, pos_rows, freq_rows, w0, w1, wo0, wo1)


def _attn_kernel(q0, k0, v0, q1, k1, v1, q2, k2, v2, o_ref,
                 num_scr, m_scr, l_scr, ve_scr, bias2_scr, bias1_scr):
    s_len = o_ref.shape[1]
    half0, half1 = _half_masks()
    first_half = lax.broadcasted_iota(jnp.int32, (BLOCK, V7X_LANES), 1) < HEAD_DIM

    @pl.when((pl.program_id(0) == 0) & (pl.program_id(1) == 0))
    def _():
        qi = lax.broadcasted_iota(jnp.int32, (2 * BLOCK, 2 * BLOCK), 0) & (BLOCK - 1)
        kj = lax.broadcasted_iota(jnp.int32, (2 * BLOCK, 2 * BLOCK), 1)
        valid_two = ((kj < BLOCK) & (kj >= qi)) | ((kj >= BLOCK) & (kj - BLOCK <= qi))
        bias2_scr[...] = jnp.where(valid_two, 0.0, NEG_BIG)
        valid_one = (lax.broadcasted_iota(jnp.int32, (2 * BLOCK, BLOCK), 1)
                     <= (lax.broadcasted_iota(jnp.int32, (2 * BLOCK, BLOCK), 0) & (BLOCK - 1)))
        bias1_scr[...] = jnp.where(valid_one, 0.0, NEG_BIG)

    for g, v_ref in enumerate((v0, v1, v2)):
        v = v_ref[...]
        for h, half in enumerate((half0, half1)):
            ve_scr[g, h] = v * half

    def block(q_ref, k_ref, g, q_row, k_row, n_keys, bias_ref, dst):
        q = q_ref[pl.ds(q_row, BLOCK), :]
        qs = jnp.concatenate([q * half0, q * half1], axis=0)
        k = k_ref[pl.ds(k_row, n_keys), :]
        s = (_dot_nt(qs, k) + bias_ref[...]).astype(BF16)
        m_half = jnp.max(s, axis=-1, keepdims=True)
        p = jnp.exp2(s - m_half)
        m = m_half.astype(F32)
        p_cat = jnp.concatenate([p[:BLOCK], p[BLOCK:]], axis=1)
        keys = pl.ds(k_row, n_keys)
        ve = jnp.concatenate(
            [jnp.concatenate([ve_scr[g, h, keys, :], jnp.broadcast_to(half, (n_keys, V7X_LANES))], axis=1)
             for h, half in enumerate((half0, half1))], axis=0)
        pv = _dot(p_cat, ve)
        num_scr[g, dst, :] = pv[:, :V7X_LANES]
        l_scr[g, dst, :] = pv[:, V7X_LANES:]
        m_scr[g, dst, :] = jnp.where(first_half, m[:BLOCK], m[BLOCK:])

    d1, d2 = DILATIONS[1], DILATIONS[2]
    stream_len = s_len // d1
    sub = d2 // d1

    for r in range(d2):
        block(q2, k2, 2, r * BLOCK, r * BLOCK, BLOCK, bias1_scr,
              pl.ds((r % d1) * stream_len + r // d1, BLOCK, stride=sub))

    for r in range(d1):
        base = r * stream_len
        block(q1, k1, 1, base, base, BLOCK, bias1_scr, pl.ds(base, BLOCK))
        for n in range(1, stream_len // BLOCK):
            block(q1, k1, 1, base + n * BLOCK, base + (n - 1) * BLOCK, 2 * BLOCK, bias2_scr,
                  pl.ds(base + n * BLOCK, BLOCK))

    def merge(n):
        rows_per = BLOCK // d1
        for r in range(d1):
            tok = pl.ds(n * BLOCK + r, rows_per, stride=d1)
            rows = pl.ds(r * stream_len + n * rows_per, rows_per)
            idx = (tok, rows, rows)
            ms = [m_scr[g, idx[g], :] for g in range(N_GROUPS)]
            top = jnp.maximum(ms[0], jnp.maximum(ms[1], ms[2]))
            ws = [jnp.exp2(mg - top) for mg in ms]
            nums = [ws[g] * num_scr[g, idx[g], :] for g in range(N_GROUPS)]
            dens = [ws[g] * l_scr[g, idx[g], :] for g in range(N_GROUPS)]
            num = nums[0] + nums[1] + nums[2]
            den = dens[0] + dens[1] + dens[2]
            o_ref[0, tok, :] = num / den

    block(q0, k0, 0, 0, 0, BLOCK, bias1_scr, pl.ds(0, BLOCK))
    for n in range(1, s_len // BLOCK):
        block(q0, k0, 0, n * BLOCK, (n - 1) * BLOCK, 2 * BLOCK, bias2_scr, pl.ds(n * BLOCK, BLOCK))
        merge(n - 1)
    merge(s_len // BLOCK - 1)


def _attn(qkv0, qkv1, qkv2):
    b, _, s, _ = qkv0.shape
    n_pair = GROUP_WIDTH // V7X_LANES

    def part(k):
        return pl.BlockSpec((None, None, s, V7X_LANES), lambda i, p, k=k: (i, k * n_pair + p, 0, 0))

    specs = [part(0), part(1), part(2)] * N_GROUPS
    stat = pltpu.VMEM((N_GROUPS, s, V7X_LANES), F32)
    scratch = [
        stat, stat, stat,
        pltpu.VMEM((N_GROUPS, 2, s, V7X_LANES), BF16),
        pltpu.VMEM((2 * BLOCK, 2 * BLOCK), F32),
        pltpu.VMEM((2 * BLOCK, BLOCK), F32),
    ]
    return pl.pallas_call(
        _attn_kernel,
        grid=(b, n_pair),
        in_specs=specs,
        out_specs=pl.BlockSpec((1, s, V7X_LANES), lambda i, p: (i, 0, p)),
        out_shape=jax.ShapeDtypeStruct((b, s, GROUP_WIDTH), F32),
        scratch_shapes=scratch,
        compiler_params=_params("arbitrary", "arbitrary"),
        name="dilated_attn",
    )(qkv0, qkv0, qkv0, qkv1, qkv1, qkv1, qkv2, qkv2, qkv2)


def _mem_scores(qm, kv_ref):
    rows = qm.shape[0]
    half0, half1 = _half_masks()
    scores = []
    for p in range(MEM_WIDTH // V7X_LANES):
        k = kv_ref[0, 0, :, V7X_LANES * p:V7X_LANES * (p + 1)]
        for c in range(rows // MEM_ROWS):
            q = qm[MEM_ROWS * c:MEM_ROWS * (c + 1), V7X_LANES * p:V7X_LANES * (p + 1)]
            scores.append(_dot_nt(jnp.concatenate([q * half0, q * half1], axis=0), k))
    return scores


def _mem_softmax_pv(scores, kv_ref):
    half0, half1 = _half_masks()
    n_pairs = MEM_WIDTH // V7X_LANES
    per_pair = len(scores) // n_pairs
    outs = []
    for p in range(n_pairs):
        v = kv_ref[0, 0, :, MEM_WIDTH + V7X_LANES * p:MEM_WIDTH + V7X_LANES * (p + 1)]
        ve = jnp.concatenate(
            [jnp.concatenate([v * half, jnp.broadcast_to(half, (N_MEM, V7X_LANES))], axis=1)
             for half in (half0, half1)], axis=0)
        chunks = []
        for s in scores[per_pair * p:per_pair * (p + 1)]:
            e = jnp.exp2(s - jnp.max(s, axis=-1, keepdims=True)).astype(BF16)
            pv = _dot(jnp.concatenate([e[:MEM_ROWS], e[MEM_ROWS:]], axis=1), ve)
            chunks.append(pv[:, :V7X_LANES] / pv[:, V7X_LANES:])
        outs.append(jnp.concatenate(chunks, axis=0))
    return jnp.concatenate(outs, axis=1)


def _tail_kernel(x_ref, mix_ref, qm0_ref, z0_ref, kv0_ref, wo0_ref, w1_ref,
                 kv1_ref, wo1_ref, rows_ref, out_ref,
                 a_scr, y_scr, h1_scr, hb_scr, *, tiles_per_seq):
    tm = x_ref.shape[1]
    halo = V7X_SUBLANES
    n_out = D_MODEL // COL_CHUNK
    step = pl.program_id(0)

    @pl.when(step == 0)
    def _():
        h1_scr[...] = jnp.zeros(h1_scr.shape, F32)
        a_scr[0:halo, :] = jnp.zeros((halo, CONV_WIDTH), F32)

    scores0 = _mem_scores(qm0_ref[0], kv0_ref)

    h1 = h1_scr[...]
    ms = jnp.mean(h1 * h1, axis=-1, keepdims=True)
    hb_scr[...] = (h1 * lax.rsqrt(ms + EPS) * rows_ref[ROW_G1:ROW_G1 + 1, :]).astype(BF16)
    seq_start = (jnp.maximum(step - 1, 0) % tiles_per_seq) == 0
    a_scr[0:halo, :] = jnp.where(seq_start, 0.0, a_scr[0:halo, :])

    def proj1(c0, width=COL_CHUNK):
        return _dot(hb_scr[...], w1_ref[:, c0:c0 + width])

    def conv_chunk(j):
        cs = slice(COL_CHUNK * j, COL_CHUNK * (j + 1))
        a_now = proj1(B_CG + COL_CHUNK * j) * proj1(B_U + COL_CHUNK * j)
        a_scr[halo:halo + tm, cs] = a_now
        conv = (rows_ref[ROW_CW:ROW_CW + 1, cs] * a_scr[halo - 2:halo - 2 + tm, cs]
                + rows_ref[ROW_CW + 1:ROW_CW + 2, cs] * a_scr[halo - 1:halo - 1 + tm, cs]
                + rows_ref[ROW_CW + 2:ROW_CW + 3, cs] * a_now)
        mix1 = proj1(B_BG + COL_CHUNK * j) * conv
        y_scr[:, cs] = (mix1 * _silu(proj1(B_Z + COL_CHUNK * j))).astype(BF16)

    qm1 = proj1(B_QM, MEM_WIDTH).astype(BF16)
    z_mem = proj1(B_Z + CONV_WIDTH, MEM_WIDTH)
    conv_chunk(0)
    scores1 = _mem_scores(qm1, kv1_ref)
    conv_chunk(1)
    mem1 = _mem_softmax_pv(scores1, kv1_ref)
    y_scr[:, CONV_WIDTH:BRANCH_B] = (mem1 * _silu(z_mem)).astype(BF16)
    conv_chunk(2)
    conv_chunk(3)
    a_scr[0:halo, :] = a_scr[tm:tm + halo, :]
    mem0 = _mem_softmax_pv(scores0, kv0_ref)
    z0 = z0_ref[0]
    y0 = jnp.concatenate(
        [mix_ref[0] * _silu(z0[:, :GROUP_WIDTH]), mem0 * _silu(z0[:, GROUP_WIDTH:])], axis=1).astype(BF16)

    y1 = y_scr[...]
    ssq = jnp.zeros((tm, 1), F32)
    for j in range(n_out):
        cs = slice(COL_CHUNK * j, COL_CHUNK * (j + 1))
        h2 = h1_scr[:, cs] + _dot(y1, wo1_ref[:, cs])
        ssq = ssq + jnp.sum(h2 * h2, axis=-1, keepdims=True)
        out_ref[0, :, cs] = h2
    scale = lax.rsqrt(ssq * (1.0 / D_MODEL) + EPS)
    out_ref[0] = out_ref[0] * scale * rows_ref[ROW_GF:ROW_GF + 1, :]

    for j in range(n_out):
        cs = slice(COL_CHUNK * j, COL_CHUNK * (j + 1))
        h1_scr[:, cs] = x_ref[0, :, cs] + _dot(y0, wo0_ref[:, cs])


def _tail(x, mix, qm0, z0, kv, wo0, w1, wo1, small_rows):
    b, s, _ = x.shape
    tm = ROW_TILE
    per_seq = s // tm
    n_tiles = b * per_seq

    def tile_a(j):
        return jnp.minimum(j, n_tiles - 1)

    def tile_b(j):
        return jnp.maximum(j - 1, 0)

    def row(width, tile):
        return pl.BlockSpec((1, tm, width), lambda j: (tile(j) // per_seq, tile(j) % per_seq, 0))

    def kv_spec(layer, tile):
        return pl.BlockSpec((1, 1, N_MEM, 2 * MEM_WIDTH), lambda j: (layer, tile(j) // per_seq, 0, 0))

    const = lambda shape: pl.BlockSpec(shape, lambda j: (0, 0))
    tile_f32 = pltpu.VMEM((tm, D_MODEL), F32)
    tile_bf16 = pltpu.VMEM((tm, D_MODEL), BF16)
    return pl.pallas_call(
        functools.partial(_tail_kernel, tiles_per_seq=per_seq),
        grid=(n_tiles + 1,),
        in_specs=[
            row(D_MODEL, tile_a), row(GROUP_WIDTH, tile_a), row(MEM_WIDTH, tile_a), row(BRANCH_A, tile_a),
            kv_spec(0, tile_a),
            _resident(wo0.shape, lambda j: (0, 0)),
            _resident(w1.shape, lambda j: (0, 0)),
            kv_spec(1, tile_b),
            _resident(wo1.shape, lambda j: (0, 0)),
            const(small_rows.shape),
        ],
        out_specs=row(D_MODEL, tile_b),
        out_shape=jax.ShapeDtypeStruct((b, s, D_MODEL), F32),
        scratch_shapes=[
            pltpu.VMEM((tm + V7X_SUBLANES, CONV_WIDTH), F32),
            pltpu.VMEM((tm, BRANCH_B), BF16),
            tile_f32, tile_bf16,
        ],
        compiler_params=_params("arbitrary"),
        name="tail",
    )(x, mix, qm0, z0, kv, wo0, w1, kv, wo1, small_rows)


def kernel(x, mem, positions, norm_g, mem_norm_g, w_mem_kv, attn_w_in, attn_w_out, conv_w_in, conv_w,
           conv_w_out, final_g):
    b, s, _ = x.shape

    inv_freq = ROPE_THETA ** (-jnp.arange(ROT_HALF, dtype=F32) * (2.0 / ROT_DIM))
    freq_rows = jnp.broadcast_to(jnp.tile(inv_freq, SMALL_ROWS // ROT_HALF)[:, None], (SMALL_ROWS, V7X_LANES))

    kv, w0 = _mem_kv(mem, mem_norm_g, w_mem_kv, attn_w_in[0])
    qkv0, qm0, z0, qkv1, qkv2, w1, wo0, wo1 = _proj0(
        x, norm_g, positions, freq_rows, w0, conv_w_in[0], attn_w_out[0], conv_w_out[0])
    qkv1 = qkv1.reshape(qkv0.shape)
    qkv2 = qkv2.reshape(qkv0.shape)
    mix = _attn(qkv0, qkv1, qkv2)
    small_rows = jnp.concatenate([norm_g[1:2], final_g.reshape(1, D_MODEL), conv_w[0]], axis=0)
    return _tail(x, mix, qm0, z0, kv, wo0, w1, wo1, small_rows)
```
